```python
import jax, jax.numpy as jnp
from jax import lax
import numpy as np

D_MODEL = 1024
BATCH = 8
SEQ = 2048
DEPTH = 1
DEC_BATCH = 128
DEC_SEQ = 1
PAST_LEN = 16384
PAGE_SIZE = 128

RET_HEADS = 4
RET_DK = D_MODEL // 8
RET_DV = D_MODEL // 8
RET_W = RET_HEADS * RET_DK
RET_VW = RET_HEADS * RET_DV
RET_CHUNK = 128
ROPE_BASE = 10000.0
CONV_W = D_MODEL // 2
CONV_K = 31
MEM_HEADS = 4
MEM_DH = D_MODEL // 8
MEM_W = MEM_HEADS * MEM_DH
N_MEM = 256
N_BRANCH = 3
IN_W = 2 * RET_W + 2 * RET_VW + 3 * CONV_W + 2 * MEM_W + N_BRANCH * D_MODEL
EPS = 1e-6

kernel_name = 'retnet_conformer_memxattn_gated_hybrid_step'

F32 = jnp.float32


def _rmsnorm(x, w):
    xf = x.astype(F32)
    y = xf * lax.rsqrt(jnp.mean(xf * xf, axis=-1, keepdims=True) + EPS)
    return (y * w.astype(F32)).astype(x.dtype)


def _layernorm(x, w, b):
    xf = x.astype(F32)
    mu = jnp.mean(xf, axis=-1, keepdims=True)
    xc = xf - mu
    var = jnp.mean(xc * xc, axis=-1, keepdims=True)
    y = xc * lax.rsqrt(var + EPS) * w.astype(F32) + b.astype(F32)
    return y.astype(x.dtype)


def _rotary(x, pos):
    half = x.shape[-1] // 2
    inv = ROPE_BASE ** (-jnp.arange(half, dtype=F32) / half)
    ang = pos[:, None] * inv[None, :]
    cos = jnp.cos(ang)[None, :, None, :]
    sin = jnp.sin(ang)[None, :, None, :]
    x1, x2 = x[..., :half], x[..., half:]
    return jnp.concatenate([x1 * cos - x2 * sin, x1 * sin + x2 * cos], axis=-1)


def _retention(q, k, v, s0):
    B, T, H, _ = q.shape
    DV = v.shape[-1]
    C = RET_CHUNK if T % RET_CHUNK == 0 else T
    n = T // C
    log_g = jnp.log1p(-jnp.exp2(-5.0 - jnp.arange(H, dtype=F32)))
    idx = jnp.arange(C, dtype=F32)
    diff = idx[:, None] - idx[None, :]
    intra = jnp.where(diff >= 0,
                      jnp.exp(jnp.maximum(diff, 0.0)[None] * log_g[:, None, None]),
                      0.0)
    q_dec = jnp.exp((idx + 1.0)[None, :] * log_g[:, None])
    k_dec = jnp.exp((C - 1.0 - idx)[None, :] * log_g[:, None])
    c_dec = jnp.exp(C * log_g)

    def to_chunks(a):
        return a.reshape(B, n, C, H, a.shape[-1]).transpose(1, 0, 3, 2, 4)

    def step(s, blk):
        qc, kc, vc = blk
        sc = jnp.einsum('bhid,bhjd->bhij', qc, kc) * intra[None]
        o = (jnp.einsum('bhij,bhje->bhie', sc, vc)
             + jnp.einsum('bhid,bhde->bhie', qc, s) * q_dec[None, :, :, None])
        s = (s * c_dec[None, :, None, None]
             + jnp.einsum('bhjd,bhje->bhde', kc * k_dec[None, :, :, None], vc))
        return s, o

    s_fin, o = lax.scan(step, s0.astype(F32), (to_chunks(q), to_chunks(k), to_chunks(v)))
    o = o.transpose(1, 0, 3, 2, 4).reshape(B, T, H, DV)
    return o, s_fin


def _group_norm_heads(o, w):
    B, T, H, DV = o.shape
    mu = jnp.mean(o, axis=-1, keepdims=True)
    oc = o - mu
    var = jnp.mean(oc * oc, axis=-1, keepdims=True)
    y = (oc * lax.rsqrt(var + EPS)).reshape(B, T, H * DV)
    return y * w.astype(F32)


def _causal_dwconv(u, buf, w, b):
    full = jnp.concatenate([buf.astype(u.dtype), u], axis=1)
    y = lax.conv_general_dilated(
        full, w.astype(u.dtype)[:, None, :], window_strides=(1,), padding='VALID',
        dimension_numbers=('NWC', 'WIO', 'NWC'), feature_group_count=u.shape[-1])
    return y + b.astype(u.dtype), full[:, -(CONV_K - 1):]


def _mem_kv(mem, mem_norm_w, w_mem_kv):
    B = mem.shape[0]
    kv = _rmsnorm(mem, mem_norm_w) @ w_mem_kv
    k, v = jnp.split(kv, 2, axis=-1)
    return (k.reshape(B, -1, MEM_HEADS, MEM_DH), v.reshape(B, -1, MEM_HEADS, MEM_DH))


def _mem_attend(q, k, v):
    B, T, _ = q.shape
    qh = q.reshape(B, T, MEM_HEADS, MEM_DH).astype(F32)
    s = jnp.einsum('bthd,bmhd->bhtm', qh, k.astype(F32)) * (MEM_DH ** -0.5)
    p = jax.nn.softmax(s, axis=-1)
    o = jnp.einsum('bhtm,bmhd->bthd', p, v.astype(F32))
    return o.reshape(B, T, MEM_W).astype(q.dtype)


def _layer(x, pos0, s0, conv_buf, mem_k, mem_v, norm_w, w_in, ret_gn_w, conv_w, conv_b,
           conv_ln_w, conv_ln_b, w_br_ret, w_br_conv, w_br_mem, w_out):
    B, T, _ = x.shape
    h = _rmsnorm(x, norm_w)
    z = h @ w_in
    sizes = (RET_W, RET_W, RET_VW, RET_VW, CONV_W, CONV_W, CONV_W, MEM_W, MEM_W)
    offs = np.cumsum(sizes).tolist()
    q_r, k_r, v_r, g_r, a_c, b_c, g_c, q_m, g_m, g_merge = jnp.split(z, offs, axis=-1)

    pos = jnp.arange(T, dtype=F32) + pos0
    q = _rotary(q_r.reshape(B, T, RET_HEADS, RET_DK).astype(F32), pos) * (RET_DK ** -0.5)
    k = _rotary(k_r.reshape(B, T, RET_HEADS, RET_DK).astype(F32), pos)
    v = v_r.reshape(B, T, RET_HEADS, RET_DV).astype(F32)
    o, s_new = _retention(q, k, v, s0)
    ret = _group_norm_heads(o, ret_gn_w).astype(x.dtype)
    br_r = (jax.nn.silu(g_r) * ret) @ w_br_ret

    u = a_c * jax.nn.sigmoid(b_c)
    c, buf_new = _causal_dwconv(u, conv_buf, conv_w, conv_b)
    c = jax.nn.silu(_layernorm(c, conv_ln_w, conv_ln_b))
    br_c = (jax.nn.silu(g_c) * c) @ w_br_conv

    m = _mem_attend(q_m, mem_k, mem_v)
    br_m = (jax.nn.silu(g_m) * m) @ w_br_mem

    g = jax.nn.sigmoid(g_merge).reshape(B, T, N_BRANCH, D_MODEL)
    merged = g[:, :, 0] * br_r + g[:, :, 1] * br_c + g[:, :, 2] * br_m
    return x + merged @ w_out, s_new, buf_new


def setup_inputs(seed: int = 0) -> dict:
    key = jax.random.key(seed)
    ks = jax.random.split(key, 21)

    def nrm(k, shape, s):
        return s * jax.random.normal(k, shape, F32)

    return {
        'x_prompt': nrm(ks[0], (BATCH, SEQ, D_MODEL), 1.0),
        'x_sample': nrm(ks[1], (DEC_BATCH, DEC_SEQ, D_MODEL), 1.0),
        'mem_prompt': nrm(ks[2], (BATCH, N_MEM, D_MODEL), 1.0),
        'state_ret': nrm(ks[3], (DEPTH, DEC_BATCH, RET_HEADS, RET_DK, RET_DV), 4.0),
        'state_conv': nrm(ks[4], (DEPTH, DEC_BATCH, CONV_K - 1, CONV_W), 0.5),
        'cache_mem_k': nrm(ks[5], (DEPTH, DEC_BATCH, N_MEM, MEM_HEADS, MEM_DH), 1.0),
        'cache_mem_v': nrm(ks[6], (DEPTH, DEC_BATCH, N_MEM, MEM_HEADS, MEM_DH), 1.0),
        'norm_w': 1.0 + nrm(ks[7], (DEPTH, D_MODEL), 0.02),
        'w_in': nrm(ks[8], (DEPTH, D_MODEL, IN_W), D_MODEL ** -0.5),
        'ret_gn_w': 1.0 + nrm(ks[9], (DEPTH, RET_VW), 0.02),
        'conv_w': nrm(ks[10], (DEPTH, CONV_K, CONV_W), CONV_K ** -0.5),
        'conv_b': nrm(ks[11], (DEPTH, CONV_W), 0.02),
        'conv_ln_w': 1.0 + nrm(ks[12], (DEPTH, CONV_W), 0.02),
        'conv_ln_b': nrm(ks[13], (DEPTH, CONV_W), 0.02),
        'mem_norm_w': 1.0 + nrm(ks[14], (DEPTH, D_MODEL), 0.02),
        'w_mem_kv': nrm(ks[15], (DEPTH, D_MODEL, 2 * MEM_W), D_MODEL ** -0.5),
        'w_br_ret': nrm(ks[16], (DEPTH, RET_VW, D_MODEL), RET_VW ** -0.5),
        'w_br_conv': nrm(ks[17], (DEPTH, CONV_W, D_MODEL), CONV_W ** -0.5),
        'w_br_mem': nrm(ks[18], (DEPTH, MEM_W, D_MODEL), MEM_W ** -0.5),
        'w_out': nrm(ks[19], (DEPTH, D_MODEL, D_MODEL), D_MODEL ** -0.5),
        'final_norm_w': 1.0 + nrm(ks[20], (D_MODEL,), 0.02),
    }


def reference(x_prompt, x_sample, mem_prompt, state_ret, state_conv, cache_mem_k, cache_mem_v,
              norm_w, w_in, ret_gn_w, conv_w, conv_b, conv_ln_w, conv_ln_b, mem_norm_w,
              w_mem_kv, w_br_ret, w_br_conv, w_br_mem, w_out, final_norm_w):
    hp, hs = x_prompt, x_sample
    bp = x_prompt.shape[0]
    ret_p, ret_s, conv_p, conv_s, mk_p, mv_p = [], [], [], [], [], []
    for l in range(DEPTH):
        lw = (norm_w[l], w_in[l], ret_gn_w[l], conv_w[l], conv_b[l], conv_ln_w[l],
              conv_ln_b[l], w_br_ret[l], w_br_conv[l], w_br_mem[l], w_out[l])
        mk, mv = _mem_kv(mem_prompt, mem_norm_w[l], w_mem_kv[l])
        s0 = jnp.zeros((bp, RET_HEADS, RET_DK, RET_DV), F32)
        buf0 = jnp.zeros((bp, CONV_K - 1, CONV_W), hp.dtype)
        hp, sp, cp = _layer(hp, 0, s0, buf0, mk, mv, *lw)
        hs, ss, cs = _layer(hs, PAST_LEN, state_ret[l], state_conv[l],
                            cache_mem_k[l], cache_mem_v[l], *lw)
        ret_p.append(sp)
        ret_s.append(ss)
        conv_p.append(cp)
        conv_s.append(cs)
        mk_p.append(mk)
        mv_p.append(mv)
    y_prompt = _rmsnorm(hp, final_norm_w)
    y_sample = _rmsnorm(hs, final_norm_w)
    return (y_prompt, y_sample, jnp.stack(ret_p), jnp.stack(ret_s), jnp.stack(conv_p),
            jnp.stack(conv_s), jnp.stack(mk_p), jnp.stack(mv_p))
```

```python
import functools

import numpy as np
import jax
import jax.numpy as jnp
from jax import lax
from jax.experimental import pallas as pl
from jax.experimental.pallas import tpu as pltpu

F32 = jnp.float32
BF16 = jnp.bfloat16

D_MODEL = 1024
RET_HEADS = 4
HEAD_DIM = 128
BRANCH_W = 512
RET_CHUNK = 128
ROPE_BASE = 10000.0
CONV_K = 31
CONV_HIST = CONV_K - 1
MEM_HEADS = 4
N_BRANCH = 3
EPS = 1e-6
PAST_LEN = 16384

OFF_QR, OFF_KR, OFF_VR, OFF_GR = 0, 512, 1024, 1536
OFF_AC, OFF_BC, OFF_GC = 2048, 2560, 3072
OFF_QM, OFF_GM = 3584, 4096
OFF_MERGE = 4608
IN_W = OFF_MERGE + N_BRANCH * D_MODEL

V7X_VMEM_BYTES = 64 * 1024 * 1024
VMEM_LIMIT_BYTES = V7X_VMEM_BYTES - 8 * 1024 * 1024

PROMPT_TILE = 256
HIST_PAD = 32
SAMPLE_BLOCK = 8


def _sigmoid(x):
    return 0.5 * jnp.tanh(0.5 * x) + 0.5


def _silu(x):
    hx = 0.5 * x
    return hx * (jnp.tanh(hx) + 1.0)


def _rms_scale(x):
    return lax.rsqrt(jnp.mean(x * x, axis=-1, keepdims=True) + EPS)


def _layernorm_rows(x):
    mu = jnp.mean(x, axis=-1, keepdims=True)
    xc = x - mu
    var = jnp.mean(xc * xc, axis=-1, keepdims=True)
    return xc * lax.rsqrt(var + EPS)


def _rotary(x, cos2, sin2):
    return x * cos2 + pltpu.roll(x, HEAD_DIM // 2, axis=1) * sin2


def _dot(a, b):
    return jnp.dot(a, b, preferred_element_type=F32)


def _dot_nt(a, b):
    return lax.dot_general(a, b, (((1,), (1,)), ((), ())), preferred_element_type=F32)


def _dot_tn(a, b):
    return lax.dot_general(a, b, (((0,), (0,)), ((), ())), preferred_element_type=F32)


def _head(x, h):
    return x[:, h * HEAD_DIM:(h + 1) * HEAD_DIM]


def _group_norm_heads(o, gn_w):
    parts = [_layernorm_rows(_head(o, h)) for h in range(RET_HEADS)]
    return jnp.concatenate(parts, axis=-1) * gn_w


def _merge_and_project(x, h_bf, w_in_ref, yr, yc, ym, w_br_ret_ref, w_br_conv_ref,
                       w_br_mem_ref, w_out_ref, final_w):
    branches = (
        _dot(yr.astype(BF16), w_br_ret_ref[...]),
        _dot(yc.astype(BF16), w_br_conv_ref[...]),
        _dot(ym.astype(BF16), w_br_mem_ref[...]),
    )
    merged = None
    for i, br in enumerate(branches):
        lo = OFF_MERGE + i * D_MODEL
        gate = _sigmoid(_dot(h_bf, w_in_ref[:, lo:lo + D_MODEL]))
        term = gate * br
        merged = term if merged is None else merged + term
    out = x + _dot(merged.astype(BF16), w_out_ref[...])
    return out * _rms_scale(out) * final_w


def _mem_kv_kernel(mem_ref, norm_w_ref, w_ref, k_ref, v_ref, kbf_ref, vbf_ref):
    m = mem_ref[0]
    hm = (m * _rms_scale(m) * norm_w_ref[...]).astype(BF16)
    kv = _dot(hm, w_ref[...])
    k = kv[:, :BRANCH_W]
    v = kv[:, BRANCH_W:]
    k_ref[0] = k
    v_ref[0] = v
    kbf_ref[0] = k.astype(BF16)
    vbf_ref[0] = v.astype(BF16)


def _mem_kv(mem, mem_norm_w, w_mem_kv_bf):
    batch, n_mem, _ = mem.shape
    blk = lambda dt: jax.ShapeDtypeStruct((batch, n_mem, BRANCH_W), dt)
    out_spec = pl.BlockSpec((1, n_mem, BRANCH_W), lambda b: (b, 0, 0))
    return pl.pallas_call(
        _mem_kv_kernel,
        grid=(batch,),
        in_specs=[
            pl.BlockSpec((1, n_mem, D_MODEL), lambda b: (b, 0, 0)),
            pl.BlockSpec((1, D_MODEL), lambda b: (0, 0)),
            pl.BlockSpec((D_MODEL, 2 * BRANCH_W), lambda b: (0, 0)),
        ],
        out_specs=[out_spec, out_spec, out_spec, out_spec],
        out_shape=[blk(F32), blk(F32), blk(BF16), blk(BF16)],
        compiler_params=pltpu.CompilerParams(dimension_semantics=("arbitrary",)),
        name="mem_kv",
    )(mem, mem_norm_w, w_mem_kv_bf)


def _prompt_kernel(c_dec, x_ref, cos_ref, sin_ref, mk_ref, mv_ref, norm_w_ref, w_in_ref,
                   gn_w_ref, conv_w_ref, conv_b_ref, ln_w_ref, ln_b_ref,
                   w_br_ret_ref, w_br_conv_ref, w_br_mem_ref, w_out_ref, final_w_ref,
                   intra_ref, qdec_ref, kdec_ref,
                   y_ref, s_ref, hist_ref, ubuf_ref, o_ref):
    t = pl.program_id(1)
    tile = x_ref.shape[1]

    @pl.when(t == 0)
    def _():
        s_ref[...] = jnp.zeros_like(s_ref)
        ubuf_ref[0:HIST_PAD, :] = jnp.zeros((HIST_PAD, BRANCH_W), F32)

    x = x_ref[0]
    h_bf = (x * _rms_scale(x) * norm_w_ref[...]).astype(BF16)

    def proj(off, width=BRANCH_W):
        return _dot(h_bf, w_in_ref[:, off:off + width])

    cos2 = cos_ref[...]
    sin2 = sin_ref[...]
    zq = proj(OFF_QR)
    zk = proj(OFF_KR)
    v_bf = proj(OFF_VR).astype(BF16)
    q_scale = HEAD_DIM ** -0.5
    for h in range(RET_HEADS):
        q_h = _rotary(_head(zq, h), cos2, sin2) * q_scale
        k_h = _rotary(_head(zk, h), cos2, sin2)
        v_h = _head(v_bf, h)
        for c in range(tile // RET_CHUNK):
            rows = slice(c * RET_CHUNK, (c + 1) * RET_CHUNK)
            qc = q_h[rows].astype(BF16)
            kc = k_h[rows]
            vc = v_h[rows]
            s = s_ref[0, h]
            sc = _dot_nt(qc, kc.astype(BF16)) * intra_ref[h]
            o = _dot(sc.astype(BF16), vc) + _dot(qc, s.astype(BF16)) * qdec_ref[h]
            kd = (kc * kdec_ref[h]).astype(BF16)
            s_ref[0, h] = s * c_dec[h] + _dot_tn(kd, vc)
            o_ref[rows, h * HEAD_DIM:(h + 1) * HEAD_DIM] = o
    ret = _group_norm_heads(o_ref[...], gn_w_ref[...])
    yr = _silu(proj(OFF_GR)) * ret

    u = proj(OFF_AC) * _sigmoid(proj(OFF_BC))
    ubuf_ref[HIST_PAD:HIST_PAD + tile, :] = u
    first = HIST_PAD - CONV_HIST
    acc = None
    for j in range(CONV_K):
        term = ubuf_ref[first + j:first + j + tile, :] * conv_w_ref[j:j + 1, :]
        acc = term if acc is None else acc + term
    conv = acc + conv_b_ref[...]
    ubuf_ref[0:HIST_PAD, :] = ubuf_ref[tile:tile + HIST_PAD, :]
    cn = _layernorm_rows(conv) * ln_w_ref[...] + ln_b_ref[...]
    yc = _silu(proj(OFF_GC)) * _silu(cn)

    @pl.when(t == pl.num_programs(1) - 1)
    def _():
        hist_ref[0] = ubuf_ref[first:HIST_PAD, :]

    qm = proj(OFF_QM).astype(BF16)
    mk = mk_ref[0]
    mv = mv_ref[0]
    att = []
    for h in range(MEM_HEADS):
        sc = _dot_nt(_head(qm, h), _head(mk, h)) * (HEAD_DIM ** -0.5)
        e = jnp.exp(sc - jnp.max(sc, axis=-1, keepdims=True))
        p = e / jnp.sum(e, axis=-1, keepdims=True)
        att.append(_dot(p.astype(BF16), _head(mv, h)))
    ym = _silu(proj(OFF_GM)) * jnp.concatenate(att, axis=-1)

    y_ref[0] = _merge_and_project(x, h_bf, w_in_ref, yr, yc, ym, w_br_ret_ref,
                                  w_br_conv_ref, w_br_mem_ref, w_out_ref, final_w_ref[...])


def _const_spec(shape, single_buffer=False):
    zeros = (0,) * len(shape)
    if single_buffer:
        return pl.BlockSpec(shape, lambda b, t: zeros, pipeline_mode=pl.Buffered(1))
    return pl.BlockSpec(shape, lambda b, t: zeros)


def _prompt_layer(x, cos2, sin2, mk_bf, mv_bf, norm_w, w_in_bf, gn_w, conv_w, conv_b, ln_w,
                  ln_b, w_br_ret_bf, w_br_conv_bf, w_br_mem_bf, w_out_bf, final_w,
                  intra, qdec, kdec, c_dec):
    batch, seq, _ = x.shape
    n_mem = mk_bf.shape[1]
    tile = PROMPT_TILE
    row = lambda w: _const_spec((1, w))
    weight = lambda a: _const_spec(a.shape, single_buffer=True)
    table = _const_spec((RET_HEADS, RET_CHUNK, RET_CHUNK))
    return pl.pallas_call(
        functools.partial(_prompt_kernel, c_dec),
        grid=(batch, seq // tile),
        in_specs=[
            pl.BlockSpec((1, tile, D_MODEL), lambda b, t: (b, t, 0)),
            pl.BlockSpec((tile, HEAD_DIM), lambda b, t: (t, 0)),
            pl.BlockSpec((tile, HEAD_DIM), lambda b, t: (t, 0)),
            pl.BlockSpec((1, n_mem, BRANCH_W), lambda b, t: (b, 0, 0)),
            pl.BlockSpec((1, n_mem, BRANCH_W), lambda b, t: (b, 0, 0)),
            row(D_MODEL), weight(w_in_bf), row(BRANCH_W),
            _const_spec((CONV_K, BRANCH_W)), row(BRANCH_W), row(BRANCH_W), row(BRANCH_W),
            weight(w_br_ret_bf), weight(w_br_conv_bf), weight(w_br_mem_bf), weight(w_out_bf),
            row(D_MODEL), table, table, table,
        ],
        out_specs=[
            pl.BlockSpec((1, tile, D_MODEL), lambda b, t: (b, t, 0)),
            pl.BlockSpec((1, RET_HEADS, HEAD_DIM, HEAD_DIM), lambda b, t: (b, 0, 0, 0)),
            pl.BlockSpec((1, CONV_HIST, BRANCH_W), lambda b, t: (b, 0, 0)),
        ],
        out_shape=[
            jax.ShapeDtypeStruct((batch, seq, D_MODEL), F32),
            jax.ShapeDtypeStruct((batch, RET_HEADS, HEAD_DIM, HEAD_DIM), F32),
            jax.ShapeDtypeStruct((batch, CONV_HIST, BRANCH_W), F32),
        ],
        scratch_shapes=[
            pltpu.VMEM((HIST_PAD + tile, BRANCH_W), F32),
            pltpu.VMEM((tile, BRANCH_W), F32),
        ],
        compiler_params=pltpu.CompilerParams(
            dimension_semantics=("arbitrary", "arbitrary"),
            vmem_limit_bytes=VMEM_LIMIT_BYTES),
        name="prompt_layer",
    )(x, cos2, sin2, mk_bf, mv_bf, norm_w, w_in_bf, gn_w, conv_w, conv_b, ln_w, ln_b,
      w_br_ret_bf, w_br_conv_bf, w_br_mem_bf, w_out_bf, final_w, intra, qdec, kdec)


SAMPLE_PROJ_BLOCK = 1536


def _sample_proj_kernel(x_ref, norm_w_ref, w_ref, z_ref):
    x = x_ref[...]
    h_bf = (x * _rms_scale(x) * norm_w_ref[...]).astype(BF16)
    z_ref[...] = _dot(h_bf, w_ref[...])


def _sample_proj(xs, norm_w, w_in_bf):
    n = xs.shape[0]
    return pl.pallas_call(
        _sample_proj_kernel,
        grid=(IN_W // SAMPLE_PROJ_BLOCK,),
        in_specs=[
            pl.BlockSpec((n, D_MODEL), lambda j: (0, 0)),
            pl.BlockSpec((1, D_MODEL), lambda j: (0, 0)),
            pl.BlockSpec((D_MODEL, SAMPLE_PROJ_BLOCK), lambda j: (0, j)),
        ],
        out_specs=pl.BlockSpec((n, SAMPLE_PROJ_BLOCK), lambda j: (0, j)),
        out_shape=jax.ShapeDtypeStruct((n, IN_W), F32),
        compiler_params=pltpu.CompilerParams(dimension_semantics=("arbitrary",)),
        name="sample_proj",
    )(xs, norm_w, w_in_bf)


def _sample_state_kernel(gamma, z_ref, cos_ref, sin_ref, s0_ref, hist_ref, ck_ref, cv_ref,
                         conv_w_ref, o_ref, conv_ref, att_ref, s_new_ref, hist_new_ref):
    nb = z_ref.shape[0]
    cos2 = cos_ref[...]
    sin2 = sin_ref[...]
    q_scale = HEAD_DIM ** -0.5
    zq = z_ref[:, OFF_QR:OFF_QR + BRANCH_W]
    zk = z_ref[:, OFF_KR:OFF_KR + BRANCH_W]
    v = z_ref[:, OFF_VR:OFF_VR + BRANCH_W]
    q = jnp.concatenate([_rotary(_head(zq, h), cos2, sin2) * q_scale
                         for h in range(RET_HEADS)], axis=-1)
    k = jnp.concatenate([_rotary(_head(zk, h), cos2, sin2) for h in range(RET_HEADS)], axis=-1)
    u = z_ref[:, OFF_AC:OFF_AC + BRANCH_W] * _sigmoid(z_ref[:, OFF_BC:OFF_BC + BRANCH_W])
    qm = z_ref[:, OFF_QM:OFF_QM + BRANCH_W]

    eye = (lax.broadcasted_iota(jnp.int32, (HEAD_DIM, HEAD_DIM), 0)
           == lax.broadcasted_iota(jnp.int32, (HEAD_DIM, HEAD_DIM), 1))
    ones_bf = jnp.ones((HEAD_DIM, HEAD_DIM), BF16)

    def as_columns(row):
        diag = jnp.where(eye, jnp.broadcast_to(row, (HEAD_DIM, HEAD_DIM)), 0.0)
        return _dot(diag.astype(BF16), ones_bf)

    w_hist = conv_w_ref[0:CONV_HIST, :]
    w_last = conv_w_ref[CONV_HIST:CONV_K, :]
    mem_scale = HEAD_DIM ** -0.5

    for b in range(nb):
        for h in range(RET_HEADS):
            q_row = _head(q, h)[b:b + 1]
            k_row = _head(k, h)[b:b + 1]
            v_row = _head(v, h)[b:b + 1]
            s0 = s0_ref[b, h]
            qk = jnp.sum(q_row * k_row, axis=-1, keepdims=True)
            qs = jnp.sum(as_columns(q_row) * s0, axis=0, keepdims=True)
            o_ref[b:b + 1, h * HEAD_DIM:(h + 1) * HEAD_DIM] = qk * v_row + gamma[h] * qs
            s_new_ref[b, h] = gamma[h] * s0 + as_columns(k_row) * v_row

        hist = hist_ref[b]
        u_row = u[b:b + 1]
        conv_ref[b:b + 1, :] = (jnp.sum(hist * w_hist, axis=0, keepdims=True)
                                + u_row * w_last)
        hist_new_ref[b, 0:CONV_HIST - 1, :] = hist[1:CONV_HIST]
        hist_new_ref[b, CONV_HIST - 1:CONV_HIST, :] = u_row

        prod = ck_ref[b] * qm[b:b + 1]
        cv = cv_ref[b]
        for h in range(MEM_HEADS):
            sc = jnp.sum(_head(prod, h), axis=-1, keepdims=True) * mem_scale
            e = jnp.exp(sc - jnp.max(sc, axis=0, keepdims=True))
            p = e / jnp.sum(e, axis=0, keepdims=True)
            att_ref[b:b + 1, h * HEAD_DIM:(h + 1) * HEAD_DIM] = jnp.sum(
                p * _head(cv, h), axis=0, keepdims=True)


def _sample_state(z, cos2, sin2, s0, hist, cache_k, cache_v, conv_w, gamma):
    n = z.shape[0]
    n_mem = cache_k.shape[1]
    nb = SAMPLE_BLOCK
    rows = lambda w: pl.BlockSpec((nb, w), lambda i: (i, 0))
    state = pl.BlockSpec((nb, RET_HEADS, HEAD_DIM, HEAD_DIM), lambda i: (i, 0, 0, 0))
    hist_spec = pl.BlockSpec((nb, CONV_HIST, BRANCH_W), lambda i: (i, 0, 0))
    cache = pl.BlockSpec((nb, n_mem, BRANCH_W), lambda i: (i, 0, 0))
    return pl.pallas_call(
        functools.partial(_sample_state_kernel, gamma),
        grid=(n // nb,),
        in_specs=[
            rows(IN_W),
            pl.BlockSpec((1, HEAD_DIM), lambda i: (0, 0)),
            pl.BlockSpec((1, HEAD_DIM), lambda i: (0, 0)),
            state, hist_spec, cache, cache,
            pl.BlockSpec((CONV_K, BRANCH_W), lambda i: (0, 0)),
        ],
        out_specs=[rows(BRANCH_W), rows(BRANCH_W), rows(BRANCH_W), state, hist_spec],
        out_shape=[
            jax.ShapeDtypeStruct((n, BRANCH_W), F32),
            jax.ShapeDtypeStruct((n, BRANCH_W), F32),
            jax.ShapeDtypeStruct((n, BRANCH_W), F32),
            jax.ShapeDtypeStruct(s0.shape, F32),
            jax.ShapeDtypeStruct(hist.shape, F32),
        ],
        compiler_params=pltpu.CompilerParams(
            dimension_semantics=("arbitrary",), vmem_limit_bytes=VMEM_LIMIT_BYTES),
        name="sample_state",
    )(z, cos2, sin2, s0, hist, cache_k, cache_v, conv_w)


def _sample_out_kernel(x_ref, z_ref, o_ref, conv_ref, att_ref, norm_w_ref, w_in_ref, gn_w_ref,
                       conv_b_ref, ln_w_ref, ln_b_ref, w_br_ret_ref, w_br_conv_ref,
                       w_br_mem_ref, w_out_ref, final_w_ref, y_ref):
    x = x_ref[...]
    h_bf = (x * _rms_scale(x) * norm_w_ref[...]).astype(BF16)
    gate = lambda off: _silu(z_ref[:, off:off + BRANCH_W])
    yr = gate(OFF_GR) * _group_norm_heads(o_ref[...], gn_w_ref[...])
    cn = _layernorm_rows(conv_ref[...] + conv_b_ref[...]) * ln_w_ref[...] + ln_b_ref[...]
    yc = gate(OFF_GC) * _silu(cn)
    ym = gate(OFF_GM) * att_ref[...]
    y_ref[...] = _merge_and_project(x, h_bf, w_in_ref, yr, yc, ym, w_br_ret_ref,
                                    w_br_conv_ref, w_br_mem_ref, w_out_ref, final_w_ref[...])


def _sample_out(xs, z, o, conv, att, norm_w, w_in_bf, gn_w, conv_b, ln_w, ln_b,
                w_br_ret_bf, w_br_conv_bf, w_br_mem_bf, w_out_bf, final_w):
    return pl.pallas_call(
        _sample_out_kernel,
        out_shape=jax.ShapeDtypeStruct(xs.shape, F32),
        compiler_params=pltpu.CompilerParams(vmem_limit_bytes=VMEM_LIMIT_BYTES),
        name="sample_out",
    )(xs, z, o, conv, att, norm_w, w_in_bf, gn_w, conv_b, ln_w, ln_b,
      w_br_ret_bf, w_br_conv_bf, w_br_mem_bf, w_out_bf, final_w)


def _rotary_tables(pos):
    half = HEAD_DIM // 2
    inv = ROPE_BASE ** (-jnp.arange(half, dtype=F32) / half)
    ang = pos[:, None] * inv[None, :]
    cos, sin = jnp.cos(ang), jnp.sin(ang)
    return jnp.concatenate([cos, cos], axis=-1), jnp.concatenate([-sin, sin], axis=-1)


def _decay_tables():
    c = RET_CHUNK
    log_g = jnp.log1p(-jnp.exp2(-5.0 - jnp.arange(RET_HEADS, dtype=F32)))
    idx = jnp.arange(c, dtype=F32)
    diff = idx[:, None] - idx[None, :]
    intra = jnp.where(diff >= 0,
                      jnp.exp(jnp.maximum(diff, 0.0)[None] * log_g[:, None, None]), 0.0)
    q_dec = jnp.exp((idx + 1.0)[None, :] * log_g[:, None])
    k_dec = jnp.exp((c - 1.0 - idx)[None, :] * log_g[:, None])
    bcast = lambda a: jnp.broadcast_to(a[:, :, None], (RET_HEADS, c, HEAD_DIM))
    return intra, bcast(q_dec), bcast(k_dec)


def _gamma_powers(n):
    return tuple(float(np.exp(np.log1p(-np.exp2(-5.0 - h)) * n)) for h in range(RET_HEADS))


def kernel(x_prompt, x_sample, mem_prompt, state_ret, state_conv, cache_mem_k, cache_mem_v,
           norm_w, w_in, ret_gn_w, conv_w, conv_b, conv_ln_w, conv_ln_b, mem_norm_w,
           w_mem_kv, w_br_ret, w_br_conv, w_br_mem, w_out, final_norm_w):
    depth = w_in.shape[0]
    assert depth == 1, "single-layer step"
    batch, seq, _ = x_prompt.shape
    n_dec, dec_seq, _ = x_sample.shape
    assert dec_seq == 1 and seq % PROMPT_TILE == 0 and n_dec % SAMPLE_BLOCK == 0
    n_mem = mem_prompt.shape[1]

    bf = lambda w: w[0].astype(BF16)
    w_in_bf, w_kv_bf = bf(w_in), bf(w_mem_kv)
    w_br_ret_bf, w_br_conv_bf, w_br_mem_bf, w_out_bf = bf(w_br_ret), bf(w_br_conv), bf(w_br_mem), bf(w_out)
    final_w = final_norm_w[None, :]

    intra, qdec, kdec = _decay_tables()
    cos_p, sin_p = _rotary_tables(jnp.arange(seq, dtype=F32))
    cos_s, sin_s = _rotary_tables(jnp.arange(dec_seq, dtype=F32) + PAST_LEN)

    mk, mv, mk_bf, mv_bf = _mem_kv(mem_prompt, mem_norm_w, w_kv_bf)

    y_prompt, s_prompt, hist_prompt = _prompt_layer(
        x_prompt, cos_p, sin_p, mk_bf, mv_bf, norm_w, w_in_bf, ret_gn_w, conv_w[0], conv_b,
        conv_ln_w, conv_ln_b, w_br_ret_bf, w_br_conv_bf, w_br_mem_bf, w_out_bf, final_w,
        intra, qdec, kdec, _gamma_powers(RET_CHUNK))

    xs = x_sample[:, 0, :]
    z_s = _sample_proj(xs, norm_w, w_in_bf)
    o_s, conv_s, att_s, s_sample, hist_sample = _sample_state(
        z_s, cos_s, sin_s, state_ret[0], state_conv[0],
        cache_mem_k[0].reshape(n_dec, n_mem, BRANCH_W),
        cache_mem_v[0].reshape(n_dec, n_mem, BRANCH_W), conv_w[0], _gamma_powers(1))
    y_sample = _sample_out(xs, z_s, o_s, conv_s, att_s, norm_w, w_in_bf, ret_gn_w, conv_b,
                           conv_ln_w, conv_ln_b, w_br_ret_bf, w_br_conv_bf, w_br_mem_bf,
                           w_out_bf, final_w)

    heads = lambda a: a.reshape(1, batch, n_mem, MEM_HEADS, HEAD_DIM)
    return (y_prompt, y_sample[:, None, :], s_prompt[None], s_sample[None],
            hist_prompt[None], hist_sample[None], heads(mk), heads(mv))
```

```python
import functools

import numpy as np
import jax
import jax.numpy as jnp
from jax import lax
from jax.experimental import pallas as pl
from jax.experimental.pallas import tpu as pltpu

F32 = jnp.float32
BF16 = jnp.bfloat16

D_MODEL = 1024
RET_HEADS = 4
HEAD_DIM = 128
BRANCH_W = 512
RET_CHUNK = 128
ROPE_BASE = 10000.0
CONV_K = 31
CONV_HIST = CONV_K - 1
MEM_HEADS = 4
N_BRANCH = 3
EPS = 1e-6
PAST_LEN = 16384

OFF_QR, OFF_KR, OFF_VR, OFF_GR = 0, 512, 1024, 1536
OFF_AC, OFF_BC, OFF_GC = 2048, 2560, 3072
OFF_QM, OFF_GM = 3584, 4096
OFF_MERGE = 4608
IN_W = OFF_MERGE + N_BRANCH * D_MODEL

V7X_VMEM_BYTES = 64 * 1024 * 1024
VMEM_LIMIT_BYTES = V7X_VMEM_BYTES - 8 * 1024 * 1024

PROMPT_TILE = 256
HIST_PAD = 32
SAMPLE_BLOCK = 8


def _sigmoid(x):
    return 0.5 * jnp.tanh(0.5 * x) + 0.5


def _silu(x):
    hx = 0.5 * x
    return hx * (jnp.tanh(hx) + 1.0)


def _rms_scale(x):
    return lax.rsqrt(jnp.mean(x * x, axis=-1, keepdims=True) + EPS)


def _layernorm_rows(x):
    mu = jnp.mean(x, axis=-1, keepdims=True)
    xc = x - mu
    var = jnp.mean(xc * xc, axis=-1, keepdims=True)
    return xc * lax.rsqrt(var + EPS)


def _rotary(x, cos2, sin2):
    return x * cos2 + pltpu.roll(x, HEAD_DIM // 2, axis=1) * sin2


def _dot(a, b):
    return jnp.dot(a, b, preferred_element_type=F32)


def _dot_nt(a, b):
    return lax.dot_general(a, b, (((1,), (1,)), ((), ())), preferred_element_type=F32)


def _dot_tn(a, b):
    return lax.dot_general(a, b, (((0,), (0,)), ((), ())), preferred_element_type=F32)


def _head(x, h):
    return x[:, h * HEAD_DIM:(h + 1) * HEAD_DIM]


def _group_norm_heads(o, gn_w):
    parts = [_layernorm_rows(_head(o, h)) for h in range(RET_HEADS)]
    return jnp.concatenate(parts, axis=-1) * gn_w


def _merge_and_project(x, h_bf, w_in_ref, yr, yc, ym, w_br_ret_ref, w_br_conv_ref,
                       w_br_mem_ref, w_out_ref, final_w):
    branches = (
        _dot(yr.astype(BF16), w_br_ret_ref[...]),
        _dot(yc.astype(BF16), w_br_conv_ref[...]),
        _dot(ym.astype(BF16), w_br_mem_ref[...]),
    )
    merged = None
    for i, br in enumerate(branches):
        lo = OFF_MERGE + i * D_MODEL
        gate = _sigmoid(_dot(h_bf, w_in_ref[:, lo:lo + D_MODEL]))
        term = gate * br
        merged = term if merged is None else merged + term
    out = x + _dot(merged.astype(BF16), w_out_ref[...])
    return out * _rms_scale(out) * final_w


def _mem_kv_kernel(mem_ref, norm_w_ref, w_ref, k_ref, v_ref, kbf_ref, vbf_ref):
    m = mem_ref[0]
    hm = (m * _rms_scale(m) * norm_w_ref[...]).astype(BF16)
    kv = _dot(hm, w_ref[...])
    k = kv[:, :BRANCH_W]
    v = kv[:, BRANCH_W:]
    n_mem = k.shape[0]
    for h in range(MEM_HEADS):
        k_ref[0, pl.ds(h, n_mem, stride=MEM_HEADS), :] = _head(k, h)
        v_ref[0, pl.ds(h, n_mem, stride=MEM_HEADS), :] = _head(v, h)
    kbf_ref[0] = k.astype(BF16)
    vbf_ref[0] = v.astype(BF16)


def _mem_kv(mem, mem_norm_w, w_mem_kv_bf):
    batch, n_mem, _ = mem.shape
    blk = lambda dt: jax.ShapeDtypeStruct((batch, n_mem, BRANCH_W), dt)
    out_spec = pl.BlockSpec((1, n_mem, BRANCH_W), lambda b: (b, 0, 0))
    rows_shape = (batch, n_mem * MEM_HEADS, HEAD_DIM)
    rows_spec = pl.BlockSpec((1, n_mem * MEM_HEADS, HEAD_DIM), lambda b: (b, 0, 0))
    return pl.pallas_call(
        _mem_kv_kernel,
        grid=(batch,),
        in_specs=[
            pl.BlockSpec((1, n_mem, D_MODEL), lambda b: (b, 0, 0)),
            pl.BlockSpec((1, D_MODEL), lambda b: (0, 0)),
            pl.BlockSpec((D_MODEL, 2 * BRANCH_W), lambda b: (0, 0)),
        ],
        out_specs=[rows_spec, rows_spec, out_spec, out_spec],
        out_shape=[jax.ShapeDtypeStruct(rows_shape, F32), jax.ShapeDtypeStruct(rows_shape, F32),
                   blk(BF16), blk(BF16)],
        compiler_params=pltpu.CompilerParams(dimension_semantics=("arbitrary",)),
        name="mem_kv",
    )(mem, mem_norm_w, w_mem_kv_bf)


def _prompt_kernel(c_dec, x_ref, cos_ref, sin_ref, mk_ref, mv_ref, norm_w_ref, w_in_ref,
                   gn_w_ref, conv_w_ref, conv_b_ref, ln_w_ref, ln_b_ref,
                   w_br_ret_ref, w_br_conv_ref, w_br_mem_ref, w_out_ref, final_w_ref,
                   intra_ref, qdec_ref, kdec_ref,
                   y_ref, s_ref, hist_ref, ubuf_ref):
    t = pl.program_id(1)
    tile = x_ref.shape[1]

    @pl.when(t == 0)
    def _():
        s_ref[...] = jnp.zeros_like(s_ref)
        ubuf_ref[0:HIST_PAD, :] = jnp.zeros((HIST_PAD, BRANCH_W), F32)

    x = x_ref[0]
    h_bf = (x * _rms_scale(x) * norm_w_ref[...]).astype(BF16)

    def proj(off, width=BRANCH_W):
        return _dot(h_bf, w_in_ref[:, off:off + width])

    u = proj(OFF_AC) * _sigmoid(proj(OFF_BC))
    ubuf_ref[HIST_PAD:HIST_PAD + tile, :] = u
    first = HIST_PAD - CONV_HIST
    conv = None
    for r in range(8):
        part = None
        for j in range(CONV_K):
            if (first + j) % 8 != r:
                continue
            term = ubuf_ref[first + j:first + j + tile, :] * conv_w_ref[j:j + 1, :]
            part = term if part is None else part + term
        conv = part if conv is None else conv + part
    conv = conv + conv_b_ref[...]
    ubuf_ref[0:HIST_PAD, :] = ubuf_ref[tile:tile + HIST_PAD, :]
    cn = _layernorm_rows(conv) * ln_w_ref[...] + ln_b_ref[...]
    yc = _silu(proj(OFF_GC)) * _silu(cn)

    @pl.when(t == pl.num_programs(1) - 1)
    def _():
        hist_ref[0] = ubuf_ref[first:HIST_PAD, :]

    cos2 = cos_ref[...]
    sin2 = sin_ref[...]
    zq = proj(OFF_QR)
    zk = proj(OFF_KR)
    v_bf = proj(OFF_VR).astype(BF16)
    q_scale = HEAD_DIM ** -0.5
    o_heads = []
    for h in range(RET_HEADS):
        q_h = _rotary(_head(zq, h), cos2, sin2) * q_scale
        k_h = _rotary(_head(zk, h), cos2, sin2)
        v_h = _head(v_bf, h)
        s = s_ref[0, h]
        o_chunks = []
        for c in range(tile // RET_CHUNK):
            rows = slice(c * RET_CHUNK, (c + 1) * RET_CHUNK)
            qc = q_h[rows].astype(BF16)
            kc = k_h[rows]
            vc = v_h[rows]
            sc = _dot_nt(qc, kc.astype(BF16)) * intra_ref[h]
            o_chunks.append(_dot(sc.astype(BF16), vc) + _dot(qc, s.astype(BF16)) * qdec_ref[h])
            kd = (kc * kdec_ref[h]).astype(BF16)
            s = s * c_dec[h] + _dot_tn(kd, vc)
        s_ref[0, h] = s
        o_heads.append(_layernorm_rows(jnp.concatenate(o_chunks, axis=0)))
    ret = jnp.concatenate(o_heads, axis=-1) * gn_w_ref[...]
    yr = _silu(proj(OFF_GR)) * ret

    qm = proj(OFF_QM).astype(BF16)
    mk = mk_ref[0]
    mv = mv_ref[0]
    att = []
    for h in range(MEM_HEADS):
        sc = _dot_nt(_head(qm, h), _head(mk, h)) * (HEAD_DIM ** -0.5)
        e = jnp.exp(sc - jnp.max(sc, axis=-1, keepdims=True))
        p = e / jnp.sum(e, axis=-1, keepdims=True)
        att.append(_dot(p.astype(BF16), _head(mv, h)))
    ym = _silu(proj(OFF_GM)) * jnp.concatenate(att, axis=-1)

    y_ref[0] = _merge_and_project(x, h_bf, w_in_ref, yr, yc, ym, w_br_ret_ref,
                                  w_br_conv_ref, w_br_mem_ref, w_out_ref, final_w_ref[...])


def _const_spec(shape, single_buffer=False):
    zeros = (0,) * len(shape)
    if single_buffer:
        return pl.BlockSpec(shape, lambda b, t: zeros, pipeline_mode=pl.Buffered(1))
    return pl.BlockSpec(shape, lambda b, t: zeros)


def _prompt_layer(x, cos2, sin2, mk_bf, mv_bf, norm_w, w_in_bf, gn_w, conv_w, conv_b, ln_w,
                  ln_b, w_br_ret_bf, w_br_conv_bf, w_br_mem_bf, w_out_bf, final_w,
                  intra, qdec, kdec, c_dec):
    batch, seq, _ = x.shape
    n_mem = mk_bf.shape[1]
    tile = PROMPT_TILE
    row = lambda w: _const_spec((1, w))
    weight = lambda a: _const_spec(a.shape, single_buffer=True)
    table = _const_spec((RET_HEADS, RET_CHUNK, RET_CHUNK))
    return pl.pallas_call(
        functools.partial(_prompt_kernel, c_dec),
        grid=(batch, seq // tile),
        in_specs=[
            pl.BlockSpec((1, tile, D_MODEL), lambda b, t: (b, t, 0)),
            pl.BlockSpec((tile, HEAD_DIM), lambda b, t: (t, 0)),
            pl.BlockSpec((tile, HEAD_DIM), lambda b, t: (t, 0)),
            pl.BlockSpec((1, n_mem, BRANCH_W), lambda b, t: (b, 0, 0)),
            pl.BlockSpec((1, n_mem, BRANCH_W), lambda b, t: (b, 0, 0)),
            row(D_MODEL), weight(w_in_bf), row(BRANCH_W),
            _const_spec((CONV_K, BRANCH_W)), row(BRANCH_W), row(BRANCH_W), row(BRANCH_W),
            weight(w_br_ret_bf), weight(w_br_conv_bf), weight(w_br_mem_bf), weight(w_out_bf),
            row(D_MODEL), table, table, table,
        ],
        out_specs=[
            pl.BlockSpec((1, tile, D_MODEL), lambda b, t: (b, t, 0)),
            pl.BlockSpec((1, RET_HEADS, HEAD_DIM, HEAD_DIM), lambda b, t: (b, 0, 0, 0)),
            pl.BlockSpec((1, CONV_HIST, BRANCH_W), lambda b, t: (b, 0, 0)),
        ],
        out_shape=[
            jax.ShapeDtypeStruct((batch, seq, D_MODEL), F32),
            jax.ShapeDtypeStruct((batch, RET_HEADS, HEAD_DIM, HEAD_DIM), F32),
            jax.ShapeDtypeStruct((batch, CONV_HIST, BRANCH_W), F32),
        ],
        scratch_shapes=[pltpu.VMEM((HIST_PAD + tile, BRANCH_W), F32)],
        compiler_params=pltpu.CompilerParams(
            dimension_semantics=("arbitrary", "arbitrary"),
            vmem_limit_bytes=VMEM_LIMIT_BYTES),
        name="prompt_layer",
    )(x, cos2, sin2, mk_bf, mv_bf, norm_w, w_in_bf, gn_w, conv_w, conv_b, ln_w, ln_b,
      w_br_ret_bf, w_br_conv_bf, w_br_mem_bf, w_out_bf, final_w, intra, qdec, kdec)


SAMPLE_PROJ_BLOCK = 1536


def _sample_proj_kernel(x_ref, norm_w_ref, w_ref, z_ref):
    x = x_ref[...]
    h_bf = (x * _rms_scale(x) * norm_w_ref[...]).astype(BF16)
    z_ref[...] = _dot(h_bf, w_ref[...])


def _sample_proj(xs, norm_w, w_in_bf):
    n = xs.shape[0]
    return pl.pallas_call(
        _sample_proj_kernel,
        grid=(IN_W // SAMPLE_PROJ_BLOCK,),
        in_specs=[
            pl.BlockSpec((n, D_MODEL), lambda j: (0, 0)),
            pl.BlockSpec((1, D_MODEL), lambda j: (0, 0)),
            pl.BlockSpec((D_MODEL, SAMPLE_PROJ_BLOCK), lambda j: (0, j)),
        ],
        out_specs=pl.BlockSpec((n, SAMPLE_PROJ_BLOCK), lambda j: (0, j)),
        out_shape=jax.ShapeDtypeStruct((n, IN_W), F32),
        compiler_params=pltpu.CompilerParams(dimension_semantics=("arbitrary",)),
        name="sample_proj",
    )(xs, norm_w, w_in_bf)


def _sample_state_kernel(gamma, z_ref, cos_ref, sin_ref, s0_ref, hist_ref, ck_ref, cv_ref,
                         conv_w_ref, o_ref, conv_ref, att_ref, s_new_ref, hist_new_ref):
    nb = z_ref.shape[0]
    n_mem = ck_ref.shape[1] // MEM_HEADS
    cos2 = cos_ref[...]
    sin2 = sin_ref[...]
    q_scale = HEAD_DIM ** -0.5
    zq = z_ref[:, OFF_QR:OFF_QR + BRANCH_W]
    zk = z_ref[:, OFF_KR:OFF_KR + BRANCH_W]
    v = z_ref[:, OFF_VR:OFF_VR + BRANCH_W]
    q = jnp.concatenate([_rotary(_head(zq, h), cos2, sin2) * q_scale
                         for h in range(RET_HEADS)], axis=-1)
    k = jnp.concatenate([_rotary(_head(zk, h), cos2, sin2) for h in range(RET_HEADS)], axis=-1)
    u = z_ref[:, OFF_AC:OFF_AC + BRANCH_W] * _sigmoid(z_ref[:, OFF_BC:OFF_BC + BRANCH_W])
    qm = z_ref[:, OFF_QM:OFF_QM + BRANCH_W]

    eye = (lax.broadcasted_iota(jnp.int32, (HEAD_DIM, HEAD_DIM), 0)
           == lax.broadcasted_iota(jnp.int32, (HEAD_DIM, HEAD_DIM), 1))
    ones_bf = jnp.ones((HEAD_DIM, HEAD_DIM), BF16)

    def as_columns(row):
        diag = jnp.where(eye, jnp.broadcast_to(row, (HEAD_DIM, HEAD_DIM)), 0.0)
        return _dot(diag.astype(BF16), ones_bf)

    w_hist = conv_w_ref[0:CONV_HIST, :]
    w_last = conv_w_ref[CONV_HIST:CONV_K, :]
    mem_scale = HEAD_DIM ** -0.5

    for b in range(nb):
        for h in range(RET_HEADS):
            q_row = _head(q, h)[b:b + 1]
            k_row = _head(k, h)[b:b + 1]
            v_row = _head(v, h)[b:b + 1]
            s0 = s0_ref[b, h]
            qk = jnp.sum(q_row * k_row, axis=-1, keepdims=True)
            qs = jnp.sum(as_columns(q_row) * s0, axis=0, keepdims=True)
            o_ref[b:b + 1, h * HEAD_DIM:(h + 1) * HEAD_DIM] = qk * v_row + gamma[h] * qs
            s_new_ref[b, h] = gamma[h] * s0 + as_columns(k_row) * v_row

        hist = hist_ref[b]
        u_row = u[b:b + 1]
        conv_ref[b:b + 1, :] = (jnp.sum(hist * w_hist, axis=0, keepdims=True)
                                + u_row * w_last)
        hist_new_ref[b, 0:CONV_HIST - 1, :] = hist[1:CONV_HIST]
        hist_new_ref[b, CONV_HIST - 1:CONV_HIST, :] = u_row

        for h in range(MEM_HEADS):
            head_rows = pl.ds(h, n_mem, stride=MEM_HEADS)
            prod = ck_ref[b, head_rows, :] * _head(qm, h)[b:b + 1]
            sc = jnp.sum(prod, axis=-1, keepdims=True) * mem_scale
            e = jnp.exp(sc - jnp.max(sc, axis=0, keepdims=True))
            p = e / jnp.sum(e, axis=0, keepdims=True)
            att_ref[b:b + 1, h * HEAD_DIM:(h + 1) * HEAD_DIM] = jnp.sum(
                p * cv_ref[b, head_rows, :], axis=0, keepdims=True)


def _sample_state(z, cos2, sin2, s0, hist, cache_k, cache_v, conv_w, gamma):
    n = z.shape[0]
    nb = SAMPLE_BLOCK
    rows = lambda w: pl.BlockSpec((nb, w), lambda i: (i, 0))
    state = pl.BlockSpec((nb, RET_HEADS, HEAD_DIM, HEAD_DIM), lambda i: (i, 0, 0, 0))
    hist_spec = pl.BlockSpec((nb, CONV_HIST, BRANCH_W), lambda i: (i, 0, 0))
    cache = pl.BlockSpec((nb,) + cache_k.shape[1:], lambda i: (i, 0, 0))
    return pl.pallas_call(
        functools.partial(_sample_state_kernel, gamma),
        grid=(n // nb,),
        in_specs=[
            rows(IN_W),
            pl.BlockSpec((1, HEAD_DIM), lambda i: (0, 0)),
            pl.BlockSpec((1, HEAD_DIM), lambda i: (0, 0)),
            state, hist_spec, cache, cache,
            pl.BlockSpec((CONV_K, BRANCH_W), lambda i: (0, 0)),
        ],
        out_specs=[rows(BRANCH_W), rows(BRANCH_W), rows(BRANCH_W), state, hist_spec],
        out_shape=[
            jax.ShapeDtypeStruct((n, BRANCH_W), F32),
            jax.ShapeDtypeStruct((n, BRANCH_W), F32),
            jax.ShapeDtypeStruct((n, BRANCH_W), F32),
            jax.ShapeDtypeStruct(s0.shape, F32),
            jax.ShapeDtypeStruct(hist.shape, F32),
        ],
        compiler_params=pltpu.CompilerParams(
            dimension_semantics=("arbitrary",), vmem_limit_bytes=VMEM_LIMIT_BYTES),
        name="sample_state",
    )(z, cos2, sin2, s0, hist, cache_k, cache_v, conv_w)


def _sample_out_kernel(x_ref, z_ref, o_ref, conv_ref, att_ref, norm_w_ref, w_in_ref, gn_w_ref,
                       conv_b_ref, ln_w_ref, ln_b_ref, w_br_ret_ref, w_br_conv_ref,
                       w_br_mem_ref, w_out_ref, final_w_ref, y_ref):
    x = x_ref[...]
    h_bf = (x * _rms_scale(x) * norm_w_ref[...]).astype(BF16)
    gate = lambda off: _silu(z_ref[:, off:off + BRANCH_W])
    yr = gate(OFF_GR) * _group_norm_heads(o_ref[...], gn_w_ref[...])
    cn = _layernorm_rows(conv_ref[...] + conv_b_ref[...]) * ln_w_ref[...] + ln_b_ref[...]
    yc = gate(OFF_GC) * _silu(cn)
    ym = gate(OFF_GM) * att_ref[...]
    y_ref[...] = _merge_and_project(x, h_bf, w_in_ref, yr, yc, ym, w_br_ret_ref,
                                    w_br_conv_ref, w_br_mem_ref, w_out_ref, final_w_ref[...])


def _sample_out(xs, z, o, conv, att, norm_w, w_in_bf, gn_w, conv_b, ln_w, ln_b,
                w_br_ret_bf, w_br_conv_bf, w_br_mem_bf, w_out_bf, final_w):
    return pl.pallas_call(
        _sample_out_kernel,
        out_shape=jax.ShapeDtypeStruct(xs.shape, F32),
        compiler_params=pltpu.CompilerParams(vmem_limit_bytes=VMEM_LIMIT_BYTES),
        name="sample_out",
    )(xs, z, o, conv, att, norm_w, w_in_bf, gn_w, conv_b, ln_w, ln_b,
      w_br_ret_bf, w_br_conv_bf, w_br_mem_bf, w_out_bf, final_w)


def _rotary_tables(pos):
    half = HEAD_DIM // 2
    inv = ROPE_BASE ** (-jnp.arange(half, dtype=F32) / half)
    ang = pos[:, None] * inv[None, :]
    cos, sin = jnp.cos(ang), jnp.sin(ang)
    return jnp.concatenate([cos, cos], axis=-1), jnp.concatenate([-sin, sin], axis=-1)


def _decay_tables():
    c = RET_CHUNK
    log_g = jnp.log1p(-jnp.exp2(-5.0 - jnp.arange(RET_HEADS, dtype=F32)))
    idx = jnp.arange(c, dtype=F32)
    diff = idx[:, None] - idx[None, :]
    intra = jnp.where(diff >= 0,
                      jnp.exp(jnp.maximum(diff, 0.0)[None] * log_g[:, None, None]), 0.0)
    q_dec = jnp.exp((idx + 1.0)[None, :] * log_g[:, None])
    k_dec = jnp.exp((c - 1.0 - idx)[None, :] * log_g[:, None])
    bcast = lambda a: jnp.broadcast_to(a[:, :, None], (RET_HEADS, c, HEAD_DIM))
    return intra, bcast(q_dec), bcast(k_dec)


def _gamma_powers(n):
    return tuple(float(np.exp(np.log1p(-np.exp2(-5.0 - h)) * n)) for h in range(RET_HEADS))


def kernel(x_prompt, x_sample, mem_prompt, state_ret, state_conv, cache_mem_k, cache_mem_v,
           norm_w, w_in, ret_gn_w, conv_w, conv_b, conv_ln_w, conv_ln_b, mem_norm_w,
           w_mem_kv, w_br_ret, w_br_conv, w_br_mem, w_out, final_norm_w):
    depth = w_in.shape[0]
    assert depth == 1, "single-layer step"
    batch, seq, _ = x_prompt.shape
    n_dec, dec_seq, _ = x_sample.shape
    assert dec_seq == 1 and seq % PROMPT_TILE == 0 and n_dec % SAMPLE_BLOCK == 0
    n_mem = mem_prompt.shape[1]

    bf = lambda w: w[0].astype(BF16)
    w_in_bf, w_kv_bf = bf(w_in), bf(w_mem_kv)
    w_br_ret_bf, w_br_conv_bf, w_br_mem_bf, w_out_bf = bf(w_br_ret), bf(w_br_conv), bf(w_br_mem), bf(w_out)
    final_w = final_norm_w[None, :]

    intra, qdec, kdec = _decay_tables()
    cos_p, sin_p = _rotary_tables(jnp.arange(seq, dtype=F32))
    cos_s, sin_s = _rotary_tables(jnp.arange(dec_seq, dtype=F32) + PAST_LEN)

    mk, mv, mk_bf, mv_bf = _mem_kv(mem_prompt, mem_norm_w, w_kv_bf)

    y_prompt, s_prompt, hist_prompt = _prompt_layer(
        x_prompt, cos_p, sin_p, mk_bf, mv_bf, norm_w, w_in_bf, ret_gn_w, conv_w[0], conv_b,
        conv_ln_w, conv_ln_b, w_br_ret_bf, w_br_conv_bf, w_br_mem_bf, w_out_bf, final_w,
        intra, qdec, kdec, _gamma_powers(RET_CHUNK))

    xs = x_sample[:, 0, :]
    z_s = _sample_proj(xs, norm_w, w_in_bf)
    o_s, conv_s, att_s, s_sample, hist_sample = _sample_state(
        z_s, cos_s, sin_s, state_ret[0], state_conv[0],
        cache_mem_k.reshape(n_dec, n_mem * MEM_HEADS, HEAD_DIM),
        cache_mem_v.reshape(n_dec, n_mem * MEM_HEADS, HEAD_DIM), conv_w[0], _gamma_powers(1))
    y_sample = _sample_out(xs, z_s, o_s, conv_s, att_s, norm_w, w_in_bf, ret_gn_w, conv_b,
                           conv_ln_w, conv_ln_b, w_br_ret_bf, w_br_conv_bf, w_br_mem_bf,
                           w_out_bf, final_w)

    heads = lambda a: a.reshape(1, batch, n_mem, MEM_HEADS, HEAD_DIM)
    return (y_prompt, y_sample[:, None, :], s_prompt[None], s_sample[None],
            hist_prompt[None], hist_sample[None], heads(mk), heads(mv))
```

```python
import functools

import numpy as np
import jax
import jax.numpy as jnp
from jax import lax
from jax.experimental import pallas as pl
from jax.experimental.pallas import tpu as pltpu

F32 = jnp.float32
BF16 = jnp.bfloat16

D_MODEL = 1024
RET_HEADS = 4
HEAD_DIM = 128
BRANCH_W = 512
RET_CHUNK = 128
ROPE_BASE = 10000.0
CONV_K = 31
CONV_HIST = CONV_K - 1
MEM_HEADS = 4
N_BRANCH = 3
EPS = 1e-6
PAST_LEN = 16384

OFF_QR, OFF_KR, OFF_VR, OFF_GR = 0, 512, 1024, 1536
OFF_AC, OFF_BC, OFF_GC = 2048, 2560, 3072
OFF_QM, OFF_GM = 3584, 4096
OFF_MERGE = 4608
IN_W = OFF_MERGE + N_BRANCH * D_MODEL

V7X_VMEM_BYTES = 64 * 1024 * 1024
VMEM_LIMIT_BYTES = V7X_VMEM_BYTES - 8 * 1024 * 1024

PROMPT_TILE = 256
PROMPT_SCHEDULER_FLAGS = None
HIST_PAD = 32
SAMPLE_BLOCK = 8


def _sigmoid(x):
    return 0.5 * jnp.tanh(0.5 * x) + 0.5


def _silu(x):
    hx = 0.5 * x
    return hx * (jnp.tanh(hx) + 1.0)


def _rms_scale(x):
    return lax.rsqrt(jnp.mean(x * x, axis=-1, keepdims=True) + EPS)


def _layernorm_rows(x):
    mu = jnp.mean(x, axis=-1, keepdims=True)
    xc = x - mu
    var = jnp.mean(xc * xc, axis=-1, keepdims=True)
    return xc * lax.rsqrt(var + EPS)


def _rotary(x, cos2, sin2):
    return x * cos2 + pltpu.roll(x, HEAD_DIM // 2, axis=1) * sin2


def _dot(a, b):
    return jnp.dot(a, b, preferred_element_type=F32)


def _dot_nt(a, b):
    return lax.dot_general(a, b, (((1,), (1,)), ((), ())), preferred_element_type=F32)


def _dot_tn(a, b):
    return lax.dot_general(a, b, (((0,), (0,)), ((), ())), preferred_element_type=F32)


def _head(x, h):
    return x[:, h * HEAD_DIM:(h + 1) * HEAD_DIM]


def _group_norm_heads(o, gn_w):
    parts = [_layernorm_rows(_head(o, h)) for h in range(RET_HEADS)]
    return jnp.concatenate(parts, axis=-1) * gn_w


def _merge_and_project(x, h_bf, w_in_ref, yr, yc, ym, w_br_ret_ref, w_br_conv_ref,
                       w_br_mem_ref, w_out_ref, final_w):
    branches = (
        _dot(yr.astype(BF16), w_br_ret_ref[...]),
        _dot(yc.astype(BF16), w_br_conv_ref[...]),
        _dot(ym.astype(BF16), w_br_mem_ref[...]),
    )
    merged = None
    for i, br in enumerate(branches):
        lo = OFF_MERGE + i * D_MODEL
        gate = _sigmoid(_dot(h_bf, w_in_ref[:, lo:lo + D_MODEL]))
        term = gate * br
        merged = term if merged is None else merged + term
    out = x + _dot(merged.astype(BF16), w_out_ref[...])
    return out * _rms_scale(out) * final_w


def _mem_kv_kernel(mem_ref, norm_w_ref, w_ref, k_ref, v_ref, kbf_ref, vbf_ref):
    m = mem_ref[0]
    hm = (m * _rms_scale(m) * norm_w_ref[...]).astype(BF16)
    kv = _dot(hm, w_ref[...])
    k = kv[:, :BRANCH_W]
    v = kv[:, BRANCH_W:]
    n_mem = k.shape[0]
    for h in range(MEM_HEADS):
        k_ref[0, pl.ds(h, n_mem, stride=MEM_HEADS), :] = _head(k, h)
        v_ref[0, pl.ds(h, n_mem, stride=MEM_HEADS), :] = _head(v, h)
    kbf_ref[0] = k.astype(BF16)
    vbf_ref[0] = v.astype(BF16)


def _mem_kv(mem, mem_norm_w, w_mem_kv_bf):
    batch, n_mem, _ = mem.shape
    blk = lambda dt: jax.ShapeDtypeStruct((batch, n_mem, BRANCH_W), dt)
    out_spec = pl.BlockSpec((1, n_mem, BRANCH_W), lambda b: (b, 0, 0))
    rows_shape = (batch, n_mem * MEM_HEADS, HEAD_DIM)
    rows_spec = pl.BlockSpec((1, n_mem * MEM_HEADS, HEAD_DIM), lambda b: (b, 0, 0))
    return pl.pallas_call(
        _mem_kv_kernel,
        grid=(batch,),
        in_specs=[
            pl.BlockSpec((1, n_mem, D_MODEL), lambda b: (b, 0, 0)),
            pl.BlockSpec((1, D_MODEL), lambda b: (0, 0)),
            pl.BlockSpec((D_MODEL, 2 * BRANCH_W), lambda b: (0, 0)),
        ],
        out_specs=[rows_spec, rows_spec, out_spec, out_spec],
        out_shape=[jax.ShapeDtypeStruct(rows_shape, F32), jax.ShapeDtypeStruct(rows_shape, F32),
                   blk(BF16), blk(BF16)],
        compiler_params=pltpu.CompilerParams(dimension_semantics=("arbitrary",)),
        name="mem_kv",
    )(mem, mem_norm_w, w_mem_kv_bf)


def _prompt_kernel(c_dec, tiles_per_seq, n_tiles,
                   x_ref, xb_ref, cos_ref, sin_ref, mk_ref, mv_ref, norm_w_ref, w_in_ref,
                   gn_w_ref, conv_w_ref, conv_b_ref, ln_w_ref, ln_b_ref,
                   w_br_ret_ref, w_br_conv_ref, w_br_mem_ref, w_out_ref, final_w_ref,
                   intra_ref, qdec_ref, kdec_ref,
                   y_ref, s_ref, hist_ref,
                   ubuf_ref, h_scr, yr_scr, yc_scr, ym_scr):
    i = pl.program_id(0)
    t = lax.rem(jnp.minimum(i, n_tiles - 1), tiles_per_seq)
    live = i < n_tiles
    tile = x_ref.shape[1]

    @pl.when(i == 0)
    def _():
        h_scr[...] = jnp.zeros_like(h_scr)
        yr_scr[...] = jnp.zeros_like(yr_scr)
        yc_scr[...] = jnp.zeros_like(yc_scr)
        ym_scr[...] = jnp.zeros_like(ym_scr)

    @pl.when(jnp.logical_and(t == 0, live))
    def _():
        s_ref[...] = jnp.zeros_like(s_ref)
        ubuf_ref[0:HIST_PAD, :] = jnp.zeros((HIST_PAD, BRANCH_W), F32)

    h_prev = h_scr[...]
    half = D_MODEL // 2

    def gate_job(k):
        lo = OFF_MERGE + k * half
        return lambda: _dot(h_prev, w_in_ref[:, lo:lo + half])

    back_jobs = [gate_job(k) for k in range(2 * N_BRANCH)] + [
        lambda: _dot(yr_scr[...], w_br_ret_ref[...]),
        lambda: _dot(yc_scr[...], w_br_conv_ref[...]),
        lambda: _dot(ym_scr[...], w_br_mem_ref[...]),
    ]
    back_out = [None] * len(back_jobs)

    def run_back(k):
        back_out[k] = back_jobs[k]()

    x = x_ref[0]
    h_bf = (x * _rms_scale(x) * norm_w_ref[...]).astype(BF16)
    run_back(8)

    def proj(off, width=BRANCH_W):
        return _dot(h_bf, w_in_ref[:, off:off + width])

    za = proj(OFF_AC)
    zb = proj(OFF_BC)
    zq = proj(OFF_QR)
    zk = proj(OFF_KR)
    v_bf = proj(OFF_VR).astype(BF16)
    u = za * _sigmoid(zb)
    ubuf_ref[HIST_PAD:HIST_PAD + tile, :] = u
    first = HIST_PAD - CONV_HIST
    conv = None
    for r in range(8):
        part = None
        for j in range(CONV_K):
            if (first + j) % 8 != r:
                continue
            term = ubuf_ref[first + j:first + j + tile, :] * conv_w_ref[j:j + 1, :]
            part = term if part is None else part + term
        conv = part if conv is None else conv + part
        run_back(r)
    conv = conv + conv_b_ref[...]
    ubuf_ref[0:HIST_PAD, :] = jnp.where(live, ubuf_ref[tile:tile + HIST_PAD, :],
                                        ubuf_ref[0:HIST_PAD, :])

    z_gc = proj(OFF_GC)
    z_gr = proj(OFF_GR)
    z_qm = proj(OFF_QM)
    z_gm = proj(OFF_GM)

    merged = []
    for k in range(2):
        cols = slice(k * half, (k + 1) * half)
        acc = None
        for b in range(N_BRANCH):
            term = _sigmoid(back_out[2 * b + k]) * back_out[2 * N_BRANCH + b][:, cols]
            acc = term if acc is None else acc + term
        merged.append(acc.astype(BF16))
    merged = jnp.concatenate(merged, axis=-1)

    cos2 = cos_ref[...]
    sin2 = sin_ref[...]
    q_scale = HEAD_DIM ** -0.5
    o_heads = []
    for h in range(RET_HEADS):
        q_h = _rotary(_head(zq, h), cos2, sin2) * q_scale
        k_h = _rotary(_head(zk, h), cos2, sin2)
        v_h = _head(v_bf, h)
        s_old = s_ref[0, h]
        s = s_old
        o_chunks = []
        for c in range(tile // RET_CHUNK):
            rows = slice(c * RET_CHUNK, (c + 1) * RET_CHUNK)
            qc = q_h[rows].astype(BF16)
            kc = k_h[rows]
            vc = v_h[rows]
            sc = _dot_nt(qc, kc.astype(BF16)) * intra_ref[h]
            o_chunks.append(_dot(sc.astype(BF16), vc) + _dot(qc, s.astype(BF16)) * qdec_ref[h])
            kd = (kc * kdec_ref[h]).astype(BF16)
            s = s * c_dec[h] + _dot_tn(kd, vc)
        s_ref[0, h] = jnp.where(live, s, s_old)
        o_heads.append(_layernorm_rows(jnp.concatenate(o_chunks, axis=0)))

    out = xb_ref[0] + _dot(merged, w_out_ref[...])

    ret = jnp.concatenate(o_heads, axis=-1) * gn_w_ref[...]
    yr = _silu(z_gr) * ret
    cn = _layernorm_rows(conv) * ln_w_ref[...] + ln_b_ref[...]
    yc = _silu(z_gc) * _silu(cn)

    qm = z_qm.astype(BF16)
    mk = mk_ref[0]
    mv = mv_ref[0]
    att = []
    for h in range(MEM_HEADS):
        sc = _dot_nt(_head(qm, h), _head(mk, h)) * (HEAD_DIM ** -0.5)
        e = jnp.exp(sc - jnp.max(sc, axis=-1, keepdims=True))
        p = e / jnp.sum(e, axis=-1, keepdims=True)
        att.append(_dot(p.astype(BF16), _head(mv, h)))
    ym = _silu(z_gm) * jnp.concatenate(att, axis=-1)

    y_ref[0] = out * _rms_scale(out) * final_w_ref[...]

    h_scr[...] = h_bf
    yr_scr[...] = yr.astype(BF16)
    yc_scr[...] = yc.astype(BF16)
    ym_scr[...] = ym.astype(BF16)

    @pl.when(jnp.logical_and(t == tiles_per_seq - 1, live))
    def _():
        hist_ref[0] = ubuf_ref[first:HIST_PAD, :]


def _const_spec(shape, single_buffer=False):
    zeros = (0,) * len(shape)
    if single_buffer:
        return pl.BlockSpec(shape, lambda i: zeros, pipeline_mode=pl.Buffered(1))
    return pl.BlockSpec(shape, lambda i: zeros)


def _prompt_layer(x, cos2, sin2, mk_bf, mv_bf, norm_w, w_in_bf, gn_w, conv_w, conv_b, ln_w,
                  ln_b, w_br_ret_bf, w_br_conv_bf, w_br_mem_bf, w_out_bf, final_w,
                  intra, qdec, kdec, c_dec):
    batch, seq, _ = x.shape
    n_mem = mk_bf.shape[1]
    tile = PROMPT_TILE
    tiles_per_seq = seq // tile
    n_tiles = batch * tiles_per_seq

    def front(i):
        j = jnp.minimum(i, n_tiles - 1)
        return j // tiles_per_seq, lax.rem(j, tiles_per_seq)

    def back(i):
        j = jnp.maximum(i - 1, 0)
        return j // tiles_per_seq, lax.rem(j, tiles_per_seq)

    row = lambda w: _const_spec((1, w))
    weight = lambda a: _const_spec(a.shape, single_buffer=True)
    table = _const_spec((RET_HEADS, RET_CHUNK, RET_CHUNK))
    x_tile = lambda which: pl.BlockSpec((1, tile, D_MODEL), lambda i: (*which(i), 0))
    per_seq = lambda shape: pl.BlockSpec(
        (1,) + shape, lambda i: (front(i)[0],) + (0,) * len(shape))
    return pl.pallas_call(
        functools.partial(_prompt_kernel, c_dec, tiles_per_seq, n_tiles),
        grid=(n_tiles + 1,),
        in_specs=[
            x_tile(front), x_tile(back),
            pl.BlockSpec((tile, HEAD_DIM), lambda i: (front(i)[1], 0)),
            pl.BlockSpec((tile, HEAD_DIM), lambda i: (front(i)[1], 0)),
            per_seq((n_mem, BRANCH_W)), per_seq((n_mem, BRANCH_W)),
            row(D_MODEL), weight(w_in_bf), row(BRANCH_W),
            _const_spec((CONV_K, BRANCH_W)), row(BRANCH_W), row(BRANCH_W), row(BRANCH_W),
            weight(w_br_ret_bf), weight(w_br_conv_bf), weight(w_br_mem_bf), weight(w_out_bf),
            row(D_MODEL), table, table, table,
        ],
        out_specs=[
            x_tile(back),
            per_seq((RET_HEADS, HEAD_DIM, HEAD_DIM)),
            per_seq((CONV_HIST, BRANCH_W)),
        ],
        out_shape=[
            jax.ShapeDtypeStruct((batch, seq, D_MODEL), F32),
            jax.ShapeDtypeStruct((batch, RET_HEADS, HEAD_DIM, HEAD_DIM), F32),
            jax.ShapeDtypeStruct((batch, CONV_HIST, BRANCH_W), F32),
        ],
        scratch_shapes=[
            pltpu.VMEM((HIST_PAD + tile, BRANCH_W), F32),
            pltpu.VMEM((tile, D_MODEL), BF16),
            pltpu.VMEM((tile, BRANCH_W), BF16),
            pltpu.VMEM((tile, BRANCH_W), BF16),
            pltpu.VMEM((tile, BRANCH_W), BF16),
        ],
        compiler_params=pltpu.CompilerParams(
            dimension_semantics=("arbitrary",), vmem_limit_bytes=VMEM_LIMIT_BYTES,
            flags=PROMPT_SCHEDULER_FLAGS),
        name="prompt_layer",
    )(x, x, cos2, sin2, mk_bf, mv_bf, norm_w, w_in_bf, gn_w, conv_w, conv_b, ln_w, ln_b,
      w_br_ret_bf, w_br_conv_bf, w_br_mem_bf, w_out_bf, final_w, intra, qdec, kdec)


SAMPLE_PROJ_BLOCK = 1536


def _sample_proj_kernel(x_ref, norm_w_ref, w_ref, z_ref):
    x = x_ref[...]
    h_bf = (x * _rms_scale(x) * norm_w_ref[...]).astype(BF16)
    z_ref[...] = _dot(h_bf, w_ref[...])


def _sample_proj(xs, norm_w, w_in_bf):
    n = xs.shape[0]
    return pl.pallas_call(
        _sample_proj_kernel,
        grid=(IN_W // SAMPLE_PROJ_BLOCK,),
        in_specs=[
            pl.BlockSpec((n, D_MODEL), lambda j: (0, 0)),
            pl.BlockSpec((1, D_MODEL), lambda j: (0, 0)),
            pl.BlockSpec((D_MODEL, SAMPLE_PROJ_BLOCK), lambda j: (0, j)),
        ],
        out_specs=pl.BlockSpec((n, SAMPLE_PROJ_BLOCK), lambda j: (0, j)),
        out_shape=jax.ShapeDtypeStruct((n, IN_W), F32),
        compiler_params=pltpu.CompilerParams(dimension_semantics=("arbitrary",)),
        name="sample_proj",
    )(xs, norm_w, w_in_bf)


def _sample_state_kernel(gamma, z_ref, cos_ref, sin_ref, s0_ref, hist_ref, ck_ref, cv_ref,
                         conv_w_ref, o_ref, conv_ref, att_ref, s_new_ref, hist_new_ref):
    nb = z_ref.shape[0]
    n_mem = ck_ref.shape[1] // MEM_HEADS
    cos2 = cos_ref[...]
    sin2 = sin_ref[...]
    q_scale = HEAD_DIM ** -0.5
    zq = z_ref[:, OFF_QR:OFF_QR + BRANCH_W]
    zk = z_ref[:, OFF_KR:OFF_KR + BRANCH_W]
    v = z_ref[:, OFF_VR:OFF_VR + BRANCH_W]
    q = jnp.concatenate([_rotary(_head(zq, h), cos2, sin2) * q_scale
                         for h in range(RET_HEADS)], axis=-1)
    k = jnp.concatenate([_rotary(_head(zk, h), cos2, sin2) for h in range(RET_HEADS)], axis=-1)
    u = z_ref[:, OFF_AC:OFF_AC + BRANCH_W] * _sigmoid(z_ref[:, OFF_BC:OFF_BC + BRANCH_W])
    qm = z_ref[:, OFF_QM:OFF_QM + BRANCH_W]

    eye = (lax.broadcasted_iota(jnp.int32, (HEAD_DIM, HEAD_DIM), 0)
           == lax.broadcasted_iota(jnp.int32, (HEAD_DIM, HEAD_DIM), 1))
    ones_bf = jnp.ones((HEAD_DIM, HEAD_DIM), BF16)

    def as_columns(row):
        diag = jnp.where(eye, jnp.broadcast_to(row, (HEAD_DIM, HEAD_DIM)), 0.0)
        return _dot(diag.astype(BF16), ones_bf)

    w_hist = conv_w_ref[0:CONV_HIST, :]
    w_last = conv_w_ref[CONV_HIST:CONV_K, :]
    mem_scale = HEAD_DIM ** -0.5

    for b in range(nb):
        for h in range(RET_HEADS):
            q_row = _head(q, h)[b:b + 1]
            k_row = _head(k, h)[b:b + 1]
            v_row = _head(v, h)[b:b + 1]
            s0 = s0_ref[b, h]
            qk = jnp.sum(q_row * k_row, axis=-1, keepdims=True)
            qs = jnp.sum(as_columns(q_row) * s0, axis=0, keepdims=True)
            o_ref[b:b + 1, h * HEAD_DIM:(h + 1) * HEAD_DIM] = qk * v_row + gamma[h] * qs
            s_new_ref[b, h] = gamma[h] * s0 + as_columns(k_row) * v_row

        hist = hist_ref[b]
        u_row = u[b:b + 1]
        conv_ref[b:b + 1, :] = (jnp.sum(hist * w_hist, axis=0, keepdims=True)
                                + u_row * w_last)
        hist_new_ref[b, 0:CONV_HIST - 1, :] = hist[1:CONV_HIST]
        hist_new_ref[b, CONV_HIST - 1:CONV_HIST, :] = u_row

        for h in range(MEM_HEADS):
            head_rows = pl.ds(h, n_mem, stride=MEM_HEADS)
            prod = ck_ref[b, head_rows, :] * _head(qm, h)[b:b + 1]
            sc = jnp.sum(prod, axis=-1, keepdims=True) * mem_scale
            e = jnp.exp(sc - jnp.max(sc, axis=0, keepdims=True))
            p = e / jnp.sum(e, axis=0, keepdims=True)
            att_ref[b:b + 1, h * HEAD_DIM:(h + 1) * HEAD_DIM] = jnp.sum(
                p * cv_ref[b, head_rows, :], axis=0, keepdims=True)


def _sample_state(z, cos2, sin2, s0, hist, cache_k, cache_v, conv_w, gamma):
    n = z.shape[0]
    nb = SAMPLE_BLOCK
    rows = lambda w: pl.BlockSpec((nb, w), lambda i: (i, 0))
    state = pl.BlockSpec((nb, RET_HEADS, HEAD_DIM, HEAD_DIM), lambda i: (i, 0, 0, 0))
    hist_spec = pl.BlockSpec((nb, CONV_HIST, BRANCH_W), lambda i: (i, 0, 0))
    cache = pl.BlockSpec((nb,) + cache_k.shape[1:], lambda i: (i, 0, 0))
    return pl.pallas_call(
        functools.partial(_sample_state_kernel, gamma),
        grid=(n // nb,),
        in_specs=[
            rows(IN_W),
            pl.BlockSpec((1, HEAD_DIM), lambda i: (0, 0)),
            pl.BlockSpec((1, HEAD_DIM), lambda i: (0, 0)),
            state, hist_spec, cache, cache,
            pl.BlockSpec((CONV_K, BRANCH_W), lambda i: (0, 0)),
        ],
        out_specs=[rows(BRANCH_W), rows(BRANCH_W), rows(BRANCH_W), state, hist_spec],
        out_shape=[
            jax.ShapeDtypeStruct((n, BRANCH_W), F32),
            jax.ShapeDtypeStruct((n, BRANCH_W), F32),
            jax.ShapeDtypeStruct((n, BRANCH_W), F32),
            jax.ShapeDtypeStruct(s0.shape, F32),
            jax.ShapeDtypeStruct(hist.shape, F32),
        ],
        compiler_params=pltpu.CompilerParams(
            dimension_semantics=("arbitrary",), vmem_limit_bytes=VMEM_LIMIT_BYTES),
        name="sample_state",
    )(z, cos2, sin2, s0, hist, cache_k, cache_v, conv_w)


def _sample_out_kernel(x_ref, z_ref, o_ref, conv_ref, att_ref, norm_w_ref, w_in_ref, gn_w_ref,
                       conv_b_ref, ln_w_ref, ln_b_ref, w_br_ret_ref, w_br_conv_ref,
                       w_br_mem_ref, w_out_ref, final_w_ref, y_ref):
    x = x_ref[...]
    h_bf = (x * _rms_scale(x) * norm_w_ref[...]).astype(BF16)
    gate = lambda off: _silu(z_ref[:, off:off + BRANCH_W])
    yr = gate(OFF_GR) * _group_norm_heads(o_ref[...], gn_w_ref[...])
    cn = _layernorm_rows(conv_ref[...] + conv_b_ref[...]) * ln_w_ref[...] + ln_b_ref[...]
    yc = gate(OFF_GC) * _silu(cn)
    ym = gate(OFF_GM) * att_ref[...]
    y_ref[...] = _merge_and_project(x, h_bf, w_in_ref, yr, yc, ym, w_br_ret_ref,
                                    w_br_conv_ref, w_br_mem_ref, w_out_ref, final_w_ref[...])


def _sample_out(xs, z, o, conv, att, norm_w, w_in_bf, gn_w, conv_b, ln_w, ln_b,
                w_br_ret_bf, w_br_conv_bf, w_br_mem_bf, w_out_bf, final_w):
    return pl.pallas_call(
        _sample_out_kernel,
        out_shape=jax.ShapeDtypeStruct(xs.shape, F32),
        compiler_params=pltpu.CompilerParams(vmem_limit_bytes=VMEM_LIMIT_BYTES),
        name="sample_out",
    )(xs, z, o, conv, att, norm_w, w_in_bf, gn_w, conv_b, ln_w, ln_b,
      w_br_ret_bf, w_br_conv_bf, w_br_mem_bf, w_out_bf, final_w)


def _rotary_tables(pos):
    half = HEAD_DIM // 2
    inv = ROPE_BASE ** (-jnp.arange(half, dtype=F32) / half)
    ang = pos[:, None] * inv[None, :]
    cos, sin = jnp.cos(ang), jnp.sin(ang)
    return jnp.concatenate([cos, cos], axis=-1), jnp.concatenate([-sin, sin], axis=-1)


def _decay_tables():
    c = RET_CHUNK
    log_g = jnp.log1p(-jnp.exp2(-5.0 - jnp.arange(RET_HEADS, dtype=F32)))
    idx = jnp.arange(c, dtype=F32)
    diff = idx[:, None] - idx[None, :]
    intra = jnp.where(diff >= 0,
                      jnp.exp(jnp.maximum(diff, 0.0)[None] * log_g[:, None, None]), 0.0)
    q_dec = jnp.exp((idx + 1.0)[None, :] * log_g[:, None])
    k_dec = jnp.exp((c - 1.0 - idx)[None, :] * log_g[:, None])
    bcast = lambda a: jnp.broadcast_to(a[:, :, None], (RET_HEADS, c, HEAD_DIM))
    return intra, bcast(q_dec), bcast(k_dec)


def _gamma_powers(n):
    return tuple(float(np.exp(np.log1p(-np.exp2(-5.0 - h)) * n)) for h in range(RET_HEADS))


def kernel(x_prompt, x_sample, mem_prompt, state_ret, state_conv, cache_mem_k, cache_mem_v,
           norm_w, w_in, ret_gn_w, conv_w, conv_b, conv_ln_w, conv_ln_b, mem_norm_w,
           w_mem_kv, w_br_ret, w_br_conv, w_br_mem, w_out, final_norm_w):
    depth = w_in.shape[0]
    assert depth == 1, "single-layer step"
    batch, seq, _ = x_prompt.shape
    n_dec, dec_seq, _ = x_sample.shape
    assert dec_seq == 1 and seq % PROMPT_TILE == 0 and n_dec % SAMPLE_BLOCK == 0
    n_mem = mem_prompt.shape[1]

    bf = lambda w: w[0].astype(BF16)
    w_in_bf, w_kv_bf = bf(w_in), bf(w_mem_kv)
    w_br_ret_bf, w_br_conv_bf, w_br_mem_bf, w_out_bf = bf(w_br_ret), bf(w_br_conv), bf(w_br_mem), bf(w_out)
    final_w = final_norm_w[None, :]

    intra, qdec, kdec = _decay_tables()
    cos_p, sin_p = _rotary_tables(jnp.arange(seq, dtype=F32))
    cos_s, sin_s = _rotary_tables(jnp.arange(dec_seq, dtype=F32) + PAST_LEN)

    mk, mv, mk_bf, mv_bf = _mem_kv(mem_prompt, mem_norm_w, w_kv_bf)

    y_prompt, s_prompt, hist_prompt = _prompt_layer(
        x_prompt, cos_p, sin_p, mk_bf, mv_bf, norm_w, w_in_bf, ret_gn_w, conv_w[0], conv_b,
        conv_ln_w, conv_ln_b, w_br_ret_bf, w_br_conv_bf, w_br_mem_bf, w_out_bf, final_w,
        intra, qdec, kdec, _gamma_powers(RET_CHUNK))

    xs = x_sample[:, 0, :]
    z_s = _sample_proj(xs, norm_w, w_in_bf)
    o_s, conv_s, att_s, s_sample, hist_sample = _sample_state(
        z_s, cos_s, sin_s, state_ret[0], state_conv[0],
        cache_mem_k.reshape(n_dec, n_mem * MEM_HEADS, HEAD_DIM),
        cache_mem_v.reshape(n_dec, n_mem * MEM_HEADS, HEAD_DIM), conv_w[0], _gamma_powers(1))
    y_sample = _sample_out(xs, z_s, o_s, conv_s, att_s, norm_w, w_in_bf, ret_gn_w, conv_b,
                           conv_ln_w, conv_ln_b, w_br_ret_bf, w_br_conv_bf, w_br_mem_bf,
                           w_out_bf, final_w)

    heads = lambda a: a.reshape(1, batch, n_mem, MEM_HEADS, HEAD_DIM)
    return (y_prompt, y_sample[:, None, :], s_prompt[None], s_sample[None],
            hist_prompt[None], hist_sample[None], heads(mk), heads(mv))
```

```python
import functools

import numpy as np
import jax
import jax.numpy as jnp
from jax import lax
from jax.experimental import pallas as pl
from jax.experimental.pallas import tpu as pltpu

F32 = jnp.float32
BF16 = jnp.bfloat16

D_MODEL = 1024
RET_HEADS = 4
HEAD_DIM = 128
BRANCH_W = 512
RET_CHUNK = 128
ROPE_BASE = 10000.0
CONV_K = 31
CONV_HIST = CONV_K - 1
MEM_HEADS = 4
N_BRANCH = 3
EPS = 1e-6
PAST_LEN = 16384

OFF_QR, OFF_KR, OFF_VR, OFF_GR = 0, 512, 1024, 1536
OFF_AC, OFF_BC, OFF_GC = 2048, 2560, 3072
OFF_QM, OFF_GM = 3584, 4096
OFF_MERGE = 4608
IN_W = OFF_MERGE + N_BRANCH * D_MODEL

V7X_VMEM_BYTES = 64 * 1024 * 1024
VMEM_LIMIT_BYTES = V7X_VMEM_BYTES - 8 * 1024 * 1024

PROMPT_TILE = 256
HIST_PAD = 32
SAMPLE_BLOCK = 8


def _sigmoid(x):
    return 0.5 * jnp.tanh(0.5 * x) + 0.5


def _silu(x):
    hx = 0.5 * x
    return hx * (jnp.tanh(hx) + 1.0)


def _rms_scale(x):
    return lax.rsqrt(jnp.mean(x * x, axis=-1, keepdims=True) + EPS)


def _layernorm_rows(x):
    mu = jnp.mean(x, axis=-1, keepdims=True)
    xc = x - mu
    var = jnp.mean(xc * xc, axis=-1, keepdims=True)
    return xc * lax.rsqrt(var + EPS)


def _rotary(x, cos2, sin2):
    return x * cos2 + pltpu.roll(x, HEAD_DIM // 2, axis=1) * sin2


def _dot(a, b):
    return jnp.dot(a, b, preferred_element_type=F32)


def _dot_nt(a, b):
    return lax.dot_general(a, b, (((1,), (1,)), ((), ())), preferred_element_type=F32)


def _dot_tn(a, b):
    return lax.dot_general(a, b, (((0,), (0,)), ((), ())), preferred_element_type=F32)


def _head(x, h):
    return x[:, h * HEAD_DIM:(h + 1) * HEAD_DIM]


def _group_norm_heads(o, gn_w):
    parts = [_layernorm_rows(_head(o, h)) for h in range(RET_HEADS)]
    return jnp.concatenate(parts, axis=-1) * gn_w


def _merge_and_project(x, gate_pre, yr, yc, ym, w_br_ret_ref, w_br_conv_ref,
                       w_br_mem_ref, w_out_ref, final_w):
    branches = (
        _dot(yr.astype(BF16), w_br_ret_ref[...]),
        _dot(yc.astype(BF16), w_br_conv_ref[...]),
        _dot(ym.astype(BF16), w_br_mem_ref[...]),
    )
    merged = None
    for i, br in enumerate(branches):
        term = _sigmoid(gate_pre(i)) * br
        merged = term if merged is None else merged + term
    out = x + _dot(merged.astype(BF16), w_out_ref[...])
    return out * _rms_scale(out) * final_w


def _mem_kv_kernel(mem_ref, norm_w_ref, w_ref, k_ref, v_ref, kbf_ref, vbf_ref):
    m = mem_ref[0]
    hm = (m * _rms_scale(m) * norm_w_ref[...]).astype(BF16)
    kv = _dot(hm, w_ref[...])
    k = kv[:, :BRANCH_W]
    v = kv[:, BRANCH_W:]
    n_mem = k.shape[0]
    for h in range(MEM_HEADS):
        k_ref[0, pl.ds(h, n_mem, stride=MEM_HEADS), :] = _head(k, h)
        v_ref[0, pl.ds(h, n_mem, stride=MEM_HEADS), :] = _head(v, h)
    kbf_ref[0] = k.astype(BF16)
    vbf_ref[0] = v.astype(BF16)


def _mem_kv(mem, mem_norm_w, w_mem_kv_bf):
    batch, n_mem, _ = mem.shape
    blk = lambda dt: jax.ShapeDtypeStruct((batch, n_mem, BRANCH_W), dt)
    out_spec = pl.BlockSpec((1, n_mem, BRANCH_W), lambda b: (b, 0, 0))
    rows_shape = (batch, n_mem * MEM_HEADS, HEAD_DIM)
    rows_spec = pl.BlockSpec((1, n_mem * MEM_HEADS, HEAD_DIM), lambda b: (b, 0, 0))
    return pl.pallas_call(
        _mem_kv_kernel,
        grid=(batch,),
        in_specs=[
            pl.BlockSpec((1, n_mem, D_MODEL), lambda b: (b, 0, 0)),
            pl.BlockSpec((1, D_MODEL), lambda b: (0, 0)),
            pl.BlockSpec((D_MODEL, 2 * BRANCH_W), lambda b: (0, 0)),
        ],
        out_specs=[rows_spec, rows_spec, out_spec, out_spec],
        out_shape=[jax.ShapeDtypeStruct(rows_shape, F32), jax.ShapeDtypeStruct(rows_shape, F32),
                   blk(BF16), blk(BF16)],
        compiler_params=pltpu.CompilerParams(dimension_semantics=("arbitrary",)),
        name="mem_kv",
    )(mem, mem_norm_w, w_mem_kv_bf)


def _prompt_kernel(c_dec, tiles_per_seq, n_tiles,
                   x_ref, xb_ref, cos_ref, sin_ref, mk_ref, mv_ref, norm_w_ref, w_in_ref,
                   gn_w_ref, conv_w_ref, conv_b_ref, ln_w_ref, ln_b_ref,
                   w_br_ret_ref, w_br_conv_ref, w_br_mem_ref, w_out_ref, final_w_ref,
                   intra_ref, qdec_ref, kdec_ref,
                   y_ref, s_ref, hist_ref,
                   ubuf_ref, h_scr, yr_scr, yc_scr, ym_scr):
    i = pl.program_id(0)
    t = lax.rem(jnp.minimum(i, n_tiles - 1), tiles_per_seq)
    live = i < n_tiles
    tile = x_ref.shape[1]

    @pl.when(i == 0)
    def _():
        h_scr[...] = jnp.zeros_like(h_scr)
        yr_scr[...] = jnp.zeros_like(yr_scr)
        yc_scr[...] = jnp.zeros_like(yc_scr)
        ym_scr[...] = jnp.zeros_like(ym_scr)

    @pl.when(jnp.logical_and(t == 0, live))
    def _():
        s_ref[...] = jnp.zeros_like(s_ref)
        ubuf_ref[0:HIST_PAD, :] = jnp.zeros((HIST_PAD, BRANCH_W), F32)

    h_prev = h_scr[...]
    half = D_MODEL // 2

    def gate_job(k):
        lo = OFF_MERGE + k * half
        return lambda: _dot(h_prev, w_in_ref[:, lo:lo + half])

    back_jobs = [gate_job(k) for k in range(2 * N_BRANCH)] + [
        lambda: _dot(yr_scr[...], w_br_ret_ref[...]),
        lambda: _dot(yc_scr[...], w_br_conv_ref[...]),
        lambda: _dot(ym_scr[...], w_br_mem_ref[...]),
    ]
    back_out = [None] * len(back_jobs)

    def run_back(k):
        back_out[k] = back_jobs[k]()

    x = x_ref[0]
    h_bf = (x * _rms_scale(x) * norm_w_ref[...]).astype(BF16)
    run_back(8)

    def proj(off, width=BRANCH_W):
        return _dot(h_bf, w_in_ref[:, off:off + width])

    za = proj(OFF_AC)
    zb = proj(OFF_BC)
    zq = proj(OFF_QR)
    zk = proj(OFF_KR)
    v_bf = proj(OFF_VR).astype(BF16)
    u = za * _sigmoid(zb)
    ubuf_ref[HIST_PAD:HIST_PAD + tile, :] = u
    first = HIST_PAD - CONV_HIST
    conv = None
    for r in range(8):
        part = None
        for j in range(CONV_K):
            if (first + j) % 8 != r:
                continue
            term = ubuf_ref[first + j:first + j + tile, :] * conv_w_ref[0, j:j + 1, :]
            part = term if part is None else part + term
        conv = part if conv is None else conv + part
        run_back(r)
    conv = conv + conv_b_ref[...]
    ubuf_ref[0:HIST_PAD, :] = jnp.where(live, ubuf_ref[tile:tile + HIST_PAD, :],
                                        ubuf_ref[0:HIST_PAD, :])

    z_gc = proj(OFF_GC)
    z_gr = proj(OFF_GR)
    z_qm = proj(OFF_QM)
    z_gm = proj(OFF_GM)

    merged = []
    for k in range(2):
        cols = slice(k * half, (k + 1) * half)
        acc = None
        for b in range(N_BRANCH):
            term = _sigmoid(back_out[2 * b + k]) * back_out[2 * N_BRANCH + b][:, cols]
            acc = term if acc is None else acc + term
        merged.append(acc.astype(BF16))
    merged = jnp.concatenate(merged, axis=-1)

    cos2 = cos_ref[...]
    sin2 = sin_ref[...]
    q_scale = HEAD_DIM ** -0.5
    o_heads = []
    for h in range(RET_HEADS):
        q_h = _rotary(_head(zq, h), cos2, sin2) * q_scale
        k_h = _rotary(_head(zk, h), cos2, sin2)
        v_h = _head(v_bf, h)
        s_old = s_ref[0, h]
        s = s_old
        o_chunks = []
        for c in range(tile // RET_CHUNK):
            rows = slice(c * RET_CHUNK, (c + 1) * RET_CHUNK)
            qc = q_h[rows].astype(BF16)
            kc = k_h[rows]
            vc = v_h[rows]
            sc = _dot_nt(qc, kc.astype(BF16)) * intra_ref[h]
            o_chunks.append(_dot(sc.astype(BF16), vc) + _dot(qc, s.astype(BF16)) * qdec_ref[h])
            kd = (kc * kdec_ref[h]).astype(BF16)
            s = s * c_dec[h] + _dot_tn(kd, vc)
        s_ref[0, h] = jnp.where(live, s, s_old)
        o_heads.append(_layernorm_rows(jnp.concatenate(o_chunks, axis=0)))

    out = xb_ref[0] + _dot(merged, w_out_ref[...])

    ret = jnp.concatenate(o_heads, axis=-1) * gn_w_ref[...]
    yr = _silu(z_gr) * ret
    cn = _layernorm_rows(conv) * ln_w_ref[...] + ln_b_ref[...]
    yc = _silu(z_gc) * _silu(cn)

    qm = z_qm.astype(BF16)
    mk = mk_ref[0]
    mv = mv_ref[0]
    att = []
    for h in range(MEM_HEADS):
        sc = _dot_nt(_head(qm, h), _head(mk, h)) * (HEAD_DIM ** -0.5)
        e = jnp.exp(sc - jnp.max(sc, axis=-1, keepdims=True))
        p = e / jnp.sum(e, axis=-1, keepdims=True)
        att.append(_dot(p.astype(BF16), _head(mv, h)))
    ym = _silu(z_gm) * jnp.concatenate(att, axis=-1)

    y_ref[0] = out * _rms_scale(out) * final_w_ref[...]

    h_scr[...] = h_bf
    yr_scr[...] = yr.astype(BF16)
    yc_scr[...] = yc.astype(BF16)
    ym_scr[...] = ym.astype(BF16)

    @pl.when(jnp.logical_and(t == tiles_per_seq - 1, live))
    def _():
        hist_ref[0] = ubuf_ref[first:HIST_PAD, :]


def _const_spec(shape, single_buffer=False):
    zeros = (0,) * len(shape)
    if single_buffer:
        return pl.BlockSpec(shape, lambda i: zeros, pipeline_mode=pl.Buffered(1))
    return pl.BlockSpec(shape, lambda i: zeros)


def _prompt_layer(x, cos2, sin2, mk_bf, mv_bf, norm_w, w_in_bf, gn_w, conv_w, conv_b, ln_w,
                  ln_b, w_br_ret_bf, w_br_conv_bf, w_br_mem_bf, w_out_bf, final_w,
                  intra, qdec, kdec, c_dec):
    batch, seq, _ = x.shape
    n_mem = mk_bf.shape[1]
    tile = PROMPT_TILE
    tiles_per_seq = seq // tile
    n_tiles = batch * tiles_per_seq

    def front(i):
        j = jnp.minimum(i, n_tiles - 1)
        return j // tiles_per_seq, lax.rem(j, tiles_per_seq)

    def back(i):
        j = jnp.maximum(i - 1, 0)
        return j // tiles_per_seq, lax.rem(j, tiles_per_seq)

    row = lambda w: _const_spec((1, w))
    weight = lambda a: _const_spec(a.shape, single_buffer=True)
    table = _const_spec((RET_HEADS, RET_CHUNK, RET_CHUNK))
    x_tile = lambda which: pl.BlockSpec((1, tile, D_MODEL), lambda i: (*which(i), 0))
    per_seq = lambda shape: pl.BlockSpec(
        (1,) + shape, lambda i: (front(i)[0],) + (0,) * len(shape))
    return pl.pallas_call(
        functools.partial(_prompt_kernel, c_dec, tiles_per_seq, n_tiles),
        grid=(n_tiles + 1,),
        in_specs=[
            x_tile(front), x_tile(back),
            pl.BlockSpec((tile, HEAD_DIM), lambda i: (front(i)[1], 0)),
            pl.BlockSpec((tile, HEAD_DIM), lambda i: (front(i)[1], 0)),
            per_seq((n_mem, BRANCH_W)), per_seq((n_mem, BRANCH_W)),
            row(D_MODEL), weight(w_in_bf), row(BRANCH_W),
            _const_spec((1, CONV_K, BRANCH_W)), row(BRANCH_W), row(BRANCH_W), row(BRANCH_W),
            weight(w_br_ret_bf), weight(w_br_conv_bf), weight(w_br_mem_bf), weight(w_out_bf),
            row(D_MODEL), table, table, table,
        ],
        out_specs=[
            x_tile(back),
            per_seq((RET_HEADS, HEAD_DIM, HEAD_DIM)),
            per_seq((CONV_HIST, BRANCH_W)),
        ],
        out_shape=[
            jax.ShapeDtypeStruct((batch, seq, D_MODEL), F32),
            jax.ShapeDtypeStruct((batch, RET_HEADS, HEAD_DIM, HEAD_DIM), F32),
            jax.ShapeDtypeStruct((batch, CONV_HIST, BRANCH_W), F32),
        ],
        scratch_shapes=[
            pltpu.VMEM((HIST_PAD + tile, BRANCH_W), F32),
            pltpu.VMEM((tile, D_MODEL), BF16),
            pltpu.VMEM((tile, BRANCH_W), BF16),
            pltpu.VMEM((tile, BRANCH_W), BF16),
            pltpu.VMEM((tile, BRANCH_W), BF16),
        ],
        compiler_params=pltpu.CompilerParams(
            dimension_semantics=("arbitrary",), vmem_limit_bytes=VMEM_LIMIT_BYTES),
        name="prompt_layer",
    )(x, x, cos2, sin2, mk_bf, mv_bf, norm_w, w_in_bf, gn_w, conv_w, conv_b, ln_w, ln_b,
      w_br_ret_bf, w_br_conv_bf, w_br_mem_bf, w_out_bf, final_w, intra, qdec, kdec)


SAMPLE_PROJ_BLOCK = 1536


def _sample_proj_kernel(x_ref, norm_w_ref, w_ref, z_ref):
    x = x_ref[...]
    h_bf = (x * _rms_scale(x) * norm_w_ref[...]).astype(BF16)
    z_ref[...] = _dot(h_bf, w_ref[...])


def _sample_proj(xs, norm_w, w_in_bf):
    n = xs.shape[0]
    return pl.pallas_call(
        _sample_proj_kernel,
        grid=(IN_W // SAMPLE_PROJ_BLOCK,),
        in_specs=[
            pl.BlockSpec((n, D_MODEL), lambda j: (0, 0)),
            pl.BlockSpec((1, D_MODEL), lambda j: (0, 0)),
            pl.BlockSpec((D_MODEL, SAMPLE_PROJ_BLOCK), lambda j: (0, j)),
        ],
        out_specs=pl.BlockSpec((n, SAMPLE_PROJ_BLOCK), lambda j: (0, j)),
        out_shape=jax.ShapeDtypeStruct((n, IN_W), F32),
        compiler_params=pltpu.CompilerParams(dimension_semantics=("arbitrary",)),
        name="sample_proj",
    )(xs, norm_w, w_in_bf)


def _sample_state_kernel(gamma, z_ref, cos_ref, sin_ref, s0_ref, hist_ref, ck_ref, cv_ref,
                         conv_w_ref, o_ref, conv_ref, att_ref, s_new_ref, hist_new_ref):
    nb = z_ref.shape[0]
    n_mem = ck_ref.shape[1] // MEM_HEADS
    cos2 = cos_ref[...]
    sin2 = sin_ref[...]
    q_scale = HEAD_DIM ** -0.5
    zq = z_ref[:, OFF_QR:OFF_QR + BRANCH_W]
    zk = z_ref[:, OFF_KR:OFF_KR + BRANCH_W]
    v = z_ref[:, OFF_VR:OFF_VR + BRANCH_W]
    q = jnp.concatenate([_rotary(_head(zq, h), cos2, sin2) * q_scale
                         for h in range(RET_HEADS)], axis=-1)
    k = jnp.concatenate([_rotary(_head(zk, h), cos2, sin2) for h in range(RET_HEADS)], axis=-1)
    u = z_ref[:, OFF_AC:OFF_AC + BRANCH_W] * _sigmoid(z_ref[:, OFF_BC:OFF_BC + BRANCH_W])
    qm = z_ref[:, OFF_QM:OFF_QM + BRANCH_W]

    eye = (lax.broadcasted_iota(jnp.int32, (HEAD_DIM, HEAD_DIM), 0)
           == lax.broadcasted_iota(jnp.int32, (HEAD_DIM, HEAD_DIM), 1))
    ones_bf = jnp.ones((HEAD_DIM, HEAD_DIM), BF16)

    def as_columns(row):
        diag = jnp.where(eye, jnp.broadcast_to(row, (HEAD_DIM, HEAD_DIM)), 0.0)
        return _dot(diag.astype(BF16), ones_bf)

    mem_scale = HEAD_DIM ** -0.5

    acc = u * conv_w_ref[0, CONV_HIST:CONV_K, :]
    for j in range(CONV_HIST):
        acc = acc + hist_ref[j] * conv_w_ref[0, j:j + 1, :]
        hist_new_ref[j] = hist_ref[j + 1] if j + 1 < CONV_HIST else u
    conv_ref[...] = acc

    for b in range(nb):
        for h in range(RET_HEADS):
            q_row = _head(q, h)[b:b + 1]
            k_row = _head(k, h)[b:b + 1]
            v_row = _head(v, h)[b:b + 1]
            s0 = s0_ref[b, h]
            qk = jnp.sum(q_row * k_row, axis=-1, keepdims=True)
            qs = jnp.sum(as_columns(q_row) * s0, axis=0, keepdims=True)
            o_ref[b:b + 1, h * HEAD_DIM:(h + 1) * HEAD_DIM] = qk * v_row + gamma[h] * qs
            s_new_ref[b, h] = gamma[h] * s0 + as_columns(k_row) * v_row

        for h in range(MEM_HEADS):
            head_rows = pl.ds(h, n_mem, stride=MEM_HEADS)
            prod = ck_ref[b, head_rows, :] * _head(qm, h)[b:b + 1]
            sc = jnp.sum(prod, axis=-1, keepdims=True) * mem_scale
            e = jnp.exp(sc - jnp.max(sc, axis=0, keepdims=True))
            p = e / jnp.sum(e, axis=0, keepdims=True)
            att_ref[b:b + 1, h * HEAD_DIM:(h + 1) * HEAD_DIM] = jnp.sum(
                p * cv_ref[b, head_rows, :], axis=0, keepdims=True)


def _sample_state(z, cos2, sin2, s0, hist, cache_k, cache_v, conv_w, gamma):
    n = z.shape[0]
    nb = SAMPLE_BLOCK
    rows = lambda w: pl.BlockSpec((nb, w), lambda i: (i, 0))
    state = pl.BlockSpec((nb, RET_HEADS, HEAD_DIM, HEAD_DIM), lambda i: (i, 0, 0, 0))
    hist_spec = pl.BlockSpec((CONV_HIST, nb, BRANCH_W), lambda i: (0, i, 0))
    cache = pl.BlockSpec((nb,) + cache_k.shape[1:], lambda i: (i, 0, 0))
    return pl.pallas_call(
        functools.partial(_sample_state_kernel, gamma),
        grid=(n // nb,),
        in_specs=[
            rows(IN_W),
            pl.BlockSpec((1, HEAD_DIM), lambda i: (0, 0)),
            pl.BlockSpec((1, HEAD_DIM), lambda i: (0, 0)),
            state, hist_spec, cache, cache,
            pl.BlockSpec((1, CONV_K, BRANCH_W), lambda i: (0, 0, 0)),
        ],
        out_specs=[rows(BRANCH_W), rows(BRANCH_W), rows(BRANCH_W), state, hist_spec],
        out_shape=[
            jax.ShapeDtypeStruct((n, BRANCH_W), F32),
            jax.ShapeDtypeStruct((n, BRANCH_W), F32),
            jax.ShapeDtypeStruct((n, BRANCH_W), F32),
            jax.ShapeDtypeStruct(s0.shape, F32),
            jax.ShapeDtypeStruct(hist.shape, F32),
        ],
        compiler_params=pltpu.CompilerParams(
            dimension_semantics=("arbitrary",), vmem_limit_bytes=VMEM_LIMIT_BYTES),
        name="sample_state",
    )(z, cos2, sin2, s0, hist, cache_k, cache_v, conv_w)


def _sample_out_kernel(x_ref, z_ref, o_ref, conv_ref, att_ref, gn_w_ref, conv_b_ref, ln_w_ref,
                       ln_b_ref, w_br_ret_ref, w_br_conv_ref, w_br_mem_ref, w_out_ref,
                       final_w_ref, y_ref):
    gate = lambda off: _silu(z_ref[:, off:off + BRANCH_W])
    yr = gate(OFF_GR) * _group_norm_heads(o_ref[...], gn_w_ref[...])
    cn = _layernorm_rows(conv_ref[...] + conv_b_ref[...]) * ln_w_ref[...] + ln_b_ref[...]
    yc = gate(OFF_GC) * _silu(cn)
    ym = gate(OFF_GM) * att_ref[...]
    gate_pre = lambda i: z_ref[:, OFF_MERGE + i * D_MODEL:OFF_MERGE + (i + 1) * D_MODEL]
    y_ref[...] = _merge_and_project(x_ref[...], gate_pre, yr, yc, ym, w_br_ret_ref,
                                    w_br_conv_ref, w_br_mem_ref, w_out_ref, final_w_ref[...])


def _sample_out(xs, z, o, conv, att, gn_w, conv_b, ln_w, ln_b,
                w_br_ret_bf, w_br_conv_bf, w_br_mem_bf, w_out_bf, final_w):
    return pl.pallas_call(
        _sample_out_kernel,
        out_shape=jax.ShapeDtypeStruct(xs.shape, F32),
        compiler_params=pltpu.CompilerParams(vmem_limit_bytes=VMEM_LIMIT_BYTES),
        name="sample_out",
    )(xs, z, o, conv, att, gn_w, conv_b, ln_w, ln_b,
      w_br_ret_bf, w_br_conv_bf, w_br_mem_bf, w_out_bf, final_w)


def _rotary_tables(pos):
    half = HEAD_DIM // 2
    inv = ROPE_BASE ** (-jnp.arange(half, dtype=F32) / half)
    ang = pos[:, None] * inv[None, :]
    cos, sin = jnp.cos(ang), jnp.sin(ang)
    return jnp.concatenate([cos, cos], axis=-1), jnp.concatenate([-sin, sin], axis=-1)


def _decay_tables():
    c = RET_CHUNK
    log_g = jnp.log1p(-jnp.exp2(-5.0 - jnp.arange(RET_HEADS, dtype=F32)))
    idx = jnp.arange(c, dtype=F32)
    diff = idx[:, None] - idx[None, :]
    intra = jnp.where(diff >= 0,
                      jnp.exp(jnp.maximum(diff, 0.0)[None] * log_g[:, None, None]), 0.0)
    q_dec = jnp.exp((idx + 1.0)[None, :] * log_g[:, None])
    k_dec = jnp.exp((c - 1.0 - idx)[None, :] * log_g[:, None])
    bcast = lambda a: jnp.broadcast_to(a[:, :, None], (RET_HEADS, c, HEAD_DIM))
    return intra, bcast(q_dec), bcast(k_dec)


def _gamma_powers(n):
    return tuple(float(np.exp(np.log1p(-np.exp2(-5.0 - h)) * n)) for h in range(RET_HEADS))


def kernel(x_prompt, x_sample, mem_prompt, state_ret, state_conv, cache_mem_k, cache_mem_v,
           norm_w, w_in, ret_gn_w, conv_w, conv_b, conv_ln_w, conv_ln_b, mem_norm_w,
           w_mem_kv, w_br_ret, w_br_conv, w_br_mem, w_out, final_norm_w):
    depth = w_in.shape[0]
    assert depth == 1, "single-layer step"
    batch, seq, _ = x_prompt.shape
    n_dec, dec_seq, _ = x_sample.shape
    assert dec_seq == 1 and seq % PROMPT_TILE == 0 and n_dec % SAMPLE_BLOCK == 0
    n_mem = mem_prompt.shape[1]

    bf = lambda w: w[0].astype(BF16)
    w_in_bf, w_kv_bf = bf(w_in), bf(w_mem_kv)
    w_br_ret_bf, w_br_conv_bf, w_br_mem_bf, w_out_bf = bf(w_br_ret), bf(w_br_conv), bf(w_br_mem), bf(w_out)
    final_w = final_norm_w[None, :]

    intra, qdec, kdec = _decay_tables()
    cos_p, sin_p = _rotary_tables(jnp.arange(seq, dtype=F32))
    cos_s, sin_s = _rotary_tables(jnp.arange(dec_seq, dtype=F32) + PAST_LEN)

    mk, mv, mk_bf, mv_bf = _mem_kv(mem_prompt, mem_norm_w, w_kv_bf)

    y_prompt, s_prompt, hist_prompt = _prompt_layer(
        x_prompt, cos_p, sin_p, mk_bf, mv_bf, norm_w, w_in_bf, ret_gn_w, conv_w, conv_b,
        conv_ln_w, conv_ln_b, w_br_ret_bf, w_br_conv_bf, w_br_mem_bf, w_out_bf, final_w,
        intra, qdec, kdec, _gamma_powers(RET_CHUNK))

    xs = x_sample[:, 0, :]
    z_s = _sample_proj(xs, norm_w, w_in_bf)
    o_s, conv_s, att_s, s_sample, hist_sample = _sample_state(
        z_s, cos_s, sin_s, state_ret[0], jnp.transpose(state_conv[0], (1, 0, 2)),
        cache_mem_k.reshape(n_dec, n_mem * MEM_HEADS, HEAD_DIM),
        cache_mem_v.reshape(n_dec, n_mem * MEM_HEADS, HEAD_DIM), conv_w, _gamma_powers(1))
    y_sample = _sample_out(xs, z_s, o_s, conv_s, att_s, ret_gn_w, conv_b, conv_ln_w, conv_ln_b,
                           w_br_ret_bf, w_br_conv_bf, w_br_mem_bf, w_out_bf, final_w)

    heads = lambda a: a.reshape(1, batch, n_mem, MEM_HEADS, HEAD_DIM)
    return (y_prompt, y_sample[:, None, :], s_prompt[None], s_sample[None],
            hist_prompt[None], jnp.transpose(hist_sample, (1, 0, 2))[None], heads(mk), heads(mv))
```

```python
import functools

import numpy as np
import jax
import jax.numpy as jnp
from jax import lax
from jax.experimental import pallas as pl
from jax.experimental.pallas import tpu as pltpu

F32 = jnp.float32
BF16 = jnp.bfloat16

D_MODEL = 1024
RET_HEADS = 4
HEAD_DIM = 128
SUBLANES = 8
BRANCH_W = 512
RET_CHUNK = 128
ROPE_BASE = 10000.0
CONV_K = 31
CONV_HIST = CONV_K - 1
MEM_HEADS = 4
N_BRANCH = 3
EPS = 1e-6
PAST_LEN = 16384

OFF_QR, OFF_KR, OFF_VR, OFF_GR = 0, 512, 1024, 1536
OFF_AC, OFF_BC, OFF_GC = 2048, 2560, 3072
OFF_QM, OFF_GM = 3584, 4096
OFF_MERGE = 4608
IN_W = OFF_MERGE + N_BRANCH * D_MODEL

V7X_VMEM_BYTES = 64 * 1024 * 1024
VMEM_LIMIT_BYTES = V7X_VMEM_BYTES - 8 * 1024 * 1024

PROMPT_TILE = 256
HIST_PAD = 32
CONV_SLABS = BRANCH_W // HEAD_DIM
CONV_PHASES = 4
SAMPLE_BLOCK = 8


def _sigmoid(x):
    return 0.5 * jnp.tanh(0.5 * x) + 0.5


def _silu(x):
    hx = 0.5 * x
    return hx * (jnp.tanh(hx) + 1.0)


def _rms_scale(x):
    return lax.rsqrt(jnp.mean(x * x, axis=-1, keepdims=True) + EPS)


def _layernorm_rows(x):
    mu = jnp.mean(x, axis=-1, keepdims=True)
    xc = x - mu
    var = jnp.mean(xc * xc, axis=-1, keepdims=True)
    return xc * lax.rsqrt(var + EPS)


def _rotary(x, cos2, sin2):
    return x * cos2 + pltpu.roll(x, HEAD_DIM // 2, axis=1) * sin2


def _dot(a, b):
    return jnp.dot(a, b, preferred_element_type=F32)


def _dot_nt(a, b):
    return lax.dot_general(a, b, (((1,), (1,)), ((), ())), preferred_element_type=F32)


def _dot_tn(a, b):
    return lax.dot_general(a, b, (((0,), (0,)), ((), ())), preferred_element_type=F32)


def _head(x, h):
    return x[:, h * HEAD_DIM:(h + 1) * HEAD_DIM]


def _group_norm_heads(o, gn_w):
    parts = [_layernorm_rows(_head(o, h)) for h in range(RET_HEADS)]
    return jnp.concatenate(parts, axis=-1) * gn_w


def _merge_and_project(x, gate_pre, yr, yc, ym, w_br_ret_ref, w_br_conv_ref,
                       w_br_mem_ref, w_out_ref, final_w):
    branches = (
        _dot(yr.astype(BF16), w_br_ret_ref[...]),
        _dot(yc.astype(BF16), w_br_conv_ref[...]),
        _dot(ym.astype(BF16), w_br_mem_ref[...]),
    )
    merged = None
    for i, br in enumerate(branches):
        term = _sigmoid(gate_pre(i)) * br
        merged = term if merged is None else merged + term
    out = x + _dot(merged.astype(BF16), w_out_ref[...])
    return out * _rms_scale(out) * final_w


def _mem_kv_kernel(mem_ref, norm_w_ref, w_ref, k_ref, v_ref, kbf_ref, vbf_ref):
    m = mem_ref[0]
    hm = (m * _rms_scale(m) * norm_w_ref[...]).astype(BF16)
    kv = _dot(hm, w_ref[...])
    k = kv[:, :BRANCH_W]
    v = kv[:, BRANCH_W:]
    n_mem = k.shape[0]
    for h in range(MEM_HEADS):
        k_ref[0, pl.ds(h, n_mem, stride=MEM_HEADS), :] = _head(k, h)
        v_ref[0, pl.ds(h, n_mem, stride=MEM_HEADS), :] = _head(v, h)
    kbf_ref[0] = k.astype(BF16)
    vbf_ref[0] = v.astype(BF16)


def _mem_kv(mem, mem_norm_w, w_mem_kv_bf):
    batch, n_mem, _ = mem.shape
    blk = lambda dt: jax.ShapeDtypeStruct((batch, n_mem, BRANCH_W), dt)
    out_spec = pl.BlockSpec((1, n_mem, BRANCH_W), lambda b: (b, 0, 0))
    rows_shape = (batch, n_mem * MEM_HEADS, HEAD_DIM)
    rows_spec = pl.BlockSpec((1, n_mem * MEM_HEADS, HEAD_DIM), lambda b: (b, 0, 0))
    return pl.pallas_call(
        _mem_kv_kernel,
        grid=(batch,),
        in_specs=[
            pl.BlockSpec((1, n_mem, D_MODEL), lambda b: (b, 0, 0)),
            pl.BlockSpec((1, D_MODEL), lambda b: (0, 0)),
            pl.BlockSpec((D_MODEL, 2 * BRANCH_W), lambda b: (0, 0)),
        ],
        out_specs=[rows_spec, rows_spec, out_spec, out_spec],
        out_shape=[jax.ShapeDtypeStruct(rows_shape, F32), jax.ShapeDtypeStruct(rows_shape, F32),
                   blk(BF16), blk(BF16)],
        compiler_params=pltpu.CompilerParams(dimension_semantics=("arbitrary",)),
        name="mem_kv",
    )(mem, mem_norm_w, w_mem_kv_bf)


def _prompt_kernel(c_dec, tiles_per_seq, n_tiles,
                   x_ref, xb_ref, cos_ref, sin_ref, mk_ref, mv_ref, norm_w_ref, w_in_ref,
                   gn_w_ref, conv_w_ref, conv_b_ref, ln_w_ref, ln_b_ref,
                   w_br_ret_ref, w_br_conv_ref, w_br_mem_ref, w_out_ref, final_w_ref,
                   intra_ref, qdec_ref, kdec_ref,
                   y_ref, s_ref, hist_ref,
                   ubuf_ref, cn_scr, h_scr, yr_scr, yc_scr, ym_scr):
    i = pl.program_id(0)
    t = lax.rem(jnp.minimum(i, n_tiles - 1), tiles_per_seq)
    live = i < n_tiles
    tile = x_ref.shape[1]

    @pl.when(i == 0)
    def _():
        h_scr[...] = jnp.zeros_like(h_scr)
        yr_scr[...] = jnp.zeros_like(yr_scr)
        yc_scr[...] = jnp.zeros_like(yc_scr)
        ym_scr[...] = jnp.zeros_like(ym_scr)

    @pl.when(jnp.logical_and(t == 0, live))
    def _():
        s_ref[...] = jnp.zeros_like(s_ref)
        ubuf_ref[:, 0:HIST_PAD, :] = jnp.zeros((CONV_SLABS, HIST_PAD, HEAD_DIM), F32)

    h_prev = h_scr[...]
    half = D_MODEL // 2

    def gate_job(k):
        lo = OFF_MERGE + k * half
        return lambda: _dot(h_prev, w_in_ref[:, lo:lo + half])

    back_jobs = [gate_job(k) for k in range(2 * N_BRANCH)] + [
        lambda: _dot(yr_scr[...], w_br_ret_ref[...]),
        lambda: _dot(yc_scr[...], w_br_conv_ref[...]),
        lambda: _dot(ym_scr[...], w_br_mem_ref[...]),
    ]
    back_out = [None] * len(back_jobs)

    def run_back(k):
        back_out[k] = back_jobs[k]()

    x = x_ref[0]
    h_bf = (x * _rms_scale(x) * norm_w_ref[...]).astype(BF16)
    run_back(8)

    def proj(off, width=BRANCH_W):
        return _dot(h_bf, w_in_ref[:, off:off + width])

    za = proj(OFF_AC)
    zb = proj(OFF_BC)
    zq = proj(OFF_QR)
    zk = proj(OFF_KR)
    v_bf = proj(OFF_VR).astype(BF16)
    u = za * _sigmoid(zb)
    for s in range(CONV_SLABS):
        ubuf_ref[s, HIST_PAD:HIST_PAD + tile, :] = _head(u, s)
    first = HIST_PAD - CONV_HIST
    lanes = lambda ref, s: ref[:, s * HEAD_DIM:(s + 1) * HEAD_DIM]
    rows_per_phase = tile // CONV_PHASES
    n_windows = CONV_K + CONV_PHASES - 1
    n_blocks = rows_per_phase // SUBLANES
    conv_out = []
    for s in range(CONV_SLABS):
        accs = [[None] * CONV_PHASES for _ in range(n_blocks)]
        for c in range(n_windows):
            taps = {p: jnp.broadcast_to(
                        conv_w_ref[0, c - p:c - p + 1, s * HEAD_DIM:(s + 1) * HEAD_DIM],
                        (SUBLANES, HEAD_DIM))
                    for p in range(CONV_PHASES) if 0 <= c - p < CONV_K}
            for blk in range(n_blocks):
                start = first + c + blk * SUBLANES * CONV_PHASES
                window = ubuf_ref[s, pl.ds(start, SUBLANES, stride=CONV_PHASES), :]
                for p, w in taps.items():
                    term = window * w
                    accs[blk][p] = term if accs[blk][p] is None else accs[blk][p] + term
            if c == n_windows // 2:
                run_back(2 * s)
        run_back(2 * s + 1)
        conv_out.append([jnp.concatenate([accs[blk][p] for blk in range(n_blocks)], axis=0)
                         + lanes(conv_b_ref, s) for p in range(CONV_PHASES)])
    for p in range(CONV_PHASES):
        conv = [conv_out[s][p] for s in range(CONV_SLABS)]
        mu = sum(jnp.sum(c, axis=-1, keepdims=True) for c in conv) * (1.0 / BRANCH_W)
        cent = [c - mu for c in conv]
        var = sum(jnp.sum(c * c, axis=-1, keepdims=True) for c in cent) * (1.0 / BRANCH_W)
        inv = lax.rsqrt(var + EPS)
        for s in range(CONV_SLABS):
            cn = cent[s] * inv * lanes(ln_w_ref, s) + lanes(ln_b_ref, s)
            cn_scr[s, pl.ds(p, rows_per_phase, stride=CONV_PHASES), :] = _silu(cn)
    ubuf_ref[:, 0:HIST_PAD, :] = jnp.where(live, ubuf_ref[:, tile:tile + HIST_PAD, :],
                                           ubuf_ref[:, 0:HIST_PAD, :])

    z_gc = proj(OFF_GC)
    z_gr = proj(OFF_GR)
    z_qm = proj(OFF_QM)
    z_gm = proj(OFF_GM)

    merged = []
    for k in range(2):
        cols = slice(k * half, (k + 1) * half)
        acc = None
        for b in range(N_BRANCH):
            term = _sigmoid(back_out[2 * b + k]) * back_out[2 * N_BRANCH + b][:, cols]
            acc = term if acc is None else acc + term
        merged.append(acc.astype(BF16))
    merged = jnp.concatenate(merged, axis=-1)

    cos2 = cos_ref[...]
    sin2 = sin_ref[...]
    q_scale = HEAD_DIM ** -0.5
    o_heads = []
    for h in range(RET_HEADS):
        q_h = _rotary(_head(zq, h), cos2, sin2) * q_scale
        k_h = _rotary(_head(zk, h), cos2, sin2)
        v_h = _head(v_bf, h)
        s_old = s_ref[0, h]
        s = s_old
        o_chunks = []
        for c in range(tile // RET_CHUNK):
            rows = slice(c * RET_CHUNK, (c + 1) * RET_CHUNK)
            qc = q_h[rows].astype(BF16)
            kc = k_h[rows]
            vc = v_h[rows]
            sc = _dot_nt(qc, kc.astype(BF16)) * intra_ref[h]
            o_chunks.append(_dot(sc.astype(BF16), vc) + _dot(qc, s.astype(BF16)) * qdec_ref[h])
            kd = (kc * kdec_ref[h]).astype(BF16)
            s = s * c_dec[h] + _dot_tn(kd, vc)
        s_ref[0, h] = jnp.where(live, s, s_old)
        o_heads.append(_layernorm_rows(jnp.concatenate(o_chunks, axis=0)))

    out = xb_ref[0] + _dot(merged, w_out_ref[...])

    ret = jnp.concatenate(o_heads, axis=-1) * gn_w_ref[...]
    yr = _silu(z_gr) * ret
    yc = _silu(z_gc) * jnp.concatenate([cn_scr[s] for s in range(CONV_SLABS)], axis=-1)

    qm = z_qm.astype(BF16)
    mk = mk_ref[0]
    mv = mv_ref[0]
    att = []
    for h in range(MEM_HEADS):
        sc = _dot_nt(_head(qm, h), _head(mk, h)) * (HEAD_DIM ** -0.5)
        e = jnp.exp(sc - jnp.max(sc, axis=-1, keepdims=True))
        p = e / jnp.sum(e, axis=-1, keepdims=True)
        att.append(_dot(p.astype(BF16), _head(mv, h)))
    ym = _silu(z_gm) * jnp.concatenate(att, axis=-1)

    y_ref[0] = out * _rms_scale(out) * final_w_ref[...]

    h_scr[...] = h_bf
    yr_scr[...] = yr.astype(BF16)
    yc_scr[...] = yc.astype(BF16)
    ym_scr[...] = ym.astype(BF16)

    @pl.when(jnp.logical_and(t == tiles_per_seq - 1, live))
    def _():
        for s in range(CONV_SLABS):
            hist_ref[0, :, s * HEAD_DIM:(s + 1) * HEAD_DIM] = ubuf_ref[s, first:HIST_PAD, :]


def _const_spec(shape, single_buffer=False):
    zeros = (0,) * len(shape)
    if single_buffer:
        return pl.BlockSpec(shape, lambda i: zeros, pipeline_mode=pl.Buffered(1))
    return pl.BlockSpec(shape, lambda i: zeros)


def _prompt_layer(x, cos2, sin2, mk_bf, mv_bf, norm_w, w_in_bf, gn_w, conv_w, conv_b, ln_w,
                  ln_b, w_br_ret_bf, w_br_conv_bf, w_br_mem_bf, w_out_bf, final_w,
                  intra, qdec, kdec, c_dec):
    batch, seq, _ = x.shape
    n_mem = mk_bf.shape[1]
    tile = PROMPT_TILE
    tiles_per_seq = seq // tile
    n_tiles = batch * tiles_per_seq

    def front(i):
        j = jnp.minimum(i, n_tiles - 1)
        return j // tiles_per_seq, lax.rem(j, tiles_per_seq)

    def back(i):
        j = jnp.maximum(i - 1, 0)
        return j // tiles_per_seq, lax.rem(j, tiles_per_seq)

    row = lambda w: _const_spec((1, w))
    weight = lambda a: _const_spec(a.shape, single_buffer=True)
    table = _const_spec((RET_HEADS, RET_CHUNK, RET_CHUNK))
    x_tile = lambda which: pl.BlockSpec((1, tile, D_MODEL), lambda i: (*which(i), 0))
    per_seq = lambda shape: pl.BlockSpec(
        (1,) + shape, lambda i: (front(i)[0],) + (0,) * len(shape))
    return pl.pallas_call(
        functools.partial(_prompt_kernel, c_dec, tiles_per_seq, n_tiles),
        grid=(n_tiles + 1,),
        in_specs=[
            x_tile(front), x_tile(back),
            pl.BlockSpec((tile, HEAD_DIM), lambda i: (front(i)[1], 0)),
            pl.BlockSpec((tile, HEAD_DIM), lambda i: (front(i)[1], 0)),
            per_seq((n_mem, BRANCH_W)), per_seq((n_mem, BRANCH_W)),
            row(D_MODEL), weight(w_in_bf), row(BRANCH_W),
            _const_spec((1, CONV_K, BRANCH_W)), row(BRANCH_W), row(BRANCH_W), row(BRANCH_W),
            weight(w_br_ret_bf), weight(w_br_conv_bf), weight(w_br_mem_bf), weight(w_out_bf),
            row(D_MODEL), table, table, table,
        ],
        out_specs=[
            x_tile(back),
            per_seq((RET_HEADS, HEAD_DIM, HEAD_DIM)),
            per_seq((CONV_HIST, BRANCH_W)),
        ],
        out_shape=[
            jax.ShapeDtypeStruct((batch, seq, D_MODEL), F32),
            jax.ShapeDtypeStruct((batch, RET_HEADS, HEAD_DIM, HEAD_DIM), F32),
            jax.ShapeDtypeStruct((batch, CONV_HIST, BRANCH_W), F32),
        ],
        scratch_shapes=[
            pltpu.VMEM((CONV_SLABS, HIST_PAD + tile, HEAD_DIM), F32),
            pltpu.VMEM((CONV_SLABS, tile, HEAD_DIM), F32),
            pltpu.VMEM((tile, D_MODEL), BF16),
            pltpu.VMEM((tile, BRANCH_W), BF16),
            pltpu.VMEM((tile, BRANCH_W), BF16),
            pltpu.VMEM((tile, BRANCH_W), BF16),
        ],
        compiler_params=pltpu.CompilerParams(
            dimension_semantics=("arbitrary",), vmem_limit_bytes=VMEM_LIMIT_BYTES),
        name="prompt_layer",
    )(x, x, cos2, sin2, mk_bf, mv_bf, norm_w, w_in_bf, gn_w, conv_w, conv_b, ln_w, ln_b,
      w_br_ret_bf, w_br_conv_bf, w_br_mem_bf, w_out_bf, final_w, intra, qdec, kdec)


SAMPLE_PROJ_BLOCK = 1536


def _sample_proj_kernel(x_ref, norm_w_ref, w_ref, z_ref):
    x = x_ref[...]
    h_bf = (x * _rms_scale(x) * norm_w_ref[...]).astype(BF16)
    z_ref[...] = _dot(h_bf, w_ref[...])


def _sample_proj(xs, norm_w, w_in_bf):
    n = xs.shape[0]
    return pl.pallas_call(
        _sample_proj_kernel,
        grid=(IN_W // SAMPLE_PROJ_BLOCK,),
        in_specs=[
            pl.BlockSpec((n, D_MODEL), lambda j: (0, 0)),
            pl.BlockSpec((1, D_MODEL), lambda j: (0, 0)),
            pl.BlockSpec((D_MODEL, SAMPLE_PROJ_BLOCK), lambda j: (0, j)),
        ],
        out_specs=pl.BlockSpec((n, SAMPLE_PROJ_BLOCK), lambda j: (0, j)),
        out_shape=jax.ShapeDtypeStruct((n, IN_W), F32),
        compiler_params=pltpu.CompilerParams(dimension_semantics=("arbitrary",)),
        name="sample_proj",
    )(xs, norm_w, w_in_bf)


def _sample_state_kernel(gamma, z_ref, cos_ref, sin_ref, s0_ref, hist_ref, ck_ref, cv_ref,
                         conv_w_ref, o_ref, conv_ref, att_ref, s_new_ref, hist_new_ref):
    nb = z_ref.shape[0]
    n_mem = ck_ref.shape[1] // MEM_HEADS
    cos2 = cos_ref[...]
    sin2 = sin_ref[...]
    q_scale = HEAD_DIM ** -0.5
    zq = z_ref[:, OFF_QR:OFF_QR + BRANCH_W]
    zk = z_ref[:, OFF_KR:OFF_KR + BRANCH_W]
    v = z_ref[:, OFF_VR:OFF_VR + BRANCH_W]
    q = jnp.concatenate([_rotary(_head(zq, h), cos2, sin2) * q_scale
                         for h in range(RET_HEADS)], axis=-1)
    k = jnp.concatenate([_rotary(_head(zk, h), cos2, sin2) for h in range(RET_HEADS)], axis=-1)
    u = z_ref[:, OFF_AC:OFF_AC + BRANCH_W] * _sigmoid(z_ref[:, OFF_BC:OFF_BC + BRANCH_W])
    qm = z_ref[:, OFF_QM:OFF_QM + BRANCH_W]

    eye = (lax.broadcasted_iota(jnp.int32, (HEAD_DIM, HEAD_DIM), 0)
           == lax.broadcasted_iota(jnp.int32, (HEAD_DIM, HEAD_DIM), 1))
    ones_bf = jnp.ones((HEAD_DIM, HEAD_DIM), BF16)

    def as_columns(row):
        diag = jnp.where(eye, jnp.broadcast_to(row, (HEAD_DIM, HEAD_DIM)), 0.0)
        return _dot(diag.astype(BF16), ones_bf)

    mem_scale = HEAD_DIM ** -0.5

    acc = u * conv_w_ref[0, CONV_HIST:CONV_K, :]
    for j in range(CONV_HIST):
        acc = acc + hist_ref[j] * conv_w_ref[0, j:j + 1, :]
        hist_new_ref[j] = hist_ref[j + 1] if j + 1 < CONV_HIST else u
    conv_ref[...] = acc

    for b in range(nb):
        for h in range(RET_HEADS):
            q_row = _head(q, h)[b:b + 1]
            k_row = _head(k, h)[b:b + 1]
            v_row = _head(v, h)[b:b + 1]
            s0 = s0_ref[b, h]
            qk = jnp.sum(q_row * k_row, axis=-1, keepdims=True)
            qs = jnp.sum(as_columns(q_row) * s0, axis=0, keepdims=True)
            o_ref[b:b + 1, h * HEAD_DIM:(h + 1) * HEAD_DIM] = qk * v_row + gamma[h] * qs
            s_new_ref[b, h] = gamma[h] * s0 + as_columns(k_row) * v_row

        for h in range(MEM_HEADS):
            head_rows = pl.ds(h, n_mem, stride=MEM_HEADS)
            prod = ck_ref[b, head_rows, :] * _head(qm, h)[b:b + 1]
            sc = jnp.sum(prod, axis=-1, keepdims=True) * mem_scale
            e = jnp.exp(sc - jnp.max(sc, axis=0, keepdims=True))
            p = e / jnp.sum(e, axis=0, keepdims=True)
            att_ref[b:b + 1, h * HEAD_DIM:(h + 1) * HEAD_DIM] = jnp.sum(
                p * cv_ref[b, head_rows, :], axis=0, keepdims=True)


def _sample_state(z, cos2, sin2, s0, hist, cache_k, cache_v, conv_w, gamma):
    n = z.shape[0]
    nb = SAMPLE_BLOCK
    rows = lambda w: pl.BlockSpec((nb, w), lambda i: (i, 0))
    state = pl.BlockSpec((nb, RET_HEADS, HEAD_DIM, HEAD_DIM), lambda i: (i, 0, 0, 0))
    hist_spec = pl.BlockSpec((CONV_HIST, nb, BRANCH_W), lambda i: (0, i, 0))
    cache = pl.BlockSpec((nb,) + cache_k.shape[1:], lambda i: (i, 0, 0))
    return pl.pallas_call(
        functools.partial(_sample_state_kernel, gamma),
        grid=(n // nb,),
        in_specs=[
            rows(IN_W),
            pl.BlockSpec((1, HEAD_DIM), lambda i: (0, 0)),
            pl.BlockSpec((1, HEAD_DIM), lambda i: (0, 0)),
            state, hist_spec, cache, cache,
            pl.BlockSpec((1, CONV_K, BRANCH_W), lambda i: (0, 0, 0)),
        ],
        out_specs=[rows(BRANCH_W), rows(BRANCH_W), rows(BRANCH_W), state, hist_spec],
        out_shape=[
            jax.ShapeDtypeStruct((n, BRANCH_W), F32),
            jax.ShapeDtypeStruct((n, BRANCH_W), F32),
            jax.ShapeDtypeStruct((n, BRANCH_W), F32),
            jax.ShapeDtypeStruct(s0.shape, F32),
            jax.ShapeDtypeStruct(hist.shape, F32),
        ],
        compiler_params=pltpu.CompilerParams(
            dimension_semantics=("arbitrary",), vmem_limit_bytes=VMEM_LIMIT_BYTES),
        name="sample_state",
    )(z, cos2, sin2, s0, hist, cache_k, cache_v, conv_w)


def _sample_out_kernel(x_ref, z_ref, o_ref, conv_ref, att_ref, gn_w_ref, conv_b_ref, ln_w_ref,
                       ln_b_ref, w_br_ret_ref, w_br_conv_ref, w_br_mem_ref, w_out_ref,
                       final_w_ref, y_ref):
    gate = lambda off: _silu(z_ref[:, off:off + BRANCH_W])
    yr = gate(OFF_GR) * _group_norm_heads(o_ref[...], gn_w_ref[...])
    cn = _layernorm_rows(conv_ref[...] + conv_b_ref[...]) * ln_w_ref[...] + ln_b_ref[...]
    yc = gate(OFF_GC) * _silu(cn)
    ym = gate(OFF_GM) * att_ref[...]
    gate_pre = lambda i: z_ref[:, OFF_MERGE + i * D_MODEL:OFF_MERGE + (i + 1) * D_MODEL]
    y_ref[...] = _merge_and_project(x_ref[...], gate_pre, yr, yc, ym, w_br_ret_ref,
                                    w_br_conv_ref, w_br_mem_ref, w_out_ref, final_w_ref[...])


def _sample_out(xs, z, o, conv, att, gn_w, conv_b, ln_w, ln_b,
                w_br_ret_bf, w_br_conv_bf, w_br_mem_bf, w_out_bf, final_w):
    return pl.pallas_call(
        _sample_out_kernel,
        out_shape=jax.ShapeDtypeStruct(xs.shape, F32),
        compiler_params=pltpu.CompilerParams(vmem_limit_bytes=VMEM_LIMIT_BYTES),
        name="sample_out",
    )(xs, z, o, conv, att, gn_w, conv_b, ln_w, ln_b,
      w_br_ret_bf, w_br_conv_bf, w_br_mem_bf, w_out_bf, final_w)


def _rotary_tables(pos):
    half = HEAD_DIM // 2
    inv = ROPE_BASE ** (-jnp.arange(half, dtype=F32) / half)
    ang = pos[:, None] * inv[None, :]
    cos, sin = jnp.cos(ang), jnp.sin(ang)
    return jnp.concatenate([cos, cos], axis=-1), jnp.concatenate([-sin, sin], axis=-1)


def _decay_tables():
    c = RET_CHUNK
    log_g = jnp.log1p(-jnp.exp2(-5.0 - jnp.arange(RET_HEADS, dtype=F32)))
    idx = jnp.arange(c, dtype=F32)
    diff = idx[:, None] - idx[None, :]
    intra = jnp.where(diff >= 0,
                      jnp.exp(jnp.maximum(diff, 0.0)[None] * log_g[:, None, None]), 0.0)
    q_dec = jnp.exp((idx + 1.0)[None, :] * log_g[:, None])
    k_dec = jnp.exp((c - 1.0 - idx)[None, :] * log_g[:, None])
    bcast = lambda a: jnp.broadcast_to(a[:, :, None], (RET_HEADS, c, HEAD_DIM))
    return intra, bcast(q_dec), bcast(k_dec)


def _gamma_powers(n):
    return tuple(float(np.exp(np.log1p(-np.exp2(-5.0 - h)) * n)) for h in range(RET_HEADS))


def kernel(x_prompt, x_sample, mem_prompt, state_ret, state_conv, cache_mem_k, cache_mem_v,
           norm_w, w_in, ret_gn_w, conv_w, conv_b, conv_ln_w, conv_ln_b, mem_norm_w,
           w_mem_kv, w_br_ret, w_br_conv, w_br_mem, w_out, final_norm_w):
    depth = w_in.shape[0]
    assert depth == 1, "single-layer step"
    batch, seq, _ = x_prompt.shape
    n_dec, dec_seq, _ = x_sample.shape
    assert dec_seq == 1 and seq % PROMPT_TILE == 0 and n_dec % SAMPLE_BLOCK == 0
    n_mem = mem_prompt.shape[1]

    bf = lambda w: w[0].astype(BF16)
    w_in_bf, w_kv_bf = bf(w_in), bf(w_mem_kv)
    w_br_ret_bf, w_br_conv_bf, w_br_mem_bf, w_out_bf = bf(w_br_ret), bf(w_br_conv), bf(w_br_mem), bf(w_out)
    final_w = final_norm_w[None, :]

    intra, qdec, kdec = _decay_tables()
    cos_p, sin_p = _rotary_tables(jnp.arange(seq, dtype=F32))
    cos_s, sin_s = _rotary_tables(jnp.arange(dec_seq, dtype=F32) + PAST_LEN)

    mk, mv, mk_bf, mv_bf = _mem_kv(mem_prompt, mem_norm_w, w_kv_bf)

    y_prompt, s_prompt, hist_prompt = _prompt_layer(
        x_prompt, cos_p, sin_p, mk_bf, mv_bf, norm_w, w_in_bf, ret_gn_w, conv_w, conv_b,
        conv_ln_w, conv_ln_b, w_br_ret_bf, w_br_conv_bf, w_br_mem_bf, w_out_bf, final_w,
        intra, qdec, kdec, _gamma_powers(RET_CHUNK))

    xs = x_sample[:, 0, :]
    z_s = _sample_proj(xs, norm_w, w_in_bf)
    o_s, conv_s, att_s, s_sample, hist_sample = _sample_state(
        z_s, cos_s, sin_s, state_ret[0], jnp.transpose(state_conv[0], (1, 0, 2)),
        cache_mem_k.reshape(n_dec, n_mem * MEM_HEADS, HEAD_DIM),
        cache_mem_v.reshape(n_dec, n_mem * MEM_HEADS, HEAD_DIM), conv_w, _gamma_powers(1))
    y_sample = _sample_out(xs, z_s, o_s, conv_s, att_s, ret_gn_w, conv_b, conv_ln_w, conv_ln_b,
                           w_br_ret_bf, w_br_conv_bf, w_br_mem_bf, w_out_bf, final_w)

    heads = lambda a: a.reshape(1, batch, n_mem, MEM_HEADS, HEAD_DIM)
    return (y_prompt, y_sample[:, None, :], s_prompt[None], s_sample[None],
            hist_prompt[None], jnp.transpose(hist_sample, (1, 0, 2))[None], heads(mk), heads(mv))
```

```python
import functools

import numpy as np
import jax
import jax.numpy as jnp
from jax import lax
from jax.experimental import pallas as pl
from jax.experimental.pallas import tpu as pltpu

F32 = jnp.float32
BF16 = jnp.bfloat16

D_MODEL = 1024
RET_HEADS = 4
HEAD_DIM = 128
BRANCH_W = 512
RET_CHUNK = 128
ROPE_BASE = 10000.0
CONV_K = 31
CONV_HIST = CONV_K - 1
MEM_HEADS = 4
N_BRANCH = 3
EPS = 1e-6
PAST_LEN = 16384

OFF_QR, OFF_KR, OFF_VR, OFF_GR = 0, 512, 1024, 1536
OFF_AC, OFF_BC, OFF_GC = 2048, 2560, 3072
OFF_QM, OFF_GM = 3584, 4096
OFF_MERGE = 4608
IN_W = OFF_MERGE + N_BRANCH * D_MODEL

V7X_VMEM_BYTES = 64 * 1024 * 1024
VMEM_LIMIT_BYTES = V7X_VMEM_BYTES - 8 * 1024 * 1024

PROMPT_TILE = 256
HIST_PAD = 32
SAMPLE_BLOCK = 8


def _sigmoid(x):
    return 0.5 * jnp.tanh(0.5 * x) + 0.5


def _silu(x):
    hx = 0.5 * x
    return hx * (jnp.tanh(hx) + 1.0)


def _rms_scale(x):
    return lax.rsqrt(jnp.mean(x * x, axis=-1, keepdims=True) + EPS)


def _layernorm_rows(x):
    mu = jnp.mean(x, axis=-1, keepdims=True)
    xc = x - mu
    var = jnp.mean(xc * xc, axis=-1, keepdims=True)
    return xc * lax.rsqrt(var + EPS)


def _rotary(x, cos2, sin2):
    return x * cos2 + pltpu.roll(x, HEAD_DIM // 2, axis=1) * sin2


def _dot(a, b):
    return jnp.dot(a, b, preferred_element_type=F32)


def _dot_nt(a, b):
    return lax.dot_general(a, b, (((1,), (1,)), ((), ())), preferred_element_type=F32)


def _dot_tn(a, b):
    return lax.dot_general(a, b, (((0,), (0,)), ((), ())), preferred_element_type=F32)


def _head(x, h):
    return x[:, h * HEAD_DIM:(h + 1) * HEAD_DIM]


def _group_norm_heads(o, gn_w):
    parts = [_layernorm_rows(_head(o, h)) for h in range(RET_HEADS)]
    return jnp.concatenate(parts, axis=-1) * gn_w


def _merge_and_project(x, gate_pre, yr, yc, ym, w_br_ret_ref, w_br_conv_ref,
                       w_br_mem_ref, w_out_ref, final_w):
    branches = (
        _dot(yr.astype(BF16), w_br_ret_ref[...]),
        _dot(yc.astype(BF16), w_br_conv_ref[...]),
        _dot(ym.astype(BF16), w_br_mem_ref[...]),
    )
    merged = None
    for i, br in enumerate(branches):
        term = _sigmoid(gate_pre(i)) * br
        merged = term if merged is None else merged + term
    out = x + _dot(merged.astype(BF16), w_out_ref[...])
    return out * _rms_scale(out) * final_w


def _mem_kv_kernel(mem_ref, norm_w_ref, w_ref, k_ref, v_ref, kbf_ref, vbf_ref):
    m = mem_ref[0]
    hm = (m * _rms_scale(m) * norm_w_ref[...]).astype(BF16)
    kv = _dot(hm, w_ref[...])
    k = kv[:, :BRANCH_W]
    v = kv[:, BRANCH_W:]
    n_mem = k.shape[0]
    for h in range(MEM_HEADS):
        k_ref[0, pl.ds(h, n_mem, stride=MEM_HEADS), :] = _head(k, h)
        v_ref[0, pl.ds(h, n_mem, stride=MEM_HEADS), :] = _head(v, h)
    kbf_ref[0] = k.astype(BF16)
    vbf_ref[0] = v.astype(BF16)


def _mem_kv(mem, mem_norm_w, w_mem_kv_bf):
    batch, n_mem, _ = mem.shape
    blk = lambda dt: jax.ShapeDtypeStruct((batch, n_mem, BRANCH_W), dt)
    out_spec = pl.BlockSpec((1, n_mem, BRANCH_W), lambda b: (b, 0, 0))
    rows_shape = (batch, n_mem * MEM_HEADS, HEAD_DIM)
    rows_spec = pl.BlockSpec((1, n_mem * MEM_HEADS, HEAD_DIM), lambda b: (b, 0, 0))
    return pl.pallas_call(
        _mem_kv_kernel,
        grid=(batch,),
        in_specs=[
            pl.BlockSpec((1, n_mem, D_MODEL), lambda b: (b, 0, 0)),
            pl.BlockSpec((1, D_MODEL), lambda b: (0, 0)),
            pl.BlockSpec((D_MODEL, 2 * BRANCH_W), lambda b: (0, 0)),
        ],
        out_specs=[rows_spec, rows_spec, out_spec, out_spec],
        out_shape=[jax.ShapeDtypeStruct(rows_shape, F32), jax.ShapeDtypeStruct(rows_shape, F32),
                   blk(BF16), blk(BF16)],
        compiler_params=pltpu.CompilerParams(dimension_semantics=("arbitrary",)),
        name="mem_kv",
    )(mem, mem_norm_w, w_mem_kv_bf)


def _prompt_kernel(c_dec, tiles_per_seq, n_tiles,
                   x_ref, xb_ref, cos_ref, sin_ref, mk_ref, mv_ref, norm_w_ref, w_in_ref,
                   gn_w_ref, conv_w_ref, conv_b_ref, ln_w_ref, ln_b_ref,
                   w_br_ret_ref, w_br_conv_ref, w_br_mem_ref, w_out_ref, final_w_ref,
                   intra_ref, qdec_ref, kdec_ref,
                   y_ref, s_ref, hist_ref,
                   ubuf_ref, h_scr, yr_scr, yc_scr, ym_scr):
    i = pl.program_id(0)
    t = lax.rem(jnp.minimum(i, n_tiles - 1), tiles_per_seq)
    live = i < n_tiles
    tile = x_ref.shape[1]

    @pl.when(i == 0)
    def _():
        h_scr[...] = jnp.zeros_like(h_scr)
        yr_scr[...] = jnp.zeros_like(yr_scr)
        yc_scr[...] = jnp.zeros_like(yc_scr)
        ym_scr[...] = jnp.zeros_like(ym_scr)

    @pl.when(jnp.logical_and(t == 0, live))
    def _():
        s_ref[...] = jnp.zeros_like(s_ref)
        ubuf_ref[0:HIST_PAD, :] = jnp.zeros((HIST_PAD, BRANCH_W), F32)

    h_prev = h_scr[...]
    half = D_MODEL // 2

    def gate_job(k):
        lo = OFF_MERGE + k * half
        return lambda: _dot(h_prev, w_in_ref[:, lo:lo + half])

    back_jobs = [gate_job(k) for k in range(2 * N_BRANCH)] + [
        lambda: _dot(yr_scr[...], w_br_ret_ref[...]),
        lambda: _dot(yc_scr[...], w_br_conv_ref[...]),
        lambda: _dot(ym_scr[...], w_br_mem_ref[...]),
    ]
    back_out = [None] * len(back_jobs)

    def run_back(k):
        back_out[k] = back_jobs[k]()

    x = x_ref[0]
    h_bf = (x * _rms_scale(x) * norm_w_ref[...]).astype(BF16)
    run_back(8)

    def proj(off, width=BRANCH_W):
        return _dot(h_bf, w_in_ref[:, off:off + width])

    za = proj(OFF_AC)
    zb = proj(OFF_BC)
    zq = proj(OFF_QR)
    zk = proj(OFF_KR)
    v_bf = proj(OFF_VR).astype(BF16)
    u = za * _sigmoid(zb)
    ubuf_ref[HIST_PAD:HIST_PAD + tile, :] = u
    first = HIST_PAD - CONV_HIST
    conv = None
    for r in range(8):
        part = None
        for j in range(CONV_K):
            if (first + j) % 8 != r:
                continue
            term = ubuf_ref[first + j:first + j + tile, :] * conv_w_ref[0, j:j + 1, :]
            part = term if part is None else part + term
        conv = part if conv is None else conv + part
        run_back(r)
    conv = conv + conv_b_ref[...]
    ubuf_ref[0:HIST_PAD, :] = jnp.where(live, ubuf_ref[tile:tile + HIST_PAD, :],
                                        ubuf_ref[0:HIST_PAD, :])

    z_gc = proj(OFF_GC)
    z_gr = proj(OFF_GR)
    z_qm = proj(OFF_QM)
    z_gm = proj(OFF_GM)

    merged = []
    for k in range(2):
        cols = slice(k * half, (k + 1) * half)
        acc = None
        for b in range(N_BRANCH):
            term = _sigmoid(back_out[2 * b + k]) * back_out[2 * N_BRANCH + b][:, cols]
            acc = term if acc is None else acc + term
        merged.append(acc.astype(BF16))
    merged = jnp.concatenate(merged, axis=-1)

    cos2 = cos_ref[...]
    sin2 = sin_ref[...]
    q_scale = HEAD_DIM ** -0.5
    o_heads = []
    for h in range(RET_HEADS):
        q_h = _rotary(_head(zq, h), cos2, sin2) * q_scale
        k_h = _rotary(_head(zk, h), cos2, sin2)
        v_h = _head(v_bf, h)
        s_old = s_ref[0, h]
        s = s_old
        o_chunks = []
        for c in range(tile // RET_CHUNK):
            rows = slice(c * RET_CHUNK, (c + 1) * RET_CHUNK)
            qc = q_h[rows].astype(BF16)
            kc = k_h[rows]
            vc = v_h[rows]
            sc = _dot_nt(qc, kc.astype(BF16)) * intra_ref[h]
            o_chunks.append(_dot(sc.astype(BF16), vc) + _dot(qc, s.astype(BF16)) * qdec_ref[h])
            kd = (kc * kdec_ref[h]).astype(BF16)
            s = s * c_dec[h] + _dot_tn(kd, vc)
        s_ref[0, h] = jnp.where(live, s, s_old)
        o_heads.append(_layernorm_rows(jnp.concatenate(o_chunks, axis=0)))

    out = xb_ref[0] + _dot(merged, w_out_ref[...])

    ret = jnp.concatenate(o_heads, axis=-1) * gn_w_ref[...]
    yr = _silu(z_gr) * ret
    cn = _layernorm_rows(conv) * ln_w_ref[...] + ln_b_ref[...]
    yc = _silu(z_gc) * _silu(cn)

    qm = z_qm.astype(BF16)
    mk = mk_ref[0]
    mv = mv_ref[0]
    att = []
    for h in range(MEM_HEADS):
        sc = _dot_nt(_head(qm, h), _head(mk, h)) * (HEAD_DIM ** -0.5)
        e = jnp.exp(sc - jnp.max(sc, axis=-1, keepdims=True))
        p = e / jnp.sum(e, axis=-1, keepdims=True)
        att.append(_dot(p.astype(BF16), _head(mv, h)))
    ym = _silu(z_gm) * jnp.concatenate(att, axis=-1)

    y_ref[0] = out * _rms_scale(out) * final_w_ref[...]

    h_scr[...] = h_bf
    yr_scr[...] = yr.astype(BF16)
    yc_scr[...] = yc.astype(BF16)
    ym_scr[...] = ym.astype(BF16)

    @pl.when(jnp.logical_and(t == tiles_per_seq - 1, live))
    def _():
        hist_ref[0] = ubuf_ref[first:HIST_PAD, :]


def _const_spec(shape, single_buffer=False):
    zeros = (0,) * len(shape)
    if single_buffer:
        return pl.BlockSpec(shape, lambda i: zeros, pipeline_mode=pl.Buffered(1))
    return pl.BlockSpec(shape, lambda i: zeros)


def _prompt_layer(x, cos2, sin2, mk_bf, mv_bf, norm_w, w_in_bf, gn_w, conv_w, conv_b, ln_w,
                  ln_b, w_br_ret_bf, w_br_conv_bf, w_br_mem_bf, w_out_bf, final_w,
                  intra, qdec, kdec, c_dec):
    batch, seq, _ = x.shape
    n_mem = mk_bf.shape[1]
    tile = PROMPT_TILE
    tiles_per_seq = seq // tile
    n_tiles = batch * tiles_per_seq

    def front(i):
        j = jnp.minimum(i, n_tiles - 1)
        return j // tiles_per_seq, lax.rem(j, tiles_per_seq)

    def back(i):
        j = jnp.maximum(i - 1, 0)
        return j // tiles_per_seq, lax.rem(j, tiles_per_seq)

    row = lambda w: _const_spec((1, w))
    weight = lambda a: _const_spec(a.shape, single_buffer=True)
    table = _const_spec((RET_HEADS, RET_CHUNK, RET_CHUNK))
    x_tile = lambda which: pl.BlockSpec((1, tile, D_MODEL), lambda i: (*which(i), 0))
    per_seq = lambda shape: pl.BlockSpec(
        (1,) + shape, lambda i: (front(i)[0],) + (0,) * len(shape))
    return pl.pallas_call(
        functools.partial(_prompt_kernel, c_dec, tiles_per_seq, n_tiles),
        grid=(n_tiles + 1,),
        in_specs=[
            x_tile(front), x_tile(back),
            pl.BlockSpec((tile, HEAD_DIM), lambda i: (front(i)[1], 0)),
            pl.BlockSpec((tile, HEAD_DIM), lambda i: (front(i)[1], 0)),
            per_seq((n_mem, BRANCH_W)), per_seq((n_mem, BRANCH_W)),
            row(D_MODEL), weight(w_in_bf), row(BRANCH_W),
            _const_spec((1, CONV_K, BRANCH_W)), row(BRANCH_W), row(BRANCH_W), row(BRANCH_W),
            weight(w_br_ret_bf), weight(w_br_conv_bf), weight(w_br_mem_bf), weight(w_out_bf),
            row(D_MODEL), table, table, table,
        ],
        out_specs=[
            x_tile(back),
            per_seq((RET_HEADS, HEAD_DIM, HEAD_DIM)),
            per_seq((CONV_HIST, BRANCH_W)),
        ],
        out_shape=[
            jax.ShapeDtypeStruct((batch, seq, D_MODEL), F32),
            jax.ShapeDtypeStruct((batch, RET_HEADS, HEAD_DIM, HEAD_DIM), F32),
            jax.ShapeDtypeStruct((batch, CONV_HIST, BRANCH_W), F32),
        ],
        scratch_shapes=[
            pltpu.VMEM((HIST_PAD + tile, BRANCH_W), F32),
            pltpu.VMEM((tile, D_MODEL), BF16),
            pltpu.VMEM((tile, BRANCH_W), BF16),
            pltpu.VMEM((tile, BRANCH_W), BF16),
            pltpu.VMEM((tile, BRANCH_W), BF16),
        ],
        compiler_params=pltpu.CompilerParams(
            dimension_semantics=("arbitrary",), vmem_limit_bytes=VMEM_LIMIT_BYTES),
        name="prompt_layer",
    )(x, x, cos2, sin2, mk_bf, mv_bf, norm_w, w_in_bf, gn_w, conv_w, conv_b, ln_w, ln_b,
      w_br_ret_bf, w_br_conv_bf, w_br_mem_bf, w_out_bf, final_w, intra, qdec, kdec)


SAMPLE_PROJ_BLOCK = 1536


def _sample_proj_kernel(x_ref, norm_w_ref, w_ref, z_ref, w_bf_ref):
    x = x_ref[...]
    h_bf = (x * _rms_scale(x) * norm_w_ref[...]).astype(BF16)
    w_bf = w_ref[0].astype(BF16)
    w_bf_ref[...] = w_bf
    z_ref[...] = _dot(h_bf, w_bf)


def _sample_proj(xs, norm_w, w_in):
    n = xs.shape[0]
    return pl.pallas_call(
        _sample_proj_kernel,
        grid=(IN_W // SAMPLE_PROJ_BLOCK,),
        in_specs=[
            pl.BlockSpec((n, D_MODEL), lambda j: (0, 0)),
            pl.BlockSpec((1, D_MODEL), lambda j: (0, 0)),
            pl.BlockSpec((1, D_MODEL, SAMPLE_PROJ_BLOCK), lambda j: (0, 0, j)),
        ],
        out_specs=[pl.BlockSpec((n, SAMPLE_PROJ_BLOCK), lambda j: (0, j)),
                   pl.BlockSpec((D_MODEL, SAMPLE_PROJ_BLOCK), lambda j: (0, j))],
        out_shape=[jax.ShapeDtypeStruct((n, IN_W), F32),
                   jax.ShapeDtypeStruct((D_MODEL, IN_W), BF16)],
        compiler_params=pltpu.CompilerParams(
            dimension_semantics=("arbitrary",), vmem_limit_bytes=VMEM_LIMIT_BYTES),
        name="sample_proj",
    )(xs, norm_w, w_in)


def _sample_state_kernel(gamma, z_ref, cos_ref, sin_ref, s0_ref, hist_ref, ck_ref, cv_ref,
                         conv_w_ref, o_ref, conv_ref, att_ref, s_new_ref, hist_new_ref):
    nb = z_ref.shape[0]
    n_mem = ck_ref.shape[1] // MEM_HEADS
    cos2 = cos_ref[...]
    sin2 = sin_ref[...]
    q_scale = HEAD_DIM ** -0.5
    zq = z_ref[:, OFF_QR:OFF_QR + BRANCH_W]
    zk = z_ref[:, OFF_KR:OFF_KR + BRANCH_W]
    v = z_ref[:, OFF_VR:OFF_VR + BRANCH_W]
    q = jnp.concatenate([_rotary(_head(zq, h), cos2, sin2) * q_scale
                         for h in range(RET_HEADS)], axis=-1)
    k = jnp.concatenate([_rotary(_head(zk, h), cos2, sin2) for h in range(RET_HEADS)], axis=-1)
    u = z_ref[:, OFF_AC:OFF_AC + BRANCH_W] * _sigmoid(z_ref[:, OFF_BC:OFF_BC + BRANCH_W])
    qm = z_ref[:, OFF_QM:OFF_QM + BRANCH_W]

    eye = (lax.broadcasted_iota(jnp.int32, (HEAD_DIM, HEAD_DIM), 0)
           == lax.broadcasted_iota(jnp.int32, (HEAD_DIM, HEAD_DIM), 1))
    ones_bf = jnp.ones((HEAD_DIM, HEAD_DIM), BF16)

    def as_columns(row):
        diag = jnp.where(eye, jnp.broadcast_to(row, (HEAD_DIM, HEAD_DIM)), 0.0)
        return _dot(diag.astype(BF16), ones_bf)

    mem_scale = HEAD_DIM ** -0.5

    acc = u * conv_w_ref[0, CONV_HIST:CONV_K, :]
    for j in range(CONV_HIST):
        acc = acc + hist_ref[j] * conv_w_ref[0, j:j + 1, :]
        hist_new_ref[j] = hist_ref[j + 1] if j + 1 < CONV_HIST else u
    conv_ref[...] = acc

    for b in range(nb):
        for h in range(RET_HEADS):
            q_row = _head(q, h)[b:b + 1]
            k_row = _head(k, h)[b:b + 1]
            v_row = _head(v, h)[b:b + 1]
            s0 = s0_ref[b, h]
            qk = jnp.sum(q_row * k_row, axis=-1, keepdims=True)
            qs = jnp.sum(as_columns(q_row) * s0, axis=0, keepdims=True)
            o_ref[b:b + 1, h * HEAD_DIM:(h + 1) * HEAD_DIM] = qk * v_row + gamma[h] * qs
            s_new_ref[b, h] = gamma[h] * s0 + as_columns(k_row) * v_row

        for h in range(MEM_HEADS):
            head_rows = pl.ds(h, n_mem, stride=MEM_HEADS)
            prod = ck_ref[b, head_rows, :] * _head(qm, h)[b:b + 1]
            sc = jnp.sum(prod, axis=-1, keepdims=True) * mem_scale
            e = jnp.exp(sc - jnp.max(sc, axis=0, keepdims=True))
            p = e / jnp.sum(e, axis=0, keepdims=True)
            att_ref[b:b + 1, h * HEAD_DIM:(h + 1) * HEAD_DIM] = jnp.sum(
                p * cv_ref[b, head_rows, :], axis=0, keepdims=True)


def _sample_state(z, cos2, sin2, s0, hist, cache_k, cache_v, conv_w, gamma):
    n = z.shape[0]
    nb = SAMPLE_BLOCK
    rows = lambda w: pl.BlockSpec((nb, w), lambda i: (i, 0))
    state = pl.BlockSpec((nb, RET_HEADS, HEAD_DIM, HEAD_DIM), lambda i: (i, 0, 0, 0))
    hist_spec = pl.BlockSpec((CONV_HIST, nb, BRANCH_W), lambda i: (0, i, 0))
    cache = pl.BlockSpec((nb,) + cache_k.shape[1:], lambda i: (i, 0, 0))
    return pl.pallas_call(
        functools.partial(_sample_state_kernel, gamma),
        grid=(n // nb,),
        in_specs=[
            rows(IN_W),
            pl.BlockSpec((1, HEAD_DIM), lambda i: (0, 0)),
            pl.BlockSpec((1, HEAD_DIM), lambda i: (0, 0)),
            state, hist_spec, cache, cache,
            pl.BlockSpec((1, CONV_K, BRANCH_W), lambda i: (0, 0, 0)),
        ],
        out_specs=[rows(BRANCH_W), rows(BRANCH_W), rows(BRANCH_W), state, hist_spec],
        out_shape=[
            jax.ShapeDtypeStruct((n, BRANCH_W), F32),
            jax.ShapeDtypeStruct((n, BRANCH_W), F32),
            jax.ShapeDtypeStruct((n, BRANCH_W), F32),
            jax.ShapeDtypeStruct(s0.shape, F32),
            jax.ShapeDtypeStruct(hist.shape, F32),
        ],
        compiler_params=pltpu.CompilerParams(
            dimension_semantics=("arbitrary",), vmem_limit_bytes=VMEM_LIMIT_BYTES),
        name="sample_state",
    )(z, cos2, sin2, s0, hist, cache_k, cache_v, conv_w)


def _sample_out_kernel(x_ref, z_ref, o_ref, conv_ref, att_ref, gn_w_ref, conv_b_ref, ln_w_ref,
                       ln_b_ref, w_br_ret_ref, w_br_conv_ref, w_br_mem_ref, w_out_ref,
                       final_w_ref, y_ref):
    gate = lambda off: _silu(z_ref[:, off:off + BRANCH_W])
    yr = gate(OFF_GR) * _group_norm_heads(o_ref[...], gn_w_ref[...])
    cn = _layernorm_rows(conv_ref[...] + conv_b_ref[...]) * ln_w_ref[...] + ln_b_ref[...]
    yc = gate(OFF_GC) * _silu(cn)
    ym = gate(OFF_GM) * att_ref[...]
    gate_pre = lambda i: z_ref[:, OFF_MERGE + i * D_MODEL:OFF_MERGE + (i + 1) * D_MODEL]
    y_ref[...] = _merge_and_project(x_ref[...], gate_pre, yr, yc, ym, w_br_ret_ref,
                                    w_br_conv_ref, w_br_mem_ref, w_out_ref, final_w_ref[...])


def _sample_out(xs, z, o, conv, att, gn_w, conv_b, ln_w, ln_b,
                w_br_ret_bf, w_br_conv_bf, w_br_mem_bf, w_out_bf, final_w):
    return pl.pallas_call(
        _sample_out_kernel,
        out_shape=jax.ShapeDtypeStruct(xs.shape, F32),
        compiler_params=pltpu.CompilerParams(vmem_limit_bytes=VMEM_LIMIT_BYTES),
        name="sample_out",
    )(xs, z, o, conv, att, gn_w, conv_b, ln_w, ln_b,
      w_br_ret_bf, w_br_conv_bf, w_br_mem_bf, w_out_bf, final_w)


def _rotary_tables(pos):
    half = HEAD_DIM // 2
    inv = ROPE_BASE ** (-jnp.arange(half, dtype=F32) / half)
    ang = pos[:, None] * inv[None, :]
    cos, sin = jnp.cos(ang), jnp.sin(ang)
    return jnp.concatenate([cos, cos], axis=-1), jnp.concatenate([-sin, sin], axis=-1)


def _decay_tables():
    c = RET_CHUNK
    log_g = jnp.log1p(-jnp.exp2(-5.0 - jnp.arange(RET_HEADS, dtype=F32)))
    idx = jnp.arange(c, dtype=F32)
    diff = idx[:, None] - idx[None, :]
    intra = jnp.where(diff >= 0,
                      jnp.exp(jnp.maximum(diff, 0.0)[None] * log_g[:, None, None]), 0.0)
    q_dec = jnp.exp((idx + 1.0)[None, :] * log_g[:, None])
    k_dec = jnp.exp((c - 1.0 - idx)[None, :] * log_g[:, None])
    bcast = lambda a: jnp.broadcast_to(a[:, :, None], (RET_HEADS, c, HEAD_DIM))
    return intra, bcast(q_dec), bcast(k_dec)


def _gamma_powers(n):
    return tuple(float(np.exp(np.log1p(-np.exp2(-5.0 - h)) * n)) for h in range(RET_HEADS))


def kernel(x_prompt, x_sample, mem_prompt, state_ret, state_conv, cache_mem_k, cache_mem_v,
           norm_w, w_in, ret_gn_w, conv_w, conv_b, conv_ln_w, conv_ln_b, mem_norm_w,
           w_mem_kv, w_br_ret, w_br_conv, w_br_mem, w_out, final_norm_w):
    depth = w_in.shape[0]
    assert depth == 1, "single-layer step"
    batch, seq, _ = x_prompt.shape
    n_dec, dec_seq, _ = x_sample.shape
    assert dec_seq == 1 and seq % PROMPT_TILE == 0 and n_dec % SAMPLE_BLOCK == 0
    n_mem = mem_prompt.shape[1]

    bf = lambda w: w[0].astype(BF16)
    w_kv_bf = bf(w_mem_kv)
    w_br_ret_bf, w_br_conv_bf, w_br_mem_bf, w_out_bf = bf(w_br_ret), bf(w_br_conv), bf(w_br_mem), bf(w_out)
    final_w = final_norm_w[None, :]

    intra, qdec, kdec = _decay_tables()
    cos_p, sin_p = _rotary_tables(jnp.arange(seq, dtype=F32))
    cos_s, sin_s = _rotary_tables(jnp.arange(dec_seq, dtype=F32) + PAST_LEN)

    xs = x_sample[:, 0, :]
    z_s, w_in_bf = _sample_proj(xs, norm_w, w_in)

    mk, mv, mk_bf, mv_bf = _mem_kv(mem_prompt, mem_norm_w, w_kv_bf)

    y_prompt, s_prompt, hist_prompt = _prompt_layer(
        x_prompt, cos_p, sin_p, mk_bf, mv_bf, norm_w, w_in_bf, ret_gn_w, conv_w, conv_b,
        conv_ln_w, conv_ln_b, w_br_ret_bf, w_br_conv_bf, w_br_mem_bf, w_out_bf, final_w,
        intra, qdec, kdec, _gamma_powers(RET_CHUNK))

    o_s, conv_s, att_s, s_sample, hist_sample = _sample_state(
        z_s, cos_s, sin_s, state_ret[0], jnp.transpose(state_conv[0], (1, 0, 2)),
        cache_mem_k.reshape(n_dec, n_mem * MEM_HEADS, HEAD_DIM),
        cache_mem_v.reshape(n_dec, n_mem * MEM_HEADS, HEAD_DIM), conv_w, _gamma_powers(1))
    y_sample = _sample_out(xs, z_s, o_s, conv_s, att_s, ret_gn_w, conv_b, conv_ln_w, conv_ln_b,
                           w_br_ret_bf, w_br_conv_bf, w_br_mem_bf, w_out_bf, final_w)

    heads = lambda a: a.reshape(1, batch, n_mem, MEM_HEADS, HEAD_DIM)
    return (y_prompt, y_sample[:, None, :], s_prompt[None], s_sample[None],
            hist_prompt[None], jnp.transpose(hist_sample, (1, 0, 2))[None], heads(mk), heads(mv))
```

```python
import functools

import numpy as np
import jax
import jax.numpy as jnp
from jax import lax
from jax.experimental import pallas as pl
from jax.experimental.pallas import tpu as pltpu

F32 = jnp.float32
BF16 = jnp.bfloat16

D_MODEL = 1024
RET_HEADS = 4
HEAD_DIM = 128
BRANCH_W = 512
RET_CHUNK = 128
ROPE_BASE = 10000.0
CONV_K = 31
CONV_HIST = CONV_K - 1
MEM_HEADS = 4
N_BRANCH = 3
EPS = 1e-6
PAST_LEN = 16384

OFF_QR, OFF_KR, OFF_VR, OFF_GR = 0, 512, 1024, 1536
OFF_AC, OFF_BC, OFF_GC = 2048, 2560, 3072
OFF_QM, OFF_GM = 3584, 4096
OFF_MERGE = 4608
IN_W = OFF_MERGE + N_BRANCH * D_MODEL

V7X_VMEM_BYTES = 64 * 1024 * 1024
VMEM_LIMIT_BYTES = V7X_VMEM_BYTES - 8 * 1024 * 1024

PROMPT_TILE = 256
HIST_PAD = 32
SAMPLE_BLOCK = 8


def _sigmoid(x):
    return 0.5 * jnp.tanh(0.5 * x) + 0.5


def _silu(x):
    hx = 0.5 * x
    return hx * (jnp.tanh(hx) + 1.0)


def _rms_scale(x):
    return lax.rsqrt(jnp.mean(x * x, axis=-1, keepdims=True) + EPS)


def _layernorm_rows(x):
    mu = jnp.mean(x, axis=-1, keepdims=True)
    xc = x - mu
    var = jnp.mean(xc * xc, axis=-1, keepdims=True)
    return xc * lax.rsqrt(var + EPS)


def _rotary(x, cos2, sin2):
    return x * cos2 + pltpu.roll(x, HEAD_DIM // 2, axis=1) * sin2


def _dot(a, b):
    return jnp.dot(a, b, preferred_element_type=F32)


def _dot_nt(a, b):
    return lax.dot_general(a, b, (((1,), (1,)), ((), ())), preferred_element_type=F32)


def _dot_tn(a, b):
    return lax.dot_general(a, b, (((0,), (0,)), ((), ())), preferred_element_type=F32)


def _head(x, h):
    return x[:, h * HEAD_DIM:(h + 1) * HEAD_DIM]


def _group_norm_heads(o, gn_w):
    parts = [_layernorm_rows(_head(o, h)) for h in range(RET_HEADS)]
    return jnp.concatenate(parts, axis=-1) * gn_w


def _merge_and_project(x, gate_pre, yr, yc, ym, w_br_ret_ref, w_br_conv_ref,
                       w_br_mem_ref, w_out_ref, final_w):
    branches = (
        _dot(yr.astype(BF16), w_br_ret_ref[...]),
        _dot(yc.astype(BF16), w_br_conv_ref[...]),
        _dot(ym.astype(BF16), w_br_mem_ref[...]),
    )
    merged = None
    for i, br in enumerate(branches):
        term = _sigmoid(gate_pre(i)) * br
        merged = term if merged is None else merged + term
    out = x + _dot(merged.astype(BF16), w_out_ref[...])
    return out * _rms_scale(out) * final_w


MEM_KV_BLOCK = 4


def _mem_kv_kernel(mem_ref, norm_w_ref, w_ref, k_ref, v_ref, kbf_ref, vbf_ref):
    nb, n_mem, _ = mem_ref.shape
    m = mem_ref[...].reshape(nb * n_mem, D_MODEL)
    hm = (m * _rms_scale(m) * norm_w_ref[...]).astype(BF16)
    kv = _dot(hm, w_ref[0].astype(BF16))
    for b in range(nb):
        k = kv[b * n_mem:(b + 1) * n_mem, :BRANCH_W]
        v = kv[b * n_mem:(b + 1) * n_mem, BRANCH_W:]
        for h in range(MEM_HEADS):
            k_ref[b, pl.ds(h, n_mem, stride=MEM_HEADS), :] = _head(k, h)
            v_ref[b, pl.ds(h, n_mem, stride=MEM_HEADS), :] = _head(v, h)
        kbf_ref[b] = k.astype(BF16)
        vbf_ref[b] = v.astype(BF16)


def _mem_kv(mem, mem_norm_w, w_mem_kv):
    batch, n_mem, _ = mem.shape
    nb = MEM_KV_BLOCK
    blk = lambda dt: jax.ShapeDtypeStruct((batch, n_mem, BRANCH_W), dt)
    out_spec = pl.BlockSpec((nb, n_mem, BRANCH_W), lambda b: (b, 0, 0))
    rows_shape = (batch, n_mem * MEM_HEADS, HEAD_DIM)
    rows_spec = pl.BlockSpec((nb, n_mem * MEM_HEADS, HEAD_DIM), lambda b: (b, 0, 0))
    return pl.pallas_call(
        _mem_kv_kernel,
        grid=(batch // nb,),
        in_specs=[
            pl.BlockSpec((nb, n_mem, D_MODEL), lambda b: (b, 0, 0)),
            pl.BlockSpec((1, D_MODEL), lambda b: (0, 0)),
            pl.BlockSpec((1, D_MODEL, 2 * BRANCH_W), lambda b: (0, 0, 0)),
        ],
        out_specs=[rows_spec, rows_spec, out_spec, out_spec],
        out_shape=[jax.ShapeDtypeStruct(rows_shape, F32), jax.ShapeDtypeStruct(rows_shape, F32),
                   blk(BF16), blk(BF16)],
        compiler_params=pltpu.CompilerParams(dimension_semantics=("arbitrary",)),
        name="mem_kv",
    )(mem, mem_norm_w, w_mem_kv)


def _prompt_kernel(c_dec, tiles_per_seq, n_tiles,
                   x_ref, xb_ref, cos_ref, sin_ref, mk_ref, mv_ref, norm_w_ref, w_in_ref,
                   gn_w_ref, conv_w_ref, conv_b_ref, ln_w_ref, ln_b_ref,
                   w_br_ret_ref, w_br_conv_ref, w_br_mem_ref, w_out_ref, final_w_ref,
                   intra_ref, qdec_ref, kdec_ref,
                   y_ref, s_ref, hist_ref,
                   ubuf_ref, h_scr, yr_scr, yc_scr, ym_scr):
    i = pl.program_id(0)
    t = lax.rem(jnp.minimum(i, n_tiles - 1), tiles_per_seq)
    live = i < n_tiles
    tile = x_ref.shape[1]

    @pl.when(i == 0)
    def _():
        h_scr[...] = jnp.zeros_like(h_scr)
        yr_scr[...] = jnp.zeros_like(yr_scr)
        yc_scr[...] = jnp.zeros_like(yc_scr)
        ym_scr[...] = jnp.zeros_like(ym_scr)

    @pl.when(jnp.logical_and(t == 0, live))
    def _():
        s_ref[...] = jnp.zeros_like(s_ref)
        ubuf_ref[0:HIST_PAD, :] = jnp.zeros((HIST_PAD, BRANCH_W), F32)

    h_prev = h_scr[...]
    half = D_MODEL // 2

    def gate_job(k):
        lo = OFF_MERGE + k * half
        return lambda: _dot(h_prev, w_in_ref[:, lo:lo + half])

    back_jobs = [gate_job(k) for k in range(2 * N_BRANCH)] + [
        lambda: _dot(yr_scr[...], w_br_ret_ref[...]),
        lambda: _dot(yc_scr[...], w_br_conv_ref[...]),
        lambda: _dot(ym_scr[...], w_br_mem_ref[...]),
    ]
    back_out = [None] * len(back_jobs)

    def run_back(k):
        back_out[k] = back_jobs[k]()

    x = x_ref[0]
    h_bf = (x * _rms_scale(x) * norm_w_ref[...]).astype(BF16)
    run_back(8)

    def proj(off, width=BRANCH_W):
        return _dot(h_bf, w_in_ref[:, off:off + width])

    za = proj(OFF_AC)
    zb = proj(OFF_BC)
    zq = proj(OFF_QR)
    zk = proj(OFF_KR)
    v_bf = proj(OFF_VR).astype(BF16)
    u = za * _sigmoid(zb)
    ubuf_ref[HIST_PAD:HIST_PAD + tile, :] = u
    first = HIST_PAD - CONV_HIST
    conv = None
    for r in range(8):
        part = None
        for j in range(CONV_K):
            if (first + j) % 8 != r:
                continue
            term = ubuf_ref[first + j:first + j + tile, :] * conv_w_ref[0, j:j + 1, :]
            part = term if part is None else part + term
        conv = part if conv is None else conv + part
        run_back(r)
    conv = conv + conv_b_ref[...]
    ubuf_ref[0:HIST_PAD, :] = jnp.where(live, ubuf_ref[tile:tile + HIST_PAD, :],
                                        ubuf_ref[0:HIST_PAD, :])

    z_gc = proj(OFF_GC)
    z_gr = proj(OFF_GR)
    z_qm = proj(OFF_QM)
    z_gm = proj(OFF_GM)

    merged = []
    for k in range(2):
        cols = slice(k * half, (k + 1) * half)
        acc = None
        for b in range(N_BRANCH):
            term = _sigmoid(back_out[2 * b + k]) * back_out[2 * N_BRANCH + b][:, cols]
            acc = term if acc is None else acc + term
        merged.append(acc.astype(BF16))
    merged = jnp.concatenate(merged, axis=-1)

    cos2 = cos_ref[...]
    sin2 = sin_ref[...]
    q_scale = HEAD_DIM ** -0.5
    o_heads = []
    for h in range(RET_HEADS):
        q_h = _rotary(_head(zq, h), cos2, sin2) * q_scale
        k_h = _rotary(_head(zk, h), cos2, sin2)
        v_h = _head(v_bf, h)
        s_old = s_ref[0, h]
        s = s_old
        o_chunks = []
        for c in range(tile // RET_CHUNK):
            rows = slice(c * RET_CHUNK, (c + 1) * RET_CHUNK)
            qc = q_h[rows].astype(BF16)
            kc = k_h[rows]
            vc = v_h[rows]
            sc = _dot_nt(qc, kc.astype(BF16)) * intra_ref[h]
            o_chunks.append(_dot(sc.astype(BF16), vc) + _dot(qc, s.astype(BF16)) * qdec_ref[h])
            kd = (kc * kdec_ref[h]).astype(BF16)
            s = s * c_dec[h] + _dot_tn(kd, vc)
        s_ref[0, h] = jnp.where(live, s, s_old)
        o_heads.append(_layernorm_rows(jnp.concatenate(o_chunks, axis=0)))

    out = xb_ref[0] + _dot(merged, w_out_ref[...])

    ret = jnp.concatenate(o_heads, axis=-1) * gn_w_ref[...]
    yr = _silu(z_gr) * ret
    cn = _layernorm_rows(conv) * ln_w_ref[...] + ln_b_ref[...]
    yc = _silu(z_gc) * _silu(cn)

    qm = z_qm.astype(BF16)
    mk = mk_ref[0]
    mv = mv_ref[0]
    att = []
    for h in range(MEM_HEADS):
        sc = _dot_nt(_head(qm, h), _head(mk, h)) * (HEAD_DIM ** -0.5)
        e = jnp.exp(sc - jnp.max(sc, axis=-1, keepdims=True))
        p = e / jnp.sum(e, axis=-1, keepdims=True)
        att.append(_dot(p.astype(BF16), _head(mv, h)))
    ym = _silu(z_gm) * jnp.concatenate(att, axis=-1)

    y_ref[0] = out * _rms_scale(out) * final_w_ref[...]

    h_scr[...] = h_bf
    yr_scr[...] = yr.astype(BF16)
    yc_scr[...] = yc.astype(BF16)
    ym_scr[...] = ym.astype(BF16)

    @pl.when(jnp.logical_and(t == tiles_per_seq - 1, live))
    def _():
        hist_ref[0] = ubuf_ref[first:HIST_PAD, :]


def _const_spec(shape, single_buffer=False):
    zeros = (0,) * len(shape)
    if single_buffer:
        return pl.BlockSpec(shape, lambda i: zeros, pipeline_mode=pl.Buffered(1))
    return pl.BlockSpec(shape, lambda i: zeros)


def _prompt_layer(x, cos2, sin2, mk_bf, mv_bf, norm_w, w_in_bf, gn_w, conv_w, conv_b, ln_w,
                  ln_b, w_br_ret_bf, w_br_conv_bf, w_br_mem_bf, w_out_bf, final_w,
                  intra, qdec, kdec, c_dec):
    batch, seq, _ = x.shape
    n_mem = mk_bf.shape[1]
    tile = PROMPT_TILE
    tiles_per_seq = seq // tile
    n_tiles = batch * tiles_per_seq

    def front(i):
        j = jnp.minimum(i, n_tiles - 1)
        return j // tiles_per_seq, lax.rem(j, tiles_per_seq)

    def back(i):
        j = jnp.maximum(i - 1, 0)
        return j // tiles_per_seq, lax.rem(j, tiles_per_seq)

    row = lambda w: _const_spec((1, w))
    weight = lambda a: _const_spec(a.shape, single_buffer=True)
    table = _const_spec((RET_HEADS, RET_CHUNK, RET_CHUNK))
    x_tile = lambda which: pl.BlockSpec((1, tile, D_MODEL), lambda i: (*which(i), 0))
    per_seq = lambda shape: pl.BlockSpec(
        (1,) + shape, lambda i: (front(i)[0],) + (0,) * len(shape))
    return pl.pallas_call(
        functools.partial(_prompt_kernel, c_dec, tiles_per_seq, n_tiles),
        grid=(n_tiles + 1,),
        in_specs=[
            x_tile(front), x_tile(back),
            pl.BlockSpec((tile, HEAD_DIM), lambda i: (front(i)[1], 0)),
            pl.BlockSpec((tile, HEAD_DIM), lambda i: (front(i)[1], 0)),
            per_seq((n_mem, BRANCH_W)), per_seq((n_mem, BRANCH_W)),
            row(D_MODEL), weight(w_in_bf), row(BRANCH_W),
            _const_spec((1, CONV_K, BRANCH_W)), row(BRANCH_W), row(BRANCH_W), row(BRANCH_W),
            weight(w_br_ret_bf), weight(w_br_conv_bf), weight(w_br_mem_bf), weight(w_out_bf),
            row(D_MODEL), table, table, table,
        ],
        out_specs=[
            x_tile(back),
            per_seq((RET_HEADS, HEAD_DIM, HEAD_DIM)),
            per_seq((CONV_HIST, BRANCH_W)),
        ],
        out_shape=[
            jax.ShapeDtypeStruct((batch, seq, D_MODEL), F32),
            jax.ShapeDtypeStruct((batch, RET_HEADS, HEAD_DIM, HEAD_DIM), F32),
            jax.ShapeDtypeStruct((batch, CONV_HIST, BRANCH_W), F32),
        ],
        scratch_shapes=[
            pltpu.VMEM((HIST_PAD + tile, BRANCH_W), F32),
            pltpu.VMEM((tile, D_MODEL), BF16),
            pltpu.VMEM((tile, BRANCH_W), BF16),
            pltpu.VMEM((tile, BRANCH_W), BF16),
            pltpu.VMEM((tile, BRANCH_W), BF16),
        ],
        compiler_params=pltpu.CompilerParams(
            dimension_semantics=("arbitrary",), vmem_limit_bytes=VMEM_LIMIT_BYTES),
        name="prompt_layer",
    )(x, x, cos2, sin2, mk_bf, mv_bf, norm_w, w_in_bf, gn_w, conv_w, conv_b, ln_w, ln_b,
      w_br_ret_bf, w_br_conv_bf, w_br_mem_bf, w_out_bf, final_w, intra, qdec, kdec)


SAMPLE_PROJ_BLOCK = 1536


def _sample_proj_kernel(x_ref, norm_w_ref, w_ref, z_ref, w_bf_ref):
    x = x_ref[...]
    h_bf = (x * _rms_scale(x) * norm_w_ref[...]).astype(BF16)
    w_bf = w_ref[0].astype(BF16)
    w_bf_ref[...] = w_bf
    z_ref[...] = _dot(h_bf, w_bf)


def _sample_proj(xs, norm_w, w_in):
    n = xs.shape[0]
    return pl.pallas_call(
        _sample_proj_kernel,
        grid=(IN_W // SAMPLE_PROJ_BLOCK,),
        in_specs=[
            pl.BlockSpec((n, D_MODEL), lambda j: (0, 0)),
            pl.BlockSpec((1, D_MODEL), lambda j: (0, 0)),
            pl.BlockSpec((1, D_MODEL, SAMPLE_PROJ_BLOCK), lambda j: (0, 0, j)),
        ],
        out_specs=[pl.BlockSpec((n, SAMPLE_PROJ_BLOCK), lambda j: (0, j)),
                   pl.BlockSpec((D_MODEL, SAMPLE_PROJ_BLOCK), lambda j: (0, j))],
        out_shape=[jax.ShapeDtypeStruct((n, IN_W), F32),
                   jax.ShapeDtypeStruct((D_MODEL, IN_W), BF16)],
        compiler_params=pltpu.CompilerParams(
            dimension_semantics=("arbitrary",), vmem_limit_bytes=VMEM_LIMIT_BYTES),
        name="sample_proj",
    )(xs, norm_w, w_in)


def _sample_state_kernel(gamma, z_ref, cos_ref, sin_ref, s0_ref, hist_ref, ck_ref, cv_ref,
                         conv_w_ref, o_ref, conv_ref, att_ref, s_new_ref, hist_new_ref):
    nb = z_ref.shape[0]
    n_mem = ck_ref.shape[1] // MEM_HEADS
    cos2 = cos_ref[...]
    sin2 = sin_ref[...]
    q_scale = HEAD_DIM ** -0.5
    zq = z_ref[:, OFF_QR:OFF_QR + BRANCH_W]
    zk = z_ref[:, OFF_KR:OFF_KR + BRANCH_W]
    v = z_ref[:, OFF_VR:OFF_VR + BRANCH_W]
    q = jnp.concatenate([_rotary(_head(zq, h), cos2, sin2) * q_scale
                         for h in range(RET_HEADS)], axis=-1)
    k = jnp.concatenate([_rotary(_head(zk, h), cos2, sin2) for h in range(RET_HEADS)], axis=-1)
    u = z_ref[:, OFF_AC:OFF_AC + BRANCH_W] * _sigmoid(z_ref[:, OFF_BC:OFF_BC + BRANCH_W])
    qm = z_ref[:, OFF_QM:OFF_QM + BRANCH_W]

    eye = (lax.broadcasted_iota(jnp.int32, (HEAD_DIM, HEAD_DIM), 0)
           == lax.broadcasted_iota(jnp.int32, (HEAD_DIM, HEAD_DIM), 1))
    ones_bf = jnp.ones((HEAD_DIM, HEAD_DIM), BF16)

    def as_columns(row):
        diag = jnp.where(eye, jnp.broadcast_to(row, (HEAD_DIM, HEAD_DIM)), 0.0)
        return _dot(diag.astype(BF16), ones_bf)

    mem_scale = HEAD_DIM ** -0.5

    acc = u * conv_w_ref[0, CONV_HIST:CONV_K, :]
    for j in range(CONV_HIST):
        acc = acc + hist_ref[j] * conv_w_ref[0, j:j + 1, :]
        hist_new_ref[j] = hist_ref[j + 1] if j + 1 < CONV_HIST else u
    conv_ref[...] = acc

    for b in range(nb):
        for h in range(RET_HEADS):
            q_row = _head(q, h)[b:b + 1]
            k_row = _head(k, h)[b:b + 1]
            v_row = _head(v, h)[b:b + 1]
            s0 = s0_ref[b, h]
            qk = jnp.sum(q_row * k_row, axis=-1, keepdims=True)
            qs = jnp.sum(as_columns(q_row) * s0, axis=0, keepdims=True)
            o_ref[b:b + 1, h * HEAD_DIM:(h + 1) * HEAD_DIM] = qk * v_row + gamma[h] * qs
            s_new_ref[b, h] = gamma[h] * s0 + as_columns(k_row) * v_row

        for h in range(MEM_HEADS):
            head_rows = pl.ds(h, n_mem, stride=MEM_HEADS)
            prod = ck_ref[b, head_rows, :] * _head(qm, h)[b:b + 1]
            sc = jnp.sum(prod, axis=-1, keepdims=True) * mem_scale
            e = jnp.exp(sc - jnp.max(sc, axis=0, keepdims=True))
            p = e / jnp.sum(e, axis=0, keepdims=True)
            att_ref[b:b + 1, h * HEAD_DIM:(h + 1) * HEAD_DIM] = jnp.sum(
                p * cv_ref[b, head_rows, :], axis=0, keepdims=True)


def _sample_state(z, cos2, sin2, s0, hist, cache_k, cache_v, conv_w, gamma):
    n = z.shape[0]
    nb = SAMPLE_BLOCK
    rows = lambda w: pl.BlockSpec((nb, w), lambda i: (i, 0))
    state = pl.BlockSpec((nb, RET_HEADS, HEAD_DIM, HEAD_DIM), lambda i: (i, 0, 0, 0))
    hist_spec = pl.BlockSpec((CONV_HIST, nb, BRANCH_W), lambda i: (0, i, 0))
    cache = pl.BlockSpec((nb,) + cache_k.shape[1:], lambda i: (i, 0, 0))
    return pl.pallas_call(
        functools.partial(_sample_state_kernel, gamma),
        grid=(n // nb,),
        in_specs=[
            rows(IN_W),
            pl.BlockSpec((1, HEAD_DIM), lambda i: (0, 0)),
            pl.BlockSpec((1, HEAD_DIM), lambda i: (0, 0)),
            state, hist_spec, cache, cache,
            pl.BlockSpec((1, CONV_K, BRANCH_W), lambda i: (0, 0, 0)),
        ],
        out_specs=[rows(BRANCH_W), rows(BRANCH_W), rows(BRANCH_W), state, hist_spec],
        out_shape=[
            jax.ShapeDtypeStruct((n, BRANCH_W), F32),
            jax.ShapeDtypeStruct((n, BRANCH_W), F32),
            jax.ShapeDtypeStruct((n, BRANCH_W), F32),
            jax.ShapeDtypeStruct(s0.shape, F32),
            jax.ShapeDtypeStruct(hist.shape, F32),
        ],
        compiler_params=pltpu.CompilerParams(
            dimension_semantics=("arbitrary",), vmem_limit_bytes=VMEM_LIMIT_BYTES),
        name="sample_state",
    )(z, cos2, sin2, s0, hist, cache_k, cache_v, conv_w)


def _sample_out_kernel(x_ref, z_ref, o_ref, conv_ref, att_ref, gn_w_ref, conv_b_ref, ln_w_ref,
                       ln_b_ref, w_br_ret_ref, w_br_conv_ref, w_br_mem_ref, w_out_ref,
                       final_w_ref, y_ref):
    gate = lambda off: _silu(z_ref[:, off:off + BRANCH_W])
    yr = gate(OFF_GR) * _group_norm_heads(o_ref[...], gn_w_ref[...])
    cn = _layernorm_rows(conv_ref[...] + conv_b_ref[...]) * ln_w_ref[...] + ln_b_ref[...]
    yc = gate(OFF_GC) * _silu(cn)
    ym = gate(OFF_GM) * att_ref[...]
    gate_pre = lambda i: z_ref[:, OFF_MERGE + i * D_MODEL:OFF_MERGE + (i + 1) * D_MODEL]
    y_ref[...] = _merge_and_project(x_ref[...], gate_pre, yr, yc, ym, w_br_ret_ref,
                                    w_br_conv_ref, w_br_mem_ref, w_out_ref, final_w_ref[...])


def _sample_out(xs, z, o, conv, att, gn_w, conv_b, ln_w, ln_b,
                w_br_ret_bf, w_br_conv_bf, w_br_mem_bf, w_out_bf, final_w):
    return pl.pallas_call(
        _sample_out_kernel,
        out_shape=jax.ShapeDtypeStruct(xs.shape, F32),
        compiler_params=pltpu.CompilerParams(vmem_limit_bytes=VMEM_LIMIT_BYTES),
        name="sample_out",
    )(xs, z, o, conv, att, gn_w, conv_b, ln_w, ln_b,
      w_br_ret_bf, w_br_conv_bf, w_br_mem_bf, w_out_bf, final_w)


def _rotary_tables(pos):
    half = HEAD_DIM // 2
    inv = ROPE_BASE ** (-jnp.arange(half, dtype=F32) / half)
    ang = pos[:, None] * inv[None, :]
    cos, sin = jnp.cos(ang), jnp.sin(ang)
    return jnp.concatenate([cos, cos], axis=-1), jnp.concatenate([-sin, sin], axis=-1)


def _decay_tables():
    c = RET_CHUNK
    log_g = jnp.log1p(-jnp.exp2(-5.0 - jnp.arange(RET_HEADS, dtype=F32)))
    idx = jnp.arange(c, dtype=F32)
    diff = idx[:, None] - idx[None, :]
    intra = jnp.where(diff >= 0,
                      jnp.exp(jnp.maximum(diff, 0.0)[None] * log_g[:, None, None]), 0.0)
    q_dec = jnp.exp((idx + 1.0)[None, :] * log_g[:, None])
    k_dec = jnp.exp((c - 1.0 - idx)[None, :] * log_g[:, None])
    bcast = lambda a: jnp.broadcast_to(a[:, :, None], (RET_HEADS, c, HEAD_DIM))
    return intra, bcast(q_dec), bcast(k_dec)


def _gamma_powers(n):
    return tuple(float(np.exp(np.log1p(-np.exp2(-5.0 - h)) * n)) for h in range(RET_HEADS))


def kernel(x_prompt, x_sample, mem_prompt, state_ret, state_conv, cache_mem_k, cache_mem_v,
           norm_w, w_in, ret_gn_w, conv_w, conv_b, conv_ln_w, conv_ln_b, mem_norm_w,
           w_mem_kv, w_br_ret, w_br_conv, w_br_mem, w_out, final_norm_w):
    depth = w_in.shape[0]
    assert depth == 1, "single-layer step"
    batch, seq, _ = x_prompt.shape
    n_dec, dec_seq, _ = x_sample.shape
    assert dec_seq == 1 and seq % PROMPT_TILE == 0 and n_dec % SAMPLE_BLOCK == 0
    n_mem = mem_prompt.shape[1]

    bf = lambda w: w[0].astype(BF16)
    w_br_ret_bf, w_br_conv_bf, w_br_mem_bf, w_out_bf = bf(w_br_ret), bf(w_br_conv), bf(w_br_mem), bf(w_out)
    final_w = final_norm_w[None, :]

    intra, qdec, kdec = _decay_tables()
    cos_p, sin_p = _rotary_tables(jnp.arange(seq, dtype=F32))
    cos_s, sin_s = _rotary_tables(jnp.arange(dec_seq, dtype=F32) + PAST_LEN)

    xs = x_sample[:, 0, :]
    z_s, w_in_bf = _sample_proj(xs, norm_w, w_in)

    mk, mv, mk_bf, mv_bf = _mem_kv(mem_prompt, mem_norm_w, w_mem_kv)

    y_prompt, s_prompt, hist_prompt = _prompt_layer(
        x_prompt, cos_p, sin_p, mk_bf, mv_bf, norm_w, w_in_bf, ret_gn_w, conv_w, conv_b,
        conv_ln_w, conv_ln_b, w_br_ret_bf, w_br_conv_bf, w_br_mem_bf, w_out_bf, final_w,
        intra, qdec, kdec, _gamma_powers(RET_CHUNK))

    o_s, conv_s, att_s, s_sample, hist_sample = _sample_state(
        z_s, cos_s, sin_s, state_ret[0], jnp.transpose(state_conv[0], (1, 0, 2)),
        cache_mem_k.reshape(n_dec, n_mem * MEM_HEADS, HEAD_DIM),
        cache_mem_v.reshape(n_dec, n_mem * MEM_HEADS, HEAD_DIM), conv_w, _gamma_powers(1))
    y_sample = _sample_out(xs, z_s, o_s, conv_s, att_s, ret_gn_w, conv_b, conv_ln_w, conv_ln_b,
                           w_br_ret_bf, w_br_conv_bf, w_br_mem_bf, w_out_bf, final_w)

    heads = lambda a: a.reshape(1, batch, n_mem, MEM_HEADS, HEAD_DIM)
    return (y_prompt, y_sample[:, None, :], s_prompt[None], s_sample[None],
            hist_prompt[None], jnp.transpose(hist_sample, (1, 0, 2))[None], heads(mk), heads(mv))
```

```python
import functools

import numpy as np
import jax
import jax.numpy as jnp
from jax import lax
from jax.experimental import pallas as pl
from jax.experimental.pallas import tpu as pltpu

F32 = jnp.float32
BF16 = jnp.bfloat16

D_MODEL = 1024
RET_HEADS = 4
HEAD_DIM = 128
BRANCH_W = 512
RET_CHUNK = 128
ROPE_BASE = 10000.0
CONV_K = 31
CONV_HIST = CONV_K - 1
MEM_HEADS = 4
N_BRANCH = 3
EPS = 1e-6
PAST_LEN = 16384

OFF_QR, OFF_KR, OFF_VR, OFF_GR = 0, 512, 1024, 1536
OFF_AC, OFF_BC, OFF_GC = 2048, 2560, 3072
OFF_QM, OFF_GM = 3584, 4096
OFF_MERGE = 4608
IN_W = OFF_MERGE + N_BRANCH * D_MODEL

V7X_VMEM_BYTES = 64 * 1024 * 1024
VMEM_LIMIT_BYTES = V7X_VMEM_BYTES - 8 * 1024 * 1024

PROMPT_TILE = 256
HIST_PAD = 32


def _sigmoid(x):
    return 0.5 * jnp.tanh(0.5 * x) + 0.5


def _silu(x):
    hx = 0.5 * x
    return hx * (jnp.tanh(hx) + 1.0)


def _rms_scale(x):
    return lax.rsqrt(jnp.mean(x * x, axis=-1, keepdims=True) + EPS)


def _layernorm_rows(x):
    mu = jnp.mean(x, axis=-1, keepdims=True)
    xc = x - mu
    var = jnp.mean(xc * xc, axis=-1, keepdims=True)
    return xc * lax.rsqrt(var + EPS)


def _rotary(x, cos2, sin2):
    return x * cos2 + pltpu.roll(x, HEAD_DIM // 2, axis=1) * sin2


def _dot(a, b):
    return jnp.dot(a, b, preferred_element_type=F32)


def _dot_nt(a, b):
    return lax.dot_general(a, b, (((1,), (1,)), ((), ())), preferred_element_type=F32)


def _dot_tn(a, b):
    return lax.dot_general(a, b, (((0,), (0,)), ((), ())), preferred_element_type=F32)


def _head(x, h):
    return x[:, h * HEAD_DIM:(h + 1) * HEAD_DIM]


def _group_norm_heads(o, gn_w):
    parts = [_layernorm_rows(_head(o, h)) for h in range(RET_HEADS)]
    return jnp.concatenate(parts, axis=-1) * gn_w


def _merge_and_project(x, gate_pre, yr, yc, ym, w_br_ret_ref, w_br_conv_ref,
                       w_br_mem_ref, w_out_ref, final_w):
    branches = (
        _dot(yr.astype(BF16), w_br_ret_ref[...]),
        _dot(yc.astype(BF16), w_br_conv_ref[...]),
        _dot(ym.astype(BF16), w_br_mem_ref[...]),
    )
    merged = None
    for i, br in enumerate(branches):
        term = _sigmoid(gate_pre(i)) * br
        merged = term if merged is None else merged + term
    out = x + _dot(merged.astype(BF16), w_out_ref[...])
    return out * _rms_scale(out) * final_w


MEM_KV_BLOCK = 4


def _mem_kv_kernel(mem_ref, norm_w_ref, w_ref, k_ref, v_ref, kbf_ref, vbf_ref):
    nb, n_mem, _ = mem_ref.shape
    m = mem_ref[...].reshape(nb * n_mem, D_MODEL)
    hm = (m * _rms_scale(m) * norm_w_ref[...]).astype(BF16)
    kv = _dot(hm, w_ref[0].astype(BF16))
    for b in range(nb):
        k = kv[b * n_mem:(b + 1) * n_mem, :BRANCH_W]
        v = kv[b * n_mem:(b + 1) * n_mem, BRANCH_W:]
        for h in range(MEM_HEADS):
            k_ref[b, pl.ds(h, n_mem, stride=MEM_HEADS), :] = _head(k, h)
            v_ref[b, pl.ds(h, n_mem, stride=MEM_HEADS), :] = _head(v, h)
        kbf_ref[b] = k.astype(BF16)
        vbf_ref[b] = v.astype(BF16)


def _mem_kv(mem, mem_norm_w, w_mem_kv):
    batch, n_mem, _ = mem.shape
    nb = MEM_KV_BLOCK
    blk = lambda dt: jax.ShapeDtypeStruct((batch, n_mem, BRANCH_W), dt)
    out_spec = pl.BlockSpec((nb, n_mem, BRANCH_W), lambda b: (b, 0, 0))
    rows_shape = (batch, n_mem * MEM_HEADS, HEAD_DIM)
    rows_spec = pl.BlockSpec((nb, n_mem * MEM_HEADS, HEAD_DIM), lambda b: (b, 0, 0))
    return pl.pallas_call(
        _mem_kv_kernel,
        grid=(batch // nb,),
        in_specs=[
            pl.BlockSpec((nb, n_mem, D_MODEL), lambda b: (b, 0, 0)),
            pl.BlockSpec((1, D_MODEL), lambda b: (0, 0)),
            pl.BlockSpec((1, D_MODEL, 2 * BRANCH_W), lambda b: (0, 0, 0)),
        ],
        out_specs=[rows_spec, rows_spec, out_spec, out_spec],
        out_shape=[jax.ShapeDtypeStruct(rows_shape, F32), jax.ShapeDtypeStruct(rows_shape, F32),
                   blk(BF16), blk(BF16)],
        compiler_params=pltpu.CompilerParams(dimension_semantics=("arbitrary",)),
        name="mem_kv",
    )(mem, mem_norm_w, w_mem_kv)


def _prompt_kernel(c_dec, gamma, tiles_per_seq, n_tiles,
                   x_ref, xb_ref, cos_ref, sin_ref, mk_ref, mv_ref, norm_w_ref, w_in_ref,
                   gn_w_ref, conv_w_ref, conv_b_ref, ln_w_ref, ln_b_ref,
                   w_br_ret_ref, w_br_conv_ref, w_br_mem_ref, w_out_ref, final_w_ref,
                   intra_ref, qdec_ref, kdec_ref,
                   zs_ref, cos_s_ref, sin_s_ref, s0_ref, ck_ref, cv_ref,
                   y_ref, s_ref, hist_ref, os_ref, atts_ref, s_new_ref,
                   ubuf_ref, h_scr, yr_scr, yc_scr, ym_scr):
    i = pl.program_id(0)
    t = lax.rem(jnp.minimum(i, n_tiles - 1), tiles_per_seq)
    live = i < n_tiles
    tile = x_ref.shape[1]

    @pl.when(i == 0)
    def _():
        h_scr[...] = jnp.zeros_like(h_scr)
        yr_scr[...] = jnp.zeros_like(yr_scr)
        yc_scr[...] = jnp.zeros_like(yc_scr)
        ym_scr[...] = jnp.zeros_like(ym_scr)

    @pl.when(jnp.logical_and(t == 0, live))
    def _():
        s_ref[...] = jnp.zeros_like(s_ref)
        ubuf_ref[0:HIST_PAD, :] = jnp.zeros((HIST_PAD, BRANCH_W), F32)

    h_prev = h_scr[...]
    half = D_MODEL // 2

    def gate_job(k):
        lo = OFF_MERGE + k * half
        return lambda: _dot(h_prev, w_in_ref[:, lo:lo + half])

    back_jobs = [gate_job(k) for k in range(2 * N_BRANCH)] + [
        lambda: _dot(yr_scr[...], w_br_ret_ref[...]),
        lambda: _dot(yc_scr[...], w_br_conv_ref[...]),
        lambda: _dot(ym_scr[...], w_br_mem_ref[...]),
    ]
    back_out = [None] * len(back_jobs)

    def run_back(k):
        back_out[k] = back_jobs[k]()

    x = x_ref[0]
    h_bf = (x * _rms_scale(x) * norm_w_ref[...]).astype(BF16)
    run_back(8)

    def proj(off, width=BRANCH_W):
        return _dot(h_bf, w_in_ref[:, off:off + width])

    za = proj(OFF_AC)
    zb = proj(OFF_BC)
    zq = proj(OFF_QR)
    zk = proj(OFF_KR)
    v_bf = proj(OFF_VR).astype(BF16)
    u = za * _sigmoid(zb)
    ubuf_ref[HIST_PAD:HIST_PAD + tile, :] = u
    first = HIST_PAD - CONV_HIST
    conv = None
    for r in range(8):
        part = None
        for j in range(CONV_K):
            if (first + j) % 8 != r:
                continue
            term = ubuf_ref[first + j:first + j + tile, :] * conv_w_ref[0, j:j + 1, :]
            part = term if part is None else part + term
        conv = part if conv is None else conv + part
        run_back(r)
    conv = conv + conv_b_ref[...]
    ubuf_ref[0:HIST_PAD, :] = jnp.where(live, ubuf_ref[tile:tile + HIST_PAD, :],
                                        ubuf_ref[0:HIST_PAD, :])

    z_gc = proj(OFF_GC)
    z_gr = proj(OFF_GR)
    z_qm = proj(OFF_QM)
    z_gm = proj(OFF_GM)

    merged = []
    for k in range(2):
        cols = slice(k * half, (k + 1) * half)
        acc = None
        for b in range(N_BRANCH):
            term = _sigmoid(back_out[2 * b + k]) * back_out[2 * N_BRANCH + b][:, cols]
            acc = term if acc is None else acc + term
        merged.append(acc.astype(BF16))
    merged = jnp.concatenate(merged, axis=-1)

    cos2 = cos_ref[...]
    sin2 = sin_ref[...]
    q_scale = HEAD_DIM ** -0.5
    o_heads = []
    for h in range(RET_HEADS):
        q_h = _rotary(_head(zq, h), cos2, sin2) * q_scale
        k_h = _rotary(_head(zk, h), cos2, sin2)
        v_h = _head(v_bf, h)
        s_old = s_ref[0, h]
        s = s_old
        o_chunks = []
        for c in range(tile // RET_CHUNK):
            rows = slice(c * RET_CHUNK, (c + 1) * RET_CHUNK)
            qc = q_h[rows].astype(BF16)
            kc = k_h[rows]
            vc = v_h[rows]
            sc = _dot_nt(qc, kc.astype(BF16)) * intra_ref[h]
            o_chunks.append(_dot(sc.astype(BF16), vc) + _dot(qc, s.astype(BF16)) * qdec_ref[h])
            kd = (kc * kdec_ref[h]).astype(BF16)
            s = s * c_dec[h] + _dot_tn(kd, vc)
        s_ref[0, h] = jnp.where(live, s, s_old)
        o_heads.append(_layernorm_rows(jnp.concatenate(o_chunks, axis=0)))

    out = xb_ref[0] + _dot(merged, w_out_ref[...])

    ret = jnp.concatenate(o_heads, axis=-1) * gn_w_ref[...]
    yr = _silu(z_gr) * ret
    cn = _layernorm_rows(conv) * ln_w_ref[...] + ln_b_ref[...]
    yc = _silu(z_gc) * _silu(cn)

    qm = z_qm.astype(BF16)
    mk = mk_ref[0]
    mv = mv_ref[0]
    att = []
    for h in range(MEM_HEADS):
        sc = _dot_nt(_head(qm, h), _head(mk, h)) * (HEAD_DIM ** -0.5)
        e = jnp.exp(sc - jnp.max(sc, axis=-1, keepdims=True))
        p = e / jnp.sum(e, axis=-1, keepdims=True)
        att.append(_dot(p.astype(BF16), _head(mv, h)))
    ym = _silu(z_gm) * jnp.concatenate(att, axis=-1)

    y_ref[0] = out * _rms_scale(out) * final_w_ref[...]

    h_scr[...] = h_bf
    yr_scr[...] = yr.astype(BF16)
    yc_scr[...] = yc.astype(BF16)
    ym_scr[...] = ym.astype(BF16)

    @pl.when(jnp.logical_and(t == tiles_per_seq - 1, live))
    def _():
        hist_ref[0] = ubuf_ref[first:HIST_PAD, :]

    _sample_sequences(gamma, zs_ref[0], cos_s_ref[...], sin_s_ref[...], s0_ref, ck_ref, cv_ref,
                      os_ref.at[0], atts_ref.at[0], s_new_ref)


def _const_spec(shape, single_buffer=False):
    zeros = (0,) * len(shape)
    if single_buffer:
        return pl.BlockSpec(shape, lambda i: zeros, pipeline_mode=pl.Buffered(1))
    return pl.BlockSpec(shape, lambda i: zeros)


def _prompt_layer(x, cos2, sin2, mk_bf, mv_bf, norm_w, w_in_bf, gn_w, conv_w, conv_b, ln_w,
                  ln_b, w_br_ret_bf, w_br_conv_bf, w_br_mem_bf, w_out_bf, final_w,
                  intra, qdec, kdec, c_dec,
                  z_s, cos_s, sin_s, s0, cache_k, cache_v, gamma):
    batch, seq, _ = x.shape
    n_mem = mk_bf.shape[1]
    tile = PROMPT_TILE
    tiles_per_seq = seq // tile
    n_tiles = batch * tiles_per_seq

    def front(i):
        j = jnp.minimum(i, n_tiles - 1)
        return j // tiles_per_seq, lax.rem(j, tiles_per_seq)

    def back(i):
        j = jnp.maximum(i - 1, 0)
        return j // tiles_per_seq, lax.rem(j, tiles_per_seq)

    row = lambda w: _const_spec((1, w))
    weight = lambda a: _const_spec(a.shape, single_buffer=True)
    table = _const_spec((RET_HEADS, RET_CHUNK, RET_CHUNK))
    n_dec = z_s.shape[0]
    per_step = n_dec // n_tiles
    assert per_step * n_tiles == n_dec
    share = lambda shape: pl.BlockSpec(
        shape, lambda i: (jnp.minimum(i, n_tiles - 1),) + (0,) * (len(shape) - 1))
    z_steps = z_s.reshape(n_tiles, per_step, IN_W)
    x_tile = lambda which: pl.BlockSpec((1, tile, D_MODEL), lambda i: (*which(i), 0))
    per_seq = lambda shape: pl.BlockSpec(
        (1,) + shape, lambda i: (front(i)[0],) + (0,) * len(shape))
    return pl.pallas_call(
        functools.partial(_prompt_kernel, c_dec, gamma, tiles_per_seq, n_tiles),
        grid=(n_tiles + 1,),
        in_specs=[
            x_tile(front), x_tile(back),
            pl.BlockSpec((tile, HEAD_DIM), lambda i: (front(i)[1], 0)),
            pl.BlockSpec((tile, HEAD_DIM), lambda i: (front(i)[1], 0)),
            per_seq((n_mem, BRANCH_W)), per_seq((n_mem, BRANCH_W)),
            row(D_MODEL), weight(w_in_bf), row(BRANCH_W),
            _const_spec((1, CONV_K, BRANCH_W)), row(BRANCH_W), row(BRANCH_W), row(BRANCH_W),
            weight(w_br_ret_bf), weight(w_br_conv_bf), weight(w_br_mem_bf), weight(w_out_bf),
            row(D_MODEL), table, table, table,
            share((1, per_step, IN_W)), row(HEAD_DIM), row(HEAD_DIM),
            share((per_step,) + s0.shape[1:]),
            share((per_step,) + cache_k.shape[1:]), share((per_step,) + cache_v.shape[1:]),
        ],
        out_specs=[
            x_tile(back),
            per_seq((RET_HEADS, HEAD_DIM, HEAD_DIM)),
            per_seq((CONV_HIST, BRANCH_W)),
            share((1, per_step, BRANCH_W)), share((1, per_step, BRANCH_W)),
            share((per_step,) + s0.shape[1:]),
        ],
        out_shape=[
            jax.ShapeDtypeStruct((batch, seq, D_MODEL), F32),
            jax.ShapeDtypeStruct((batch, RET_HEADS, HEAD_DIM, HEAD_DIM), F32),
            jax.ShapeDtypeStruct((batch, CONV_HIST, BRANCH_W), F32),
            jax.ShapeDtypeStruct((n_tiles, per_step, BRANCH_W), F32),
            jax.ShapeDtypeStruct((n_tiles, per_step, BRANCH_W), F32),
            jax.ShapeDtypeStruct(s0.shape, F32),
        ],
        scratch_shapes=[
            pltpu.VMEM((HIST_PAD + tile, BRANCH_W), F32),
            pltpu.VMEM((tile, D_MODEL), BF16),
            pltpu.VMEM((tile, BRANCH_W), BF16),
            pltpu.VMEM((tile, BRANCH_W), BF16),
            pltpu.VMEM((tile, BRANCH_W), BF16),
        ],
        compiler_params=pltpu.CompilerParams(
            dimension_semantics=("arbitrary",), vmem_limit_bytes=VMEM_LIMIT_BYTES),
        name="prompt_layer",
    )(x, x, cos2, sin2, mk_bf, mv_bf, norm_w, w_in_bf, gn_w, conv_w, conv_b, ln_w, ln_b,
      w_br_ret_bf, w_br_conv_bf, w_br_mem_bf, w_out_bf, final_w, intra, qdec, kdec,
      z_steps, cos_s, sin_s, s0, cache_k, cache_v)


SAMPLE_PROJ_BLOCK = 1536


def _sample_proj_kernel(x_ref, norm_w_ref, w_ref, z_ref, w_bf_ref):
    x = x_ref[...]
    h_bf = (x * _rms_scale(x) * norm_w_ref[...]).astype(BF16)
    w_bf = w_ref[0].astype(BF16)
    w_bf_ref[...] = w_bf
    z_ref[...] = _dot(h_bf, w_bf)


def _sample_proj(xs, norm_w, w_in):
    n = xs.shape[0]
    return pl.pallas_call(
        _sample_proj_kernel,
        grid=(IN_W // SAMPLE_PROJ_BLOCK,),
        in_specs=[
            pl.BlockSpec((n, D_MODEL), lambda j: (0, 0)),
            pl.BlockSpec((1, D_MODEL), lambda j: (0, 0)),
            pl.BlockSpec((1, D_MODEL, SAMPLE_PROJ_BLOCK), lambda j: (0, 0, j)),
        ],
        out_specs=[pl.BlockSpec((n, SAMPLE_PROJ_BLOCK), lambda j: (0, j)),
                   pl.BlockSpec((D_MODEL, SAMPLE_PROJ_BLOCK), lambda j: (0, j))],
        out_shape=[jax.ShapeDtypeStruct((n, IN_W), F32),
                   jax.ShapeDtypeStruct((D_MODEL, IN_W), BF16)],
        compiler_params=pltpu.CompilerParams(
            dimension_semantics=("arbitrary",), vmem_limit_bytes=VMEM_LIMIT_BYTES),
        name="sample_proj",
    )(xs, norm_w, w_in)


def _sample_sequences(gamma, z, cos2, sin2, s0_ref, ck_ref, cv_ref, o_ref, att_ref, s_new_ref):
    n = z.shape[0]
    n_mem = ck_ref.shape[1] // MEM_HEADS
    q_scale = HEAD_DIM ** -0.5
    mem_scale = HEAD_DIM ** -0.5
    zq = z[:, OFF_QR:OFF_QR + BRANCH_W]
    zk = z[:, OFF_KR:OFF_KR + BRANCH_W]
    v = z[:, OFF_VR:OFF_VR + BRANCH_W]
    qm = z[:, OFF_QM:OFF_QM + BRANCH_W]
    q = [_rotary(_head(zq, h), cos2, sin2) * q_scale for h in range(RET_HEADS)]
    k = [_rotary(_head(zk, h), cos2, sin2) for h in range(RET_HEADS)]

    eye = (lax.broadcasted_iota(jnp.int32, (HEAD_DIM, HEAD_DIM), 0)
           == lax.broadcasted_iota(jnp.int32, (HEAD_DIM, HEAD_DIM), 1))
    ones_bf = jnp.ones((HEAD_DIM, HEAD_DIM), BF16)

    def as_columns(row):
        diag = jnp.where(eye, jnp.broadcast_to(row, (HEAD_DIM, HEAD_DIM)), 0.0)
        return _dot(diag.astype(BF16), ones_bf)

    for b in range(n):
        for h in range(RET_HEADS):
            q_row = q[h][b:b + 1]
            k_row = k[h][b:b + 1]
            v_row = _head(v, h)[b:b + 1]
            s0 = s0_ref[b, h]
            qk = jnp.sum(q_row * k_row, axis=-1, keepdims=True)
            qs = jnp.sum(as_columns(q_row) * s0, axis=0, keepdims=True)
            o_ref[b:b + 1, h * HEAD_DIM:(h + 1) * HEAD_DIM] = qk * v_row + gamma[h] * qs
            s_new_ref[b, h] = gamma[h] * s0 + as_columns(k_row) * v_row

        for h in range(MEM_HEADS):
            head_rows = pl.ds(h, n_mem, stride=MEM_HEADS)
            prod = ck_ref[b, head_rows, :] * _head(qm, h)[b:b + 1]
            sc = jnp.sum(prod, axis=-1, keepdims=True) * mem_scale
            e = jnp.exp(sc - jnp.max(sc, axis=0, keepdims=True))
            p = e / jnp.sum(e, axis=0, keepdims=True)
            att_ref[b:b + 1, h * HEAD_DIM:(h + 1) * HEAD_DIM] = jnp.sum(
                p * cv_ref[b, head_rows, :], axis=0, keepdims=True)


def _sample_out_kernel(x_ref, z_ref, o_ref, att_ref, hist_ref, conv_w_ref, gn_w_ref, conv_b_ref,
                       ln_w_ref, ln_b_ref, w_br_ret_ref, w_br_conv_ref, w_br_mem_ref, w_out_ref,
                       final_w_ref, y_ref, hist_new_ref):
    gate = lambda off: _silu(z_ref[:, off:off + BRANCH_W])
    yr = gate(OFF_GR) * _group_norm_heads(o_ref[...], gn_w_ref[...])

    u = z_ref[:, OFF_AC:OFF_AC + BRANCH_W] * _sigmoid(z_ref[:, OFF_BC:OFF_BC + BRANCH_W])
    conv = u * conv_w_ref[0, CONV_HIST:CONV_K, :] + conv_b_ref[...]
    for j in range(CONV_HIST):
        conv = conv + hist_ref[j] * conv_w_ref[0, j:j + 1, :]
        hist_new_ref[j] = hist_ref[j + 1] if j + 1 < CONV_HIST else u
    cn = _layernorm_rows(conv) * ln_w_ref[...] + ln_b_ref[...]
    yc = gate(OFF_GC) * _silu(cn)

    ym = gate(OFF_GM) * att_ref[...]
    gate_pre = lambda i: z_ref[:, OFF_MERGE + i * D_MODEL:OFF_MERGE + (i + 1) * D_MODEL]
    y_ref[...] = _merge_and_project(x_ref[...], gate_pre, yr, yc, ym, w_br_ret_ref,
                                    w_br_conv_ref, w_br_mem_ref, w_out_ref, final_w_ref[...])


def _sample_out(xs, z, o, att, hist, conv_w, gn_w, conv_b, ln_w, ln_b,
                w_br_ret_bf, w_br_conv_bf, w_br_mem_bf, w_out_bf, final_w):
    return pl.pallas_call(
        _sample_out_kernel,
        out_shape=[jax.ShapeDtypeStruct(xs.shape, F32), jax.ShapeDtypeStruct(hist.shape, F32)],
        compiler_params=pltpu.CompilerParams(vmem_limit_bytes=VMEM_LIMIT_BYTES),
        name="sample_out",
    )(xs, z, o, att, hist, conv_w, gn_w, conv_b, ln_w, ln_b,
      w_br_ret_bf, w_br_conv_bf, w_br_mem_bf, w_out_bf, final_w)


def _rotary_tables(pos):
    half = HEAD_DIM // 2
    inv = ROPE_BASE ** (-jnp.arange(half, dtype=F32) / half)
    ang = pos[:, None] * inv[None, :]
    cos, sin = jnp.cos(ang), jnp.sin(ang)
    return jnp.concatenate([cos, cos], axis=-1), jnp.concatenate([-sin, sin], axis=-1)


def _decay_tables():
    c = RET_CHUNK
    log_g = jnp.log1p(-jnp.exp2(-5.0 - jnp.arange(RET_HEADS, dtype=F32)))
    idx = jnp.arange(c, dtype=F32)
    diff = idx[:, None] - idx[None, :]
    intra = jnp.where(diff >= 0,
                      jnp.exp(jnp.maximum(diff, 0.0)[None] * log_g[:, None, None]), 0.0)
    q_dec = jnp.exp((idx + 1.0)[None, :] * log_g[:, None])
    k_dec = jnp.exp((c - 1.0 - idx)[None, :] * log_g[:, None])
    bcast = lambda a: jnp.broadcast_to(a[:, :, None], (RET_HEADS, c, HEAD_DIM))
    return intra, bcast(q_dec), bcast(k_dec)


def _gamma_powers(n):
    return tuple(float(np.exp(np.log1p(-np.exp2(-5.0 - h)) * n)) for h in range(RET_HEADS))


def kernel(x_prompt, x_sample, mem_prompt, state_ret, state_conv, cache_mem_k, cache_mem_v,
           norm_w, w_in, ret_gn_w, conv_w, conv_b, conv_ln_w, conv_ln_b, mem_norm_w,
           w_mem_kv, w_br_ret, w_br_conv, w_br_mem, w_out, final_norm_w):
    depth = w_in.shape[0]
    assert depth == 1, "single-layer step"
    batch, seq, _ = x_prompt.shape
    n_dec, dec_seq, _ = x_sample.shape
    assert dec_seq == 1 and seq % PROMPT_TILE == 0
    n_mem = mem_prompt.shape[1]

    bf = lambda w: w[0].astype(BF16)
    w_br_ret_bf, w_br_conv_bf, w_br_mem_bf, w_out_bf = bf(w_br_ret), bf(w_br_conv), bf(w_br_mem), bf(w_out)
    final_w = final_norm_w[None, :]

    intra, qdec, kdec = _decay_tables()
    cos_p, sin_p = _rotary_tables(jnp.arange(seq, dtype=F32))
    cos_s, sin_s = _rotary_tables(jnp.arange(dec_seq, dtype=F32) + PAST_LEN)

    xs = x_sample[:, 0, :]
    z_s, w_in_bf = _sample_proj(xs, norm_w, w_in)

    mk, mv, mk_bf, mv_bf = _mem_kv(mem_prompt, mem_norm_w, w_mem_kv)

    y_prompt, s_prompt, hist_prompt, o_s, att_s, s_sample = _prompt_layer(
        x_prompt, cos_p, sin_p, mk_bf, mv_bf, norm_w, w_in_bf, ret_gn_w, conv_w, conv_b,
        conv_ln_w, conv_ln_b, w_br_ret_bf, w_br_conv_bf, w_br_mem_bf, w_out_bf, final_w,
        intra, qdec, kdec, _gamma_powers(RET_CHUNK),
        z_s, cos_s, sin_s, state_ret[0],
        cache_mem_k.reshape(n_dec, n_mem * MEM_HEADS, HEAD_DIM),
        cache_mem_v.reshape(n_dec, n_mem * MEM_HEADS, HEAD_DIM), _gamma_powers(1))

    y_sample, hist_sample = _sample_out(
        xs, z_s, o_s.reshape(n_dec, BRANCH_W), att_s.reshape(n_dec, BRANCH_W),
        jnp.transpose(state_conv[0], (1, 0, 2)), conv_w, ret_gn_w, conv_b, conv_ln_w, conv_ln_b,
        w_br_ret_bf, w_br_conv_bf, w_br_mem_bf, w_out_bf, final_w)

    heads = lambda a: a.reshape(1, batch, n_mem, MEM_HEADS, HEAD_DIM)
    return (y_prompt, y_sample[:, None, :], s_prompt[None], s_sample[None],
            hist_prompt[None], jnp.transpose(hist_sample, (1, 0, 2))[None], heads(mk), heads(mv))
```

```python
import functools

import numpy as np
import jax
import jax.numpy as jnp
from jax import lax
from jax.experimental import pallas as pl
from jax.experimental.pallas import tpu as pltpu

F32 = jnp.float32
BF16 = jnp.bfloat16

D_MODEL = 1024
RET_HEADS = 4
HEAD_DIM = 128
SUBLANES = 8
BRANCH_W = 512
RET_CHUNK = 128
ROPE_BASE = 10000.0
CONV_K = 31
CONV_HIST = CONV_K - 1
MEM_HEADS = 4
N_BRANCH = 3
EPS = 1e-6
PAST_LEN = 16384

OFF_QR, OFF_KR, OFF_VR, OFF_GR = 0, 512, 1024, 1536
OFF_AC, OFF_BC, OFF_GC = 2048, 2560, 3072
OFF_QM, OFF_GM = 3584, 4096
OFF_MERGE = 4608
IN_W = OFF_MERGE + N_BRANCH * D_MODEL

V7X_VMEM_BYTES = 64 * 1024 * 1024
VMEM_LIMIT_BYTES = V7X_VMEM_BYTES - 8 * 1024 * 1024

PROMPT_TILE = 256
HIST_PAD = 32


def _sigmoid(x):
    return 0.5 * jnp.tanh(0.5 * x) + 0.5


def _silu(x):
    hx = 0.5 * x
    return hx * (jnp.tanh(hx) + 1.0)


def _rms_scale(x):
    return lax.rsqrt(jnp.mean(x * x, axis=-1, keepdims=True) + EPS)


def _layernorm_rows(x):
    mu = jnp.mean(x, axis=-1, keepdims=True)
    xc = x - mu
    var = jnp.mean(xc * xc, axis=-1, keepdims=True)
    return xc * lax.rsqrt(var + EPS)


def _rotary(x, cos2, sin2):
    return x * cos2 + pltpu.roll(x, HEAD_DIM // 2, axis=1) * sin2


def _dot(a, b):
    return jnp.dot(a, b, preferred_element_type=F32)


def _dot_nt(a, b):
    return lax.dot_general(a, b, (((1,), (1,)), ((), ())), preferred_element_type=F32)


def _dot_tn(a, b):
    return lax.dot_general(a, b, (((0,), (0,)), ((), ())), preferred_element_type=F32)


def _head(x, h):
    return x[:, h * HEAD_DIM:(h + 1) * HEAD_DIM]


def _group_norm_heads(o, gn_w):
    parts = [_layernorm_rows(_head(o, h)) for h in range(RET_HEADS)]
    return jnp.concatenate(parts, axis=-1) * gn_w


def _merge_and_project(x, gate_pre, yr, yc, ym, w_br_ret_ref, w_br_conv_ref,
                       w_br_mem_ref, w_out_ref, final_w):
    branches = (
        _dot(yr.astype(BF16), w_br_ret_ref[...]),
        _dot(yc.astype(BF16), w_br_conv_ref[...]),
        _dot(ym.astype(BF16), w_br_mem_ref[...]),
    )
    merged = None
    for i, br in enumerate(branches):
        term = _sigmoid(gate_pre(i)) * br
        merged = term if merged is None else merged + term
    out = x + _dot(merged.astype(BF16), w_out_ref[...])
    return out * _rms_scale(out) * final_w


MEM_KV_BLOCK = 4


def _mem_kv_kernel(mem_ref, norm_w_ref, w_ref, k_ref, v_ref, kbf_ref, vbf_ref):
    nb, n_mem, _ = mem_ref.shape
    m = mem_ref[...].reshape(nb * n_mem, D_MODEL)
    hm = (m * _rms_scale(m) * norm_w_ref[...]).astype(BF16)
    kv = _dot(hm, w_ref[0].astype(BF16))
    for b in range(nb):
        k = kv[b * n_mem:(b + 1) * n_mem, :BRANCH_W]
        v = kv[b * n_mem:(b + 1) * n_mem, BRANCH_W:]
        for h in range(MEM_HEADS):
            k_ref[b, pl.ds(h, n_mem, stride=MEM_HEADS), :] = _head(k, h)
            v_ref[b, pl.ds(h, n_mem, stride=MEM_HEADS), :] = _head(v, h)
        kbf_ref[b] = k.astype(BF16)
        vbf_ref[b] = v.astype(BF16)


def _mem_kv(mem, mem_norm_w, w_mem_kv):
    batch, n_mem, _ = mem.shape
    nb = MEM_KV_BLOCK
    blk = lambda dt: jax.ShapeDtypeStruct((batch, n_mem, BRANCH_W), dt)
    out_spec = pl.BlockSpec((nb, n_mem, BRANCH_W), lambda b: (b, 0, 0))
    rows_shape = (batch, n_mem * MEM_HEADS, HEAD_DIM)
    rows_spec = pl.BlockSpec((nb, n_mem * MEM_HEADS, HEAD_DIM), lambda b: (b, 0, 0))
    return pl.pallas_call(
        _mem_kv_kernel,
        grid=(batch // nb,),
        in_specs=[
            pl.BlockSpec((nb, n_mem, D_MODEL), lambda b: (b, 0, 0)),
            pl.BlockSpec((1, D_MODEL), lambda b: (0, 0)),
            pl.BlockSpec((1, D_MODEL, 2 * BRANCH_W), lambda b: (0, 0, 0)),
        ],
        out_specs=[rows_spec, rows_spec, out_spec, out_spec],
        out_shape=[jax.ShapeDtypeStruct(rows_shape, F32), jax.ShapeDtypeStruct(rows_shape, F32),
                   blk(BF16), blk(BF16)],
        compiler_params=pltpu.CompilerParams(dimension_semantics=("arbitrary",)),
        name="mem_kv",
    )(mem, mem_norm_w, w_mem_kv)


def _prompt_kernel(c_dec, gamma, tiles_per_seq, n_tiles,
                   x_ref, xb_ref, cos_ref, sin_ref, mk_ref, mv_ref, norm_w_ref, w_in_ref,
                   gn_w_ref, conv_w_ref, conv_b_ref, ln_w_ref, ln_b_ref,
                   w_br_ret_ref, w_br_conv_ref, w_br_mem_ref, w_out_ref, final_w_ref,
                   intra_ref, qdec_ref, kdec_ref,
                   zs_ref, cos_s_ref, sin_s_ref, s0_ref, ck_ref, cv_ref,
                   y_ref, s_ref, hist_ref, os_ref, atts_ref, s_new_ref,
                   ubuf_ref, h_scr, yr_scr, yc_scr, ym_scr, os_scr, atts_scr):
    i = pl.program_id(0)
    t = lax.rem(jnp.minimum(i, n_tiles - 1), tiles_per_seq)
    live = i < n_tiles
    tile = x_ref.shape[1]

    @pl.when(i == 0)
    def _():
        h_scr[...] = jnp.zeros_like(h_scr)
        yr_scr[...] = jnp.zeros_like(yr_scr)
        yc_scr[...] = jnp.zeros_like(yc_scr)
        ym_scr[...] = jnp.zeros_like(ym_scr)
        os_scr[...] = jnp.zeros_like(os_scr)
        atts_scr[...] = jnp.zeros_like(atts_scr)

    @pl.when(jnp.logical_and(t == 0, live))
    def _():
        s_ref[...] = jnp.zeros_like(s_ref)
        ubuf_ref[0:HIST_PAD, :] = jnp.zeros((HIST_PAD, BRANCH_W), F32)

    h_prev = h_scr[...]
    half = D_MODEL // 2

    def gate_job(k):
        lo = OFF_MERGE + k * half
        return lambda: _dot(h_prev, w_in_ref[:, lo:lo + half])

    back_jobs = [gate_job(k) for k in range(2 * N_BRANCH)] + [
        lambda: _dot(yr_scr[...], w_br_ret_ref[...]),
        lambda: _dot(yc_scr[...], w_br_conv_ref[...]),
        lambda: _dot(ym_scr[...], w_br_mem_ref[...]),
    ]
    back_out = [None] * len(back_jobs)

    def run_back(k):
        back_out[k] = back_jobs[k]()

    x = x_ref[0]
    h_bf = (x * _rms_scale(x) * norm_w_ref[...]).astype(BF16)
    run_back(8)

    def proj(off, width=BRANCH_W):
        return _dot(h_bf, w_in_ref[:, off:off + width])

    za = proj(OFF_AC)
    zb = proj(OFF_BC)
    zq = proj(OFF_QR)
    zk = proj(OFF_KR)
    v_bf = proj(OFF_VR).astype(BF16)
    u = za * _sigmoid(zb)
    ubuf_ref[HIST_PAD:HIST_PAD + tile, :] = u
    first = HIST_PAD - CONV_HIST
    conv = None
    for r in range(8):
        part = None
        for j in range(CONV_K):
            if (first + j) % 8 != r:
                continue
            term = ubuf_ref[first + j:first + j + tile, :] * conv_w_ref[0, j:j + 1, :]
            part = term if part is None else part + term
        conv = part if conv is None else conv + part
        run_back(r)
    conv = conv + conv_b_ref[...]
    ubuf_ref[0:HIST_PAD, :] = jnp.where(live, ubuf_ref[tile:tile + HIST_PAD, :],
                                        ubuf_ref[0:HIST_PAD, :])

    z_gc = proj(OFF_GC)
    z_gr = proj(OFF_GR)
    z_qm = proj(OFF_QM)
    z_gm = proj(OFF_GM)

    merged = []
    for k in range(2):
        cols = slice(k * half, (k + 1) * half)
        acc = None
        for b in range(N_BRANCH):
            term = _sigmoid(back_out[2 * b + k]) * back_out[2 * N_BRANCH + b][:, cols]
            acc = term if acc is None else acc + term
        merged.append(acc.astype(BF16))
    merged = jnp.concatenate(merged, axis=-1)

    cos2 = cos_ref[...]
    sin2 = sin_ref[...]
    q_scale = HEAD_DIM ** -0.5
    o_heads = []
    for h in range(RET_HEADS):
        q_h = _rotary(_head(zq, h), cos2, sin2) * q_scale
        k_h = _rotary(_head(zk, h), cos2, sin2)
        v_h = _head(v_bf, h)
        s_old = s_ref[0, h]
        s = s_old
        o_chunks = []
        for c in range(tile // RET_CHUNK):
            rows = slice(c * RET_CHUNK, (c + 1) * RET_CHUNK)
            qc = q_h[rows].astype(BF16)
            kc = k_h[rows]
            vc = v_h[rows]
            sc = _dot_nt(qc, kc.astype(BF16)) * intra_ref[h]
            o_chunks.append(_dot(sc.astype(BF16), vc) + _dot(qc, s.astype(BF16)) * qdec_ref[h])
            kd = (kc * kdec_ref[h]).astype(BF16)
            s = s * c_dec[h] + _dot_tn(kd, vc)
        s_ref[0, h] = jnp.where(live, s, s_old)
        o_heads.append(_layernorm_rows(jnp.concatenate(o_chunks, axis=0)))

    out = xb_ref[0] + _dot(merged, w_out_ref[...])

    ret = jnp.concatenate(o_heads, axis=-1) * gn_w_ref[...]
    yr = _silu(z_gr) * ret
    cn = _layernorm_rows(conv) * ln_w_ref[...] + ln_b_ref[...]
    yc = _silu(z_gc) * _silu(cn)

    qm = z_qm.astype(BF16)
    mk = mk_ref[0]
    mv = mv_ref[0]
    att = []
    for h in range(MEM_HEADS):
        sc = _dot_nt(_head(qm, h), _head(mk, h)) * (HEAD_DIM ** -0.5)
        e = jnp.exp(sc - jnp.max(sc, axis=-1, keepdims=True))
        p = e / jnp.sum(e, axis=-1, keepdims=True)
        att.append(_dot(p.astype(BF16), _head(mv, h)))
    ym = _silu(z_gm) * jnp.concatenate(att, axis=-1)

    y_ref[0] = out * _rms_scale(out) * final_w_ref[...]

    h_scr[...] = h_bf
    yr_scr[...] = yr.astype(BF16)
    yc_scr[...] = yc.astype(BF16)
    ym_scr[...] = ym.astype(BF16)

    @pl.when(jnp.logical_and(t == tiles_per_seq - 1, live))
    def _():
        hist_ref[0] = ubuf_ref[first:HIST_PAD, :]

    per_step = s0_ref.shape[0]
    r0 = lax.rem(jnp.minimum(i, n_tiles - 1), SUBLANES // per_step) * per_step
    row_id = lax.broadcasted_iota(jnp.int32, (SUBLANES, 1), 0)

    def own_rows(off):
        block = zs_ref[:, off:off + BRANCH_W]
        return jnp.concatenate(
            [jnp.sum(jnp.where(row_id == r0 + b, block, 0.0), axis=0, keepdims=True)
             for b in range(per_step)], axis=0)

    o_rows, att_rows = _sample_sequences(
        gamma, own_rows(OFF_QR), own_rows(OFF_KR), own_rows(OFF_VR), own_rows(OFF_QM),
        cos_s_ref[...], sin_s_ref[...], s0_ref, ck_ref, cv_ref, s_new_ref)
    for scr, out_ref, rows in ((os_scr, os_ref, o_rows), (atts_scr, atts_ref, att_rows)):
        block = scr[...]
        for b in range(per_step):
            mine = jnp.logical_and(row_id == r0 + b, live)
            block = jnp.where(mine, rows[b], block)
        scr[...] = block
        out_ref[...] = block


def _const_spec(shape, single_buffer=False):
    zeros = (0,) * len(shape)
    if single_buffer:
        return pl.BlockSpec(shape, lambda i: zeros, pipeline_mode=pl.Buffered(1))
    return pl.BlockSpec(shape, lambda i: zeros)


def _prompt_layer(x, cos2, sin2, mk_bf, mv_bf, norm_w, w_in_bf, gn_w, conv_w, conv_b, ln_w,
                  ln_b, w_br_ret_bf, w_br_conv_bf, w_br_mem_bf, w_out_bf, final_w,
                  intra, qdec, kdec, c_dec,
                  z_s, cos_s, sin_s, s0, cache_k, cache_v, gamma):
    batch, seq, _ = x.shape
    n_mem = mk_bf.shape[1]
    tile = PROMPT_TILE
    tiles_per_seq = seq // tile
    n_tiles = batch * tiles_per_seq

    def front(i):
        j = jnp.minimum(i, n_tiles - 1)
        return j // tiles_per_seq, lax.rem(j, tiles_per_seq)

    def back(i):
        j = jnp.maximum(i - 1, 0)
        return j // tiles_per_seq, lax.rem(j, tiles_per_seq)

    row = lambda w: _const_spec((1, w))
    weight = lambda a: _const_spec(a.shape, single_buffer=True)
    table = _const_spec((RET_HEADS, RET_CHUNK, RET_CHUNK))
    n_dec = z_s.shape[0]
    per_step = n_dec // n_tiles
    assert per_step * n_tiles == n_dec
    share = lambda shape: pl.BlockSpec(
        shape, lambda i: (jnp.minimum(i, n_tiles - 1),) + (0,) * (len(shape) - 1))
    assert SUBLANES % per_step == 0
    rows8 = lambda w: pl.BlockSpec(
        (SUBLANES, w), lambda i: (jnp.minimum(i, n_tiles - 1) // (SUBLANES // per_step), 0))
    x_tile = lambda which: pl.BlockSpec((1, tile, D_MODEL), lambda i: (*which(i), 0))
    per_seq = lambda shape: pl.BlockSpec(
        (1,) + shape, lambda i: (front(i)[0],) + (0,) * len(shape))
    return pl.pallas_call(
        functools.partial(_prompt_kernel, c_dec, gamma, tiles_per_seq, n_tiles),
        grid=(n_tiles + 1,),
        in_specs=[
            x_tile(front), x_tile(back),
            pl.BlockSpec((tile, HEAD_DIM), lambda i: (front(i)[1], 0)),
            pl.BlockSpec((tile, HEAD_DIM), lambda i: (front(i)[1], 0)),
            per_seq((n_mem, BRANCH_W)), per_seq((n_mem, BRANCH_W)),
            row(D_MODEL), weight(w_in_bf), row(BRANCH_W),
            _const_spec((1, CONV_K, BRANCH_W)), row(BRANCH_W), row(BRANCH_W), row(BRANCH_W),
            weight(w_br_ret_bf), weight(w_br_conv_bf), weight(w_br_mem_bf), weight(w_out_bf),
            row(D_MODEL), table, table, table,
            rows8(IN_W), row(HEAD_DIM), row(HEAD_DIM),
            share((per_step,) + s0.shape[1:]),
            share((per_step,) + cache_k.shape[1:]), share((per_step,) + cache_v.shape[1:]),
        ],
        out_specs=[
            x_tile(back),
            per_seq((RET_HEADS, HEAD_DIM, HEAD_DIM)),
            per_seq((CONV_HIST, BRANCH_W)),
            rows8(BRANCH_W), rows8(BRANCH_W),
            share((per_step,) + s0.shape[1:]),
        ],
        out_shape=[
            jax.ShapeDtypeStruct((batch, seq, D_MODEL), F32),
            jax.ShapeDtypeStruct((batch, RET_HEADS, HEAD_DIM, HEAD_DIM), F32),
            jax.ShapeDtypeStruct((batch, CONV_HIST, BRANCH_W), F32),
            jax.ShapeDtypeStruct((n_dec, BRANCH_W), F32),
            jax.ShapeDtypeStruct((n_dec, BRANCH_W), F32),
            jax.ShapeDtypeStruct(s0.shape, F32),
        ],
        scratch_shapes=[
            pltpu.VMEM((HIST_PAD + tile, BRANCH_W), F32),
            pltpu.VMEM((tile, D_MODEL), BF16),
            pltpu.VMEM((tile, BRANCH_W), BF16),
            pltpu.VMEM((tile, BRANCH_W), BF16),
            pltpu.VMEM((tile, BRANCH_W), BF16),
            pltpu.VMEM((SUBLANES, BRANCH_W), F32),
            pltpu.VMEM((SUBLANES, BRANCH_W), F32),
        ],
        compiler_params=pltpu.CompilerParams(
            dimension_semantics=("arbitrary",), vmem_limit_bytes=VMEM_LIMIT_BYTES),
        name="prompt_layer",
    )(x, x, cos2, sin2, mk_bf, mv_bf, norm_w, w_in_bf, gn_w, conv_w, conv_b, ln_w, ln_b,
      w_br_ret_bf, w_br_conv_bf, w_br_mem_bf, w_out_bf, final_w, intra, qdec, kdec,
      z_s, cos_s, sin_s, s0, cache_k, cache_v)


SAMPLE_PROJ_BLOCK = 1536


def _sample_proj_kernel(x_ref, norm_w_ref, w_ref, z_ref, w_bf_ref):
    x = x_ref[...]
    h_bf = (x * _rms_scale(x) * norm_w_ref[...]).astype(BF16)
    w_bf = w_ref[0].astype(BF16)
    w_bf_ref[...] = w_bf
    z_ref[...] = _dot(h_bf, w_bf)


def _sample_proj(xs, norm_w, w_in):
    n = xs.shape[0]
    return pl.pallas_call(
        _sample_proj_kernel,
        grid=(IN_W // SAMPLE_PROJ_BLOCK,),
        in_specs=[
            pl.BlockSpec((n, D_MODEL), lambda j: (0, 0)),
            pl.BlockSpec((1, D_MODEL), lambda j: (0, 0)),
            pl.BlockSpec((1, D_MODEL, SAMPLE_PROJ_BLOCK), lambda j: (0, 0, j)),
        ],
        out_specs=[pl.BlockSpec((n, SAMPLE_PROJ_BLOCK), lambda j: (0, j)),
                   pl.BlockSpec((D_MODEL, SAMPLE_PROJ_BLOCK), lambda j: (0, j))],
        out_shape=[jax.ShapeDtypeStruct((n, IN_W), F32),
                   jax.ShapeDtypeStruct((D_MODEL, IN_W), BF16)],
        compiler_params=pltpu.CompilerParams(
            dimension_semantics=("arbitrary",), vmem_limit_bytes=VMEM_LIMIT_BYTES),
        name="sample_proj",
    )(xs, norm_w, w_in)


def _sample_sequences(gamma, zq, zk, v, qm, cos2, sin2, s0_ref, ck_ref, cv_ref, s_new_ref):
    n = zq.shape[0]
    n_mem = ck_ref.shape[1] // MEM_HEADS
    q_scale = HEAD_DIM ** -0.5
    mem_scale = HEAD_DIM ** -0.5
    q = [_rotary(_head(zq, h), cos2, sin2) * q_scale for h in range(RET_HEADS)]
    k = [_rotary(_head(zk, h), cos2, sin2) for h in range(RET_HEADS)]

    eye = (lax.broadcasted_iota(jnp.int32, (HEAD_DIM, HEAD_DIM), 0)
           == lax.broadcasted_iota(jnp.int32, (HEAD_DIM, HEAD_DIM), 1))
    ones_bf = jnp.ones((HEAD_DIM, HEAD_DIM), BF16)

    def as_columns(row):
        diag = jnp.where(eye, jnp.broadcast_to(row, (HEAD_DIM, HEAD_DIM)), 0.0)
        return _dot(diag.astype(BF16), ones_bf)

    o_rows, att_rows = [], []
    for b in range(n):
        o_heads = []
        for h in range(RET_HEADS):
            q_row = q[h][b:b + 1]
            k_row = k[h][b:b + 1]
            v_row = _head(v, h)[b:b + 1]
            s0 = s0_ref[b, h]
            qk = jnp.sum(q_row * k_row, axis=-1, keepdims=True)
            qs = jnp.sum(as_columns(q_row) * s0, axis=0, keepdims=True)
            o_heads.append(qk * v_row + gamma[h] * qs)
            s_new_ref[b, h] = gamma[h] * s0 + as_columns(k_row) * v_row
        o_rows.append(jnp.concatenate(o_heads, axis=-1))

        att_heads = []
        for h in range(MEM_HEADS):
            head_rows = pl.ds(h, n_mem, stride=MEM_HEADS)
            prod = ck_ref[b, head_rows, :] * _head(qm, h)[b:b + 1]
            sc = jnp.sum(prod, axis=-1, keepdims=True) * mem_scale
            e = jnp.exp(sc - jnp.max(sc, axis=0, keepdims=True))
            p = e / jnp.sum(e, axis=0, keepdims=True)
            att_heads.append(jnp.sum(p * cv_ref[b, head_rows, :], axis=0, keepdims=True))
        att_rows.append(jnp.concatenate(att_heads, axis=-1))
    return o_rows, att_rows


def _sample_out_kernel(x_ref, z_ref, o_ref, att_ref, hist_ref, conv_w_ref, gn_w_ref, conv_b_ref,
                       ln_w_ref, ln_b_ref, w_br_ret_ref, w_br_conv_ref, w_br_mem_ref, w_out_ref,
                       final_w_ref, y_ref, hist_new_ref):
    gate = lambda off: _silu(z_ref[:, off:off + BRANCH_W])
    yr = gate(OFF_GR) * _group_norm_heads(o_ref[...], gn_w_ref[...])

    u = z_ref[:, OFF_AC:OFF_AC + BRANCH_W] * _sigmoid(z_ref[:, OFF_BC:OFF_BC + BRANCH_W])
    conv = u * conv_w_ref[0, CONV_HIST:CONV_K, :] + conv_b_ref[...]
    for j in range(CONV_HIST):
        conv = conv + hist_ref[j] * conv_w_ref[0, j:j + 1, :]
        hist_new_ref[j] = hist_ref[j + 1] if j + 1 < CONV_HIST else u
    cn = _layernorm_rows(conv) * ln_w_ref[...] + ln_b_ref[...]
    yc = gate(OFF_GC) * _silu(cn)

    ym = gate(OFF_GM) * att_ref[...]
    gate_pre = lambda i: z_ref[:, OFF_MERGE + i * D_MODEL:OFF_MERGE + (i + 1) * D_MODEL]
    y_ref[...] = _merge_and_project(x_ref[...], gate_pre, yr, yc, ym, w_br_ret_ref,
                                    w_br_conv_ref, w_br_mem_ref, w_out_ref, final_w_ref[...])


def _sample_out(xs, z, o, att, hist, conv_w, gn_w, conv_b, ln_w, ln_b,
                w_br_ret_bf, w_br_conv_bf, w_br_mem_bf, w_out_bf, final_w):
    return pl.pallas_call(
        _sample_out_kernel,
        out_shape=[jax.ShapeDtypeStruct(xs.shape, F32), jax.ShapeDtypeStruct(hist.shape, F32)],
        compiler_params=pltpu.CompilerParams(vmem_limit_bytes=VMEM_LIMIT_BYTES),
        name="sample_out",
    )(xs, z, o, att, hist, conv_w, gn_w, conv_b, ln_w, ln_b,
      w_br_ret_bf, w_br_conv_bf, w_br_mem_bf, w_out_bf, final_w)


def _rotary_tables(pos):
    half = HEAD_DIM // 2
    inv = ROPE_BASE ** (-jnp.arange(half, dtype=F32) / half)
    ang = pos[:, None] * inv[None, :]
    cos, sin = jnp.cos(ang), jnp.sin(ang)
    return jnp.concatenate([cos, cos], axis=-1), jnp.concatenate([-sin, sin], axis=-1)


def _decay_tables():
    c = RET_CHUNK
    log_g = jnp.log1p(-jnp.exp2(-5.0 - jnp.arange(RET_HEADS, dtype=F32)))
    idx = jnp.arange(c, dtype=F32)
    diff = idx[:, None] - idx[None, :]
    intra = jnp.where(diff >= 0,
                      jnp.exp(jnp.maximum(diff, 0.0)[None] * log_g[:, None, None]), 0.0)
    q_dec = jnp.exp((idx + 1.0)[None, :] * log_g[:, None])
    k_dec = jnp.exp((c - 1.0 - idx)[None, :] * log_g[:, None])
    bcast = lambda a: jnp.broadcast_to(a[:, :, None], (RET_HEADS, c, HEAD_DIM))
    return intra, bcast(q_dec), bcast(k_dec)


def _gamma_powers(n):
    return tuple(float(np.exp(np.log1p(-np.exp2(-5.0 - h)) * n)) for h in range(RET_HEADS))


def kernel(x_prompt, x_sample, mem_prompt, state_ret, state_conv, cache_mem_k, cache_mem_v,
           norm_w, w_in, ret_gn_w, conv_w, conv_b, conv_ln_w, conv_ln_b, mem_norm_w,
           w_mem_kv, w_br_ret, w_br_conv, w_br_mem, w_out, final_norm_w):
    depth = w_in.shape[0]
    assert depth == 1, "single-layer step"
    batch, seq, _ = x_prompt.shape
    n_dec, dec_seq, _ = x_sample.shape
    assert dec_seq == 1 and seq % PROMPT_TILE == 0
    n_mem = mem_prompt.shape[1]

    bf = lambda w: w[0].astype(BF16)
    w_br_ret_bf, w_br_conv_bf, w_br_mem_bf, w_out_bf = bf(w_br_ret), bf(w_br_conv), bf(w_br_mem), bf(w_out)
    final_w = final_norm_w[None, :]

    intra, qdec, kdec = _decay_tables()
    cos_p, sin_p = _rotary_tables(jnp.arange(seq, dtype=F32))
    cos_s, sin_s = _rotary_tables(jnp.arange(dec_seq, dtype=F32) + PAST_LEN)

    xs = x_sample[:, 0, :]
    z_s, w_in_bf = _sample_proj(xs, norm_w, w_in)

    mk, mv, mk_bf, mv_bf = _mem_kv(mem_prompt, mem_norm_w, w_mem_kv)

    y_prompt, s_prompt, hist_prompt, o_s, att_s, s_sample = _prompt_layer(
        x_prompt, cos_p, sin_p, mk_bf, mv_bf, norm_w, w_in_bf, ret_gn_w, conv_w, conv_b,
        conv_ln_w, conv_ln_b, w_br_ret_bf, w_br_conv_bf, w_br_mem_bf, w_out_bf, final_w,
        intra, qdec, kdec, _gamma_powers(RET_CHUNK),
        z_s, cos_s, sin_s, state_ret[0],
        cache_mem_k.reshape(n_dec, n_mem * MEM_HEADS, HEAD_DIM),
        cache_mem_v.reshape(n_dec, n_mem * MEM_HEADS, HEAD_DIM), _gamma_powers(1))

    y_sample, hist_sample = _sample_out(
        xs, z_s, o_s, att_s,
        jnp.transpose(state_conv[0], (1, 0, 2)), conv_w, ret_gn_w, conv_b, conv_ln_w, conv_ln_b,
        w_br_ret_bf, w_br_conv_bf, w_br_mem_bf, w_out_bf, final_w)

    heads = lambda a: a.reshape(1, batch, n_mem, MEM_HEADS, HEAD_DIM)
    return (y_prompt, y_sample[:, None, :], s_prompt[None], s_sample[None],
            hist_prompt[None], jnp.transpose(hist_sample, (1, 0, 2))[None], heads(mk), heads(mv))
```

```python
import functools

import numpy as np
import jax
import jax.numpy as jnp
from jax import lax
from jax.experimental import pallas as pl
from jax.experimental.pallas import tpu as pltpu

F32 = jnp.float32
BF16 = jnp.bfloat16

D_MODEL = 1024
RET_HEADS = 4
HEAD_DIM = 128
SUBLANES = 8
BRANCH_W = 512
RET_CHUNK = 128
ROPE_BASE = 10000.0
CONV_K = 31
CONV_HIST = CONV_K - 1
MEM_HEADS = 4
N_BRANCH = 3
EPS = 1e-6
PAST_LEN = 16384

OFF_QR, OFF_KR, OFF_VR, OFF_GR = 0, 512, 1024, 1536
OFF_AC, OFF_BC, OFF_GC = 2048, 2560, 3072
OFF_QM, OFF_GM = 3584, 4096
OFF_MERGE = 4608
IN_W = OFF_MERGE + N_BRANCH * D_MODEL

V7X_VMEM_BYTES = 64 * 1024 * 1024
VMEM_LIMIT_BYTES = V7X_VMEM_BYTES - 8 * 1024 * 1024

PROMPT_TILE = 256
HIST_PAD = 32


def _sigmoid(x):
    return 0.5 * jnp.tanh(0.5 * x) + 0.5


def _silu(x):
    hx = 0.5 * x
    return hx * (jnp.tanh(hx) + 1.0)


def _rms_scale(x):
    return lax.rsqrt(jnp.mean(x * x, axis=-1, keepdims=True) + EPS)


def _layernorm_rows(x):
    mu = jnp.mean(x, axis=-1, keepdims=True)
    xc = x - mu
    var = jnp.mean(xc * xc, axis=-1, keepdims=True)
    return xc * lax.rsqrt(var + EPS)


def _rotary(x, cos2, sin2):
    return x * cos2 + pltpu.roll(x, HEAD_DIM // 2, axis=1) * sin2


def _dot(a, b):
    return jnp.dot(a, b, preferred_element_type=F32)


def _dot_nt(a, b):
    return lax.dot_general(a, b, (((1,), (1,)), ((), ())), preferred_element_type=F32)


def _dot_tn(a, b):
    return lax.dot_general(a, b, (((0,), (0,)), ((), ())), preferred_element_type=F32)


def _head(x, h):
    return x[:, h * HEAD_DIM:(h + 1) * HEAD_DIM]


def _group_norm_heads(o, gn_w):
    parts = [_layernorm_rows(_head(o, h)) for h in range(RET_HEADS)]
    return jnp.concatenate(parts, axis=-1) * gn_w


def _merge_and_project(x, gate_pre, yr, yc, ym, w_br_ret_ref, w_br_conv_ref,
                       w_br_mem_ref, w_out_ref, final_w):
    branches = (
        _dot(yr.astype(BF16), w_br_ret_ref[...]),
        _dot(yc.astype(BF16), w_br_conv_ref[...]),
        _dot(ym.astype(BF16), w_br_mem_ref[...]),
    )
    merged = None
    for i, br in enumerate(branches):
        term = _sigmoid(gate_pre(i)) * br
        merged = term if merged is None else merged + term
    out = x + _dot(merged.astype(BF16), w_out_ref[...])
    return out * _rms_scale(out) * final_w


MEM_KV_BLOCK = 4


def _mem_kv_kernel(n_cast, mem_ref, norm_w_ref, w_ref, *refs):
    cast_in, (k_ref, v_ref, kbf_ref, vbf_ref), cast_out = (
        refs[:n_cast], refs[n_cast:n_cast + 4], refs[n_cast + 4:])

    @pl.when(pl.program_id(0) == 0)
    def _():
        for src, dst in zip(cast_in, cast_out):
            dst[...] = src[0].astype(BF16)

    nb, n_mem, _ = mem_ref.shape
    m = mem_ref[...].reshape(nb * n_mem, D_MODEL)
    hm = (m * _rms_scale(m) * norm_w_ref[...]).astype(BF16)
    kv = _dot(hm, w_ref[0].astype(BF16))
    for b in range(nb):
        k = kv[b * n_mem:(b + 1) * n_mem, :BRANCH_W]
        v = kv[b * n_mem:(b + 1) * n_mem, BRANCH_W:]
        for h in range(MEM_HEADS):
            k_ref[b, pl.ds(h, n_mem, stride=MEM_HEADS), :] = _head(k, h)
            v_ref[b, pl.ds(h, n_mem, stride=MEM_HEADS), :] = _head(v, h)
        kbf_ref[b] = k.astype(BF16)
        vbf_ref[b] = v.astype(BF16)


def _mem_kv(mem, mem_norm_w, w_mem_kv, to_cast):
    batch, n_mem, _ = mem.shape
    nb = MEM_KV_BLOCK
    blk = lambda dt: jax.ShapeDtypeStruct((batch, n_mem, BRANCH_W), dt)
    out_spec = pl.BlockSpec((nb, n_mem, BRANCH_W), lambda b: (b, 0, 0))
    rows_shape = (batch, n_mem * MEM_HEADS, HEAD_DIM)
    rows_spec = pl.BlockSpec((nb, n_mem * MEM_HEADS, HEAD_DIM), lambda b: (b, 0, 0))
    return pl.pallas_call(
        functools.partial(_mem_kv_kernel, len(to_cast)),
        grid=(batch // nb,),
        in_specs=[
            pl.BlockSpec((nb, n_mem, D_MODEL), lambda b: (b, 0, 0)),
            pl.BlockSpec((1, D_MODEL), lambda b: (0, 0)),
            pl.BlockSpec((1, D_MODEL, 2 * BRANCH_W), lambda b: (0, 0, 0)),
        ] + [pl.BlockSpec(w.shape, lambda b: (0, 0, 0), pipeline_mode=pl.Buffered(1))
             for w in to_cast],
        out_specs=[rows_spec, rows_spec, out_spec, out_spec]
                  + [pl.BlockSpec(w.shape[1:], lambda b: (0, 0)) for w in to_cast],
        out_shape=[jax.ShapeDtypeStruct(rows_shape, F32), jax.ShapeDtypeStruct(rows_shape, F32),
                   blk(BF16), blk(BF16)]
                  + [jax.ShapeDtypeStruct(w.shape[1:], BF16) for w in to_cast],
        compiler_params=pltpu.CompilerParams(
            dimension_semantics=("arbitrary",), vmem_limit_bytes=VMEM_LIMIT_BYTES),
        name="mem_kv",
    )(mem, mem_norm_w, w_mem_kv, *to_cast)


def _prompt_kernel(c_dec, gamma, tiles_per_seq, n_tiles,
                   x_ref, xb_ref, cos_ref, sin_ref, mk_ref, mv_ref, norm_w_ref, w_in_ref,
                   gn_w_ref, conv_w_ref, conv_b_ref, ln_w_ref, ln_b_ref,
                   w_br_ret_ref, w_br_conv_ref, w_br_mem_ref, w_out_ref, final_w_ref,
                   intra_ref, qdec_ref, kdec_ref,
                   zs_ref, cos_s_ref, sin_s_ref, s0_ref, ck_ref, cv_ref,
                   y_ref, s_ref, hist_ref, os_ref, atts_ref, s_new_ref,
                   ubuf_ref, h_scr, yr_scr, yc_scr, ym_scr, os_scr, atts_scr):
    i = pl.program_id(0)
    t = lax.rem(jnp.minimum(i, n_tiles - 1), tiles_per_seq)
    live = i < n_tiles
    tile = x_ref.shape[1]

    @pl.when(i == 0)
    def _():
        h_scr[...] = jnp.zeros_like(h_scr)
        yr_scr[...] = jnp.zeros_like(yr_scr)
        yc_scr[...] = jnp.zeros_like(yc_scr)
        ym_scr[...] = jnp.zeros_like(ym_scr)
        os_scr[...] = jnp.zeros_like(os_scr)
        atts_scr[...] = jnp.zeros_like(atts_scr)

    @pl.when(jnp.logical_and(t == 0, live))
    def _():
        s_ref[...] = jnp.zeros_like(s_ref)
        ubuf_ref[0:HIST_PAD, :] = jnp.zeros((HIST_PAD, BRANCH_W), F32)

    h_prev = h_scr[...]
    half = D_MODEL // 2

    def gate_job(k):
        lo = OFF_MERGE + k * half
        return lambda: _dot(h_prev, w_in_ref[:, lo:lo + half])

    back_jobs = [gate_job(k) for k in range(2 * N_BRANCH)] + [
        lambda: _dot(yr_scr[...], w_br_ret_ref[...]),
        lambda: _dot(yc_scr[...], w_br_conv_ref[...]),
        lambda: _dot(ym_scr[...], w_br_mem_ref[...]),
    ]
    back_out = [None] * len(back_jobs)

    def run_back(k):
        back_out[k] = back_jobs[k]()

    x = x_ref[0]
    h_bf = (x * _rms_scale(x) * norm_w_ref[...]).astype(BF16)
    run_back(8)

    def proj(off, width=BRANCH_W):
        return _dot(h_bf, w_in_ref[:, off:off + width])

    za = proj(OFF_AC)
    zb = proj(OFF_BC)
    zq = proj(OFF_QR)
    zk = proj(OFF_KR)
    v_bf = proj(OFF_VR).astype(BF16)
    u = za * _sigmoid(zb)
    ubuf_ref[HIST_PAD:HIST_PAD + tile, :] = u
    first = HIST_PAD - CONV_HIST
    conv = None
    for r in range(8):
        part = None
        for j in range(CONV_K):
            if (first + j) % 8 != r:
                continue
            term = ubuf_ref[first + j:first + j + tile, :] * conv_w_ref[0, j:j + 1, :]
            part = term if part is None else part + term
        conv = part if conv is None else conv + part
        run_back(r)
    conv = conv + conv_b_ref[...]
    ubuf_ref[0:HIST_PAD, :] = jnp.where(live, ubuf_ref[tile:tile + HIST_PAD, :],
                                        ubuf_ref[0:HIST_PAD, :])

    z_gc = proj(OFF_GC)
    z_gr = proj(OFF_GR)
    z_qm = proj(OFF_QM)
    z_gm = proj(OFF_GM)

    merged = []
    for k in range(2):
        cols = slice(k * half, (k + 1) * half)
        acc = None
        for b in range(N_BRANCH):
            term = _sigmoid(back_out[2 * b + k]) * back_out[2 * N_BRANCH + b][:, cols]
            acc = term if acc is None else acc + term
        merged.append(acc.astype(BF16))
    merged = jnp.concatenate(merged, axis=-1)

    cos2 = cos_ref[...]
    sin2 = sin_ref[...]
    q_scale = HEAD_DIM ** -0.5
    o_heads = []
    for h in range(RET_HEADS):
        q_h = _rotary(_head(zq, h), cos2, sin2) * q_scale
        k_h = _rotary(_head(zk, h), cos2, sin2)
        v_h = _head(v_bf, h)
        s_old = s_ref[0, h]
        s = s_old
        o_chunks = []
        for c in range(tile // RET_CHUNK):
            rows = slice(c * RET_CHUNK, (c + 1) * RET_CHUNK)
            qc = q_h[rows].astype(BF16)
            kc = k_h[rows]
            vc = v_h[rows]
            sc = _dot_nt(qc, kc.astype(BF16)) * intra_ref[h]
            o_chunks.append(_dot(sc.astype(BF16), vc) + _dot(qc, s.astype(BF16)) * qdec_ref[h])
            kd = (kc * kdec_ref[h]).astype(BF16)
            s = s * c_dec[h] + _dot_tn(kd, vc)
        s_ref[0, h] = jnp.where(live, s, s_old)
        o_heads.append(_layernorm_rows(jnp.concatenate(o_chunks, axis=0)))

    out = xb_ref[0] + _dot(merged, w_out_ref[...])

    ret = jnp.concatenate(o_heads, axis=-1) * gn_w_ref[...]
    yr = _silu(z_gr) * ret
    cn = _layernorm_rows(conv) * ln_w_ref[...] + ln_b_ref[...]
    yc = _silu(z_gc) * _silu(cn)

    qm = z_qm.astype(BF16)
    mk = mk_ref[0]
    mv = mv_ref[0]
    att = []
    for h in range(MEM_HEADS):
        sc = _dot_nt(_head(qm, h), _head(mk, h)) * (HEAD_DIM ** -0.5)
        e = jnp.exp(sc - jnp.max(sc, axis=-1, keepdims=True))
        p = e / jnp.sum(e, axis=-1, keepdims=True)
        att.append(_dot(p.astype(BF16), _head(mv, h)))
    ym = _silu(z_gm) * jnp.concatenate(att, axis=-1)

    y_ref[0] = out * _rms_scale(out) * final_w_ref[...]

    h_scr[...] = h_bf
    yr_scr[...] = yr.astype(BF16)
    yc_scr[...] = yc.astype(BF16)
    ym_scr[...] = ym.astype(BF16)

    @pl.when(jnp.logical_and(t == tiles_per_seq - 1, live))
    def _():
        hist_ref[0] = ubuf_ref[first:HIST_PAD, :]

    per_step = s0_ref.shape[0]
    r0 = lax.rem(jnp.minimum(i, n_tiles - 1), SUBLANES // per_step) * per_step
    row_id = lax.broadcasted_iota(jnp.int32, (SUBLANES, 1), 0)

    def own_rows(off):
        block = zs_ref[:, off:off + BRANCH_W]
        return jnp.concatenate(
            [jnp.sum(jnp.where(row_id == r0 + b, block, 0.0), axis=0, keepdims=True)
             for b in range(per_step)], axis=0)

    o_rows, att_rows = _sample_sequences(
        gamma, own_rows(OFF_QR), own_rows(OFF_KR), own_rows(OFF_VR), own_rows(OFF_QM),
        cos_s_ref[...], sin_s_ref[...], s0_ref, ck_ref, cv_ref, s_new_ref)
    for scr, out_ref, rows in ((os_scr, os_ref, o_rows), (atts_scr, atts_ref, att_rows)):
        block = scr[...]
        for b in range(per_step):
            mine = jnp.logical_and(row_id == r0 + b, live)
            block = jnp.where(mine, rows[b], block)
        scr[...] = block
        out_ref[...] = block


def _const_spec(shape, single_buffer=False):
    zeros = (0,) * len(shape)
    if single_buffer:
        return pl.BlockSpec(shape, lambda i: zeros, pipeline_mode=pl.Buffered(1))
    return pl.BlockSpec(shape, lambda i: zeros)


def _prompt_layer(x, cos2, sin2, mk_bf, mv_bf, norm_w, w_in_bf, gn_w, conv_w, conv_b, ln_w,
                  ln_b, w_br_ret_bf, w_br_conv_bf, w_br_mem_bf, w_out_bf, final_w,
                  intra, qdec, kdec, c_dec,
                  z_s, cos_s, sin_s, s0, cache_k, cache_v, gamma):
    batch, seq, _ = x.shape
    n_mem = mk_bf.shape[1]
    tile = PROMPT_TILE
    tiles_per_seq = seq // tile
    n_tiles = batch * tiles_per_seq

    def front(i):
        j = jnp.minimum(i, n_tiles - 1)
        return j // tiles_per_seq, lax.rem(j, tiles_per_seq)

    def back(i):
        j = jnp.maximum(i - 1, 0)
        return j // tiles_per_seq, lax.rem(j, tiles_per_seq)

    row = lambda w: _const_spec((1, w))
    weight = lambda a: _const_spec(a.shape, single_buffer=True)
    table = _const_spec((RET_HEADS, RET_CHUNK, RET_CHUNK))
    n_dec = z_s.shape[0]
    per_step = n_dec // n_tiles
    assert per_step * n_tiles == n_dec
    share = lambda shape: pl.BlockSpec(
        shape, lambda i: (jnp.minimum(i, n_tiles - 1),) + (0,) * (len(shape) - 1))
    assert SUBLANES % per_step == 0
    rows8 = lambda w: pl.BlockSpec(
        (SUBLANES, w), lambda i: (jnp.minimum(i, n_tiles - 1) // (SUBLANES // per_step), 0))
    x_tile = lambda which: pl.BlockSpec((1, tile, D_MODEL), lambda i: (*which(i), 0))
    per_seq = lambda shape: pl.BlockSpec(
        (1,) + shape, lambda i: (front(i)[0],) + (0,) * len(shape))
    return pl.pallas_call(
        functools.partial(_prompt_kernel, c_dec, gamma, tiles_per_seq, n_tiles),
        grid=(n_tiles + 1,),
        in_specs=[
            x_tile(front), x_tile(back),
            pl.BlockSpec((tile, HEAD_DIM), lambda i: (front(i)[1], 0)),
            pl.BlockSpec((tile, HEAD_DIM), lambda i: (front(i)[1], 0)),
            per_seq((n_mem, BRANCH_W)), per_seq((n_mem, BRANCH_W)),
            row(D_MODEL), weight(w_in_bf), row(BRANCH_W),
            _const_spec((1, CONV_K, BRANCH_W)), row(BRANCH_W), row(BRANCH_W), row(BRANCH_W),
            weight(w_br_ret_bf), weight(w_br_conv_bf), weight(w_br_mem_bf), weight(w_out_bf),
            row(D_MODEL), table, table, table,
            rows8(IN_W), row(HEAD_DIM), row(HEAD_DIM),
            share((per_step,) + s0.shape[1:]),
            share((per_step,) + cache_k.shape[1:]), share((per_step,) + cache_v.shape[1:]),
        ],
        out_specs=[
            x_tile(back),
            per_seq((RET_HEADS, HEAD_DIM, HEAD_DIM)),
            per_seq((CONV_HIST, BRANCH_W)),
            rows8(BRANCH_W), rows8(BRANCH_W),
            share((per_step,) + s0.shape[1:]),
        ],
        out_shape=[
            jax.ShapeDtypeStruct((batch, seq, D_MODEL), F32),
            jax.ShapeDtypeStruct((batch, RET_HEADS, HEAD_DIM, HEAD_DIM), F32),
            jax.ShapeDtypeStruct((batch, CONV_HIST, BRANCH_W), F32),
            jax.ShapeDtypeStruct((n_dec, BRANCH_W), F32),
            jax.ShapeDtypeStruct((n_dec, BRANCH_W), F32),
            jax.ShapeDtypeStruct(s0.shape, F32),
        ],
        scratch_shapes=[
            pltpu.VMEM((HIST_PAD + tile, BRANCH_W), F32),
            pltpu.VMEM((tile, D_MODEL), BF16),
            pltpu.VMEM((tile, BRANCH_W), BF16),
            pltpu.VMEM((tile, BRANCH_W), BF16),
            pltpu.VMEM((tile, BRANCH_W), BF16),
            pltpu.VMEM((SUBLANES, BRANCH_W), F32),
            pltpu.VMEM((SUBLANES, BRANCH_W), F32),
        ],
        compiler_params=pltpu.CompilerParams(
            dimension_semantics=("arbitrary",), vmem_limit_bytes=VMEM_LIMIT_BYTES),
        name="prompt_layer",
    )(x, x, cos2, sin2, mk_bf, mv_bf, norm_w, w_in_bf, gn_w, conv_w, conv_b, ln_w, ln_b,
      w_br_ret_bf, w_br_conv_bf, w_br_mem_bf, w_out_bf, final_w, intra, qdec, kdec,
      z_s, cos_s, sin_s, s0, cache_k, cache_v)


SAMPLE_PROJ_BLOCK = 1536


def _sample_proj_kernel(x_ref, norm_w_ref, w_ref, z_ref, w_bf_ref):
    x = x_ref[...]
    h_bf = (x * _rms_scale(x) * norm_w_ref[...]).astype(BF16)
    w_bf = w_ref[0].astype(BF16)
    w_bf_ref[...] = w_bf
    z_ref[...] = _dot(h_bf, w_bf)


def _sample_proj(xs, norm_w, w_in):
    n = xs.shape[0]
    return pl.pallas_call(
        _sample_proj_kernel,
        grid=(IN_W // SAMPLE_PROJ_BLOCK,),
        in_specs=[
            pl.BlockSpec((n, D_MODEL), lambda j: (0, 0)),
            pl.BlockSpec((1, D_MODEL), lambda j: (0, 0)),
            pl.BlockSpec((1, D_MODEL, SAMPLE_PROJ_BLOCK), lambda j: (0, 0, j)),
        ],
        out_specs=[pl.BlockSpec((n, SAMPLE_PROJ_BLOCK), lambda j: (0, j)),
                   pl.BlockSpec((D_MODEL, SAMPLE_PROJ_BLOCK), lambda j: (0, j))],
        out_shape=[jax.ShapeDtypeStruct((n, IN_W), F32),
                   jax.ShapeDtypeStruct((D_MODEL, IN_W), BF16)],
        compiler_params=pltpu.CompilerParams(
            dimension_semantics=("arbitrary",), vmem_limit_bytes=VMEM_LIMIT_BYTES),
        name="sample_proj",
    )(xs, norm_w, w_in)


def _sample_sequences(gamma, zq, zk, v, qm, cos2, sin2, s0_ref, ck_ref, cv_ref, s_new_ref):
    n = zq.shape[0]
    n_mem = ck_ref.shape[1] // MEM_HEADS
    q_scale = HEAD_DIM ** -0.5
    mem_scale = HEAD_DIM ** -0.5
    q = [_rotary(_head(zq, h), cos2, sin2) * q_scale for h in range(RET_HEADS)]
    k = [_rotary(_head(zk, h), cos2, sin2) for h in range(RET_HEADS)]

    eye = (lax.broadcasted_iota(jnp.int32, (HEAD_DIM, HEAD_DIM), 0)
           == lax.broadcasted_iota(jnp.int32, (HEAD_DIM, HEAD_DIM), 1))
    ones_bf = jnp.ones((HEAD_DIM, HEAD_DIM), BF16)

    def as_columns(row):
        diag = jnp.where(eye, jnp.broadcast_to(row, (HEAD_DIM, HEAD_DIM)), 0.0)
        return _dot(diag.astype(BF16), ones_bf)

    o_rows, att_rows = [], []
    for b in range(n):
        o_heads = []
        for h in range(RET_HEADS):
            q_row = q[h][b:b + 1]
            k_row = k[h][b:b + 1]
            v_row = _head(v, h)[b:b + 1]
            s0 = s0_ref[b, h]
            qk = jnp.sum(q_row * k_row, axis=-1, keepdims=True)
            qs = jnp.sum(as_columns(q_row) * s0, axis=0, keepdims=True)
            o_heads.append(qk * v_row + gamma[h] * qs)
            s_new_ref[b, h] = gamma[h] * s0 + as_columns(k_row) * v_row
        o_rows.append(jnp.concatenate(o_heads, axis=-1))

        att_heads = []
        for h in range(MEM_HEADS):
            head_rows = pl.ds(h, n_mem, stride=MEM_HEADS)
            prod = ck_ref[b, head_rows, :] * _head(qm, h)[b:b + 1]
            sc = jnp.sum(prod, axis=-1, keepdims=True) * mem_scale
            e = jnp.exp(sc - jnp.max(sc, axis=0, keepdims=True))
            p = e / jnp.sum(e, axis=0, keepdims=True)
            att_heads.append(jnp.sum(p * cv_ref[b, head_rows, :], axis=0, keepdims=True))
        att_rows.append(jnp.concatenate(att_heads, axis=-1))
    return o_rows, att_rows


def _sample_out_kernel(x_ref, z_ref, o_ref, att_ref, hist_ref, conv_w_ref, gn_w_ref, conv_b_ref,
                       ln_w_ref, ln_b_ref, w_br_ret_ref, w_br_conv_ref, w_br_mem_ref, w_out_ref,
                       final_w_ref, y_ref, hist_new_ref):
    gate = lambda off: _silu(z_ref[:, off:off + BRANCH_W])
    yr = gate(OFF_GR) * _group_norm_heads(o_ref[...], gn_w_ref[...])

    u = z_ref[:, OFF_AC:OFF_AC + BRANCH_W] * _sigmoid(z_ref[:, OFF_BC:OFF_BC + BRANCH_W])
    conv = u * conv_w_ref[0, CONV_HIST:CONV_K, :] + conv_b_ref[...]
    for j in range(CONV_HIST):
        conv = conv + hist_ref[j] * conv_w_ref[0, j:j + 1, :]
        hist_new_ref[j] = hist_ref[j + 1] if j + 1 < CONV_HIST else u
    cn = _layernorm_rows(conv) * ln_w_ref[...] + ln_b_ref[...]
    yc = gate(OFF_GC) * _silu(cn)

    ym = gate(OFF_GM) * att_ref[...]
    gate_pre = lambda i: z_ref[:, OFF_MERGE + i * D_MODEL:OFF_MERGE + (i + 1) * D_MODEL]
    y_ref[...] = _merge_and_project(x_ref[...], gate_pre, yr, yc, ym, w_br_ret_ref,
                                    w_br_conv_ref, w_br_mem_ref, w_out_ref, final_w_ref[...])


def _sample_out(xs, z, o, att, hist, conv_w, gn_w, conv_b, ln_w, ln_b,
                w_br_ret_bf, w_br_conv_bf, w_br_mem_bf, w_out_bf, final_w):
    return pl.pallas_call(
        _sample_out_kernel,
        out_shape=[jax.ShapeDtypeStruct(xs.shape, F32), jax.ShapeDtypeStruct(hist.shape, F32)],
        compiler_params=pltpu.CompilerParams(vmem_limit_bytes=VMEM_LIMIT_BYTES),
        name="sample_out",
    )(xs, z, o, att, hist, conv_w, gn_w, conv_b, ln_w, ln_b,
      w_br_ret_bf, w_br_conv_bf, w_br_mem_bf, w_out_bf, final_w)


def _rotary_tables(pos):
    half = HEAD_DIM // 2
    inv = ROPE_BASE ** (-np.arange(half, dtype=np.float64) / half)
    ang = np.asarray(pos, np.float64)[:, None] * inv[None, :]
    cos, sin = np.cos(ang), np.sin(ang)
    table = lambda a, b: np.concatenate([a, b], axis=-1).astype(np.float32)
    return table(cos, cos), table(-sin, sin)


def _decay_tables():
    c = RET_CHUNK
    log_g = np.log1p(-np.exp2(-5.0 - np.arange(RET_HEADS, dtype=np.float64)))
    idx = np.arange(c, dtype=np.float64)
    diff = idx[:, None] - idx[None, :]
    intra = np.where(diff >= 0, np.exp(np.maximum(diff, 0.0)[None] * log_g[:, None, None]), 0.0)
    q_dec = np.exp((idx + 1.0)[None, :] * log_g[:, None])
    k_dec = np.exp((c - 1.0 - idx)[None, :] * log_g[:, None])
    bcast = lambda a: np.ascontiguousarray(
        np.broadcast_to(a[:, :, None], (RET_HEADS, c, HEAD_DIM))).astype(np.float32)
    return intra.astype(np.float32), bcast(q_dec), bcast(k_dec)


def _gamma_powers(n):
    return tuple(float(np.exp(np.log1p(-np.exp2(-5.0 - h)) * n)) for h in range(RET_HEADS))


def kernel(x_prompt, x_sample, mem_prompt, state_ret, state_conv, cache_mem_k, cache_mem_v,
           norm_w, w_in, ret_gn_w, conv_w, conv_b, conv_ln_w, conv_ln_b, mem_norm_w,
           w_mem_kv, w_br_ret, w_br_conv, w_br_mem, w_out, final_norm_w):
    depth = w_in.shape[0]
    assert depth == 1, "single-layer step"
    batch, seq, _ = x_prompt.shape
    n_dec, dec_seq, _ = x_sample.shape
    assert dec_seq == 1 and seq % PROMPT_TILE == 0
    n_mem = mem_prompt.shape[1]

    final_w = final_norm_w[None, :]

    intra, qdec, kdec = _decay_tables()
    cos_p, sin_p = _rotary_tables(np.arange(seq))
    cos_s, sin_s = _rotary_tables(np.arange(dec_seq) + PAST_LEN)

    xs = x_sample[:, 0, :]
    z_s, w_in_bf = _sample_proj(xs, norm_w, w_in)

    mk, mv, mk_bf, mv_bf, w_br_ret_bf, w_br_conv_bf, w_br_mem_bf, w_out_bf = _mem_kv(
        mem_prompt, mem_norm_w, w_mem_kv, (w_br_ret, w_br_conv, w_br_mem, w_out))

    y_prompt, s_prompt, hist_prompt, o_s, att_s, s_sample = _prompt_layer(
        x_prompt, cos_p, sin_p, mk_bf, mv_bf, norm_w, w_in_bf, ret_gn_w, conv_w, conv_b,
        conv_ln_w, conv_ln_b, w_br_ret_bf, w_br_conv_bf, w_br_mem_bf, w_out_bf, final_w,
        intra, qdec, kdec, _gamma_powers(RET_CHUNK),
        z_s, cos_s, sin_s, state_ret[0],
        cache_mem_k.reshape(n_dec, n_mem * MEM_HEADS, HEAD_DIM),
        cache_mem_v.reshape(n_dec, n_mem * MEM_HEADS, HEAD_DIM), _gamma_powers(1))

    y_sample, hist_sample = _sample_out(
        xs, z_s, o_s, att_s,
        jnp.transpose(state_conv[0], (1, 0, 2)), conv_w, ret_gn_w, conv_b, conv_ln_w, conv_ln_b,
        w_br_ret_bf, w_br_conv_bf, w_br_mem_bf, w_out_bf, final_w)

    heads = lambda a: a.reshape(1, batch, n_mem, MEM_HEADS, HEAD_DIM)
    return (y_prompt, y_sample[:, None, :], s_prompt[None], s_sample[None],
            hist_prompt[None], jnp.transpose(hist_sample, (1, 0, 2))[None], heads(mk), heads(mv))
```

```python
import functools

import numpy as np
import jax
import jax.numpy as jnp
from jax import lax
from jax.experimental import pallas as pl
from jax.experimental.pallas import tpu as pltpu

F32 = jnp.float32
BF16 = jnp.bfloat16

D_MODEL = 1024
RET_HEADS = 4
HEAD_DIM = 128
SUBLANES = 8
BRANCH_W = 512
RET_CHUNK = 128
ROPE_BASE = 10000.0
CONV_K = 31
CONV_HIST = CONV_K - 1
MEM_HEADS = 4
N_BRANCH = 3
EPS = 1e-6
PAST_LEN = 16384

OFF_QR, OFF_KR, OFF_VR, OFF_GR = 0, 512, 1024, 1536
OFF_AC, OFF_BC, OFF_GC = 2048, 2560, 3072
OFF_QM, OFF_GM = 3584, 4096
OFF_MERGE = 4608
IN_W = OFF_MERGE + N_BRANCH * D_MODEL

V7X_VMEM_BYTES = 64 * 1024 * 1024
VMEM_LIMIT_BYTES = V7X_VMEM_BYTES - 8 * 1024 * 1024

PROMPT_TILE = 256
HIST_PAD = 32


def _sigmoid(x):
    return 0.5 * jnp.tanh(0.5 * x) + 0.5


def _silu(x):
    hx = 0.5 * x
    return hx * (jnp.tanh(hx) + 1.0)


def _rms_scale(x):
    return lax.rsqrt(jnp.mean(x * x, axis=-1, keepdims=True) + EPS)


def _layernorm_rows(x):
    mu = jnp.mean(x, axis=-1, keepdims=True)
    xc = x - mu
    var = jnp.mean(xc * xc, axis=-1, keepdims=True)
    return xc * lax.rsqrt(var + EPS)


def _rotary(x, cos2, sin2):
    return x * cos2 + pltpu.roll(x, HEAD_DIM // 2, axis=1) * sin2


def _dot(a, b):
    return jnp.dot(a, b, preferred_element_type=F32)


def _dot_nt(a, b):
    return lax.dot_general(a, b, (((1,), (1,)), ((), ())), preferred_element_type=F32)


def _dot_tn(a, b):
    return lax.dot_general(a, b, (((0,), (0,)), ((), ())), preferred_element_type=F32)


def _head(x, h):
    return x[:, h * HEAD_DIM:(h + 1) * HEAD_DIM]


def _group_norm_heads(o, gn_w):
    parts = [_layernorm_rows(_head(o, h)) for h in range(RET_HEADS)]
    return jnp.concatenate(parts, axis=-1) * gn_w


def _merge_and_project(x, gate_pre, yr, yc, ym, w_br_ret_ref, w_br_conv_ref,
                       w_br_mem_ref, w_out_ref, final_w):
    branches = (
        _dot(yr.astype(BF16), w_br_ret_ref[...]),
        _dot(yc.astype(BF16), w_br_conv_ref[...]),
        _dot(ym.astype(BF16), w_br_mem_ref[...]),
    )
    merged = None
    for i, br in enumerate(branches):
        term = _sigmoid(gate_pre(i)) * br
        merged = term if merged is None else merged + term
    out = x + _dot(merged.astype(BF16), w_out_ref[...])
    return out * _rms_scale(out) * final_w


MEM_KV_BLOCK = 4


def _mem_kv_kernel(n_cast, mem_ref, norm_w_ref, w_ref, *refs):
    cast_in, (k_ref, v_ref, kbf_ref, vbf_ref), cast_out = (
        refs[:n_cast], refs[n_cast:n_cast + 4], refs[n_cast + 4:])

    @pl.when(pl.program_id(0) == 0)
    def _():
        for src, dst in zip(cast_in, cast_out):
            dst[...] = src[0].astype(BF16)

    nb, n_mem, _ = mem_ref.shape
    m = mem_ref[...].reshape(nb * n_mem, D_MODEL)
    hm = (m * _rms_scale(m) * norm_w_ref[...]).astype(BF16)
    kv = _dot(hm, w_ref[0].astype(BF16))
    for b in range(nb):
        k = kv[b * n_mem:(b + 1) * n_mem, :BRANCH_W]
        v = kv[b * n_mem:(b + 1) * n_mem, BRANCH_W:]
        for h in range(MEM_HEADS):
            k_ref[b, pl.ds(h, n_mem, stride=MEM_HEADS), :] = _head(k, h)
            v_ref[b, pl.ds(h, n_mem, stride=MEM_HEADS), :] = _head(v, h)
        kbf_ref[b] = k.astype(BF16)
        vbf_ref[b] = v.astype(BF16)


def _mem_kv(mem, mem_norm_w, w_mem_kv, to_cast):
    batch, n_mem, _ = mem.shape
    nb = MEM_KV_BLOCK
    blk = lambda dt: jax.ShapeDtypeStruct((batch, n_mem, BRANCH_W), dt)
    out_spec = pl.BlockSpec((nb, n_mem, BRANCH_W), lambda b: (b, 0, 0))
    rows_shape = (batch, n_mem * MEM_HEADS, HEAD_DIM)
    rows_spec = pl.BlockSpec((nb, n_mem * MEM_HEADS, HEAD_DIM), lambda b: (b, 0, 0))
    return pl.pallas_call(
        functools.partial(_mem_kv_kernel, len(to_cast)),
        grid=(batch // nb,),
        in_specs=[
            pl.BlockSpec((nb, n_mem, D_MODEL), lambda b: (b, 0, 0)),
            pl.BlockSpec((1, D_MODEL), lambda b: (0, 0)),
            pl.BlockSpec((1, D_MODEL, 2 * BRANCH_W), lambda b: (0, 0, 0)),
        ] + [pl.BlockSpec(w.shape, lambda b: (0, 0, 0), pipeline_mode=pl.Buffered(1))
             for w in to_cast],
        out_specs=[rows_spec, rows_spec, out_spec, out_spec]
                  + [pl.BlockSpec(w.shape[1:], lambda b: (0, 0)) for w in to_cast],
        out_shape=[jax.ShapeDtypeStruct(rows_shape, F32), jax.ShapeDtypeStruct(rows_shape, F32),
                   blk(BF16), blk(BF16)]
                  + [jax.ShapeDtypeStruct(w.shape[1:], BF16) for w in to_cast],
        compiler_params=pltpu.CompilerParams(
            dimension_semantics=("arbitrary",), vmem_limit_bytes=VMEM_LIMIT_BYTES),
        name="mem_kv",
    )(mem, mem_norm_w, w_mem_kv, *to_cast)


def _prompt_kernel(c_dec, gamma, tiles_per_seq, n_tiles,
                   x_ref, xb_ref, cos_ref, sin_ref, mk_ref, mv_ref, norm_w_ref, w_in_ref,
                   gn_w_ref, conv_w_ref, conv_b_ref, ln_w_ref, ln_b_ref,
                   w_br_ret_ref, w_br_conv_ref, w_br_mem_ref, w_out_ref, final_w_ref,
                   intra_ref, qdec_ref, kdec_ref,
                   zs_ref, cos_s_ref, sin_s_ref, s0_ref, ck_ref, cv_ref,
                   y_ref, s_ref, hist_ref, os_ref, atts_ref, s_new_ref,
                   ubuf_ref, h_scr, yr_scr, yc_scr, ym_scr, os_scr, atts_scr):
    i = pl.program_id(0)
    t = lax.rem(jnp.minimum(i, n_tiles - 1), tiles_per_seq)
    live = i < n_tiles
    tile = x_ref.shape[1]

    @pl.when(i == 0)
    def _():
        h_scr[...] = jnp.zeros_like(h_scr)
        yr_scr[...] = jnp.zeros_like(yr_scr)
        yc_scr[...] = jnp.zeros_like(yc_scr)
        ym_scr[...] = jnp.zeros_like(ym_scr)
        os_scr[...] = jnp.zeros_like(os_scr)
        atts_scr[...] = jnp.zeros_like(atts_scr)

    @pl.when(jnp.logical_and(t == 0, live))
    def _():
        s_ref[...] = jnp.zeros_like(s_ref)
        ubuf_ref[0:HIST_PAD, :] = jnp.zeros((HIST_PAD, BRANCH_W), F32)

    h_prev = h_scr[...]
    half = D_MODEL // 2

    def gate_job(k):
        lo = OFF_MERGE + k * half
        return lambda: _dot(h_prev, w_in_ref[:, lo:lo + half])

    back_jobs = [gate_job(k) for k in range(2 * N_BRANCH)] + [
        lambda: _dot(yr_scr[...], w_br_ret_ref[...]),
        lambda: _dot(yc_scr[...], w_br_conv_ref[...]),
        lambda: _dot(ym_scr[...], w_br_mem_ref[...]),
    ]
    back_out = [None] * len(back_jobs)

    def run_back(k):
        back_out[k] = back_jobs[k]()

    x = x_ref[0]
    h_bf = (x * _rms_scale(x) * norm_w_ref[...]).astype(BF16)
    run_back(8)

    def proj(off, width=BRANCH_W):
        return _dot(h_bf, w_in_ref[:, off:off + width])

    za = proj(OFF_AC)
    zb = proj(OFF_BC)
    zq = proj(OFF_QR)
    zk = proj(OFF_KR)
    v_bf = proj(OFF_VR).astype(BF16)
    u = za * _sigmoid(zb)
    ubuf_ref[HIST_PAD:HIST_PAD + tile, :] = u
    first = HIST_PAD - CONV_HIST
    conv = None
    for r in range(8):
        part = None
        for j in range(CONV_K):
            if (first + j) % 8 != r:
                continue
            term = ubuf_ref[first + j:first + j + tile, :] * conv_w_ref[0, j:j + 1, :]
            part = term if part is None else part + term
        conv = part if conv is None else conv + part
        run_back(r)
    conv = conv + conv_b_ref[...]
    ubuf_ref[0:HIST_PAD, :] = jnp.where(live, ubuf_ref[tile:tile + HIST_PAD, :],
                                        ubuf_ref[0:HIST_PAD, :])

    z_gc = proj(OFF_GC)
    z_gr = proj(OFF_GR)
    z_qm = proj(OFF_QM)
    z_gm = proj(OFF_GM)

    merged = []
    for k in range(2):
        cols = slice(k * half, (k + 1) * half)
        acc = None
        for b in range(N_BRANCH):
            term = _sigmoid(back_out[2 * b + k]) * back_out[2 * N_BRANCH + b][:, cols]
            acc = term if acc is None else acc + term
        merged.append(acc.astype(BF16))
    merged = jnp.concatenate(merged, axis=-1)

    cos2 = cos_ref[...]
    sin2 = sin_ref[...]
    q_scale = HEAD_DIM ** -0.5
    o_heads = []
    for h in range(RET_HEADS):
        q_h = _rotary(_head(zq, h), cos2, sin2) * q_scale
        k_h = _rotary(_head(zk, h), cos2, sin2)
        v_h = _head(v_bf, h)
        s_old = s_ref[0, h]
        s = s_old
        o_chunks = []
        for c in range(tile // RET_CHUNK):
            rows = slice(c * RET_CHUNK, (c + 1) * RET_CHUNK)
            qc = q_h[rows].astype(BF16)
            kc = k_h[rows]
            vc = v_h[rows]
            sc = _dot_nt(qc, kc.astype(BF16)) * intra_ref[h]
            o_chunks.append(_dot(sc.astype(BF16), vc) + _dot(qc, s.astype(BF16)) * qdec_ref[h])
            kd = (kc * kdec_ref[h]).astype(BF16)
            s = s * c_dec[h] + _dot_tn(kd, vc)
        s_ref[0, h] = jnp.where(live, s, s_old)
        o_heads.append(_layernorm_rows(jnp.concatenate(o_chunks, axis=0)))

    out = xb_ref[0] + _dot(merged, w_out_ref[...])

    ret = jnp.concatenate(o_heads, axis=-1) * gn_w_ref[...]
    yr = _silu(z_gr) * ret
    cn = _layernorm_rows(conv) * ln_w_ref[...] + ln_b_ref[...]
    yc = _silu(z_gc) * _silu(cn)

    qm = z_qm.astype(BF16)
    mk = mk_ref[0]
    mv = mv_ref[0]
    att = []
    for h in range(MEM_HEADS):
        sc = _dot_nt(_head(qm, h), _head(mk, h)) * (HEAD_DIM ** -0.5)
        e = jnp.exp(sc - jnp.max(sc, axis=-1, keepdims=True))
        p = e / jnp.sum(e, axis=-1, keepdims=True)
        att.append(_dot(p.astype(BF16), _head(mv, h)))
    ym = _silu(z_gm) * jnp.concatenate(att, axis=-1)

    y_ref[0] = out * _rms_scale(out) * final_w_ref[...]

    h_scr[...] = h_bf
    yr_scr[...] = yr.astype(BF16)
    yc_scr[...] = yc.astype(BF16)
    ym_scr[...] = ym.astype(BF16)

    @pl.when(jnp.logical_and(t == tiles_per_seq - 1, live))
    def _():
        hist_ref[0] = ubuf_ref[first:HIST_PAD, :]

    per_step = s0_ref.shape[0]
    r0 = lax.rem(jnp.minimum(i, n_tiles - 1), SUBLANES // per_step) * per_step
    row_id = lax.broadcasted_iota(jnp.int32, (SUBLANES, 1), 0)

    def own_rows(off):
        block = zs_ref[:, off:off + BRANCH_W]
        return jnp.concatenate(
            [jnp.sum(jnp.where(row_id == r0 + b, block, 0.0), axis=0, keepdims=True)
             for b in range(per_step)], axis=0)

    o_rows, att_rows = _sample_sequences(
        gamma, own_rows(OFF_QR), own_rows(OFF_KR), own_rows(OFF_VR), own_rows(OFF_QM),
        cos_s_ref[...], sin_s_ref[...], s0_ref, ck_ref, cv_ref, s_new_ref)
    for scr, out_ref, rows in ((os_scr, os_ref, o_rows), (atts_scr, atts_ref, att_rows)):
        block = scr[...]
        for b in range(per_step):
            mine = jnp.logical_and(row_id == r0 + b, live)
            block = jnp.where(mine, rows[b], block)
        scr[...] = block
        out_ref[...] = block


def _const_spec(shape, single_buffer=False):
    zeros = (0,) * len(shape)
    if single_buffer:
        return pl.BlockSpec(shape, lambda i: zeros, pipeline_mode=pl.Buffered(1))
    return pl.BlockSpec(shape, lambda i: zeros)


def _prompt_layer(x, cos2, sin2, mk_bf, mv_bf, norm_w, w_in_bf, gn_w, conv_w, conv_b, ln_w,
                  ln_b, w_br_ret_bf, w_br_conv_bf, w_br_mem_bf, w_out_bf, final_w,
                  intra, qdec, kdec, c_dec,
                  z_s, cos_s, sin_s, s0, cache_k, cache_v, gamma):
    batch, seq, _ = x.shape
    n_mem = mk_bf.shape[1]
    tile = PROMPT_TILE
    tiles_per_seq = seq // tile
    n_tiles = batch * tiles_per_seq

    def front(i):
        j = jnp.minimum(i, n_tiles - 1)
        return j // tiles_per_seq, lax.rem(j, tiles_per_seq)

    def back(i):
        j = jnp.maximum(i - 1, 0)
        return j // tiles_per_seq, lax.rem(j, tiles_per_seq)

    row = lambda w: _const_spec((1, w))
    weight = lambda a: _const_spec(a.shape, single_buffer=True)
    table = _const_spec((RET_HEADS, RET_CHUNK, RET_CHUNK))
    n_dec = z_s.shape[0]
    per_step = n_dec // n_tiles
    assert per_step * n_tiles == n_dec
    share = lambda shape: pl.BlockSpec(
        shape, lambda i: (jnp.minimum(i, n_tiles - 1),) + (0,) * (len(shape) - 1))
    assert SUBLANES % per_step == 0
    rows8 = lambda w: pl.BlockSpec(
        (SUBLANES, w), lambda i: (jnp.minimum(i, n_tiles - 1) // (SUBLANES // per_step), 0))
    x_tile = lambda which: pl.BlockSpec((1, tile, D_MODEL), lambda i: (*which(i), 0))
    per_seq = lambda shape: pl.BlockSpec(
        (1,) + shape, lambda i: (front(i)[0],) + (0,) * len(shape))
    return pl.pallas_call(
        functools.partial(_prompt_kernel, c_dec, gamma, tiles_per_seq, n_tiles),
        grid=(n_tiles + 1,),
        in_specs=[
            x_tile(front), x_tile(back),
            pl.BlockSpec((tile, HEAD_DIM), lambda i: (front(i)[1], 0)),
            pl.BlockSpec((tile, HEAD_DIM), lambda i: (front(i)[1], 0)),
            per_seq((n_mem, BRANCH_W)), per_seq((n_mem, BRANCH_W)),
            row(D_MODEL), weight(w_in_bf), row(BRANCH_W),
            _const_spec((1, CONV_K, BRANCH_W)), row(BRANCH_W), row(BRANCH_W), row(BRANCH_W),
            weight(w_br_ret_bf), weight(w_br_conv_bf), weight(w_br_mem_bf), weight(w_out_bf),
            row(D_MODEL), table, table, table,
            rows8(IN_W), row(HEAD_DIM), row(HEAD_DIM),
            share((per_step,) + s0.shape[1:]),
            share((per_step,) + cache_k.shape[1:]), share((per_step,) + cache_v.shape[1:]),
        ],
        out_specs=[
            x_tile(back),
            per_seq((RET_HEADS, HEAD_DIM, HEAD_DIM)),
            per_seq((CONV_HIST, BRANCH_W)),
            rows8(BRANCH_W), rows8(BRANCH_W),
            share((per_step,) + s0.shape[1:]),
        ],
        out_shape=[
            jax.ShapeDtypeStruct((batch, seq, D_MODEL), F32),
            jax.ShapeDtypeStruct((batch, RET_HEADS, HEAD_DIM, HEAD_DIM), F32),
            jax.ShapeDtypeStruct((batch, CONV_HIST, BRANCH_W), F32),
            jax.ShapeDtypeStruct((n_dec, BRANCH_W), F32),
            jax.ShapeDtypeStruct((n_dec, BRANCH_W), F32),
            jax.ShapeDtypeStruct(s0.shape, F32),
        ],
        scratch_shapes=[
            pltpu.VMEM((HIST_PAD + tile, BRANCH_W), F32),
            pltpu.VMEM((tile, D_MODEL), BF16),
            pltpu.VMEM((tile, BRANCH_W), BF16),
            pltpu.VMEM((tile, BRANCH_W), BF16),
            pltpu.VMEM((tile, BRANCH_W), BF16),
            pltpu.VMEM((SUBLANES, BRANCH_W), F32),
            pltpu.VMEM((SUBLANES, BRANCH_W), F32),
        ],
        compiler_params=pltpu.CompilerParams(
            dimension_semantics=("arbitrary",), vmem_limit_bytes=VMEM_LIMIT_BYTES),
        name="prompt_layer",
    )(x, x, cos2, sin2, mk_bf, mv_bf, norm_w, w_in_bf, gn_w, conv_w, conv_b, ln_w, ln_b,
      w_br_ret_bf, w_br_conv_bf, w_br_mem_bf, w_out_bf, final_w, intra, qdec, kdec,
      z_s, cos_s, sin_s, s0, cache_k, cache_v)


SAMPLE_PROJ_BLOCK = 1536


def _sample_proj_kernel(x_ref, norm_w_ref, w_ref, z_ref, w_bf_ref):
    x = x_ref[...]
    h_bf = (x * _rms_scale(x) * norm_w_ref[...]).astype(BF16)
    w_bf = w_ref[0].astype(BF16)
    w_bf_ref[...] = w_bf
    z_ref[...] = _dot(h_bf, w_bf)


def _sample_proj(xs, norm_w, w_in):
    n = xs.shape[0]
    return pl.pallas_call(
        _sample_proj_kernel,
        grid=(IN_W // SAMPLE_PROJ_BLOCK,),
        in_specs=[
            pl.BlockSpec((n, D_MODEL), lambda j: (0, 0)),
            pl.BlockSpec((1, D_MODEL), lambda j: (0, 0)),
            pl.BlockSpec((1, D_MODEL, SAMPLE_PROJ_BLOCK), lambda j: (0, 0, j)),
        ],
        out_specs=[pl.BlockSpec((n, SAMPLE_PROJ_BLOCK), lambda j: (0, j)),
                   pl.BlockSpec((D_MODEL, SAMPLE_PROJ_BLOCK), lambda j: (0, j))],
        out_shape=[jax.ShapeDtypeStruct((n, IN_W), F32),
                   jax.ShapeDtypeStruct((D_MODEL, IN_W), BF16)],
        compiler_params=pltpu.CompilerParams(
            dimension_semantics=("arbitrary",), vmem_limit_bytes=VMEM_LIMIT_BYTES),
        name="sample_proj",
    )(xs, norm_w, w_in)


def _sample_sequences(gamma, zq, zk, v, qm, cos2, sin2, s0_ref, ck_ref, cv_ref, s_new_ref):
    n = zq.shape[0]
    n_mem = ck_ref.shape[1] // MEM_HEADS
    q_scale = HEAD_DIM ** -0.5
    mem_scale = HEAD_DIM ** -0.5
    q = [_rotary(_head(zq, h), cos2, sin2) * q_scale for h in range(RET_HEADS)]
    k = [_rotary(_head(zk, h), cos2, sin2) for h in range(RET_HEADS)]

    eye = (lax.broadcasted_iota(jnp.int32, (HEAD_DIM, HEAD_DIM), 0)
           == lax.broadcasted_iota(jnp.int32, (HEAD_DIM, HEAD_DIM), 1))
    ones_bf = jnp.ones((HEAD_DIM, HEAD_DIM), BF16)

    def as_columns(row):
        diag = jnp.where(eye, jnp.broadcast_to(row, (HEAD_DIM, HEAD_DIM)), 0.0)
        return _dot(diag.astype(BF16), ones_bf)

    o_rows, att_rows = [], []
    for b in range(n):
        o_heads = []
        for h in range(RET_HEADS):
            q_row = q[h][b:b + 1]
            k_row = k[h][b:b + 1]
            v_row = _head(v, h)[b:b + 1]
            s0 = s0_ref[b, h]
            qk = jnp.sum(q_row * k_row, axis=-1, keepdims=True)
            q_lhs = jnp.broadcast_to(q_row, (SUBLANES, HEAD_DIM)).astype(BF16)
            qs = _dot(q_lhs, s0.astype(BF16))[0:1]
            o_heads.append(qk * v_row + gamma[h] * qs)
            s_new_ref[b, h] = gamma[h] * s0 + as_columns(k_row) * v_row
        o_rows.append(jnp.concatenate(o_heads, axis=-1))

        att_heads = []
        for h in range(MEM_HEADS):
            head_rows = pl.ds(h, n_mem, stride=MEM_HEADS)
            prod = ck_ref[b, head_rows, :] * (_head(qm, h)[b:b + 1] * mem_scale)
            sc = jnp.sum(prod, axis=-1, keepdims=True)
            e = jnp.exp(sc - jnp.max(sc, axis=0, keepdims=True))
            weighted = jnp.sum(e * cv_ref[b, head_rows, :], axis=0, keepdims=True)
            att_heads.append(weighted / jnp.sum(e, axis=0, keepdims=True))
        att_rows.append(jnp.concatenate(att_heads, axis=-1))
    return o_rows, att_rows


def _sample_out_kernel(x_ref, z_ref, o_ref, att_ref, hist_ref, conv_w_ref, gn_w_ref, conv_b_ref,
                       ln_w_ref, ln_b_ref, w_br_ret_ref, w_br_conv_ref, w_br_mem_ref, w_out_ref,
                       final_w_ref, y_ref, hist_new_ref):
    gate = lambda off: _silu(z_ref[:, off:off + BRANCH_W])
    yr = gate(OFF_GR) * _group_norm_heads(o_ref[...], gn_w_ref[...])

    u = z_ref[:, OFF_AC:OFF_AC + BRANCH_W] * _sigmoid(z_ref[:, OFF_BC:OFF_BC + BRANCH_W])
    conv = u * conv_w_ref[0, CONV_HIST:CONV_K, :] + conv_b_ref[...]
    for j in range(CONV_HIST):
        conv = conv + hist_ref[j] * conv_w_ref[0, j:j + 1, :]
        hist_new_ref[j] = hist_ref[j + 1] if j + 1 < CONV_HIST else u
    cn = _layernorm_rows(conv) * ln_w_ref[...] + ln_b_ref[...]
    yc = gate(OFF_GC) * _silu(cn)

    ym = gate(OFF_GM) * att_ref[...]
    gate_pre = lambda i: z_ref[:, OFF_MERGE + i * D_MODEL:OFF_MERGE + (i + 1) * D_MODEL]
    y_ref[...] = _merge_and_project(x_ref[...], gate_pre, yr, yc, ym, w_br_ret_ref,
                                    w_br_conv_ref, w_br_mem_ref, w_out_ref, final_w_ref[...])


def _sample_out(xs, z, o, att, hist, conv_w, gn_w, conv_b, ln_w, ln_b,
                w_br_ret_bf, w_br_conv_bf, w_br_mem_bf, w_out_bf, final_w):
    return pl.pallas_call(
        _sample_out_kernel,
        out_shape=[jax.ShapeDtypeStruct(xs.shape, F32), jax.ShapeDtypeStruct(hist.shape, F32)],
        compiler_params=pltpu.CompilerParams(vmem_limit_bytes=VMEM_LIMIT_BYTES),
        name="sample_out",
    )(xs, z, o, att, hist, conv_w, gn_w, conv_b, ln_w, ln_b,
      w_br_ret_bf, w_br_conv_bf, w_br_mem_bf, w_out_bf, final_w)


def _rotary_tables(pos):
    half = HEAD_DIM // 2
    inv = ROPE_BASE ** (-np.arange(half, dtype=np.float64) / half)
    ang = np.asarray(pos, np.float64)[:, None] * inv[None, :]
    cos, sin = np.cos(ang), np.sin(ang)
    table = lambda a, b: np.concatenate([a, b], axis=-1).astype(np.float32)
    return table(cos, cos), table(-sin, sin)


def _decay_tables():
    c = RET_CHUNK
    log_g = np.log1p(-np.exp2(-5.0 - np.arange(RET_HEADS, dtype=np.float64)))
    idx = np.arange(c, dtype=np.float64)
    diff = idx[:, None] - idx[None, :]
    intra = np.where(diff >= 0, np.exp(np.maximum(diff, 0.0)[None] * log_g[:, None, None]), 0.0)
    q_dec = np.exp((idx + 1.0)[None, :] * log_g[:, None])
    k_dec = np.exp((c - 1.0 - idx)[None, :] * log_g[:, None])
    bcast = lambda a: np.ascontiguousarray(
        np.broadcast_to(a[:, :, None], (RET_HEADS, c, HEAD_DIM))).astype(np.float32)
    return intra.astype(np.float32), bcast(q_dec), bcast(k_dec)


def _gamma_powers(n):
    return tuple(float(np.exp(np.log1p(-np.exp2(-5.0 - h)) * n)) for h in range(RET_HEADS))


def kernel(x_prompt, x_sample, mem_prompt, state_ret, state_conv, cache_mem_k, cache_mem_v,
           norm_w, w_in, ret_gn_w, conv_w, conv_b, conv_ln_w, conv_ln_b, mem_norm_w,
           w_mem_kv, w_br_ret, w_br_conv, w_br_mem, w_out, final_norm_w):
    depth = w_in.shape[0]
    assert depth == 1, "single-layer step"
    batch, seq, _ = x_prompt.shape
    n_dec, dec_seq, _ = x_sample.shape
    assert dec_seq == 1 and seq % PROMPT_TILE == 0
    n_mem = mem_prompt.shape[1]

    final_w = final_norm_w[None, :]

    intra, qdec, kdec = _decay_tables()
    cos_p, sin_p = _rotary_tables(np.arange(seq))
    cos_s, sin_s = _rotary_tables(np.arange(dec_seq) + PAST_LEN)

    xs = x_sample[:, 0, :]
    z_s, w_in_bf = _sample_proj(xs, norm_w, w_in)

    mk, mv, mk_bf, mv_bf, w_br_ret_bf, w_br_conv_bf, w_br_mem_bf, w_out_bf = _mem_kv(
        mem_prompt, mem_norm_w, w_mem_kv, (w_br_ret, w_br_conv, w_br_mem, w_out))

    y_prompt, s_prompt, hist_prompt, o_s, att_s, s_sample = _prompt_layer(
        x_prompt, cos_p, sin_p, mk_bf, mv_bf, norm_w, w_in_bf, ret_gn_w, conv_w, conv_b,
        conv_ln_w, conv_ln_b, w_br_ret_bf, w_br_conv_bf, w_br_mem_bf, w_out_bf, final_w,
        intra, qdec, kdec, _gamma_powers(RET_CHUNK),
        z_s, cos_s, sin_s, state_ret[0],
        cache_mem_k.reshape(n_dec, n_mem * MEM_HEADS, HEAD_DIM),
        cache_mem_v.reshape(n_dec, n_mem * MEM_HEADS, HEAD_DIM), _gamma_powers(1))

    y_sample, hist_sample = _sample_out(
        xs, z_s, o_s, att_s,
        jnp.transpose(state_conv[0], (1, 0, 2)), conv_w, ret_gn_w, conv_b, conv_ln_w, conv_ln_b,
        w_br_ret_bf, w_br_conv_bf, w_br_mem_bf, w_out_bf, final_w)

    heads = lambda a: a.reshape(1, batch, n_mem, MEM_HEADS, HEAD_DIM)
    return (y_prompt, y_sample[:, None, :], s_prompt[None], s_sample[None],
            hist_prompt[None], jnp.transpose(hist_sample, (1, 0, 2))[None], heads(mk), heads(mv))
```

```python
import functools

import numpy as np
import jax
import jax.numpy as jnp
from jax import lax
from jax.experimental import pallas as pl
from jax.experimental.pallas import tpu as pltpu

F32 = jnp.float32
BF16 = jnp.bfloat16

D_MODEL = 1024
RET_HEADS = 4
HEAD_DIM = 128
SUBLANES = 8
BRANCH_W = 512
RET_CHUNK = 128
ROPE_BASE = 10000.0
CONV_K = 31
CONV_HIST = CONV_K - 1
MEM_HEADS = 4
N_BRANCH = 3
EPS = 1e-6
PAST_LEN = 16384

OFF_QR, OFF_KR, OFF_VR, OFF_GR = 0, 512, 1024, 1536
OFF_AC, OFF_BC, OFF_GC = 2048, 2560, 3072
OFF_QM, OFF_GM = 3584, 4096
OFF_MERGE = 4608
IN_W = OFF_MERGE + N_BRANCH * D_MODEL

V7X_VMEM_BYTES = 64 * 1024 * 1024
VMEM_LIMIT_BYTES = V7X_VMEM_BYTES - 8 * 1024 * 1024

PROMPT_TILE = 256
HIST_PAD = 32


def _sigmoid(x):
    return 0.5 * jnp.tanh(0.5 * x) + 0.5


def _silu(x):
    hx = 0.5 * x
    return hx * (jnp.tanh(hx) + 1.0)


def _rms_scale(x):
    return lax.rsqrt(jnp.mean(x * x, axis=-1, keepdims=True) + EPS)


def _layernorm_rows(x):
    mu = jnp.mean(x, axis=-1, keepdims=True)
    xc = x - mu
    var = jnp.mean(xc * xc, axis=-1, keepdims=True)
    return xc * lax.rsqrt(var + EPS)


def _rotary(x, cos2, sin2):
    return x * cos2 + pltpu.roll(x, HEAD_DIM // 2, axis=1) * sin2


def _dot(a, b):
    return jnp.dot(a, b, preferred_element_type=F32)


def _dot_nt(a, b):
    return lax.dot_general(a, b, (((1,), (1,)), ((), ())), preferred_element_type=F32)


def _dot_tn(a, b):
    return lax.dot_general(a, b, (((0,), (0,)), ((), ())), preferred_element_type=F32)


def _head(x, h):
    return x[:, h * HEAD_DIM:(h + 1) * HEAD_DIM]


def _group_norm_heads(o, gn_w):
    parts = [_layernorm_rows(_head(o, h)) for h in range(RET_HEADS)]
    return jnp.concatenate(parts, axis=-1) * gn_w


def _merge_and_project(x, gate_pre, yr, yc, ym, w_br_ret_ref, w_br_conv_ref,
                       w_br_mem_ref, w_out_ref, final_w):
    branches = (
        _dot(yr.astype(BF16), w_br_ret_ref[...]),
        _dot(yc.astype(BF16), w_br_conv_ref[...]),
        _dot(ym.astype(BF16), w_br_mem_ref[...]),
    )
    merged = None
    for i, br in enumerate(branches):
        term = _sigmoid(gate_pre(i)) * br
        merged = term if merged is None else merged + term
    out = x + _dot(merged.astype(BF16), w_out_ref[...])
    return out * _rms_scale(out) * final_w


MEM_KV_BLOCK = 4


def _mem_kv_kernel(n_cast, mem_ref, norm_w_ref, w_ref, *refs):
    cast_in, (k_ref, v_ref, kbf_ref, vbf_ref), cast_out = (
        refs[:n_cast], refs[n_cast:n_cast + 4], refs[n_cast + 4:])

    @pl.when(pl.program_id(0) == 0)
    def _():
        for src, dst in zip(cast_in, cast_out):
            dst[...] = src[0].astype(BF16)

    nb, n_mem, _ = mem_ref.shape
    m = mem_ref[...].reshape(nb * n_mem, D_MODEL)
    hm = (m * _rms_scale(m) * norm_w_ref[...]).astype(BF16)
    kv = _dot(hm, w_ref[0].astype(BF16))
    for b in range(nb):
        k = kv[b * n_mem:(b + 1) * n_mem, :BRANCH_W]
        v = kv[b * n_mem:(b + 1) * n_mem, BRANCH_W:]
        for h in range(MEM_HEADS):
            k_ref[b, pl.ds(h, n_mem, stride=MEM_HEADS), :] = _head(k, h)
            v_ref[b, pl.ds(h, n_mem, stride=MEM_HEADS), :] = _head(v, h)
        kbf_ref[b] = k.astype(BF16)
        vbf_ref[b] = v.astype(BF16)


def _mem_kv(mem, mem_norm_w, w_mem_kv, to_cast):
    batch, n_mem, _ = mem.shape
    nb = MEM_KV_BLOCK
    blk = lambda dt: jax.ShapeDtypeStruct((batch, n_mem, BRANCH_W), dt)
    out_spec = pl.BlockSpec((nb, n_mem, BRANCH_W), lambda b: (b, 0, 0))
    rows_shape = (batch, n_mem * MEM_HEADS, HEAD_DIM)
    rows_spec = pl.BlockSpec((nb, n_mem * MEM_HEADS, HEAD_DIM), lambda b: (b, 0, 0))
    return pl.pallas_call(
        functools.partial(_mem_kv_kernel, len(to_cast)),
        grid=(batch // nb,),
        in_specs=[
            pl.BlockSpec((nb, n_mem, D_MODEL), lambda b: (b, 0, 0)),
            pl.BlockSpec((1, D_MODEL), lambda b: (0, 0)),
            pl.BlockSpec((1, D_MODEL, 2 * BRANCH_W), lambda b: (0, 0, 0)),
        ] + [pl.BlockSpec(w.shape, lambda b: (0, 0, 0), pipeline_mode=pl.Buffered(1))
             for w in to_cast],
        out_specs=[rows_spec, rows_spec, out_spec, out_spec]
                  + [pl.BlockSpec(w.shape[1:], lambda b: (0, 0)) for w in to_cast],
        out_shape=[jax.ShapeDtypeStruct(rows_shape, F32), jax.ShapeDtypeStruct(rows_shape, F32),
                   blk(BF16), blk(BF16)]
                  + [jax.ShapeDtypeStruct(w.shape[1:], BF16) for w in to_cast],
        compiler_params=pltpu.CompilerParams(
            dimension_semantics=("arbitrary",), vmem_limit_bytes=VMEM_LIMIT_BYTES),
        name="mem_kv",
    )(mem, mem_norm_w, w_mem_kv, *to_cast)


def _prompt_kernel(c_dec, gamma, tiles_per_seq, n_tiles,
                   x_ref, xb_ref, cos_ref, sin_ref, mk_ref, mv_ref, norm_w_ref, w_in_ref,
                   gn_w_ref, conv_w_ref, conv_b_ref, ln_w_ref, ln_b_ref,
                   w_br_ret_ref, w_br_conv_ref, w_br_mem_ref, w_out_ref, final_w_ref,
                   intra_ref, qdec_ref, kdec_ref,
                   zs_ref, cos_s_ref, sin_s_ref, s0_ref, ck_ref, cv_ref,
                   y_ref, s_ref, hist_ref, os_ref, atts_ref, s_new_ref,
                   ubuf_ref, h_scr, yr_scr, yc_scr, ym_scr, os_scr, atts_scr):
    i = pl.program_id(0)
    t = lax.rem(jnp.minimum(i, n_tiles - 1), tiles_per_seq)
    live = i < n_tiles
    tile = x_ref.shape[1]

    @pl.when(i == 0)
    def _():
        h_scr[...] = jnp.zeros_like(h_scr)
        yr_scr[...] = jnp.zeros_like(yr_scr)
        yc_scr[...] = jnp.zeros_like(yc_scr)
        ym_scr[...] = jnp.zeros_like(ym_scr)
        os_scr[...] = jnp.zeros_like(os_scr)
        atts_scr[...] = jnp.zeros_like(atts_scr)
        hist_ref[...] = jnp.zeros_like(hist_ref)

    @pl.when(jnp.logical_and(t == 0, live))
    def _():
        s_ref[...] = jnp.zeros_like(s_ref)
        ubuf_ref[0:HIST_PAD, :] = jnp.zeros((HIST_PAD, BRANCH_W), F32)

    h_prev = h_scr[...]
    half = D_MODEL // 2

    def gate_job(k):
        lo = OFF_MERGE + k * half
        return lambda: _dot(h_prev, w_in_ref[:, lo:lo + half])

    back_jobs = [gate_job(k) for k in range(2 * N_BRANCH)] + [
        lambda: _dot(yr_scr[...], w_br_ret_ref[...]),
        lambda: _dot(yc_scr[...], w_br_conv_ref[...]),
        lambda: _dot(ym_scr[...], w_br_mem_ref[...]),
    ]
    back_out = [None] * len(back_jobs)

    def run_back(k):
        back_out[k] = back_jobs[k]()

    x = x_ref[0]
    h_bf = (x * _rms_scale(x) * norm_w_ref[...]).astype(BF16)
    run_back(8)

    def proj(off, width=BRANCH_W):
        return _dot(h_bf, w_in_ref[:, off:off + width])

    za = proj(OFF_AC)
    zb = proj(OFF_BC)
    zq = proj(OFF_QR)
    zk = proj(OFF_KR)
    v_bf = proj(OFF_VR).astype(BF16)
    u = za * _sigmoid(zb)
    ubuf_ref[HIST_PAD:HIST_PAD + tile, :] = u
    first = HIST_PAD - CONV_HIST
    conv = None
    for r in range(8):
        part = None
        for j in range(CONV_K):
            if (first + j) % 8 != r:
                continue
            term = ubuf_ref[first + j:first + j + tile, :] * conv_w_ref[0, j:j + 1, :]
            part = term if part is None else part + term
        conv = part if conv is None else conv + part
        run_back(r)
    conv = conv + conv_b_ref[...]
    ubuf_ref[0:HIST_PAD, :] = jnp.where(live, ubuf_ref[tile:tile + HIST_PAD, :],
                                        ubuf_ref[0:HIST_PAD, :])

    z_gc = proj(OFF_GC)
    z_gr = proj(OFF_GR)
    z_qm = proj(OFF_QM)
    z_gm = proj(OFF_GM)

    merged = []
    for k in range(2):
        cols = slice(k * half, (k + 1) * half)
        acc = None
        for b in range(N_BRANCH):
            term = _sigmoid(back_out[2 * b + k]) * back_out[2 * N_BRANCH + b][:, cols]
            acc = term if acc is None else acc + term
        merged.append(acc.astype(BF16))
    merged = jnp.concatenate(merged, axis=-1)

    cos2 = cos_ref[...]
    sin2 = sin_ref[...]
    q_scale = HEAD_DIM ** -0.5
    o_heads = []
    for h in range(RET_HEADS):
        q_h = _rotary(_head(zq, h), cos2, sin2) * q_scale
        k_h = _rotary(_head(zk, h), cos2, sin2)
        v_h = _head(v_bf, h)
        s_old = s_ref[0, h]
        s = s_old
        o_chunks = []
        for c in range(tile // RET_CHUNK):
            rows = slice(c * RET_CHUNK, (c + 1) * RET_CHUNK)
            qc = q_h[rows].astype(BF16)
            kc = k_h[rows]
            vc = v_h[rows]
            sc = _dot_nt(qc, kc.astype(BF16)) * intra_ref[h]
            o_chunks.append(_dot(sc.astype(BF16), vc) + _dot(qc, s.astype(BF16)) * qdec_ref[h])
            kd = (kc * kdec_ref[h]).astype(BF16)
            s = s * c_dec[h] + _dot_tn(kd, vc)
        s_ref[0, h] = jnp.where(live, s, s_old)
        o_heads.append(_layernorm_rows(jnp.concatenate(o_chunks, axis=0)))

    out = xb_ref[0] + _dot(merged, w_out_ref[...])

    ret = jnp.concatenate(o_heads, axis=-1) * gn_w_ref[...]
    yr = _silu(z_gr) * ret
    cn = _layernorm_rows(conv) * ln_w_ref[...] + ln_b_ref[...]
    yc = _silu(z_gc) * _silu(cn)

    qm = z_qm.astype(BF16)
    mk = mk_ref[0]
    mv = mv_ref[0]
    att = []
    for h in range(MEM_HEADS):
        sc = _dot_nt(_head(qm, h), _head(mk, h)) * (HEAD_DIM ** -0.5)
        e = jnp.exp(sc - jnp.max(sc, axis=-1, keepdims=True))
        p = e / jnp.sum(e, axis=-1, keepdims=True)
        att.append(_dot(p.astype(BF16), _head(mv, h)))
    ym = _silu(z_gm) * jnp.concatenate(att, axis=-1)

    y_ref[0] = out * _rms_scale(out) * final_w_ref[...]

    h_scr[...] = h_bf
    yr_scr[...] = yr.astype(BF16)
    yc_scr[...] = yc.astype(BF16)
    ym_scr[...] = ym.astype(BF16)

    @pl.when(jnp.logical_and(t == tiles_per_seq - 1, live))
    def _():
        seq_id = jnp.minimum(i, n_tiles - 1) // tiles_per_seq
        mine = lax.broadcasted_iota(jnp.int32, (hist_ref.shape[1], 1), 0) == seq_id
        for j in range(CONV_HIST):
            hist_ref[j] = jnp.where(mine, ubuf_ref[first + j:first + j + 1, :], hist_ref[j])

    per_step = s0_ref.shape[0]
    r0 = lax.rem(jnp.minimum(i, n_tiles - 1), SUBLANES // per_step) * per_step
    row_id = lax.broadcasted_iota(jnp.int32, (SUBLANES, 1), 0)

    def own_rows(off):
        block = zs_ref[:, off:off + BRANCH_W]
        return jnp.concatenate(
            [jnp.sum(jnp.where(row_id == r0 + b, block, 0.0), axis=0, keepdims=True)
             for b in range(per_step)], axis=0)

    o_rows, att_rows = _sample_sequences(
        gamma, own_rows(OFF_QR), own_rows(OFF_KR), own_rows(OFF_VR), own_rows(OFF_QM),
        cos_s_ref[...], sin_s_ref[...], s0_ref, ck_ref, cv_ref, s_new_ref)
    for scr, out_ref, rows in ((os_scr, os_ref, o_rows), (atts_scr, atts_ref, att_rows)):
        block = scr[...]
        for b in range(per_step):
            mine = jnp.logical_and(row_id == r0 + b, live)
            block = jnp.where(mine, rows[b], block)
        scr[...] = block
        out_ref[...] = block


def _const_spec(shape, single_buffer=False):
    zeros = (0,) * len(shape)
    if single_buffer:
        return pl.BlockSpec(shape, lambda i: zeros, pipeline_mode=pl.Buffered(1))
    return pl.BlockSpec(shape, lambda i: zeros)


def _prompt_layer(x, cos2, sin2, mk_bf, mv_bf, norm_w, w_in_bf, gn_w, conv_w, conv_b, ln_w,
                  ln_b, w_br_ret_bf, w_br_conv_bf, w_br_mem_bf, w_out_bf, final_w,
                  intra, qdec, kdec, c_dec,
                  z_s, cos_s, sin_s, s0, cache_k, cache_v, gamma):
    batch, seq, _ = x.shape
    n_mem = mk_bf.shape[1]
    tile = PROMPT_TILE
    tiles_per_seq = seq // tile
    n_tiles = batch * tiles_per_seq

    def front(i):
        j = jnp.minimum(i, n_tiles - 1)
        return j // tiles_per_seq, lax.rem(j, tiles_per_seq)

    def back(i):
        j = jnp.maximum(i - 1, 0)
        return j // tiles_per_seq, lax.rem(j, tiles_per_seq)

    row = lambda w: _const_spec((1, w))
    weight = lambda a: _const_spec(a.shape, single_buffer=True)
    table = _const_spec((RET_HEADS, RET_CHUNK, RET_CHUNK))
    n_dec = z_s.shape[0]
    per_step = n_dec // n_tiles
    assert per_step * n_tiles == n_dec
    share = lambda shape: pl.BlockSpec(
        shape, lambda i: (jnp.minimum(i, n_tiles - 1),) + (0,) * (len(shape) - 1))
    assert SUBLANES % per_step == 0
    rows8 = lambda w: pl.BlockSpec(
        (SUBLANES, w), lambda i: (jnp.minimum(i, n_tiles - 1) // (SUBLANES // per_step), 0))
    x_tile = lambda which: pl.BlockSpec((1, tile, D_MODEL), lambda i: (*which(i), 0))
    per_seq = lambda shape: pl.BlockSpec(
        (1,) + shape, lambda i: (front(i)[0],) + (0,) * len(shape))
    return pl.pallas_call(
        functools.partial(_prompt_kernel, c_dec, gamma, tiles_per_seq, n_tiles),
        grid=(n_tiles + 1,),
        in_specs=[
            x_tile(front), x_tile(back),
            pl.BlockSpec((tile, HEAD_DIM), lambda i: (front(i)[1], 0)),
            pl.BlockSpec((tile, HEAD_DIM), lambda i: (front(i)[1], 0)),
            per_seq((n_mem, BRANCH_W)), per_seq((n_mem, BRANCH_W)),
            row(D_MODEL), weight(w_in_bf), row(BRANCH_W),
            _const_spec((1, CONV_K, BRANCH_W)), row(BRANCH_W), row(BRANCH_W), row(BRANCH_W),
            weight(w_br_ret_bf), weight(w_br_conv_bf), weight(w_br_mem_bf), weight(w_out_bf),
            row(D_MODEL), table, table, table,
            rows8(IN_W), row(HEAD_DIM), row(HEAD_DIM),
            share((per_step,) + s0.shape[1:]),
            share((per_step,) + cache_k.shape[1:]), share((per_step,) + cache_v.shape[1:]),
        ],
        out_specs=[
            x_tile(back),
            per_seq((RET_HEADS, HEAD_DIM, HEAD_DIM)),
            _const_spec((CONV_HIST, batch, BRANCH_W)),
            rows8(BRANCH_W), rows8(BRANCH_W),
            share((per_step,) + s0.shape[1:]),
        ],
        out_shape=[
            jax.ShapeDtypeStruct((batch, seq, D_MODEL), F32),
            jax.ShapeDtypeStruct((batch, RET_HEADS, HEAD_DIM, HEAD_DIM), F32),
            jax.ShapeDtypeStruct((CONV_HIST, batch, BRANCH_W), F32),
            jax.ShapeDtypeStruct((n_dec, BRANCH_W), F32),
            jax.ShapeDtypeStruct((n_dec, BRANCH_W), F32),
            jax.ShapeDtypeStruct(s0.shape, F32),
        ],
        scratch_shapes=[
            pltpu.VMEM((HIST_PAD + tile, BRANCH_W), F32),
            pltpu.VMEM((tile, D_MODEL), BF16),
            pltpu.VMEM((tile, BRANCH_W), BF16),
            pltpu.VMEM((tile, BRANCH_W), BF16),
            pltpu.VMEM((tile, BRANCH_W), BF16),
            pltpu.VMEM((SUBLANES, BRANCH_W), F32),
            pltpu.VMEM((SUBLANES, BRANCH_W), F32),
        ],
        compiler_params=pltpu.CompilerParams(
            dimension_semantics=("arbitrary",), vmem_limit_bytes=VMEM_LIMIT_BYTES),
        name="prompt_layer",
    )(x, x, cos2, sin2, mk_bf, mv_bf, norm_w, w_in_bf, gn_w, conv_w, conv_b, ln_w, ln_b,
      w_br_ret_bf, w_br_conv_bf, w_br_mem_bf, w_out_bf, final_w, intra, qdec, kdec,
      z_s, cos_s, sin_s, s0, cache_k, cache_v)


SAMPLE_PROJ_BLOCK = 1536


def _sample_proj_kernel(x_ref, norm_w_ref, w_ref, z_ref, w_bf_ref):
    x = x_ref[...]
    h_bf = (x * _rms_scale(x) * norm_w_ref[...]).astype(BF16)
    w_bf = w_ref[0].astype(BF16)
    w_bf_ref[...] = w_bf
    z_ref[...] = _dot(h_bf, w_bf)


def _sample_proj(xs, norm_w, w_in):
    n = xs.shape[0]
    return pl.pallas_call(
        _sample_proj_kernel,
        grid=(IN_W // SAMPLE_PROJ_BLOCK,),
        in_specs=[
            pl.BlockSpec((n, D_MODEL), lambda j: (0, 0)),
            pl.BlockSpec((1, D_MODEL), lambda j: (0, 0)),
            pl.BlockSpec((1, D_MODEL, SAMPLE_PROJ_BLOCK), lambda j: (0, 0, j)),
        ],
        out_specs=[pl.BlockSpec((n, SAMPLE_PROJ_BLOCK), lambda j: (0, j)),
                   pl.BlockSpec((D_MODEL, SAMPLE_PROJ_BLOCK), lambda j: (0, j))],
        out_shape=[jax.ShapeDtypeStruct((n, IN_W), F32),
                   jax.ShapeDtypeStruct((D_MODEL, IN_W), BF16)],
        compiler_params=pltpu.CompilerParams(
            dimension_semantics=("arbitrary",), vmem_limit_bytes=VMEM_LIMIT_BYTES),
        name="sample_proj",
    )(xs, norm_w, w_in)


def _sample_sequences(gamma, zq, zk, v, qm, cos2, sin2, s0_ref, ck_ref, cv_ref, s_new_ref):
    n = zq.shape[0]
    n_mem = ck_ref.shape[1] // MEM_HEADS
    q_scale = HEAD_DIM ** -0.5
    mem_scale = HEAD_DIM ** -0.5
    q = [_rotary(_head(zq, h), cos2, sin2) * q_scale for h in range(RET_HEADS)]
    k = [_rotary(_head(zk, h), cos2, sin2) for h in range(RET_HEADS)]

    eye = (lax.broadcasted_iota(jnp.int32, (HEAD_DIM, HEAD_DIM), 0)
           == lax.broadcasted_iota(jnp.int32, (HEAD_DIM, HEAD_DIM), 1))
    ones_bf = jnp.ones((HEAD_DIM, HEAD_DIM), BF16)

    def as_columns(row):
        diag = jnp.where(eye, jnp.broadcast_to(row, (HEAD_DIM, HEAD_DIM)), 0.0)
        return _dot(diag.astype(BF16), ones_bf)

    o_rows, att_rows = [], []
    for b in range(n):
        o_heads = []
        for h in range(RET_HEADS):
            q_row = q[h][b:b + 1]
            k_row = k[h][b:b + 1]
            v_row = _head(v, h)[b:b + 1]
            s0 = s0_ref[b, h]
            qk = jnp.sum(q_row * k_row, axis=-1, keepdims=True)
            q_lhs = jnp.broadcast_to(q_row, (SUBLANES, HEAD_DIM)).astype(BF16)
            qs = _dot(q_lhs, s0.astype(BF16))[0:1]
            o_heads.append(qk * v_row + gamma[h] * qs)
            s_new_ref[b, h] = gamma[h] * s0 + as_columns(k_row) * v_row
        o_rows.append(jnp.concatenate(o_heads, axis=-1))

        att_heads = []
        for h in range(MEM_HEADS):
            head_rows = pl.ds(h, n_mem, stride=MEM_HEADS)
            prod = ck_ref[b, head_rows, :] * (_head(qm, h)[b:b + 1] * mem_scale)
            sc = jnp.sum(prod, axis=-1, keepdims=True)
            e = jnp.exp(sc - jnp.max(sc, axis=0, keepdims=True))
            weighted = jnp.sum(e * cv_ref[b, head_rows, :], axis=0, keepdims=True)
            att_heads.append(weighted / jnp.sum(e, axis=0, keepdims=True))
        att_rows.append(jnp.concatenate(att_heads, axis=-1))
    return o_rows, att_rows


def _sample_out_kernel(x_ref, z_ref, o_ref, att_ref, hist_ref, conv_w_ref, gn_w_ref, conv_b_ref,
                       ln_w_ref, ln_b_ref, w_br_ret_ref, w_br_conv_ref, w_br_mem_ref, w_out_ref,
                       final_w_ref, y_ref, hist_new_ref):
    gate = lambda off: _silu(z_ref[:, off:off + BRANCH_W])
    yr = gate(OFF_GR) * _group_norm_heads(o_ref[...], gn_w_ref[...])

    u = z_ref[:, OFF_AC:OFF_AC + BRANCH_W] * _sigmoid(z_ref[:, OFF_BC:OFF_BC + BRANCH_W])
    conv = u * conv_w_ref[0, CONV_HIST:CONV_K, :] + conv_b_ref[...]
    for j in range(CONV_HIST):
        conv = conv + hist_ref[j] * conv_w_ref[0, j:j + 1, :]
        hist_new_ref[j] = hist_ref[j + 1] if j + 1 < CONV_HIST else u
    cn = _layernorm_rows(conv) * ln_w_ref[...] + ln_b_ref[...]
    yc = gate(OFF_GC) * _silu(cn)

    ym = gate(OFF_GM) * att_ref[...]
    gate_pre = lambda i: z_ref[:, OFF_MERGE + i * D_MODEL:OFF_MERGE + (i + 1) * D_MODEL]
    y_ref[...] = _merge_and_project(x_ref[...], gate_pre, yr, yc, ym, w_br_ret_ref,
                                    w_br_conv_ref, w_br_mem_ref, w_out_ref, final_w_ref[...])


SAMPLE_OUT_BLOCK = 32


def _sample_out(xs, z, o, att, hist, conv_w, gn_w, conv_b, ln_w, ln_b,
                w_br_ret_bf, w_br_conv_bf, w_br_mem_bf, w_out_bf, final_w):
    n = xs.shape[0]
    nb = SAMPLE_OUT_BLOCK
    rows = lambda w: pl.BlockSpec((nb, w), lambda i: (i, 0))
    hist_spec = pl.BlockSpec((CONV_HIST, nb, BRANCH_W), lambda i: (0, i, 0))
    const = lambda a: pl.BlockSpec(a.shape, lambda i: (0,) * a.ndim)
    weight = lambda a: pl.BlockSpec(a.shape, lambda i: (0,) * a.ndim,
                                    pipeline_mode=pl.Buffered(1))
    return pl.pallas_call(
        _sample_out_kernel,
        grid=(n // nb,),
        in_specs=[rows(D_MODEL), rows(IN_W), rows(BRANCH_W), rows(BRANCH_W), hist_spec,
                  const(conv_w), const(gn_w), const(conv_b), const(ln_w), const(ln_b),
                  weight(w_br_ret_bf), weight(w_br_conv_bf), weight(w_br_mem_bf),
                  weight(w_out_bf), const(final_w)],
        out_specs=[rows(D_MODEL), hist_spec],
        out_shape=[jax.ShapeDtypeStruct(xs.shape, F32), jax.ShapeDtypeStruct(hist.shape, F32)],
        compiler_params=pltpu.CompilerParams(
            dimension_semantics=("arbitrary",), vmem_limit_bytes=VMEM_LIMIT_BYTES),
        name="sample_out",
    )(xs, z, o, att, hist, conv_w, gn_w, conv_b, ln_w, ln_b,
      w_br_ret_bf, w_br_conv_bf, w_br_mem_bf, w_out_bf, final_w)


def _rotary_tables(pos):
    half = HEAD_DIM // 2
    inv = ROPE_BASE ** (-np.arange(half, dtype=np.float64) / half)
    ang = np.asarray(pos, np.float64)[:, None] * inv[None, :]
    cos, sin = np.cos(ang), np.sin(ang)
    table = lambda a, b: np.concatenate([a, b], axis=-1).astype(np.float32)
    return table(cos, cos), table(-sin, sin)


def _decay_tables():
    c = RET_CHUNK
    log_g = np.log1p(-np.exp2(-5.0 - np.arange(RET_HEADS, dtype=np.float64)))
    idx = np.arange(c, dtype=np.float64)
    diff = idx[:, None] - idx[None, :]
    intra = np.where(diff >= 0, np.exp(np.maximum(diff, 0.0)[None] * log_g[:, None, None]), 0.0)
    q_dec = np.exp((idx + 1.0)[None, :] * log_g[:, None])
    k_dec = np.exp((c - 1.0 - idx)[None, :] * log_g[:, None])
    bcast = lambda a: np.ascontiguousarray(
        np.broadcast_to(a[:, :, None], (RET_HEADS, c, HEAD_DIM))).astype(np.float32)
    return intra.astype(np.float32), bcast(q_dec), bcast(k_dec)


def _gamma_powers(n):
    return tuple(float(np.exp(np.log1p(-np.exp2(-5.0 - h)) * n)) for h in range(RET_HEADS))


def kernel(x_prompt, x_sample, mem_prompt, state_ret, state_conv, cache_mem_k, cache_mem_v,
           norm_w, w_in, ret_gn_w, conv_w, conv_b, conv_ln_w, conv_ln_b, mem_norm_w,
           w_mem_kv, w_br_ret, w_br_conv, w_br_mem, w_out, final_norm_w):
    depth = w_in.shape[0]
    assert depth == 1, "single-layer step"
    batch, seq, _ = x_prompt.shape
    n_dec, dec_seq, _ = x_sample.shape
    assert dec_seq == 1 and seq % PROMPT_TILE == 0
    n_mem = mem_prompt.shape[1]

    final_w = final_norm_w[None, :]

    intra, qdec, kdec = _decay_tables()
    cos_p, sin_p = _rotary_tables(np.arange(seq))
    cos_s, sin_s = _rotary_tables(np.arange(dec_seq) + PAST_LEN)

    xs = x_sample[:, 0, :]
    z_s, w_in_bf = _sample_proj(xs, norm_w, w_in)

    mk, mv, mk_bf, mv_bf, w_br_ret_bf, w_br_conv_bf, w_br_mem_bf, w_out_bf = _mem_kv(
        mem_prompt, mem_norm_w, w_mem_kv, (w_br_ret, w_br_conv, w_br_mem, w_out))

    y_prompt, s_prompt, hist_prompt, o_s, att_s, s_sample = _prompt_layer(
        x_prompt, cos_p, sin_p, mk_bf, mv_bf, norm_w, w_in_bf, ret_gn_w, conv_w, conv_b,
        conv_ln_w, conv_ln_b, w_br_ret_bf, w_br_conv_bf, w_br_mem_bf, w_out_bf, final_w,
        intra, qdec, kdec, _gamma_powers(RET_CHUNK),
        z_s, cos_s, sin_s, state_ret[0],
        cache_mem_k.reshape(n_dec, n_mem * MEM_HEADS, HEAD_DIM),
        cache_mem_v.reshape(n_dec, n_mem * MEM_HEADS, HEAD_DIM), _gamma_powers(1))

    y_sample, hist_sample = _sample_out(
        xs, z_s, o_s, att_s,
        jnp.transpose(state_conv[0], (1, 0, 2)), conv_w, ret_gn_w, conv_b, conv_ln_w, conv_ln_b,
        w_br_ret_bf, w_br_conv_bf, w_br_mem_bf, w_out_bf, final_w)

    heads = lambda a: a.reshape(1, batch, n_mem, MEM_HEADS, HEAD_DIM)
    return (y_prompt, y_sample[:, None, :], s_prompt[None], s_sample[None],
            jnp.transpose(hist_prompt, (1, 0, 2))[None],
            jnp.transpose(hist_sample, (1, 0, 2))[None], heads(mk), heads(mv))
```

```python
import functools

import numpy as np
import jax
import jax.numpy as jnp
from jax import lax
from jax.experimental import pallas as pl
from jax.experimental.pallas import tpu as pltpu

F32 = jnp.float32
BF16 = jnp.bfloat16

D_MODEL = 1024
RET_HEADS = 4
HEAD_DIM = 128
SUBLANES = 8
BRANCH_W = 512
RET_CHUNK = 128
ROPE_BASE = 10000.0
CONV_K = 31
CONV_HIST = CONV_K - 1
MEM_HEADS = 4
N_BRANCH = 3
EPS = 1e-6
PAST_LEN = 16384

OFF_QR, OFF_KR, OFF_VR, OFF_GR = 0, 512, 1024, 1536
OFF_AC, OFF_BC, OFF_GC = 2048, 2560, 3072
OFF_QM, OFF_GM = 3584, 4096
OFF_MERGE = 4608
IN_W = OFF_MERGE + N_BRANCH * D_MODEL

V7X_VMEM_BYTES = 64 * 1024 * 1024
VMEM_LIMIT_BYTES = V7X_VMEM_BYTES - 8 * 1024 * 1024

PROMPT_TILE = 256
HIST_PAD = 32


def _sigmoid(x):
    return 0.5 * jnp.tanh(0.5 * x) + 0.5


def _silu(x):
    hx = 0.5 * x
    return hx * (jnp.tanh(hx) + 1.0)


def _rms_scale(x):
    return lax.rsqrt(jnp.mean(x * x, axis=-1, keepdims=True) + EPS)


def _layernorm_rows(x):
    mu = jnp.mean(x, axis=-1, keepdims=True)
    xc = x - mu
    var = jnp.mean(xc * xc, axis=-1, keepdims=True)
    return xc * lax.rsqrt(var + EPS)


def _rotary(x, cos2, sin2):
    return x * cos2 + pltpu.roll(x, HEAD_DIM // 2, axis=1) * sin2


def _dot(a, b):
    return jnp.dot(a, b, preferred_element_type=F32)


def _dot_nt(a, b):
    return lax.dot_general(a, b, (((1,), (1,)), ((), ())), preferred_element_type=F32)


def _dot_tn(a, b):
    return lax.dot_general(a, b, (((0,), (0,)), ((), ())), preferred_element_type=F32)


def _head(x, h):
    return x[:, h * HEAD_DIM:(h + 1) * HEAD_DIM]


def _group_norm_heads(o, gn_w):
    parts = [_layernorm_rows(_head(o, h)) for h in range(RET_HEADS)]
    return jnp.concatenate(parts, axis=-1) * gn_w


def _merge_and_project(x, gate_pre, yr, yc, ym, w_br_ret_ref, w_br_conv_ref,
                       w_br_mem_ref, w_out_ref, final_w):
    branches = (
        _dot(yr.astype(BF16), w_br_ret_ref[...]),
        _dot(yc.astype(BF16), w_br_conv_ref[...]),
        _dot(ym.astype(BF16), w_br_mem_ref[...]),
    )
    merged = None
    for i, br in enumerate(branches):
        term = _sigmoid(gate_pre(i)) * br
        merged = term if merged is None else merged + term
    out = x + _dot(merged.astype(BF16), w_out_ref[...])
    return out * _rms_scale(out) * final_w


MEM_KV_BLOCK = 4


def _mem_kv_kernel(n_cast, mem_ref, norm_w_ref, w_ref, *refs):
    cast_in, (k_ref, v_ref, kbf_ref, vbf_ref), cast_out = (
        refs[:n_cast], refs[n_cast:n_cast + 4], refs[n_cast + 4:])

    @pl.when(pl.program_id(0) == 0)
    def _():
        for src, dst in zip(cast_in, cast_out):
            dst[...] = src[0].astype(BF16)

    nb, n_mem, _ = mem_ref.shape
    m = mem_ref[...].reshape(nb * n_mem, D_MODEL)
    hm = (m * _rms_scale(m) * norm_w_ref[...]).astype(BF16)
    kv = _dot(hm, w_ref[0].astype(BF16))
    for b in range(nb):
        k = kv[b * n_mem:(b + 1) * n_mem, :BRANCH_W]
        v = kv[b * n_mem:(b + 1) * n_mem, BRANCH_W:]
        for h in range(MEM_HEADS):
            k_ref[b, pl.ds(h, n_mem, stride=MEM_HEADS), :] = _head(k, h)
            v_ref[b, pl.ds(h, n_mem, stride=MEM_HEADS), :] = _head(v, h)
        kbf_ref[b] = k.astype(BF16)
        vbf_ref[b] = v.astype(BF16)


def _mem_kv(mem, mem_norm_w, w_mem_kv, to_cast):
    batch, n_mem, _ = mem.shape
    nb = MEM_KV_BLOCK
    blk = lambda dt: jax.ShapeDtypeStruct((batch, n_mem, BRANCH_W), dt)
    out_spec = pl.BlockSpec((nb, n_mem, BRANCH_W), lambda b: (b, 0, 0))
    rows_shape = (batch, n_mem * MEM_HEADS, HEAD_DIM)
    rows_spec = pl.BlockSpec((nb, n_mem * MEM_HEADS, HEAD_DIM), lambda b: (b, 0, 0))
    return pl.pallas_call(
        functools.partial(_mem_kv_kernel, len(to_cast)),
        grid=(batch // nb,),
        in_specs=[
            pl.BlockSpec((nb, n_mem, D_MODEL), lambda b: (b, 0, 0)),
            pl.BlockSpec((1, D_MODEL), lambda b: (0, 0)),
            pl.BlockSpec((1, D_MODEL, 2 * BRANCH_W), lambda b: (0, 0, 0)),
        ] + [pl.BlockSpec(w.shape, lambda b: (0, 0, 0), pipeline_mode=pl.Buffered(1))
             for w in to_cast],
        out_specs=[rows_spec, rows_spec, out_spec, out_spec]
                  + [pl.BlockSpec(w.shape[1:], lambda b: (0, 0)) for w in to_cast],
        out_shape=[jax.ShapeDtypeStruct(rows_shape, F32), jax.ShapeDtypeStruct(rows_shape, F32),
                   blk(BF16), blk(BF16)]
                  + [jax.ShapeDtypeStruct(w.shape[1:], BF16) for w in to_cast],
        compiler_params=pltpu.CompilerParams(
            dimension_semantics=("arbitrary",), vmem_limit_bytes=VMEM_LIMIT_BYTES),
        name="mem_kv",
    )(mem, mem_norm_w, w_mem_kv, *to_cast)


def _prompt_kernel(c_dec, gamma, tiles_per_seq, n_tiles,
                   x_ref, xb_ref, cos_ref, sin_ref, mk_ref, mv_ref, norm_w_ref, w_in_ref,
                   gn_w_ref, conv_w_ref, conv_b_ref, ln_w_ref, ln_b_ref,
                   w_br_ret_ref, w_br_conv_ref, w_br_mem_ref, w_out_ref, final_w_ref,
                   intra_ref, qdec_ref, kdec_ref,
                   zs_ref, cos_s_ref, sin_s_ref, s0_ref, ck_ref, cv_ref,
                   y_ref, s_ref, hist_ref, os_ref, atts_ref, s_new_ref,
                   ubuf_ref, h_scr, yr_scr, yc_scr, ym_scr, os_scr, atts_scr):
    i = pl.program_id(0)
    t = lax.rem(jnp.minimum(i, n_tiles - 1), tiles_per_seq)
    live = i < n_tiles
    tile = x_ref.shape[1]

    @pl.when(i == 0)
    def _():
        h_scr[...] = jnp.zeros_like(h_scr)
        yr_scr[...] = jnp.zeros_like(yr_scr)
        yc_scr[...] = jnp.zeros_like(yc_scr)
        ym_scr[...] = jnp.zeros_like(ym_scr)
        os_scr[...] = jnp.zeros_like(os_scr)
        atts_scr[...] = jnp.zeros_like(atts_scr)
        hist_ref[...] = jnp.zeros_like(hist_ref)

    @pl.when(jnp.logical_and(t == 0, live))
    def _():
        s_ref[...] = jnp.zeros_like(s_ref)
        ubuf_ref[0:HIST_PAD, :] = jnp.zeros((HIST_PAD, BRANCH_W), F32)

    h_prev = h_scr[...]
    half = D_MODEL // 2

    def gate_job(k):
        lo = OFF_MERGE + k * half
        return lambda: _dot(h_prev, w_in_ref[:, lo:lo + half])

    back_jobs = [gate_job(k) for k in range(2 * N_BRANCH)] + [
        lambda: _dot(yr_scr[...], w_br_ret_ref[...]),
        lambda: _dot(yc_scr[...], w_br_conv_ref[...]),
        lambda: _dot(ym_scr[...], w_br_mem_ref[...]),
    ]
    back_out = [None] * len(back_jobs)

    def run_back(k):
        back_out[k] = back_jobs[k]()

    x = x_ref[0]
    h_bf = (x * _rms_scale(x) * norm_w_ref[...]).astype(BF16)
    run_back(8)

    def proj(off, width=BRANCH_W):
        return _dot(h_bf, w_in_ref[:, off:off + width])

    za = proj(OFF_AC)
    zb = proj(OFF_BC)
    zq = proj(OFF_QR)
    zk = proj(OFF_KR)
    v_bf = proj(OFF_VR).astype(BF16)
    u = za * _sigmoid(zb)
    ubuf_ref[HIST_PAD:HIST_PAD + tile, :] = u
    first = HIST_PAD - CONV_HIST
    conv = None
    for r in range(8):
        part = None
        for j in range(CONV_K):
            if (first + j) % 8 != r:
                continue
            term = ubuf_ref[first + j:first + j + tile, :] * conv_w_ref[0, j:j + 1, :]
            part = term if part is None else part + term
        conv = part if conv is None else conv + part
        run_back(r)
    conv = conv + conv_b_ref[...]
    ubuf_ref[0:HIST_PAD, :] = jnp.where(live, ubuf_ref[tile:tile + HIST_PAD, :],
                                        ubuf_ref[0:HIST_PAD, :])

    z_gc = proj(OFF_GC)
    z_gr = proj(OFF_GR)
    z_qm = proj(OFF_QM)
    z_gm = proj(OFF_GM)

    merged = []
    for k in range(2):
        cols = slice(k * half, (k + 1) * half)
        acc = None
        for b in range(N_BRANCH):
            term = _sigmoid(back_out[2 * b + k]) * back_out[2 * N_BRANCH + b][:, cols]
            acc = term if acc is None else acc + term
        merged.append(acc.astype(BF16))
    merged = jnp.concatenate(merged, axis=-1)

    cos2 = cos_ref[...]
    sin2 = sin_ref[...]
    q_scale = HEAD_DIM ** -0.5
    o_heads = []
    for h in range(RET_HEADS):
        q_h = _rotary(_head(zq, h), cos2, sin2) * q_scale
        k_h = _rotary(_head(zk, h), cos2, sin2)
        v_h = _head(v_bf, h)
        s_old = s_ref[0, h]
        s = s_old
        o_chunks = []
        for c in range(tile // RET_CHUNK):
            rows = slice(c * RET_CHUNK, (c + 1) * RET_CHUNK)
            qc = q_h[rows].astype(BF16)
            kc = k_h[rows]
            vc = v_h[rows]
            sc = _dot_nt(qc, kc.astype(BF16)) * intra_ref[h]
            o_chunks.append(_dot(sc.astype(BF16), vc) + _dot(qc, s.astype(BF16)) * qdec_ref[h])
            kd = (kc * kdec_ref[h]).astype(BF16)
            s = s * c_dec[h] + _dot_tn(kd, vc)
        s_ref[0, h] = jnp.where(live, s, s_old)
        o_heads.append(_layernorm_rows(jnp.concatenate(o_chunks, axis=0)))

    out = xb_ref[0] + _dot(merged, w_out_ref[...])

    ret = jnp.concatenate(o_heads, axis=-1) * gn_w_ref[...]
    yr = _silu(z_gr) * ret
    cn = _layernorm_rows(conv) * ln_w_ref[...] + ln_b_ref[...]
    yc = _silu(z_gc) * _silu(cn)

    qm = z_qm.astype(BF16)
    mk = mk_ref[0]
    mv = mv_ref[0]
    att = []
    for h in range(MEM_HEADS):
        sc = _dot_nt(_head(qm, h), _head(mk, h)) * (HEAD_DIM ** -0.5)
        e = jnp.exp(sc - jnp.max(sc, axis=-1, keepdims=True))
        p = e / jnp.sum(e, axis=-1, keepdims=True)
        att.append(_dot(p.astype(BF16), _head(mv, h)))
    ym = _silu(z_gm) * jnp.concatenate(att, axis=-1)

    y_ref[0] = out * _rms_scale(out) * final_w_ref[...]

    h_scr[...] = h_bf
    yr_scr[...] = yr.astype(BF16)
    yc_scr[...] = yc.astype(BF16)
    ym_scr[...] = ym.astype(BF16)

    @pl.when(jnp.logical_and(t == tiles_per_seq - 1, live))
    def _():
        seq_id = jnp.minimum(i, n_tiles - 1) // tiles_per_seq
        mine = lax.broadcasted_iota(jnp.int32, (hist_ref.shape[1], 1), 0) == seq_id
        for j in range(CONV_HIST):
            hist_ref[j] = jnp.where(mine, ubuf_ref[first + j:first + j + 1, :], hist_ref[j])

    per_step = s0_ref.shape[0]
    r0 = lax.rem(jnp.minimum(i, n_tiles - 1), SUBLANES // per_step) * per_step
    row_id = lax.broadcasted_iota(jnp.int32, (SUBLANES, 1), 0)

    def own_rows(off):
        block = zs_ref[:, off:off + BRANCH_W]
        return jnp.concatenate(
            [jnp.sum(jnp.where(row_id == r0 + b, block, 0.0), axis=0, keepdims=True)
             for b in range(per_step)], axis=0)

    o_rows, att_rows = _sample_sequences(
        gamma, own_rows(OFF_QR), own_rows(OFF_KR), own_rows(OFF_VR), own_rows(OFF_QM),
        cos_s_ref[...], sin_s_ref[...], s0_ref, ck_ref, cv_ref, s_new_ref)
    for scr, out_ref, rows in ((os_scr, os_ref, o_rows), (atts_scr, atts_ref, att_rows)):
        block = scr[...]
        for b in range(per_step):
            mine = jnp.logical_and(row_id == r0 + b, live)
            block = jnp.where(mine, rows[b], block)
        scr[...] = block
        out_ref[...] = block


def _const_spec(shape, single_buffer=False):
    zeros = (0,) * len(shape)
    if single_buffer:
        return pl.BlockSpec(shape, lambda i: zeros, pipeline_mode=pl.Buffered(1))
    return pl.BlockSpec(shape, lambda i: zeros)


def _prompt_layer(x, cos2, sin2, mk_bf, mv_bf, norm_w, w_in_bf, gn_w, conv_w, conv_b, ln_w,
                  ln_b, w_br_ret_bf, w_br_conv_bf, w_br_mem_bf, w_out_bf, final_w,
                  intra, qdec, kdec, c_dec,
                  z_s, cos_s, sin_s, s0, cache_k, cache_v, gamma):
    batch, seq, _ = x.shape
    n_mem = mk_bf.shape[1]
    tile = PROMPT_TILE
    tiles_per_seq = seq // tile
    n_tiles = batch * tiles_per_seq

    def front(i):
        j = jnp.minimum(i, n_tiles - 1)
        return j // tiles_per_seq, lax.rem(j, tiles_per_seq)

    def back(i):
        j = jnp.maximum(i - 1, 0)
        return j // tiles_per_seq, lax.rem(j, tiles_per_seq)

    row = lambda w: _const_spec((1, w))
    weight = lambda a: _const_spec(a.shape, single_buffer=True)
    table = _const_spec((RET_HEADS, RET_CHUNK, RET_CHUNK))
    n_dec = z_s.shape[0]
    per_step = n_dec // n_tiles
    assert per_step * n_tiles == n_dec
    share = lambda shape: pl.BlockSpec(
        shape, lambda i: (jnp.minimum(i, n_tiles - 1),) + (0,) * (len(shape) - 1))
    assert SUBLANES % per_step == 0
    rows8 = lambda w: pl.BlockSpec(
        (SUBLANES, w), lambda i: (jnp.minimum(i, n_tiles - 1) // (SUBLANES // per_step), 0))
    x_tile = lambda which: pl.BlockSpec((1, tile, D_MODEL), lambda i: (*which(i), 0))
    per_seq = lambda shape: pl.BlockSpec(
        (1,) + shape, lambda i: (front(i)[0],) + (0,) * len(shape))
    return pl.pallas_call(
        functools.partial(_prompt_kernel, c_dec, gamma, tiles_per_seq, n_tiles),
        grid=(n_tiles + 1,),
        in_specs=[
            x_tile(front), x_tile(back),
            pl.BlockSpec((tile, HEAD_DIM), lambda i: (front(i)[1], 0)),
            pl.BlockSpec((tile, HEAD_DIM), lambda i: (front(i)[1], 0)),
            per_seq((n_mem, BRANCH_W)), per_seq((n_mem, BRANCH_W)),
            row(D_MODEL), weight(w_in_bf), row(BRANCH_W),
            _const_spec((1, CONV_K, BRANCH_W)), row(BRANCH_W), row(BRANCH_W), row(BRANCH_W),
            weight(w_br_ret_bf), weight(w_br_conv_bf), weight(w_br_mem_bf), weight(w_out_bf),
            row(D_MODEL), table, table, table,
            rows8(IN_W), row(HEAD_DIM), row(HEAD_DIM),
            share((per_step,) + s0.shape[1:]),
            share((per_step,) + cache_k.shape[1:]), share((per_step,) + cache_v.shape[1:]),
        ],
        out_specs=[
            x_tile(back),
            per_seq((RET_HEADS, HEAD_DIM, HEAD_DIM)),
            _const_spec((CONV_HIST, batch, BRANCH_W)),
            rows8(BRANCH_W), rows8(BRANCH_W),
            share((per_step,) + s0.shape[1:]),
        ],
        out_shape=[
            jax.ShapeDtypeStruct((batch, seq, D_MODEL), F32),
            jax.ShapeDtypeStruct((batch, RET_HEADS, HEAD_DIM, HEAD_DIM), F32),
            jax.ShapeDtypeStruct((CONV_HIST, batch, BRANCH_W), F32),
            jax.ShapeDtypeStruct((n_dec, BRANCH_W), F32),
            jax.ShapeDtypeStruct((n_dec, BRANCH_W), F32),
            jax.ShapeDtypeStruct(s0.shape, F32),
        ],
        scratch_shapes=[
            pltpu.VMEM((HIST_PAD + tile, BRANCH_W), F32),
            pltpu.VMEM((tile, D_MODEL), BF16),
            pltpu.VMEM((tile, BRANCH_W), BF16),
            pltpu.VMEM((tile, BRANCH_W), BF16),
            pltpu.VMEM((tile, BRANCH_W), BF16),
            pltpu.VMEM((SUBLANES, BRANCH_W), F32),
            pltpu.VMEM((SUBLANES, BRANCH_W), F32),
        ],
        compiler_params=pltpu.CompilerParams(
            dimension_semantics=("arbitrary",), vmem_limit_bytes=VMEM_LIMIT_BYTES),
        name="prompt_layer",
    )(x, x, cos2, sin2, mk_bf, mv_bf, norm_w, w_in_bf, gn_w, conv_w, conv_b, ln_w, ln_b,
      w_br_ret_bf, w_br_conv_bf, w_br_mem_bf, w_out_bf, final_w, intra, qdec, kdec,
      z_s, cos_s, sin_s, s0, cache_k, cache_v)


SAMPLE_PROJ_BLOCK = 1536


def _sample_proj_kernel(x_ref, norm_w_ref, w_ref, z_ref, w_bf_ref):
    x = x_ref[:, 0, :]
    h_bf = (x * _rms_scale(x) * norm_w_ref[...]).astype(BF16)
    w_bf = w_ref[0].astype(BF16)
    w_bf_ref[...] = w_bf
    z_ref[...] = _dot(h_bf, w_bf)


def _sample_proj(xs, norm_w, w_in):
    n = xs.shape[0]
    return pl.pallas_call(
        _sample_proj_kernel,
        grid=(IN_W // SAMPLE_PROJ_BLOCK,),
        in_specs=[
            pl.BlockSpec((n, 1, D_MODEL), lambda j: (0, 0, 0)),
            pl.BlockSpec((1, D_MODEL), lambda j: (0, 0)),
            pl.BlockSpec((1, D_MODEL, SAMPLE_PROJ_BLOCK), lambda j: (0, 0, j)),
        ],
        out_specs=[pl.BlockSpec((n, SAMPLE_PROJ_BLOCK), lambda j: (0, j)),
                   pl.BlockSpec((D_MODEL, SAMPLE_PROJ_BLOCK), lambda j: (0, j))],
        out_shape=[jax.ShapeDtypeStruct((n, IN_W), F32),
                   jax.ShapeDtypeStruct((D_MODEL, IN_W), BF16)],
        compiler_params=pltpu.CompilerParams(
            dimension_semantics=("arbitrary",), vmem_limit_bytes=VMEM_LIMIT_BYTES),
        name="sample_proj",
    )(xs, norm_w, w_in)


def _sample_sequences(gamma, zq, zk, v, qm, cos2, sin2, s0_ref, ck_ref, cv_ref, s_new_ref):
    n = zq.shape[0]
    n_mem = ck_ref.shape[1] // MEM_HEADS
    q_scale = HEAD_DIM ** -0.5
    mem_scale = HEAD_DIM ** -0.5
    q = [_rotary(_head(zq, h), cos2, sin2) * q_scale for h in range(RET_HEADS)]
    k = [_rotary(_head(zk, h), cos2, sin2) for h in range(RET_HEADS)]

    eye = (lax.broadcasted_iota(jnp.int32, (HEAD_DIM, HEAD_DIM), 0)
           == lax.broadcasted_iota(jnp.int32, (HEAD_DIM, HEAD_DIM), 1))
    ones_bf = jnp.ones((HEAD_DIM, HEAD_DIM), BF16)

    def as_columns(row):
        diag = jnp.where(eye, jnp.broadcast_to(row, (HEAD_DIM, HEAD_DIM)), 0.0)
        return _dot(diag.astype(BF16), ones_bf)

    o_rows, att_rows = [], []
    for b in range(n):
        o_heads = []
        for h in range(RET_HEADS):
            q_row = q[h][b:b + 1]
            k_row = k[h][b:b + 1]
            v_row = _head(v, h)[b:b + 1]
            s0 = s0_ref[b, h]
            qk = jnp.sum(q_row * k_row, axis=-1, keepdims=True)
            q_lhs = jnp.broadcast_to(q_row, (SUBLANES, HEAD_DIM)).astype(BF16)
            qs = _dot(q_lhs, s0.astype(BF16))[0:1]
            o_heads.append(qk * v_row + gamma[h] * qs)
            s_new_ref[b, h] = gamma[h] * s0 + as_columns(k_row) * v_row
        o_rows.append(jnp.concatenate(o_heads, axis=-1))

        att_heads = []
        for h in range(MEM_HEADS):
            head_rows = pl.ds(h, n_mem, stride=MEM_HEADS)
            prod = ck_ref[b, head_rows, :] * (_head(qm, h)[b:b + 1] * mem_scale)
            sc = jnp.sum(prod, axis=-1, keepdims=True)
            e = jnp.exp(sc - jnp.max(sc, axis=0, keepdims=True))
            weighted = jnp.sum(e * cv_ref[b, head_rows, :], axis=0, keepdims=True)
            att_heads.append(weighted / jnp.sum(e, axis=0, keepdims=True))
        att_rows.append(jnp.concatenate(att_heads, axis=-1))
    return o_rows, att_rows


def _sample_out_kernel(x_ref, z_ref, o_ref, att_ref, hist_ref, conv_w_ref, gn_w_ref, conv_b_ref,
                       ln_w_ref, ln_b_ref, w_br_ret_ref, w_br_conv_ref, w_br_mem_ref, w_out_ref,
                       final_w_ref, y_ref, hist_new_ref):
    gate = lambda off: _silu(z_ref[:, off:off + BRANCH_W])
    yr = gate(OFF_GR) * _group_norm_heads(o_ref[...], gn_w_ref[...])

    u = z_ref[:, OFF_AC:OFF_AC + BRANCH_W] * _sigmoid(z_ref[:, OFF_BC:OFF_BC + BRANCH_W])
    conv = u * conv_w_ref[0, CONV_HIST:CONV_K, :] + conv_b_ref[...]
    for j in range(CONV_HIST):
        conv = conv + hist_ref[j] * conv_w_ref[0, j:j + 1, :]
        hist_new_ref[j] = hist_ref[j + 1] if j + 1 < CONV_HIST else u
    cn = _layernorm_rows(conv) * ln_w_ref[...] + ln_b_ref[...]
    yc = gate(OFF_GC) * _silu(cn)

    ym = gate(OFF_GM) * att_ref[...]
    gate_pre = lambda i: z_ref[:, OFF_MERGE + i * D_MODEL:OFF_MERGE + (i + 1) * D_MODEL]
    y_ref[:, 0, :] = _merge_and_project(x_ref[:, 0, :], gate_pre, yr, yc, ym, w_br_ret_ref,
                                        w_br_conv_ref, w_br_mem_ref, w_out_ref, final_w_ref[...])


SAMPLE_OUT_BLOCK = 32


def _sample_out(xs, z, o, att, hist, conv_w, gn_w, conv_b, ln_w, ln_b,
                w_br_ret_bf, w_br_conv_bf, w_br_mem_bf, w_out_bf, final_w):
    n = xs.shape[0]
    nb = SAMPLE_OUT_BLOCK
    rows = lambda w: pl.BlockSpec((nb, w), lambda i: (i, 0))
    hist_spec = pl.BlockSpec((CONV_HIST, nb, BRANCH_W), lambda i: (0, i, 0))
    tokens = pl.BlockSpec((nb, 1, D_MODEL), lambda i: (i, 0, 0))
    const = lambda a: pl.BlockSpec(a.shape, lambda i: (0,) * a.ndim)
    weight = lambda a: pl.BlockSpec(a.shape, lambda i: (0,) * a.ndim,
                                    pipeline_mode=pl.Buffered(1))
    return pl.pallas_call(
        _sample_out_kernel,
        grid=(n // nb,),
        in_specs=[tokens, rows(IN_W), rows(BRANCH_W), rows(BRANCH_W), hist_spec,
                  const(conv_w), const(gn_w), const(conv_b), const(ln_w), const(ln_b),
                  weight(w_br_ret_bf), weight(w_br_conv_bf), weight(w_br_mem_bf),
                  weight(w_out_bf), const(final_w)],
        out_specs=[tokens, hist_spec],
        out_shape=[jax.ShapeDtypeStruct(xs.shape, F32), jax.ShapeDtypeStruct(hist.shape, F32)],
        compiler_params=pltpu.CompilerParams(
            dimension_semantics=("arbitrary",), vmem_limit_bytes=VMEM_LIMIT_BYTES),
        name="sample_out",
    )(xs, z, o, att, hist, conv_w, gn_w, conv_b, ln_w, ln_b,
      w_br_ret_bf, w_br_conv_bf, w_br_mem_bf, w_out_bf, final_w)


def _rotary_tables(pos):
    half = HEAD_DIM // 2
    inv = ROPE_BASE ** (-np.arange(half, dtype=np.float64) / half)
    ang = np.asarray(pos, np.float64)[:, None] * inv[None, :]
    cos, sin = np.cos(ang), np.sin(ang)
    table = lambda a, b: np.concatenate([a, b], axis=-1).astype(np.float32)
    return table(cos, cos), table(-sin, sin)


def _decay_tables():
    c = RET_CHUNK
    log_g = np.log1p(-np.exp2(-5.0 - np.arange(RET_HEADS, dtype=np.float64)))
    idx = np.arange(c, dtype=np.float64)
    diff = idx[:, None] - idx[None, :]
    intra = np.where(diff >= 0, np.exp(np.maximum(diff, 0.0)[None] * log_g[:, None, None]), 0.0)
    q_dec = np.exp((idx + 1.0)[None, :] * log_g[:, None])
    k_dec = np.exp((c - 1.0 - idx)[None, :] * log_g[:, None])
    bcast = lambda a: np.ascontiguousarray(
        np.broadcast_to(a[:, :, None], (RET_HEADS, c, HEAD_DIM))).astype(np.float32)
    return intra.astype(np.float32), bcast(q_dec), bcast(k_dec)


def _gamma_powers(n):
    return tuple(float(np.exp(np.log1p(-np.exp2(-5.0 - h)) * n)) for h in range(RET_HEADS))


def kernel(x_prompt, x_sample, mem_prompt, state_ret, state_conv, cache_mem_k, cache_mem_v,
           norm_w, w_in, ret_gn_w, conv_w, conv_b, conv_ln_w, conv_ln_b, mem_norm_w,
           w_mem_kv, w_br_ret, w_br_conv, w_br_mem, w_out, final_norm_w):
    depth = w_in.shape[0]
    assert depth == 1, "single-layer step"
    batch, seq, _ = x_prompt.shape
    n_dec, dec_seq, _ = x_sample.shape
    assert dec_seq == 1 and seq % PROMPT_TILE == 0
    n_mem = mem_prompt.shape[1]

    final_w = final_norm_w[None, :]

    intra, qdec, kdec = _decay_tables()
    cos_p, sin_p = _rotary_tables(np.arange(seq))
    cos_s, sin_s = _rotary_tables(np.arange(dec_seq) + PAST_LEN)

    z_s, w_in_bf = _sample_proj(x_sample, norm_w, w_in)

    mk, mv, mk_bf, mv_bf, w_br_ret_bf, w_br_conv_bf, w_br_mem_bf, w_out_bf = _mem_kv(
        mem_prompt, mem_norm_w, w_mem_kv, (w_br_ret, w_br_conv, w_br_mem, w_out))

    y_prompt, s_prompt, hist_prompt, o_s, att_s, s_sample = _prompt_layer(
        x_prompt, cos_p, sin_p, mk_bf, mv_bf, norm_w, w_in_bf, ret_gn_w, conv_w, conv_b,
        conv_ln_w, conv_ln_b, w_br_ret_bf, w_br_conv_bf, w_br_mem_bf, w_out_bf, final_w,
        intra, qdec, kdec, _gamma_powers(RET_CHUNK),
        z_s, cos_s, sin_s, state_ret[0],
        cache_mem_k.reshape(n_dec, n_mem * MEM_HEADS, HEAD_DIM),
        cache_mem_v.reshape(n_dec, n_mem * MEM_HEADS, HEAD_DIM), _gamma_powers(1))

    y_sample, hist_sample = _sample_out(
        x_sample, z_s, o_s, att_s,
        jnp.transpose(state_conv[0], (1, 0, 2)), conv_w, ret_gn_w, conv_b, conv_ln_w, conv_ln_b,
        w_br_ret_bf, w_br_conv_bf, w_br_mem_bf, w_out_bf, final_w)

    heads = lambda a: a.reshape(1, batch, n_mem, MEM_HEADS, HEAD_DIM)
    return (y_prompt, y_sample, s_prompt[None], s_sample[None],
            jnp.transpose(hist_prompt, (1, 0, 2))[None],
            jnp.transpose(hist_sample, (1, 0, 2))[None], heads(mk), heads(mv))
```

```python
import functools

import numpy as np
import jax
import jax.numpy as jnp
from jax import lax
from jax.experimental import pallas as pl
from jax.experimental.pallas import tpu as pltpu

F32 = jnp.float32
BF16 = jnp.bfloat16

D_MODEL = 1024
RET_HEADS = 4
HEAD_DIM = 128
SUBLANES = 8
BRANCH_W = 512
RET_CHUNK = 128
ROPE_BASE = 10000.0
CONV_K = 31
CONV_HIST = CONV_K - 1
MEM_HEADS = 4
N_BRANCH = 3
EPS = 1e-6
PAST_LEN = 16384

OFF_QR, OFF_KR, OFF_VR, OFF_GR = 0, 512, 1024, 1536
OFF_AC, OFF_BC, OFF_GC = 2048, 2560, 3072
OFF_QM, OFF_GM = 3584, 4096
OFF_MERGE = 4608
IN_W = OFF_MERGE + N_BRANCH * D_MODEL

V7X_VMEM_BYTES = 64 * 1024 * 1024
VMEM_LIMIT_BYTES = V7X_VMEM_BYTES - 8 * 1024 * 1024

PROMPT_TILE = 256
HIST_PAD = 32


def _sigmoid(x):
    return 0.5 * jnp.tanh(0.5 * x) + 0.5


def _silu(x):
    hx = 0.5 * x
    return hx * (jnp.tanh(hx) + 1.0)


def _rms_scale(x):
    return lax.rsqrt(jnp.mean(x * x, axis=-1, keepdims=True) + EPS)


def _layernorm_rows(x):
    mu = jnp.mean(x, axis=-1, keepdims=True)
    xc = x - mu
    var = jnp.mean(xc * xc, axis=-1, keepdims=True)
    return xc * lax.rsqrt(var + EPS)


def _rotary(x, cos2, sin2):
    return x * cos2 + pltpu.roll(x, HEAD_DIM // 2, axis=1) * sin2


def _dot(a, b):
    return jnp.dot(a, b, preferred_element_type=F32)


def _dot_nt(a, b):
    return lax.dot_general(a, b, (((1,), (1,)), ((), ())), preferred_element_type=F32)


def _dot_tn(a, b):
    return lax.dot_general(a, b, (((0,), (0,)), ((), ())), preferred_element_type=F32)


def _head(x, h):
    return x[:, h * HEAD_DIM:(h + 1) * HEAD_DIM]


def _group_norm_heads(o, gn_w):
    parts = [_layernorm_rows(_head(o, h)) for h in range(RET_HEADS)]
    return jnp.concatenate(parts, axis=-1) * gn_w


def _merge_and_project(x, gate_pre, yr, yc, ym, w_br_ret_ref, w_br_conv_ref,
                       w_br_mem_ref, w_out_ref, final_w):
    branches = (
        _dot(yr.astype(BF16), w_br_ret_ref[...]),
        _dot(yc.astype(BF16), w_br_conv_ref[...]),
        _dot(ym.astype(BF16), w_br_mem_ref[...]),
    )
    merged = None
    for i, br in enumerate(branches):
        term = _sigmoid(gate_pre(i)) * br
        merged = term if merged is None else merged + term
    out = x + _dot(merged.astype(BF16), w_out_ref[...])
    return out * _rms_scale(out) * final_w


MEM_KV_BLOCK = 4


def _mem_kv_kernel(n_cast, mem_ref, norm_w_ref, w_ref, *refs):
    cast_in, (k_ref, v_ref, kbf_ref, vbf_ref), cast_out = (
        refs[:n_cast], refs[n_cast:n_cast + 4], refs[n_cast + 4:])

    @pl.when(pl.program_id(0) == 0)
    def _():
        for src, dst in zip(cast_in, cast_out):
            dst[...] = src[0].astype(BF16)

    nb, n_mem, _ = mem_ref.shape
    m = mem_ref[...].reshape(nb * n_mem, D_MODEL)
    hm = (m * _rms_scale(m) * norm_w_ref[...]).astype(BF16)
    kv = _dot(hm, w_ref[0].astype(BF16))
    for b in range(nb):
        k = kv[b * n_mem:(b + 1) * n_mem, :BRANCH_W]
        v = kv[b * n_mem:(b + 1) * n_mem, BRANCH_W:]
        for h in range(MEM_HEADS):
            k_ref[b, pl.ds(h, n_mem, stride=MEM_HEADS), :] = _head(k, h)
            v_ref[b, pl.ds(h, n_mem, stride=MEM_HEADS), :] = _head(v, h)
        kbf_ref[b] = k.astype(BF16)
        vbf_ref[b] = v.astype(BF16)


def _mem_kv(mem, mem_norm_w, w_mem_kv, to_cast):
    batch, n_mem, _ = mem.shape
    nb = MEM_KV_BLOCK
    blk = lambda dt: jax.ShapeDtypeStruct((batch, n_mem, BRANCH_W), dt)
    out_spec = pl.BlockSpec((nb, n_mem, BRANCH_W), lambda b: (b, 0, 0))
    rows_shape = (batch, n_mem * MEM_HEADS, HEAD_DIM)
    rows_spec = pl.BlockSpec((nb, n_mem * MEM_HEADS, HEAD_DIM), lambda b: (b, 0, 0))
    return pl.pallas_call(
        functools.partial(_mem_kv_kernel, len(to_cast)),
        grid=(batch // nb,),
        in_specs=[
            pl.BlockSpec((nb, n_mem, D_MODEL), lambda b: (b, 0, 0)),
            pl.BlockSpec((1, D_MODEL), lambda b: (0, 0)),
            pl.BlockSpec((1, D_MODEL, 2 * BRANCH_W), lambda b: (0, 0, 0)),
        ] + [pl.BlockSpec(w.shape, lambda b: (0, 0, 0), pipeline_mode=pl.Buffered(1))
             for w in to_cast],
        out_specs=[rows_spec, rows_spec, out_spec, out_spec]
                  + [pl.BlockSpec(w.shape[1:], lambda b: (0, 0)) for w in to_cast],
        out_shape=[jax.ShapeDtypeStruct(rows_shape, F32), jax.ShapeDtypeStruct(rows_shape, F32),
                   blk(BF16), blk(BF16)]
                  + [jax.ShapeDtypeStruct(w.shape[1:], BF16) for w in to_cast],
        compiler_params=pltpu.CompilerParams(
            dimension_semantics=("arbitrary",), vmem_limit_bytes=VMEM_LIMIT_BYTES),
        name="mem_kv",
    )(mem, mem_norm_w, w_mem_kv, *to_cast)


def _prompt_kernel(c_dec, gamma, tiles_per_seq, n_tiles,
                   x_ref, xb_ref, cos_ref, sin_ref, mk_ref, mv_ref, norm_w_ref, w_in_ref,
                   gn_w_ref, conv_w_ref, conv_b_ref, ln_w_ref, ln_b_ref,
                   w_br_ret_ref, w_br_conv_ref, w_br_mem_ref, w_out_ref, final_w_ref,
                   intra_ref, qdec_ref, kdec_ref,
                   zs_ref, cos_s_ref, sin_s_ref, s0_ref, ck_ref, cv_ref,
                   y_ref, s_ref, hist_ref, os_ref, atts_ref, s_new_ref,
                   ubuf_ref, h_scr, yr_scr, yc_scr, ym_scr, os_scr, atts_scr):
    i = pl.program_id(0)
    t = lax.rem(jnp.minimum(i, n_tiles - 1), tiles_per_seq)
    live = i < n_tiles
    tile = x_ref.shape[1]

    @pl.when(i == 0)
    def _():
        h_scr[...] = jnp.zeros_like(h_scr)
        yr_scr[...] = jnp.zeros_like(yr_scr)
        yc_scr[...] = jnp.zeros_like(yc_scr)
        ym_scr[...] = jnp.zeros_like(ym_scr)
        os_scr[...] = jnp.zeros_like(os_scr)
        atts_scr[...] = jnp.zeros_like(atts_scr)
        hist_ref[...] = jnp.zeros_like(hist_ref)

    @pl.when(jnp.logical_and(t == 0, live))
    def _():
        s_ref[...] = jnp.zeros_like(s_ref)
        ubuf_ref[0:HIST_PAD, :] = jnp.zeros((HIST_PAD, BRANCH_W), F32)

    h_prev = h_scr[...]
    half = D_MODEL // 2

    def gate_job(k):
        lo = OFF_MERGE + k * half
        return lambda: _dot(h_prev, w_in_ref[:, lo:lo + half])

    back_jobs = [gate_job(k) for k in range(2 * N_BRANCH)] + [
        lambda: _dot(yr_scr[...], w_br_ret_ref[...]),
        lambda: _dot(yc_scr[...], w_br_conv_ref[...]),
        lambda: _dot(ym_scr[...], w_br_mem_ref[...]),
    ]
    back_out = [None] * len(back_jobs)

    def run_back(k):
        back_out[k] = back_jobs[k]()

    x = x_ref[0]
    h_bf = (x * _rms_scale(x) * norm_w_ref[...]).astype(BF16)
    run_back(8)

    def proj(off, width=BRANCH_W):
        return _dot(h_bf, w_in_ref[:, off:off + width])

    za = proj(OFF_AC)
    zb = proj(OFF_BC)
    zq = proj(OFF_QR)
    zk = proj(OFF_KR)
    v_bf = proj(OFF_VR).astype(BF16)
    u = za * _sigmoid(zb)
    ubuf_ref[HIST_PAD:HIST_PAD + tile, :] = u
    first = HIST_PAD - CONV_HIST
    conv = None
    for r in range(8):
        part = None
        for j in range(CONV_K):
            if (first + j) % 8 != r:
                continue
            term = ubuf_ref[first + j:first + j + tile, :] * conv_w_ref[0, j:j + 1, :]
            part = term if part is None else part + term
        conv = part if conv is None else conv + part
        run_back(r)
    conv = conv + conv_b_ref[...]
    ubuf_ref[0:HIST_PAD, :] = jnp.where(live, ubuf_ref[tile:tile + HIST_PAD, :],
                                        ubuf_ref[0:HIST_PAD, :])

    z_gc = proj(OFF_GC)
    z_gr = proj(OFF_GR)
    z_qm = proj(OFF_QM)
    z_gm = proj(OFF_GM)

    merged = []
    for k in range(2):
        cols = slice(k * half, (k + 1) * half)
        acc = None
        for b in range(N_BRANCH):
            term = _sigmoid(back_out[2 * b + k]) * back_out[2 * N_BRANCH + b][:, cols]
            acc = term if acc is None else acc + term
        merged.append(acc.astype(BF16))
    merged = jnp.concatenate(merged, axis=-1)

    cos2 = cos_ref[...]
    sin2 = sin_ref[...]
    q_scale = HEAD_DIM ** -0.5
    def block_diag(a, b):
        zero = jnp.zeros_like(a)
        return jnp.concatenate([jnp.concatenate([a, zero], axis=-1),
                                jnp.concatenate([zero, b], axis=-1)], axis=0)

    o_heads = []
    for h0 in range(0, RET_HEADS, 2):
        pair = (h0, h0 + 1)
        q_p = [_rotary(_head(zq, h), cos2, sin2) * q_scale for h in pair]
        k_p = [_rotary(_head(zk, h), cos2, sin2) for h in pair]
        v_p = [_head(v_bf, h) for h in pair]
        s_old = [s_ref[0, h] for h in pair]
        s = list(s_old)
        intra2 = jnp.concatenate([intra_ref[h] for h in pair], axis=-1)
        qdec2 = jnp.concatenate([qdec_ref[h] for h in pair], axis=-1)
        o_chunks = []
        for c in range(tile // RET_CHUNK):
            rows = slice(c * RET_CHUNK, (c + 1) * RET_CHUNK)
            kc = [k[rows] for k in k_p]
            vc = [v[rows] for v in v_p]
            qc2 = jnp.concatenate([q[rows] for q in q_p], axis=-1).astype(BF16)
            sc2 = _dot_nt(qc2, block_diag(*[k.astype(BF16) for k in kc])) * intra2
            o_chunks.append(_dot(sc2.astype(BF16), block_diag(*vc))
                            + _dot(qc2, block_diag(*[x.astype(BF16) for x in s])) * qdec2)
            for n, h in enumerate(pair):
                kd = (kc[n] * kdec_ref[h]).astype(BF16)
                s[n] = s[n] * c_dec[h] + _dot_tn(kd, vc[n])
        o_pair = jnp.concatenate(o_chunks, axis=0)
        for n, h in enumerate(pair):
            s_ref[0, h] = jnp.where(live, s[n], s_old[n])
            o_heads.append(_layernorm_rows(_head(o_pair, n)))

    out = xb_ref[0] + _dot(merged, w_out_ref[...])

    ret = jnp.concatenate(o_heads, axis=-1) * gn_w_ref[...]
    yr = _silu(z_gr) * ret
    cn = _layernorm_rows(conv) * ln_w_ref[...] + ln_b_ref[...]
    yc = _silu(z_gc) * _silu(cn)

    qm = z_qm.astype(BF16)
    mk = mk_ref[0]
    mv = mv_ref[0]
    att = []
    for h in range(MEM_HEADS):
        sc = _dot_nt(_head(qm, h), _head(mk, h)) * (HEAD_DIM ** -0.5)
        e = jnp.exp(sc - jnp.max(sc, axis=-1, keepdims=True))
        p = e / jnp.sum(e, axis=-1, keepdims=True)
        att.append(_dot(p.astype(BF16), _head(mv, h)))
    ym = _silu(z_gm) * jnp.concatenate(att, axis=-1)

    y_ref[0] = out * _rms_scale(out) * final_w_ref[...]

    h_scr[...] = h_bf
    yr_scr[...] = yr.astype(BF16)
    yc_scr[...] = yc.astype(BF16)
    ym_scr[...] = ym.astype(BF16)

    @pl.when(jnp.logical_and(t == tiles_per_seq - 1, live))
    def _():
        seq_id = jnp.minimum(i, n_tiles - 1) // tiles_per_seq
        mine = lax.broadcasted_iota(jnp.int32, (hist_ref.shape[1], 1), 0) == seq_id
        for j in range(CONV_HIST):
            hist_ref[j] = jnp.where(mine, ubuf_ref[first + j:first + j + 1, :], hist_ref[j])

    per_step = s0_ref.shape[0]
    r0 = lax.rem(jnp.minimum(i, n_tiles - 1), SUBLANES // per_step) * per_step
    row_id = lax.broadcasted_iota(jnp.int32, (SUBLANES, 1), 0)

    def own_rows(off):
        block = zs_ref[:, off:off + BRANCH_W]
        return jnp.concatenate(
            [jnp.sum(jnp.where(row_id == r0 + b, block, 0.0), axis=0, keepdims=True)
             for b in range(per_step)], axis=0)

    o_rows, att_rows = _sample_sequences(
        gamma, own_rows(OFF_QR), own_rows(OFF_KR), own_rows(OFF_VR), own_rows(OFF_QM),
        cos_s_ref[...], sin_s_ref[...], s0_ref, ck_ref, cv_ref, s_new_ref)
    for scr, out_ref, rows in ((os_scr, os_ref, o_rows), (atts_scr, atts_ref, att_rows)):
        block = scr[...]
        for b in range(per_step):
            mine = jnp.logical_and(row_id == r0 + b, live)
            block = jnp.where(mine, rows[b], block)
        scr[...] = block
        out_ref[...] = block


def _const_spec(shape, single_buffer=False):
    zeros = (0,) * len(shape)
    if single_buffer:
        return pl.BlockSpec(shape, lambda i: zeros, pipeline_mode=pl.Buffered(1))
    return pl.BlockSpec(shape, lambda i: zeros)


def _prompt_layer(x, cos2, sin2, mk_bf, mv_bf, norm_w, w_in_bf, gn_w, conv_w, conv_b, ln_w,
                  ln_b, w_br_ret_bf, w_br_conv_bf, w_br_mem_bf, w_out_bf, final_w,
                  intra, qdec, kdec, c_dec,
                  z_s, cos_s, sin_s, s0, cache_k, cache_v, gamma):
    batch, seq, _ = x.shape
    n_mem = mk_bf.shape[1]
    tile = PROMPT_TILE
    tiles_per_seq = seq // tile
    n_tiles = batch * tiles_per_seq

    def front(i):
        j = jnp.minimum(i, n_tiles - 1)
        return j // tiles_per_seq, lax.rem(j, tiles_per_seq)

    def back(i):
        j = jnp.maximum(i - 1, 0)
        return j // tiles_per_seq, lax.rem(j, tiles_per_seq)

    row = lambda w: _const_spec((1, w))
    weight = lambda a: _const_spec(a.shape, single_buffer=True)
    table = _const_spec((RET_HEADS, RET_CHUNK, RET_CHUNK))
    n_dec = z_s.shape[0]
    per_step = n_dec // n_tiles
    assert per_step * n_tiles == n_dec
    share = lambda shape: pl.BlockSpec(
        shape, lambda i: (jnp.minimum(i, n_tiles - 1),) + (0,) * (len(shape) - 1))
    assert SUBLANES % per_step == 0
    rows8 = lambda w: pl.BlockSpec(
        (SUBLANES, w), lambda i: (jnp.minimum(i, n_tiles - 1) // (SUBLANES // per_step), 0))
    x_tile = lambda which: pl.BlockSpec((1, tile, D_MODEL), lambda i: (*which(i), 0))
    per_seq = lambda shape: pl.BlockSpec(
        (1,) + shape, lambda i: (front(i)[0],) + (0,) * len(shape))
    return pl.pallas_call(
        functools.partial(_prompt_kernel, c_dec, gamma, tiles_per_seq, n_tiles),
        grid=(n_tiles + 1,),
        in_specs=[
            x_tile(front), x_tile(back),
            pl.BlockSpec((tile, HEAD_DIM), lambda i: (front(i)[1], 0)),
            pl.BlockSpec((tile, HEAD_DIM), lambda i: (front(i)[1], 0)),
            per_seq((n_mem, BRANCH_W)), per_seq((n_mem, BRANCH_W)),
            row(D_MODEL), weight(w_in_bf), row(BRANCH_W),
            _const_spec((1, CONV_K, BRANCH_W)), row(BRANCH_W), row(BRANCH_W), row(BRANCH_W),
            weight(w_br_ret_bf), weight(w_br_conv_bf), weight(w_br_mem_bf), weight(w_out_bf),
            row(D_MODEL), table, table, table,
            rows8(IN_W), row(HEAD_DIM), row(HEAD_DIM),
            share((per_step,) + s0.shape[1:]),
            share((per_step,) + cache_k.shape[1:]), share((per_step,) + cache_v.shape[1:]),
        ],
        out_specs=[
            x_tile(back),
            per_seq((RET_HEADS, HEAD_DIM, HEAD_DIM)),
            _const_spec((CONV_HIST, batch, BRANCH_W)),
            rows8(BRANCH_W), rows8(BRANCH_W),
            share((per_step,) + s0.shape[1:]),
        ],
        out_shape=[
            jax.ShapeDtypeStruct((batch, seq, D_MODEL), F32),
            jax.ShapeDtypeStruct((batch, RET_HEADS, HEAD_DIM, HEAD_DIM), F32),
            jax.ShapeDtypeStruct((CONV_HIST, batch, BRANCH_W), F32),
            jax.ShapeDtypeStruct((n_dec, BRANCH_W), F32),
            jax.ShapeDtypeStruct((n_dec, BRANCH_W), F32),
            jax.ShapeDtypeStruct(s0.shape, F32),
        ],
        scratch_shapes=[
            pltpu.VMEM((HIST_PAD + tile, BRANCH_W), F32),
            pltpu.VMEM((tile, D_MODEL), BF16),
            pltpu.VMEM((tile, BRANCH_W), BF16),
            pltpu.VMEM((tile, BRANCH_W), BF16),
            pltpu.VMEM((tile, BRANCH_W), BF16),
            pltpu.VMEM((SUBLANES, BRANCH_W), F32),
            pltpu.VMEM((SUBLANES, BRANCH_W), F32),
        ],
        compiler_params=pltpu.CompilerParams(
            dimension_semantics=("arbitrary",), vmem_limit_bytes=VMEM_LIMIT_BYTES),
        name="prompt_layer",
    )(x, x, cos2, sin2, mk_bf, mv_bf, norm_w, w_in_bf, gn_w, conv_w, conv_b, ln_w, ln_b,
      w_br_ret_bf, w_br_conv_bf, w_br_mem_bf, w_out_bf, final_w, intra, qdec, kdec,
      z_s, cos_s, sin_s, s0, cache_k, cache_v)


SAMPLE_PROJ_BLOCK = 1536


def _sample_proj_kernel(x_ref, norm_w_ref, w_ref, z_ref, w_bf_ref):
    x = x_ref[:, 0, :]
    h_bf = (x * _rms_scale(x) * norm_w_ref[...]).astype(BF16)
    w_bf = w_ref[0].astype(BF16)
    w_bf_ref[...] = w_bf
    z_ref[...] = _dot(h_bf, w_bf)


def _sample_proj(xs, norm_w, w_in):
    n = xs.shape[0]
    return pl.pallas_call(
        _sample_proj_kernel,
        grid=(IN_W // SAMPLE_PROJ_BLOCK,),
        in_specs=[
            pl.BlockSpec((n, 1, D_MODEL), lambda j: (0, 0, 0)),
            pl.BlockSpec((1, D_MODEL), lambda j: (0, 0)),
            pl.BlockSpec((1, D_MODEL, SAMPLE_PROJ_BLOCK), lambda j: (0, 0, j)),
        ],
        out_specs=[pl.BlockSpec((n, SAMPLE_PROJ_BLOCK), lambda j: (0, j)),
                   pl.BlockSpec((D_MODEL, SAMPLE_PROJ_BLOCK), lambda j: (0, j))],
        out_shape=[jax.ShapeDtypeStruct((n, IN_W), F32),
                   jax.ShapeDtypeStruct((D_MODEL, IN_W), BF16)],
        compiler_params=pltpu.CompilerParams(
            dimension_semantics=("arbitrary",), vmem_limit_bytes=VMEM_LIMIT_BYTES),
        name="sample_proj",
    )(xs, norm_w, w_in)


def _sample_sequences(gamma, zq, zk, v, qm, cos2, sin2, s0_ref, ck_ref, cv_ref, s_new_ref):
    n = zq.shape[0]
    n_mem = ck_ref.shape[1] // MEM_HEADS
    q_scale = HEAD_DIM ** -0.5
    mem_scale = HEAD_DIM ** -0.5
    q = [_rotary(_head(zq, h), cos2, sin2) * q_scale for h in range(RET_HEADS)]
    k = [_rotary(_head(zk, h), cos2, sin2) for h in range(RET_HEADS)]

    eye = (lax.broadcasted_iota(jnp.int32, (HEAD_DIM, HEAD_DIM), 0)
           == lax.broadcasted_iota(jnp.int32, (HEAD_DIM, HEAD_DIM), 1))
    ones_bf = jnp.ones((HEAD_DIM, HEAD_DIM), BF16)

    def as_columns(row):
        diag = jnp.where(eye, jnp.broadcast_to(row, (HEAD_DIM, HEAD_DIM)), 0.0)
        return _dot(diag.astype(BF16), ones_bf)

    o_rows, att_rows = [], []
    for b in range(n):
        o_heads = []
        for h in range(RET_HEADS):
            q_row = q[h][b:b + 1]
            k_row = k[h][b:b + 1]
            v_row = _head(v, h)[b:b + 1]
            s0 = s0_ref[b, h]
            qk = jnp.sum(q_row * k_row, axis=-1, keepdims=True)
            q_lhs = jnp.broadcast_to(q_row, (SUBLANES, HEAD_DIM)).astype(BF16)
            qs = _dot(q_lhs, s0.astype(BF16))[0:1]
            o_heads.append(qk * v_row + gamma[h] * qs)
            s_new_ref[b, h] = gamma[h] * s0 + as_columns(k_row) * v_row
        o_rows.append(jnp.concatenate(o_heads, axis=-1))

        att_heads = []
        for h in range(MEM_HEADS):
            head_rows = pl.ds(h, n_mem, stride=MEM_HEADS)
            prod = ck_ref[b, head_rows, :] * (_head(qm, h)[b:b + 1] * mem_scale)
            sc = jnp.sum(prod, axis=-1, keepdims=True)
            e = jnp.exp(sc - jnp.max(sc, axis=0, keepdims=True))
            weighted = jnp.sum(e * cv_ref[b, head_rows, :], axis=0, keepdims=True)
            att_heads.append(weighted / jnp.sum(e, axis=0, keepdims=True))
        att_rows.append(jnp.concatenate(att_heads, axis=-1))
    return o_rows, att_rows


def _sample_out_kernel(x_ref, z_ref, o_ref, att_ref, hist_ref, conv_w_ref, gn_w_ref, conv_b_ref,
                       ln_w_ref, ln_b_ref, w_br_ret_ref, w_br_conv_ref, w_br_mem_ref, w_out_ref,
                       final_w_ref, y_ref, hist_new_ref):
    gate = lambda off: _silu(z_ref[:, off:off + BRANCH_W])
    yr = gate(OFF_GR) * _group_norm_heads(o_ref[...], gn_w_ref[...])

    u = z_ref[:, OFF_AC:OFF_AC + BRANCH_W] * _sigmoid(z_ref[:, OFF_BC:OFF_BC + BRANCH_W])
    conv = u * conv_w_ref[0, CONV_HIST:CONV_K, :] + conv_b_ref[...]
    for j in range(CONV_HIST):
        conv = conv + hist_ref[j] * conv_w_ref[0, j:j + 1, :]
        hist_new_ref[j] = hist_ref[j + 1] if j + 1 < CONV_HIST else u
    cn = _layernorm_rows(conv) * ln_w_ref[...] + ln_b_ref[...]
    yc = gate(OFF_GC) * _silu(cn)

    ym = gate(OFF_GM) * att_ref[...]
    gate_pre = lambda i: z_ref[:, OFF_MERGE + i * D_MODEL:OFF_MERGE + (i + 1) * D_MODEL]
    y_ref[:, 0, :] = _merge_and_project(x_ref[:, 0, :], gate_pre, yr, yc, ym, w_br_ret_ref,
                                        w_br_conv_ref, w_br_mem_ref, w_out_ref, final_w_ref[...])


SAMPLE_OUT_BLOCK = 32


def _sample_out(xs, z, o, att, hist, conv_w, gn_w, conv_b, ln_w, ln_b,
                w_br_ret_bf, w_br_conv_bf, w_br_mem_bf, w_out_bf, final_w):
    n = xs.shape[0]
    nb = SAMPLE_OUT_BLOCK
    rows = lambda w: pl.BlockSpec((nb, w), lambda i: (i, 0))
    hist_spec = pl.BlockSpec((CONV_HIST, nb, BRANCH_W), lambda i: (0, i, 0))
    tokens = pl.BlockSpec((nb, 1, D_MODEL), lambda i: (i, 0, 0))
    const = lambda a: pl.BlockSpec(a.shape, lambda i: (0,) * a.ndim)
    weight = lambda a: pl.BlockSpec(a.shape, lambda i: (0,) * a.ndim,
                                    pipeline_mode=pl.Buffered(1))
    return pl.pallas_call(
        _sample_out_kernel,
        grid=(n // nb,),
        in_specs=[tokens, rows(IN_W), rows(BRANCH_W), rows(BRANCH_W), hist_spec,
                  const(conv_w), const(gn_w), const(conv_b), const(ln_w), const(ln_b),
                  weight(w_br_ret_bf), weight(w_br_conv_bf), weight(w_br_mem_bf),
                  weight(w_out_bf), const(final_w)],
        out_specs=[tokens, hist_spec],
        out_shape=[jax.ShapeDtypeStruct(xs.shape, F32), jax.ShapeDtypeStruct(hist.shape, F32)],
        compiler_params=pltpu.CompilerParams(
            dimension_semantics=("arbitrary",), vmem_limit_bytes=VMEM_LIMIT_BYTES),
        name="sample_out",
    )(xs, z, o, att, hist, conv_w, gn_w, conv_b, ln_w, ln_b,
      w_br_ret_bf, w_br_conv_bf, w_br_mem_bf, w_out_bf, final_w)


def _rotary_tables(pos):
    half = HEAD_DIM // 2
    inv = ROPE_BASE ** (-np.arange(half, dtype=np.float64) / half)
    ang = np.asarray(pos, np.float64)[:, None] * inv[None, :]
    cos, sin = np.cos(ang), np.sin(ang)
    table = lambda a, b: np.concatenate([a, b], axis=-1).astype(np.float32)
    return table(cos, cos), table(-sin, sin)


def _decay_tables():
    c = RET_CHUNK
    log_g = np.log1p(-np.exp2(-5.0 - np.arange(RET_HEADS, dtype=np.float64)))
    idx = np.arange(c, dtype=np.float64)
    diff = idx[:, None] - idx[None, :]
    intra = np.where(diff >= 0, np.exp(np.maximum(diff, 0.0)[None] * log_g[:, None, None]), 0.0)
    q_dec = np.exp((idx + 1.0)[None, :] * log_g[:, None])
    k_dec = np.exp((c - 1.0 - idx)[None, :] * log_g[:, None])
    bcast = lambda a: np.ascontiguousarray(
        np.broadcast_to(a[:, :, None], (RET_HEADS, c, HEAD_DIM))).astype(np.float32)
    return intra.astype(np.float32), bcast(q_dec), bcast(k_dec)


def _gamma_powers(n):
    return tuple(float(np.exp(np.log1p(-np.exp2(-5.0 - h)) * n)) for h in range(RET_HEADS))


def kernel(x_prompt, x_sample, mem_prompt, state_ret, state_conv, cache_mem_k, cache_mem_v,
           norm_w, w_in, ret_gn_w, conv_w, conv_b, conv_ln_w, conv_ln_b, mem_norm_w,
           w_mem_kv, w_br_ret, w_br_conv, w_br_mem, w_out, final_norm_w):
    depth = w_in.shape[0]
    assert depth == 1, "single-layer step"
    batch, seq, _ = x_prompt.shape
    n_dec, dec_seq, _ = x_sample.shape
    assert dec_seq == 1 and seq % PROMPT_TILE == 0
    n_mem = mem_prompt.shape[1]

    final_w = final_norm_w[None, :]

    intra, qdec, kdec = _decay_tables()
    cos_p, sin_p = _rotary_tables(np.arange(seq))
    cos_s, sin_s = _rotary_tables(np.arange(dec_seq) + PAST_LEN)

    z_s, w_in_bf = _sample_proj(x_sample, norm_w, w_in)

    mk, mv, mk_bf, mv_bf, w_br_ret_bf, w_br_conv_bf, w_br_mem_bf, w_out_bf = _mem_kv(
        mem_prompt, mem_norm_w, w_mem_kv, (w_br_ret, w_br_conv, w_br_mem, w_out))

    y_prompt, s_prompt, hist_prompt, o_s, att_s, s_sample = _prompt_layer(
        x_prompt, cos_p, sin_p, mk_bf, mv_bf, norm_w, w_in_bf, ret_gn_w, conv_w, conv_b,
        conv_ln_w, conv_ln_b, w_br_ret_bf, w_br_conv_bf, w_br_mem_bf, w_out_bf, final_w,
        intra, qdec, kdec, _gamma_powers(RET_CHUNK),
        z_s, cos_s, sin_s, state_ret[0],
        cache_mem_k.reshape(n_dec, n_mem * MEM_HEADS, HEAD_DIM),
        cache_mem_v.reshape(n_dec, n_mem * MEM_HEADS, HEAD_DIM), _gamma_powers(1))

    y_sample, hist_sample = _sample_out(
        x_sample, z_s, o_s, att_s,
        jnp.transpose(state_conv[0], (1, 0, 2)), conv_w, ret_gn_w, conv_b, conv_ln_w, conv_ln_b,
        w_br_ret_bf, w_br_conv_bf, w_br_mem_bf, w_out_bf, final_w)

    heads = lambda a: a.reshape(1, batch, n_mem, MEM_HEADS, HEAD_DIM)
    return (y_prompt, y_sample, s_prompt[None], s_sample[None],
            jnp.transpose(hist_prompt, (1, 0, 2))[None],
            jnp.transpose(hist_sample, (1, 0, 2))[None], heads(mk), heads(mv))
```

```python
import functools

import numpy as np
import jax
import jax.numpy as jnp
from jax import lax
from jax.experimental import pallas as pl
from jax.experimental.pallas import tpu as pltpu

F32 = jnp.float32
BF16 = jnp.bfloat16

D_MODEL = 1024
RET_HEADS = 4
HEAD_DIM = 128
SUBLANES = 8
BRANCH_W = 512
RET_CHUNK = 128
ROPE_BASE = 10000.0
CONV_K = 31
CONV_HIST = CONV_K - 1
MEM_HEADS = 4
N_BRANCH = 3
EPS = 1e-6
PAST_LEN = 16384

OFF_QR, OFF_KR, OFF_VR, OFF_GR = 0, 512, 1024, 1536
OFF_AC, OFF_BC, OFF_GC = 2048, 2560, 3072
OFF_QM, OFF_GM = 3584, 4096
OFF_MERGE = 4608
IN_W = OFF_MERGE + N_BRANCH * D_MODEL

V7X_VMEM_BYTES = 64 * 1024 * 1024
VMEM_LIMIT_BYTES = V7X_VMEM_BYTES - 8 * 1024 * 1024

PROMPT_TILE = 256
HIST_PAD = 32


def _sigmoid(x):
    return 0.5 * jnp.tanh(0.5 * x) + 0.5


def _silu(x):
    hx = 0.5 * x
    return hx * (jnp.tanh(hx) + 1.0)


def _rms_scale(x):
    return lax.rsqrt(jnp.mean(x * x, axis=-1, keepdims=True) + EPS)


def _layernorm_rows(x):
    mu = jnp.mean(x, axis=-1, keepdims=True)
    xc = x - mu
    var = jnp.mean(xc * xc, axis=-1, keepdims=True)
    return xc * lax.rsqrt(var + EPS)


def _rotary(x, cos2, sin2):
    return x * cos2 + pltpu.roll(x, HEAD_DIM // 2, axis=1) * sin2


def _dot(a, b):
    return jnp.dot(a, b, preferred_element_type=F32)


def _dot_nt(a, b):
    return lax.dot_general(a, b, (((1,), (1,)), ((), ())), preferred_element_type=F32)


def _dot_tn(a, b):
    return lax.dot_general(a, b, (((0,), (0,)), ((), ())), preferred_element_type=F32)


def _head(x, h):
    return x[:, h * HEAD_DIM:(h + 1) * HEAD_DIM]


def _group_norm_heads(o, gn_w):
    parts = [_layernorm_rows(_head(o, h)) for h in range(RET_HEADS)]
    return jnp.concatenate(parts, axis=-1) * gn_w


def _merge_and_project(x, gate_pre, yr, yc, ym, w_br_ret_ref, w_br_conv_ref,
                       w_br_mem_ref, w_out_ref, final_w):
    branches = (
        _dot(yr.astype(BF16), w_br_ret_ref[...]),
        _dot(yc.astype(BF16), w_br_conv_ref[...]),
        _dot(ym.astype(BF16), w_br_mem_ref[...]),
    )
    merged = None
    for i, br in enumerate(branches):
        term = _sigmoid(gate_pre(i)) * br
        merged = term if merged is None else merged + term
    out = x + _dot(merged.astype(BF16), w_out_ref[...])
    return out * _rms_scale(out) * final_w


MEM_KV_BLOCK = 2


def _mem_kv_kernel(n_cast, mem_ref, norm_w_ref, w_ref, *refs):
    cast_in, (k_ref, v_ref, kbf_ref, vbf_ref), cast_out = (
        refs[:n_cast], refs[n_cast:n_cast + 4], refs[n_cast + 4:])

    for src, dst in zip(cast_in, cast_out):
        dst[...] = src[0].astype(BF16)

    nb, n_mem, _ = mem_ref.shape
    m = mem_ref[...].reshape(nb * n_mem, D_MODEL)
    hm = (m * _rms_scale(m) * norm_w_ref[...]).astype(BF16)
    kv = _dot(hm, w_ref[0].astype(BF16))
    for b in range(nb):
        k = kv[b * n_mem:(b + 1) * n_mem, :BRANCH_W]
        v = kv[b * n_mem:(b + 1) * n_mem, BRANCH_W:]
        for h in range(MEM_HEADS):
            k_ref[b, pl.ds(h, n_mem, stride=MEM_HEADS), :] = _head(k, h)
            v_ref[b, pl.ds(h, n_mem, stride=MEM_HEADS), :] = _head(v, h)
        kbf_ref[b] = k.astype(BF16)
        vbf_ref[b] = v.astype(BF16)


def _mem_kv(mem, mem_norm_w, w_mem_kv, to_cast):
    batch, n_mem, _ = mem.shape
    nb = MEM_KV_BLOCK
    steps = batch // nb
    blk = lambda dt: jax.ShapeDtypeStruct((batch, n_mem, BRANCH_W), dt)
    out_spec = pl.BlockSpec((nb, n_mem, BRANCH_W), lambda b: (b, 0, 0))
    rows_shape = (batch, n_mem * MEM_HEADS, HEAD_DIM)
    rows_spec = pl.BlockSpec((nb, n_mem * MEM_HEADS, HEAD_DIM), lambda b: (b, 0, 0))
    return pl.pallas_call(
        functools.partial(_mem_kv_kernel, len(to_cast)),
        grid=(steps,),
        in_specs=[
            pl.BlockSpec((nb, n_mem, D_MODEL), lambda b: (b, 0, 0)),
            pl.BlockSpec((1, D_MODEL), lambda b: (0, 0)),
            pl.BlockSpec((1, D_MODEL, 2 * BRANCH_W), lambda b: (0, 0, 0)),
        ] + [pl.BlockSpec((1, w.shape[1] // steps, w.shape[2]), lambda b: (0, b, 0))
             for w in to_cast],
        out_specs=[rows_spec, rows_spec, out_spec, out_spec]
                  + [pl.BlockSpec((w.shape[1] // steps, w.shape[2]), lambda b: (b, 0))
                     for w in to_cast],
        out_shape=[jax.ShapeDtypeStruct(rows_shape, F32), jax.ShapeDtypeStruct(rows_shape, F32),
                   blk(BF16), blk(BF16)]
                  + [jax.ShapeDtypeStruct(w.shape[1:], BF16) for w in to_cast],
        compiler_params=pltpu.CompilerParams(
            dimension_semantics=("arbitrary",), vmem_limit_bytes=VMEM_LIMIT_BYTES),
        name="mem_kv",
    )(mem, mem_norm_w, w_mem_kv, *to_cast)


def _prompt_kernel(c_dec, gamma, tiles_per_seq, n_tiles,
                   x_ref, xb_ref, cos_ref, sin_ref, mk_ref, mv_ref, norm_w_ref, w_in_ref,
                   gn_w_ref, conv_w_ref, conv_b_ref, ln_w_ref, ln_b_ref,
                   w_br_ret_ref, w_br_conv_ref, w_br_mem_ref, w_out_ref, final_w_ref,
                   intra_ref, qdec_ref, kdec_ref,
                   zs_ref, cos_s_ref, sin_s_ref, s0_ref, ck_ref, cv_ref,
                   y_ref, s_ref, hist_ref, os_ref, atts_ref, s_new_ref,
                   ubuf_ref, h_scr, yr_scr, yc_scr, ym_scr, os_scr, atts_scr):
    i = pl.program_id(0)
    t = lax.rem(jnp.minimum(i, n_tiles - 1), tiles_per_seq)
    live = i < n_tiles
    tile = x_ref.shape[1]

    @pl.when(i == 0)
    def _():
        h_scr[...] = jnp.zeros_like(h_scr)
        yr_scr[...] = jnp.zeros_like(yr_scr)
        yc_scr[...] = jnp.zeros_like(yc_scr)
        ym_scr[...] = jnp.zeros_like(ym_scr)
        os_scr[...] = jnp.zeros_like(os_scr)
        atts_scr[...] = jnp.zeros_like(atts_scr)
        hist_ref[...] = jnp.zeros_like(hist_ref)

    @pl.when(jnp.logical_and(t == 0, live))
    def _():
        s_ref[...] = jnp.zeros_like(s_ref)
        ubuf_ref[0:HIST_PAD, :] = jnp.zeros((HIST_PAD, BRANCH_W), F32)

    h_prev = h_scr[...]
    half = D_MODEL // 2

    def gate_job(k):
        lo = OFF_MERGE + k * half
        return lambda: _dot(h_prev, w_in_ref[:, lo:lo + half])

    back_jobs = [gate_job(k) for k in range(2 * N_BRANCH)] + [
        lambda: _dot(yr_scr[...], w_br_ret_ref[...]),
        lambda: _dot(yc_scr[...], w_br_conv_ref[...]),
        lambda: _dot(ym_scr[...], w_br_mem_ref[...]),
    ]
    back_out = [None] * len(back_jobs)

    def run_back(k):
        back_out[k] = back_jobs[k]()

    x = x_ref[0]
    h_bf = (x * _rms_scale(x) * norm_w_ref[...]).astype(BF16)
    run_back(8)

    def proj(off, width=BRANCH_W):
        return _dot(h_bf, w_in_ref[:, off:off + width])

    za = proj(OFF_AC)
    zb = proj(OFF_BC)
    zq = proj(OFF_QR)
    zk = proj(OFF_KR)
    v_bf = proj(OFF_VR).astype(BF16)
    u = za * _sigmoid(zb)
    ubuf_ref[HIST_PAD:HIST_PAD + tile, :] = u
    first = HIST_PAD - CONV_HIST
    conv = None
    for r in range(8):
        part = None
        for j in range(CONV_K):
            if (first + j) % 8 != r:
                continue
            term = ubuf_ref[first + j:first + j + tile, :] * conv_w_ref[0, j:j + 1, :]
            part = term if part is None else part + term
        conv = part if conv is None else conv + part
        run_back(r)
    conv = conv + conv_b_ref[...]
    ubuf_ref[0:HIST_PAD, :] = jnp.where(live, ubuf_ref[tile:tile + HIST_PAD, :],
                                        ubuf_ref[0:HIST_PAD, :])

    z_gc = proj(OFF_GC)
    z_gr = proj(OFF_GR)
    z_qm = proj(OFF_QM)
    z_gm = proj(OFF_GM)

    merged = []
    for k in range(2):
        cols = slice(k * half, (k + 1) * half)
        acc = None
        for b in range(N_BRANCH):
            term = _sigmoid(back_out[2 * b + k]) * back_out[2 * N_BRANCH + b][:, cols]
            acc = term if acc is None else acc + term
        merged.append(acc.astype(BF16))
    merged = jnp.concatenate(merged, axis=-1)

    cos2 = cos_ref[...]
    sin2 = sin_ref[...]
    q_scale = HEAD_DIM ** -0.5
    def block_diag(a, b):
        zero = jnp.zeros_like(a)
        return jnp.concatenate([jnp.concatenate([a, zero], axis=-1),
                                jnp.concatenate([zero, b], axis=-1)], axis=0)

    o_heads = []
    for h0 in range(0, RET_HEADS, 2):
        pair = (h0, h0 + 1)
        q_p = [_rotary(_head(zq, h), cos2, sin2) * q_scale for h in pair]
        k_p = [_rotary(_head(zk, h), cos2, sin2) for h in pair]
        v_p = [_head(v_bf, h) for h in pair]
        s_old = [s_ref[0, h] for h in pair]
        s = list(s_old)
        intra2 = jnp.concatenate([intra_ref[h] for h in pair], axis=-1)
        qdec2 = jnp.concatenate([qdec_ref[h] for h in pair], axis=-1)
        o_chunks = []
        for c in range(tile // RET_CHUNK):
            rows = slice(c * RET_CHUNK, (c + 1) * RET_CHUNK)
            kc = [k[rows] for k in k_p]
            vc = [v[rows] for v in v_p]
            qc2 = jnp.concatenate([q[rows] for q in q_p], axis=-1).astype(BF16)
            sc2 = _dot_nt(qc2, block_diag(*[k.astype(BF16) for k in kc])) * intra2
            o_chunks.append(_dot(sc2.astype(BF16), block_diag(*vc))
                            + _dot(qc2, block_diag(*[x.astype(BF16) for x in s])) * qdec2)
            for n, h in enumerate(pair):
                kd = (kc[n] * kdec_ref[h]).astype(BF16)
                s[n] = s[n] * c_dec[h] + _dot_tn(kd, vc[n])
        o_pair = jnp.concatenate(o_chunks, axis=0)
        for n, h in enumerate(pair):
            s_ref[0, h] = jnp.where(live, s[n], s_old[n])
            o_heads.append(_layernorm_rows(_head(o_pair, n)))

    out = xb_ref[0] + _dot(merged, w_out_ref[...])

    ret = jnp.concatenate(o_heads, axis=-1) * gn_w_ref[...]
    yr = _silu(z_gr) * ret
    cn = _layernorm_rows(conv) * ln_w_ref[...] + ln_b_ref[...]
    yc = _silu(z_gc) * _silu(cn)

    qm = z_qm.astype(BF16)
    mk = mk_ref[0]
    mv = mv_ref[0]
    att = []
    for h in range(MEM_HEADS):
        sc = _dot_nt(_head(qm, h), _head(mk, h)) * (HEAD_DIM ** -0.5)
        e = jnp.exp(sc - jnp.max(sc, axis=-1, keepdims=True))
        p = e / jnp.sum(e, axis=-1, keepdims=True)
        att.append(_dot(p.astype(BF16), _head(mv, h)))
    ym = _silu(z_gm) * jnp.concatenate(att, axis=-1)

    y_ref[0] = out * _rms_scale(out) * final_w_ref[...]

    h_scr[...] = h_bf
    yr_scr[...] = yr.astype(BF16)
    yc_scr[...] = yc.astype(BF16)
    ym_scr[...] = ym.astype(BF16)

    @pl.when(jnp.logical_and(t == tiles_per_seq - 1, live))
    def _():
        seq_id = jnp.minimum(i, n_tiles - 1) // tiles_per_seq
        mine = lax.broadcasted_iota(jnp.int32, (hist_ref.shape[1], 1), 0) == seq_id
        for j in range(CONV_HIST):
            hist_ref[j] = jnp.where(mine, ubuf_ref[first + j:first + j + 1, :], hist_ref[j])

    per_step = s0_ref.shape[0]
    r0 = lax.rem(jnp.minimum(i, n_tiles - 1), SUBLANES // per_step) * per_step
    row_id = lax.broadcasted_iota(jnp.int32, (SUBLANES, 1), 0)

    def own_rows(off):
        block = zs_ref[:, off:off + BRANCH_W]
        return jnp.concatenate(
            [jnp.sum(jnp.where(row_id == r0 + b, block, 0.0), axis=0, keepdims=True)
             for b in range(per_step)], axis=0)

    o_rows, att_rows = _sample_sequences(
        gamma, own_rows(OFF_QR), own_rows(OFF_KR), own_rows(OFF_VR), own_rows(OFF_QM),
        cos_s_ref[...], sin_s_ref[...], s0_ref, ck_ref, cv_ref, s_new_ref)
    for scr, out_ref, rows in ((os_scr, os_ref, o_rows), (atts_scr, atts_ref, att_rows)):
        block = scr[...]
        for b in range(per_step):
            mine = jnp.logical_and(row_id == r0 + b, live)
            block = jnp.where(mine, rows[b], block)
        scr[...] = block
        out_ref[...] = block


def _const_spec(shape, single_buffer=False):
    zeros = (0,) * len(shape)
    if single_buffer:
        return pl.BlockSpec(shape, lambda i: zeros, pipeline_mode=pl.Buffered(1))
    return pl.BlockSpec(shape, lambda i: zeros)


def _prompt_layer(x, cos2, sin2, mk_bf, mv_bf, norm_w, w_in_bf, gn_w, conv_w, conv_b, ln_w,
                  ln_b, w_br_ret_bf, w_br_conv_bf, w_br_mem_bf, w_out_bf, final_w,
                  intra, qdec, kdec, c_dec,
                  z_s, cos_s, sin_s, s0, cache_k, cache_v, gamma):
    batch, seq, _ = x.shape
    n_mem = mk_bf.shape[1]
    tile = PROMPT_TILE
    tiles_per_seq = seq // tile
    n_tiles = batch * tiles_per_seq

    def front(i):
        j = jnp.minimum(i, n_tiles - 1)
        return j // tiles_per_seq, lax.rem(j, tiles_per_seq)

    def back(i):
        j = jnp.maximum(i - 1, 0)
        return j // tiles_per_seq, lax.rem(j, tiles_per_seq)

    row = lambda w: _const_spec((1, w))
    weight = lambda a: _const_spec(a.shape, single_buffer=True)
    table = _const_spec((RET_HEADS, RET_CHUNK, RET_CHUNK))
    n_dec = z_s.shape[0]
    per_step = n_dec // n_tiles
    assert per_step * n_tiles == n_dec
    share = lambda shape: pl.BlockSpec(
        shape, lambda i: (jnp.minimum(i, n_tiles - 1),) + (0,) * (len(shape) - 1))
    assert SUBLANES % per_step == 0
    rows8 = lambda w: pl.BlockSpec(
        (SUBLANES, w), lambda i: (jnp.minimum(i, n_tiles - 1) // (SUBLANES // per_step), 0))
    x_tile = lambda which: pl.BlockSpec((1, tile, D_MODEL), lambda i: (*which(i), 0))
    per_seq = lambda shape: pl.BlockSpec(
        (1,) + shape, lambda i: (front(i)[0],) + (0,) * len(shape))
    return pl.pallas_call(
        functools.partial(_prompt_kernel, c_dec, gamma, tiles_per_seq, n_tiles),
        grid=(n_tiles + 1,),
        in_specs=[
            x_tile(front), x_tile(back),
            pl.BlockSpec((tile, HEAD_DIM), lambda i: (front(i)[1], 0)),
            pl.BlockSpec((tile, HEAD_DIM), lambda i: (front(i)[1], 0)),
            per_seq((n_mem, BRANCH_W)), per_seq((n_mem, BRANCH_W)),
            row(D_MODEL), weight(w_in_bf), row(BRANCH_W),
            _const_spec((1, CONV_K, BRANCH_W)), row(BRANCH_W), row(BRANCH_W), row(BRANCH_W),
            weight(w_br_ret_bf), weight(w_br_conv_bf), weight(w_br_mem_bf), weight(w_out_bf),
            row(D_MODEL), table, table, table,
            rows8(IN_W), row(HEAD_DIM), row(HEAD_DIM),
            share((per_step,) + s0.shape[1:]),
            share((per_step,) + cache_k.shape[1:]), share((per_step,) + cache_v.shape[1:]),
        ],
        out_specs=[
            x_tile(back),
            per_seq((RET_HEADS, HEAD_DIM, HEAD_DIM)),
            _const_spec((CONV_HIST, batch, BRANCH_W)),
            rows8(BRANCH_W), rows8(BRANCH_W),
            share((per_step,) + s0.shape[1:]),
        ],
        out_shape=[
            jax.ShapeDtypeStruct((batch, seq, D_MODEL), F32),
            jax.ShapeDtypeStruct((batch, RET_HEADS, HEAD_DIM, HEAD_DIM), F32),
            jax.ShapeDtypeStruct((CONV_HIST, batch, BRANCH_W), F32),
            jax.ShapeDtypeStruct((n_dec, BRANCH_W), F32),
            jax.ShapeDtypeStruct((n_dec, BRANCH_W), F32),
            jax.ShapeDtypeStruct(s0.shape, F32),
        ],
        scratch_shapes=[
            pltpu.VMEM((HIST_PAD + tile, BRANCH_W), F32),
            pltpu.VMEM((tile, D_MODEL), BF16),
            pltpu.VMEM((tile, BRANCH_W), BF16),
            pltpu.VMEM((tile, BRANCH_W), BF16),
            pltpu.VMEM((tile, BRANCH_W), BF16),
            pltpu.VMEM((SUBLANES, BRANCH_W), F32),
            pltpu.VMEM((SUBLANES, BRANCH_W), F32),
        ],
        compiler_params=pltpu.CompilerParams(
            dimension_semantics=("arbitrary",), vmem_limit_bytes=VMEM_LIMIT_BYTES),
        name="prompt_layer",
    )(x, x, cos2, sin2, mk_bf, mv_bf, norm_w, w_in_bf, gn_w, conv_w, conv_b, ln_w, ln_b,
      w_br_ret_bf, w_br_conv_bf, w_br_mem_bf, w_out_bf, final_w, intra, qdec, kdec,
      z_s, cos_s, sin_s, s0, cache_k, cache_v)


SAMPLE_PROJ_BLOCK = 1536


def _sample_proj_kernel(x_ref, norm_w_ref, w_ref, z_ref, w_bf_ref):
    x = x_ref[:, 0, :]
    h_bf = (x * _rms_scale(x) * norm_w_ref[...]).astype(BF16)
    w_bf = w_ref[0].astype(BF16)
    w_bf_ref[...] = w_bf
    z_ref[...] = _dot(h_bf, w_bf)


def _sample_proj(xs, norm_w, w_in):
    n = xs.shape[0]
    return pl.pallas_call(
        _sample_proj_kernel,
        grid=(IN_W // SAMPLE_PROJ_BLOCK,),
        in_specs=[
            pl.BlockSpec((n, 1, D_MODEL), lambda j: (0, 0, 0)),
            pl.BlockSpec((1, D_MODEL), lambda j: (0, 0)),
            pl.BlockSpec((1, D_MODEL, SAMPLE_PROJ_BLOCK), lambda j: (0, 0, j)),
        ],
        out_specs=[pl.BlockSpec((n, SAMPLE_PROJ_BLOCK), lambda j: (0, j)),
                   pl.BlockSpec((D_MODEL, SAMPLE_PROJ_BLOCK), lambda j: (0, j))],
        out_shape=[jax.ShapeDtypeStruct((n, IN_W), F32),
                   jax.ShapeDtypeStruct((D_MODEL, IN_W), BF16)],
        compiler_params=pltpu.CompilerParams(
            dimension_semantics=("arbitrary",), vmem_limit_bytes=VMEM_LIMIT_BYTES),
        name="sample_proj",
    )(xs, norm_w, w_in)


def _sample_sequences(gamma, zq, zk, v, qm, cos2, sin2, s0_ref, ck_ref, cv_ref, s_new_ref):
    n = zq.shape[0]
    n_mem = ck_ref.shape[1] // MEM_HEADS
    q_scale = HEAD_DIM ** -0.5
    mem_scale = HEAD_DIM ** -0.5
    q = [_rotary(_head(zq, h), cos2, sin2) * q_scale for h in range(RET_HEADS)]
    k = [_rotary(_head(zk, h), cos2, sin2) for h in range(RET_HEADS)]

    eye = (lax.broadcasted_iota(jnp.int32, (HEAD_DIM, HEAD_DIM), 0)
           == lax.broadcasted_iota(jnp.int32, (HEAD_DIM, HEAD_DIM), 1))
    ones_bf = jnp.ones((HEAD_DIM, HEAD_DIM), BF16)

    def as_columns(row):
        diag = jnp.where(eye, jnp.broadcast_to(row, (HEAD_DIM, HEAD_DIM)), 0.0)
        return _dot(diag.astype(BF16), ones_bf)

    o_rows, att_rows = [], []
    for b in range(n):
        o_heads = []
        for h in range(RET_HEADS):
            q_row = q[h][b:b + 1]
            k_row = k[h][b:b + 1]
            v_row = _head(v, h)[b:b + 1]
            s0 = s0_ref[b, h]
            qk = jnp.sum(q_row * k_row, axis=-1, keepdims=True)
            q_lhs = jnp.broadcast_to(q_row, (SUBLANES, HEAD_DIM)).astype(BF16)
            qs = _dot(q_lhs, s0.astype(BF16))[0:1]
            o_heads.append(qk * v_row + gamma[h] * qs)
            s_new_ref[b, h] = gamma[h] * s0 + as_columns(k_row) * v_row
        o_rows.append(jnp.concatenate(o_heads, axis=-1))

        att_heads = []
        for h in range(MEM_HEADS):
            head_rows = pl.ds(h, n_mem, stride=MEM_HEADS)
            prod = ck_ref[b, head_rows, :] * (_head(qm, h)[b:b + 1] * mem_scale)
            sc = jnp.sum(prod, axis=-1, keepdims=True)
            e = jnp.exp(sc - jnp.max(sc, axis=0, keepdims=True))
            weighted = jnp.sum(e * cv_ref[b, head_rows, :], axis=0, keepdims=True)
            att_heads.append(weighted / jnp.sum(e, axis=0, keepdims=True))
        att_rows.append(jnp.concatenate(att_heads, axis=-1))
    return o_rows, att_rows


def _sample_out_kernel(x_ref, z_ref, o_ref, att_ref, hist_ref, conv_w_ref, gn_w_ref, conv_b_ref,
                       ln_w_ref, ln_b_ref, w_br_ret_ref, w_br_conv_ref, w_br_mem_ref, w_out_ref,
                       final_w_ref, y_ref, hist_new_ref):
    gate = lambda off: _silu(z_ref[:, off:off + BRANCH_W])
    yr = gate(OFF_GR) * _group_norm_heads(o_ref[...], gn_w_ref[...])

    u = z_ref[:, OFF_AC:OFF_AC + BRANCH_W] * _sigmoid(z_ref[:, OFF_BC:OFF_BC + BRANCH_W])
    conv = u * conv_w_ref[0, CONV_HIST:CONV_K, :] + conv_b_ref[...]
    for j in range(CONV_HIST):
        conv = conv + hist_ref[j] * conv_w_ref[0, j:j + 1, :]
        hist_new_ref[j] = hist_ref[j + 1] if j + 1 < CONV_HIST else u
    cn = _layernorm_rows(conv) * ln_w_ref[...] + ln_b_ref[...]
    yc = gate(OFF_GC) * _silu(cn)

    ym = gate(OFF_GM) * att_ref[...]
    gate_pre = lambda i: z_ref[:, OFF_MERGE + i * D_MODEL:OFF_MERGE + (i + 1) * D_MODEL]
    y_ref[:, 0, :] = _merge_and_project(x_ref[:, 0, :], gate_pre, yr, yc, ym, w_br_ret_ref,
                                        w_br_conv_ref, w_br_mem_ref, w_out_ref, final_w_ref[...])


SAMPLE_OUT_BLOCK = 32


def _sample_out(xs, z, o, att, hist, conv_w, gn_w, conv_b, ln_w, ln_b,
                w_br_ret_bf, w_br_conv_bf, w_br_mem_bf, w_out_bf, final_w):
    n = xs.shape[0]
    nb = SAMPLE_OUT_BLOCK
    rows = lambda w: pl.BlockSpec((nb, w), lambda i: (i, 0))
    hist_spec = pl.BlockSpec((CONV_HIST, nb, BRANCH_W), lambda i: (0, i, 0))
    tokens = pl.BlockSpec((nb, 1, D_MODEL), lambda i: (i, 0, 0))
    const = lambda a: pl.BlockSpec(a.shape, lambda i: (0,) * a.ndim)
    weight = lambda a: pl.BlockSpec(a.shape, lambda i: (0,) * a.ndim,
                                    pipeline_mode=pl.Buffered(1))
    return pl.pallas_call(
        _sample_out_kernel,
        grid=(n // nb,),
        in_specs=[tokens, rows(IN_W), rows(BRANCH_W), rows(BRANCH_W), hist_spec,
                  const(conv_w), const(gn_w), const(conv_b), const(ln_w), const(ln_b),
                  weight(w_br_ret_bf), weight(w_br_conv_bf), weight(w_br_mem_bf),
                  weight(w_out_bf), const(final_w)],
        out_specs=[tokens, hist_spec],
        out_shape=[jax.ShapeDtypeStruct(xs.shape, F32), jax.ShapeDtypeStruct(hist.shape, F32)],
        compiler_params=pltpu.CompilerParams(
            dimension_semantics=("arbitrary",), vmem_limit_bytes=VMEM_LIMIT_BYTES),
        name="sample_out",
    )(xs, z, o, att, hist, conv_w, gn_w, conv_b, ln_w, ln_b,
      w_br_ret_bf, w_br_conv_bf, w_br_mem_bf, w_out_bf, final_w)


def _rotary_tables(pos):
    half = HEAD_DIM // 2
    inv = ROPE_BASE ** (-np.arange(half, dtype=np.float64) / half)
    ang = np.asarray(pos, np.float64)[:, None] * inv[None, :]
    cos, sin = np.cos(ang), np.sin(ang)
    table = lambda a, b: np.concatenate([a, b], axis=-1).astype(np.float32)
    return table(cos, cos), table(-sin, sin)


def _decay_tables():
    c = RET_CHUNK
    log_g = np.log1p(-np.exp2(-5.0 - np.arange(RET_HEADS, dtype=np.float64)))
    idx = np.arange(c, dtype=np.float64)
    diff = idx[:, None] - idx[None, :]
    intra = np.where(diff >= 0, np.exp(np.maximum(diff, 0.0)[None] * log_g[:, None, None]), 0.0)
    q_dec = np.exp((idx + 1.0)[None, :] * log_g[:, None])
    k_dec = np.exp((c - 1.0 - idx)[None, :] * log_g[:, None])
    bcast = lambda a: np.ascontiguousarray(
        np.broadcast_to(a[:, :, None], (RET_HEADS, c, HEAD_DIM))).astype(np.float32)
    return intra.astype(np.float32), bcast(q_dec), bcast(k_dec)


def _gamma_powers(n):
    return tuple(float(np.exp(np.log1p(-np.exp2(-5.0 - h)) * n)) for h in range(RET_HEADS))


def kernel(x_prompt, x_sample, mem_prompt, state_ret, state_conv, cache_mem_k, cache_mem_v,
           norm_w, w_in, ret_gn_w, conv_w, conv_b, conv_ln_w, conv_ln_b, mem_norm_w,
           w_mem_kv, w_br_ret, w_br_conv, w_br_mem, w_out, final_norm_w):
    depth = w_in.shape[0]
    assert depth == 1, "single-layer step"
    batch, seq, _ = x_prompt.shape
    n_dec, dec_seq, _ = x_sample.shape
    assert dec_seq == 1 and seq % PROMPT_TILE == 0
    n_mem = mem_prompt.shape[1]

    final_w = final_norm_w[None, :]

    intra, qdec, kdec = _decay_tables()
    cos_p, sin_p = _rotary_tables(np.arange(seq))
    cos_s, sin_s = _rotary_tables(np.arange(dec_seq) + PAST_LEN)

    z_s, w_in_bf = _sample_proj(x_sample, norm_w, w_in)

    mk, mv, mk_bf, mv_bf, w_br_ret_bf, w_br_conv_bf, w_br_mem_bf, w_out_bf = _mem_kv(
        mem_prompt, mem_norm_w, w_mem_kv, (w_br_ret, w_br_conv, w_br_mem, w_out))

    y_prompt, s_prompt, hist_prompt, o_s, att_s, s_sample = _prompt_layer(
        x_prompt, cos_p, sin_p, mk_bf, mv_bf, norm_w, w_in_bf, ret_gn_w, conv_w, conv_b,
        conv_ln_w, conv_ln_b, w_br_ret_bf, w_br_conv_bf, w_br_mem_bf, w_out_bf, final_w,
        intra, qdec, kdec, _gamma_powers(RET_CHUNK),
        z_s, cos_s, sin_s, state_ret[0],
        cache_mem_k.reshape(n_dec, n_mem * MEM_HEADS, HEAD_DIM),
        cache_mem_v.reshape(n_dec, n_mem * MEM_HEADS, HEAD_DIM), _gamma_powers(1))

    y_sample, hist_sample = _sample_out(
        x_sample, z_s, o_s, att_s,
        jnp.transpose(state_conv[0], (1, 0, 2)), conv_w, ret_gn_w, conv_b, conv_ln_w, conv_ln_b,
        w_br_ret_bf, w_br_conv_bf, w_br_mem_bf, w_out_bf, final_w)

    heads = lambda a: a.reshape(1, batch, n_mem, MEM_HEADS, HEAD_DIM)
    return (y_prompt, y_sample, s_prompt[None], s_sample[None],
            jnp.transpose(hist_prompt, (1, 0, 2))[None],
            jnp.transpose(hist_sample, (1, 0, 2))[None], heads(mk), heads(mv))
```

```python
import functools

import numpy as np
import jax
import jax.numpy as jnp
from jax import lax
from jax.experimental import pallas as pl
from jax.experimental.pallas import tpu as pltpu

F32 = jnp.float32
BF16 = jnp.bfloat16

D_MODEL = 1024
RET_HEADS = 4
HEAD_DIM = 128
SUBLANES = 8
BRANCH_W = 512
RET_CHUNK = 128
ROPE_BASE = 10000.0
CONV_K = 31
CONV_HIST = CONV_K - 1
MEM_HEADS = 4
N_BRANCH = 3
EPS = 1e-6
PAST_LEN = 16384

OFF_QR, OFF_KR, OFF_VR, OFF_GR = 0, 512, 1024, 1536
OFF_AC, OFF_BC, OFF_GC = 2048, 2560, 3072
OFF_QM, OFF_GM = 3584, 4096
OFF_MERGE = 4608
IN_W = OFF_MERGE + N_BRANCH * D_MODEL

V7X_VMEM_BYTES = 64 * 1024 * 1024
VMEM_LIMIT_BYTES = V7X_VMEM_BYTES - 8 * 1024 * 1024

PROMPT_TILE = 256
HIST_PAD = 32


def _sigmoid(x):
    return 0.5 * jnp.tanh(0.5 * x) + 0.5


def _silu(x):
    hx = 0.5 * x
    return hx * (jnp.tanh(hx) + 1.0)


def _rms_scale(x):
    return lax.rsqrt(jnp.mean(x * x, axis=-1, keepdims=True) + EPS)


def _layernorm_rows(x):
    mu = jnp.mean(x, axis=-1, keepdims=True)
    xc = x - mu
    var = jnp.mean(xc * xc, axis=-1, keepdims=True)
    return xc * lax.rsqrt(var + EPS)


def _rotary(x, cos2, sin2):
    return x * cos2 + pltpu.roll(x, HEAD_DIM // 2, axis=1) * sin2


def _dot(a, b):
    return jnp.dot(a, b, preferred_element_type=F32)


def _dot_nt(a, b):
    return lax.dot_general(a, b, (((1,), (1,)), ((), ())), preferred_element_type=F32)


def _dot_tn(a, b):
    return lax.dot_general(a, b, (((0,), (0,)), ((), ())), preferred_element_type=F32)


def _head(x, h):
    return x[:, h * HEAD_DIM:(h + 1) * HEAD_DIM]


def _group_norm_heads(o, gn_w):
    parts = [_layernorm_rows(_head(o, h)) for h in range(RET_HEADS)]
    return jnp.concatenate(parts, axis=-1) * gn_w


def _merge_and_project(x, gate_pre, yr, yc, ym, w_br_ret_ref, w_br_conv_ref,
                       w_br_mem_ref, w_out_ref, final_w):
    branches = (
        _dot(yr.astype(BF16), w_br_ret_ref[...]),
        _dot(yc.astype(BF16), w_br_conv_ref[...]),
        _dot(ym.astype(BF16), w_br_mem_ref[...]),
    )
    merged = None
    for i, br in enumerate(branches):
        term = _sigmoid(gate_pre(i)) * br
        merged = term if merged is None else merged + term
    out = x + _dot(merged.astype(BF16), w_out_ref[...])
    return out * _rms_scale(out) * final_w


MEM_KV_BLOCK = 4


def _mem_kv_kernel(n_cast, mem_ref, norm_w_ref, w_ref, *refs):
    cast_in, (k_ref, v_ref, kbf_ref, vbf_ref), cast_out = (
        refs[:n_cast], refs[n_cast:n_cast + 4], refs[n_cast + 4:])

    @pl.when(pl.program_id(0) == 0)
    def _():
        for src, dst in zip(cast_in, cast_out):
            dst[...] = src[0].astype(BF16)

    nb, n_mem, _ = mem_ref.shape
    m = mem_ref[...].reshape(nb * n_mem, D_MODEL)
    hm = (m * _rms_scale(m) * norm_w_ref[...]).astype(BF16)
    kv = _dot(hm, w_ref[0].astype(BF16))
    for b in range(nb):
        k = kv[b * n_mem:(b + 1) * n_mem, :BRANCH_W]
        v = kv[b * n_mem:(b + 1) * n_mem, BRANCH_W:]
        for h in range(MEM_HEADS):
            k_ref[b, pl.ds(h, n_mem, stride=MEM_HEADS), :] = _head(k, h)
            v_ref[b, pl.ds(h, n_mem, stride=MEM_HEADS), :] = _head(v, h)
        kbf_ref[b] = k.astype(BF16)
        vbf_ref[b] = v.astype(BF16)


def _mem_kv(mem, mem_norm_w, w_mem_kv, to_cast):
    batch, n_mem, _ = mem.shape
    nb = MEM_KV_BLOCK
    blk = lambda dt: jax.ShapeDtypeStruct((batch, n_mem, BRANCH_W), dt)
    out_spec = pl.BlockSpec((nb, n_mem, BRANCH_W), lambda b: (b, 0, 0))
    rows_shape = (batch, n_mem * MEM_HEADS, HEAD_DIM)
    rows_spec = pl.BlockSpec((nb, n_mem * MEM_HEADS, HEAD_DIM), lambda b: (b, 0, 0))
    return pl.pallas_call(
        functools.partial(_mem_kv_kernel, len(to_cast)),
        grid=(batch // nb,),
        in_specs=[
            pl.BlockSpec((nb, n_mem, D_MODEL), lambda b: (b, 0, 0)),
            pl.BlockSpec((1, D_MODEL), lambda b: (0, 0)),
            pl.BlockSpec((1, D_MODEL, 2 * BRANCH_W), lambda b: (0, 0, 0)),
        ] + [pl.BlockSpec(w.shape, lambda b: (0, 0, 0), pipeline_mode=pl.Buffered(1))
             for w in to_cast],
        out_specs=[rows_spec, rows_spec, out_spec, out_spec]
                  + [pl.BlockSpec(w.shape[1:], lambda b: (0, 0)) for w in to_cast],
        out_shape=[jax.ShapeDtypeStruct(rows_shape, F32), jax.ShapeDtypeStruct(rows_shape, F32),
                   blk(BF16), blk(BF16)]
                  + [jax.ShapeDtypeStruct(w.shape[1:], BF16) for w in to_cast],
        compiler_params=pltpu.CompilerParams(
            dimension_semantics=("arbitrary",), vmem_limit_bytes=VMEM_LIMIT_BYTES),
        name="mem_kv",
    )(mem, mem_norm_w, w_mem_kv, *to_cast)


def _prompt_kernel(c_dec, gamma, tiles_per_seq, n_tiles, *refs):
    (x_ref, xb_ref, cos_ref, sin_ref, mk_ref, mv_ref, norm_w_ref, w_in_ref,
     gn_w_ref, conv_w_ref, conv_b_ref, ln_w_ref, ln_b_ref,
     w_br_ret_ref, w_br_conv_ref, w_br_mem_ref, w_out_ref, final_w_ref,
     intra_ref, qdec_ref, kdec_ref,
     zs_ref, cos_s_ref, sin_s_ref, s0_ref, ck_ref, cv_ref,
     y_ref, s_ref, hist_ref, os_ref, atts_ref, s_new_ref,
     ubuf_ref, h_scr, yr_scr, yc_scr, ym_scr, os_scr, atts_scr) = refs
    i = pl.program_id(0)

    @pl.when(i == 0)
    def _():
        h_scr[...] = jnp.zeros_like(h_scr)
        yr_scr[...] = jnp.zeros_like(yr_scr)
        yc_scr[...] = jnp.zeros_like(yc_scr)
        ym_scr[...] = jnp.zeros_like(ym_scr)
        os_scr[...] = jnp.zeros_like(os_scr)
        atts_scr[...] = jnp.zeros_like(atts_scr)
        hist_ref[...] = jnp.zeros_like(hist_ref)

    @pl.when(i < n_tiles)
    def _():
        _prompt_step(c_dec, gamma, tiles_per_seq, *refs)

    @pl.when(i == n_tiles)
    def _():
        gate = lambda b: _dot(h_scr[...], w_in_ref[:, OFF_MERGE + b * D_MODEL:
                                                   OFF_MERGE + (b + 1) * D_MODEL])
        y_ref[0] = _merge_and_project(
            xb_ref[0], gate, yr_scr[...], yc_scr[...], ym_scr[...],
            w_br_ret_ref, w_br_conv_ref, w_br_mem_ref, w_out_ref, final_w_ref[...])


def _prompt_step(c_dec, gamma, tiles_per_seq,
                 x_ref, xb_ref, cos_ref, sin_ref, mk_ref, mv_ref, norm_w_ref, w_in_ref,
                 gn_w_ref, conv_w_ref, conv_b_ref, ln_w_ref, ln_b_ref,
                 w_br_ret_ref, w_br_conv_ref, w_br_mem_ref, w_out_ref, final_w_ref,
                 intra_ref, qdec_ref, kdec_ref,
                 zs_ref, cos_s_ref, sin_s_ref, s0_ref, ck_ref, cv_ref,
                 y_ref, s_ref, hist_ref, os_ref, atts_ref, s_new_ref,
                 ubuf_ref, h_scr, yr_scr, yc_scr, ym_scr, os_scr, atts_scr):
    i = pl.program_id(0)
    t = lax.rem(i, tiles_per_seq)
    tile = x_ref.shape[1]

    @pl.when(t == 0)
    def _():
        s_ref[...] = jnp.zeros_like(s_ref)
        ubuf_ref[0:HIST_PAD, :] = jnp.zeros((HIST_PAD, BRANCH_W), F32)

    h_prev = h_scr[...]
    half = D_MODEL // 2

    def gate_job(k):
        lo = OFF_MERGE + k * half
        return lambda: _dot(h_prev, w_in_ref[:, lo:lo + half])

    back_jobs = [gate_job(k) for k in range(2 * N_BRANCH)] + [
        lambda: _dot(yr_scr[...], w_br_ret_ref[...]),
        lambda: _dot(yc_scr[...], w_br_conv_ref[...]),
        lambda: _dot(ym_scr[...], w_br_mem_ref[...]),
    ]
    back_out = [None] * len(back_jobs)

    def run_back(k):
        back_out[k] = back_jobs[k]()

    x = x_ref[0]
    h_bf = (x * _rms_scale(x) * norm_w_ref[...]).astype(BF16)
    run_back(8)

    def proj(off, width=BRANCH_W):
        return _dot(h_bf, w_in_ref[:, off:off + width])

    za = proj(OFF_AC)
    zb = proj(OFF_BC)
    zq = proj(OFF_QR)
    zk = proj(OFF_KR)
    v_bf = proj(OFF_VR).astype(BF16)
    u = za * _sigmoid(zb)
    ubuf_ref[HIST_PAD:HIST_PAD + tile, :] = u
    first = HIST_PAD - CONV_HIST
    conv = None
    for r in range(8):
        part = None
        for j in range(CONV_K):
            if (first + j) % 8 != r:
                continue
            term = ubuf_ref[first + j:first + j + tile, :] * conv_w_ref[0, j:j + 1, :]
            part = term if part is None else part + term
        conv = part if conv is None else conv + part
        run_back(r)
    conv = conv + conv_b_ref[...]
    ubuf_ref[0:HIST_PAD, :] = ubuf_ref[tile:tile + HIST_PAD, :]

    z_gc = proj(OFF_GC)
    z_gr = proj(OFF_GR)
    z_qm = proj(OFF_QM)
    z_gm = proj(OFF_GM)

    merged = []
    for k in range(2):
        cols = slice(k * half, (k + 1) * half)
        acc = None
        for b in range(N_BRANCH):
            term = _sigmoid(back_out[2 * b + k]) * back_out[2 * N_BRANCH + b][:, cols]
            acc = term if acc is None else acc + term
        merged.append(acc.astype(BF16))
    merged = jnp.concatenate(merged, axis=-1)

    cos2 = cos_ref[...]
    sin2 = sin_ref[...]
    q_scale = HEAD_DIM ** -0.5
    def block_diag(a, b):
        zero = jnp.zeros_like(a)
        return jnp.concatenate([jnp.concatenate([a, zero], axis=-1),
                                jnp.concatenate([zero, b], axis=-1)], axis=0)

    o_heads = []
    for h0 in range(0, RET_HEADS, 2):
        pair = (h0, h0 + 1)
        q_p = [_rotary(_head(zq, h), cos2, sin2) * q_scale for h in pair]
        k_p = [_rotary(_head(zk, h), cos2, sin2) for h in pair]
        v_p = [_head(v_bf, h) for h in pair]
        s_old = [s_ref[0, h] for h in pair]
        s = list(s_old)
        intra2 = jnp.concatenate([intra_ref[h] for h in pair], axis=-1)
        qdec2 = jnp.concatenate([qdec_ref[h] for h in pair], axis=-1)
        o_chunks = []
        for c in range(tile // RET_CHUNK):
            rows = slice(c * RET_CHUNK, (c + 1) * RET_CHUNK)
            kc = [k[rows] for k in k_p]
            vc = [v[rows] for v in v_p]
            qc2 = jnp.concatenate([q[rows] for q in q_p], axis=-1).astype(BF16)
            sc2 = _dot_nt(qc2, block_diag(*[k.astype(BF16) for k in kc])) * intra2
            o_chunks.append(_dot(sc2.astype(BF16), block_diag(*vc))
                            + _dot(qc2, block_diag(*[x.astype(BF16) for x in s])) * qdec2)
            for n, h in enumerate(pair):
                kd = (kc[n] * kdec_ref[h]).astype(BF16)
                s[n] = s[n] * c_dec[h] + _dot_tn(kd, vc[n])
        o_pair = jnp.concatenate(o_chunks, axis=0)
        for n, h in enumerate(pair):
            s_ref[0, h] = s[n]
            o_heads.append(_layernorm_rows(_head(o_pair, n)))

    out = xb_ref[0] + _dot(merged, w_out_ref[...])

    ret = jnp.concatenate(o_heads, axis=-1) * gn_w_ref[...]
    yr = _silu(z_gr) * ret
    cn = _layernorm_rows(conv) * ln_w_ref[...] + ln_b_ref[...]
    yc = _silu(z_gc) * _silu(cn)

    qm = z_qm.astype(BF16)
    mk = mk_ref[0]
    mv = mv_ref[0]
    att = []
    for h in range(MEM_HEADS):
        sc = _dot_nt(_head(qm, h), _head(mk, h)) * (HEAD_DIM ** -0.5)
        e = jnp.exp(sc - jnp.max(sc, axis=-1, keepdims=True))
        p = e / jnp.sum(e, axis=-1, keepdims=True)
        att.append(_dot(p.astype(BF16), _head(mv, h)))
    ym = _silu(z_gm) * jnp.concatenate(att, axis=-1)

    y_ref[0] = out * _rms_scale(out) * final_w_ref[...]

    h_scr[...] = h_bf
    yr_scr[...] = yr.astype(BF16)
    yc_scr[...] = yc.astype(BF16)
    ym_scr[...] = ym.astype(BF16)

    @pl.when(t == tiles_per_seq - 1)
    def _():
        seq_id = i // tiles_per_seq
        mine = lax.broadcasted_iota(jnp.int32, (hist_ref.shape[1], 1), 0) == seq_id
        for j in range(CONV_HIST):
            hist_ref[j] = jnp.where(mine, ubuf_ref[first + j:first + j + 1, :], hist_ref[j])

    per_step = s0_ref.shape[0]
    r0 = lax.rem(i, SUBLANES // per_step) * per_step
    row_id = lax.broadcasted_iota(jnp.int32, (SUBLANES, 1), 0)

    def own_rows(off):
        block = zs_ref[:, off:off + BRANCH_W]
        return jnp.concatenate(
            [jnp.sum(jnp.where(row_id == r0 + b, block, 0.0), axis=0, keepdims=True)
             for b in range(per_step)], axis=0)

    o_rows, att_rows = _sample_sequences(
        gamma, own_rows(OFF_QR), own_rows(OFF_KR), own_rows(OFF_VR), own_rows(OFF_QM),
        cos_s_ref[...], sin_s_ref[...], s0_ref, ck_ref, cv_ref, s_new_ref)
    for scr, out_ref, rows in ((os_scr, os_ref, o_rows), (atts_scr, atts_ref, att_rows)):
        block = scr[...]
        for b in range(per_step):
            block = jnp.where(row_id == r0 + b, rows[b], block)
        scr[...] = block
        out_ref[...] = block


def _const_spec(shape, single_buffer=False):
    zeros = (0,) * len(shape)
    if single_buffer:
        return pl.BlockSpec(shape, lambda i: zeros, pipeline_mode=pl.Buffered(1))
    return pl.BlockSpec(shape, lambda i: zeros)


def _prompt_layer(x, cos2, sin2, mk_bf, mv_bf, norm_w, w_in_bf, gn_w, conv_w, conv_b, ln_w,
                  ln_b, w_br_ret_bf, w_br_conv_bf, w_br_mem_bf, w_out_bf, final_w,
                  intra, qdec, kdec, c_dec,
                  z_s, cos_s, sin_s, s0, cache_k, cache_v, gamma):
    batch, seq, _ = x.shape
    n_mem = mk_bf.shape[1]
    tile = PROMPT_TILE
    tiles_per_seq = seq // tile
    n_tiles = batch * tiles_per_seq

    def front(i):
        j = jnp.minimum(i, n_tiles - 1)
        return j // tiles_per_seq, lax.rem(j, tiles_per_seq)

    def back(i):
        j = jnp.maximum(i - 1, 0)
        return j // tiles_per_seq, lax.rem(j, tiles_per_seq)

    row = lambda w: _const_spec((1, w))
    weight = lambda a: _const_spec(a.shape, single_buffer=True)
    table = _const_spec((RET_HEADS, RET_CHUNK, RET_CHUNK))
    n_dec = z_s.shape[0]
    per_step = n_dec // n_tiles
    assert per_step * n_tiles == n_dec
    share = lambda shape: pl.BlockSpec(
        shape, lambda i: (jnp.minimum(i, n_tiles - 1),) + (0,) * (len(shape) - 1))
    assert SUBLANES % per_step == 0
    rows8 = lambda w: pl.BlockSpec(
        (SUBLANES, w), lambda i: (jnp.minimum(i, n_tiles - 1) // (SUBLANES // per_step), 0))
    x_tile = lambda which: pl.BlockSpec((1, tile, D_MODEL), lambda i: (*which(i), 0))
    per_seq = lambda shape: pl.BlockSpec(
        (1,) + shape, lambda i: (front(i)[0],) + (0,) * len(shape))
    return pl.pallas_call(
        functools.partial(_prompt_kernel, c_dec, gamma, tiles_per_seq, n_tiles),
        grid=(n_tiles + 1,),
        in_specs=[
            x_tile(front), x_tile(back),
            pl.BlockSpec((tile, HEAD_DIM), lambda i: (front(i)[1], 0)),
            pl.BlockSpec((tile, HEAD_DIM), lambda i: (front(i)[1], 0)),
            per_seq((n_mem, BRANCH_W)), per_seq((n_mem, BRANCH_W)),
            row(D_MODEL), weight(w_in_bf), row(BRANCH_W),
            _const_spec((1, CONV_K, BRANCH_W)), row(BRANCH_W), row(BRANCH_W), row(BRANCH_W),
            weight(w_br_ret_bf), weight(w_br_conv_bf), weight(w_br_mem_bf), weight(w_out_bf),
            row(D_MODEL), table, table, table,
            rows8(IN_W), row(HEAD_DIM), row(HEAD_DIM),
            share((per_step,) + s0.shape[1:]),
            share((per_step,) + cache_k.shape[1:]), share((per_step,) + cache_v.shape[1:]),
        ],
        out_specs=[
            x_tile(back),
            per_seq((RET_HEADS, HEAD_DIM, HEAD_DIM)),
            _const_spec((CONV_HIST, batch, BRANCH_W)),
            rows8(BRANCH_W), rows8(BRANCH_W),
            share((per_step,) + s0.shape[1:]),
        ],
        out_shape=[
            jax.ShapeDtypeStruct((batch, seq, D_MODEL), F32),
            jax.ShapeDtypeStruct((batch, RET_HEADS, HEAD_DIM, HEAD_DIM), F32),
            jax.ShapeDtypeStruct((CONV_HIST, batch, BRANCH_W), F32),
            jax.ShapeDtypeStruct((n_dec, BRANCH_W), F32),
            jax.ShapeDtypeStruct((n_dec, BRANCH_W), F32),
            jax.ShapeDtypeStruct(s0.shape, F32),
        ],
        scratch_shapes=[
            pltpu.VMEM((HIST_PAD + tile, BRANCH_W), F32),
            pltpu.VMEM((tile, D_MODEL), BF16),
            pltpu.VMEM((tile, BRANCH_W), BF16),
            pltpu.VMEM((tile, BRANCH_W), BF16),
            pltpu.VMEM((tile, BRANCH_W), BF16),
            pltpu.VMEM((SUBLANES, BRANCH_W), F32),
            pltpu.VMEM((SUBLANES, BRANCH_W), F32),
        ],
        compiler_params=pltpu.CompilerParams(
            dimension_semantics=("arbitrary",), vmem_limit_bytes=VMEM_LIMIT_BYTES),
        name="prompt_layer",
    )(x, x, cos2, sin2, mk_bf, mv_bf, norm_w, w_in_bf, gn_w, conv_w, conv_b, ln_w, ln_b,
      w_br_ret_bf, w_br_conv_bf, w_br_mem_bf, w_out_bf, final_w, intra, qdec, kdec,
      z_s, cos_s, sin_s, s0, cache_k, cache_v)


SAMPLE_PROJ_BLOCK = 1536


def _sample_proj_kernel(x_ref, norm_w_ref, w_ref, z_ref, w_bf_ref):
    x = x_ref[:, 0, :]
    h_bf = (x * _rms_scale(x) * norm_w_ref[...]).astype(BF16)
    w_bf = w_ref[0].astype(BF16)
    w_bf_ref[...] = w_bf
    z_ref[...] = _dot(h_bf, w_bf)


def _sample_proj(xs, norm_w, w_in):
    n = xs.shape[0]
    return pl.pallas_call(
        _sample_proj_kernel,
        grid=(IN_W // SAMPLE_PROJ_BLOCK,),
        in_specs=[
            pl.BlockSpec((n, 1, D_MODEL), lambda j: (0, 0, 0)),
            pl.BlockSpec((1, D_MODEL), lambda j: (0, 0)),
            pl.BlockSpec((1, D_MODEL, SAMPLE_PROJ_BLOCK), lambda j: (0, 0, j)),
        ],
        out_specs=[pl.BlockSpec((n, SAMPLE_PROJ_BLOCK), lambda j: (0, j)),
                   pl.BlockSpec((D_MODEL, SAMPLE_PROJ_BLOCK), lambda j: (0, j))],
        out_shape=[jax.ShapeDtypeStruct((n, IN_W), F32),
                   jax.ShapeDtypeStruct((D_MODEL, IN_W), BF16)],
        compiler_params=pltpu.CompilerParams(
            dimension_semantics=("arbitrary",), vmem_limit_bytes=VMEM_LIMIT_BYTES),
        name="sample_proj",
    )(xs, norm_w, w_in)


def _sample_sequences(gamma, zq, zk, v, qm, cos2, sin2, s0_ref, ck_ref, cv_ref, s_new_ref):
    n = zq.shape[0]
    n_mem = ck_ref.shape[1] // MEM_HEADS
    q_scale = HEAD_DIM ** -0.5
    mem_scale = HEAD_DIM ** -0.5
    q = [_rotary(_head(zq, h), cos2, sin2) * q_scale for h in range(RET_HEADS)]
    k = [_rotary(_head(zk, h), cos2, sin2) for h in range(RET_HEADS)]

    eye = (lax.broadcasted_iota(jnp.int32, (HEAD_DIM, HEAD_DIM), 0)
           == lax.broadcasted_iota(jnp.int32, (HEAD_DIM, HEAD_DIM), 1))
    ones_bf = jnp.ones((HEAD_DIM, HEAD_DIM), BF16)

    def as_columns(row):
        diag = jnp.where(eye, jnp.broadcast_to(row, (HEAD_DIM, HEAD_DIM)), 0.0)
        return _dot(diag.astype(BF16), ones_bf)

    o_rows, att_rows = [], []
    for b in range(n):
        o_heads = []
        for h in range(RET_HEADS):
            q_row = q[h][b:b + 1]
            k_row = k[h][b:b + 1]
            v_row = _head(v, h)[b:b + 1]
            s0 = s0_ref[b, h]
            qk = jnp.sum(q_row * k_row, axis=-1, keepdims=True)
            q_lhs = jnp.broadcast_to(q_row, (SUBLANES, HEAD_DIM)).astype(BF16)
            qs = _dot(q_lhs, s0.astype(BF16))[0:1]
            o_heads.append(qk * v_row + gamma[h] * qs)
            s_new_ref[b, h] = gamma[h] * s0 + as_columns(k_row) * v_row
        o_rows.append(jnp.concatenate(o_heads, axis=-1))

        att_heads = []
        for h in range(MEM_HEADS):
            head_rows = pl.ds(h, n_mem, stride=MEM_HEADS)
            prod = ck_ref[b, head_rows, :] * (_head(qm, h)[b:b + 1] * mem_scale)
            sc = jnp.sum(prod, axis=-1, keepdims=True)
            e = jnp.exp(sc - jnp.max(sc, axis=0, keepdims=True))
            weighted = jnp.sum(e * cv_ref[b, head_rows, :], axis=0, keepdims=True)
            att_heads.append(weighted / jnp.sum(e, axis=0, keepdims=True))
        att_rows.append(jnp.concatenate(att_heads, axis=-1))
    return o_rows, att_rows


def _sample_out_kernel(x_ref, z_ref, o_ref, att_ref, hist_ref, conv_w_ref, gn_w_ref, conv_b_ref,
                       ln_w_ref, ln_b_ref, w_br_ret_ref, w_br_conv_ref, w_br_mem_ref, w_out_ref,
                       final_w_ref, y_ref, hist_new_ref):
    gate = lambda off: _silu(z_ref[:, off:off + BRANCH_W])
    yr = gate(OFF_GR) * _group_norm_heads(o_ref[...], gn_w_ref[...])

    u = z_ref[:, OFF_AC:OFF_AC + BRANCH_W] * _sigmoid(z_ref[:, OFF_BC:OFF_BC + BRANCH_W])
    conv = u * conv_w_ref[0, CONV_HIST:CONV_K, :] + conv_b_ref[...]
    for j in range(CONV_HIST):
        conv = conv + hist_ref[j] * conv_w_ref[0, j:j + 1, :]
        hist_new_ref[j] = hist_ref[j + 1] if j + 1 < CONV_HIST else u
    cn = _layernorm_rows(conv) * ln_w_ref[...] + ln_b_ref[...]
    yc = gate(OFF_GC) * _silu(cn)

    ym = gate(OFF_GM) * att_ref[...]
    gate_pre = lambda i: z_ref[:, OFF_MERGE + i * D_MODEL:OFF_MERGE + (i + 1) * D_MODEL]
    y_ref[:, 0, :] = _merge_and_project(x_ref[:, 0, :], gate_pre, yr, yc, ym, w_br_ret_ref,
                                        w_br_conv_ref, w_br_mem_ref, w_out_ref, final_w_ref[...])


SAMPLE_OUT_BLOCK = 32


def _sample_out(xs, z, o, att, hist, conv_w, gn_w, conv_b, ln_w, ln_b,
                w_br_ret_bf, w_br_conv_bf, w_br_mem_bf, w_out_bf, final_w):
    n = xs.shape[0]
    nb = SAMPLE_OUT_BLOCK
    rows = lambda w: pl.BlockSpec((nb, w), lambda i: (i, 0))
    hist_spec = pl.BlockSpec((CONV_HIST, nb, BRANCH_W), lambda i: (0, i, 0))
    tokens = pl.BlockSpec((nb, 1, D_MODEL), lambda i: (i, 0, 0))
    const = lambda a: pl.BlockSpec(a.shape, lambda i: (0,) * a.ndim)
    weight = lambda a: pl.BlockSpec(a.shape, lambda i: (0,) * a.ndim,
                                    pipeline_mode=pl.Buffered(1))
    return pl.pallas_call(
        _sample_out_kernel,
        grid=(n // nb,),
        in_specs=[tokens, rows(IN_W), rows(BRANCH_W), rows(BRANCH_W), hist_spec,
                  const(conv_w), const(gn_w), const(conv_b), const(ln_w), const(ln_b),
                  weight(w_br_ret_bf), weight(w_br_conv_bf), weight(w_br_mem_bf),
                  weight(w_out_bf), const(final_w)],
        out_specs=[tokens, hist_spec],
        out_shape=[jax.ShapeDtypeStruct(xs.shape, F32), jax.ShapeDtypeStruct(hist.shape, F32)],
        compiler_params=pltpu.CompilerParams(
            dimension_semantics=("arbitrary",), vmem_limit_bytes=VMEM_LIMIT_BYTES),
        name="sample_out",
    )(xs, z, o, att, hist, conv_w, gn_w, conv_b, ln_w, ln_b,
      w_br_ret_bf, w_br_conv_bf, w_br_mem_bf, w_out_bf, final_w)


def _rotary_tables(pos):
    half = HEAD_DIM // 2
    inv = ROPE_BASE ** (-np.arange(half, dtype=np.float64) / half)
    ang = np.asarray(pos, np.float64)[:, None] * inv[None, :]
    cos, sin = np.cos(ang), np.sin(ang)
    table = lambda a, b: np.concatenate([a, b], axis=-1).astype(np.float32)
    return table(cos, cos), table(-sin, sin)


def _decay_tables():
    c = RET_CHUNK
    log_g = np.log1p(-np.exp2(-5.0 - np.arange(RET_HEADS, dtype=np.float64)))
    idx = np.arange(c, dtype=np.float64)
    diff = idx[:, None] - idx[None, :]
    intra = np.where(diff >= 0, np.exp(np.maximum(diff, 0.0)[None] * log_g[:, None, None]), 0.0)
    q_dec = np.exp((idx + 1.0)[None, :] * log_g[:, None])
    k_dec = np.exp((c - 1.0 - idx)[None, :] * log_g[:, None])
    bcast = lambda a: np.ascontiguousarray(
        np.broadcast_to(a[:, :, None], (RET_HEADS, c, HEAD_DIM))).astype(np.float32)
    return intra.astype(np.float32), bcast(q_dec), bcast(k_dec)


def _gamma_powers(n):
    return tuple(float(np.exp(np.log1p(-np.exp2(-5.0 - h)) * n)) for h in range(RET_HEADS))


def kernel(x_prompt, x_sample, mem_prompt, state_ret, state_conv, cache_mem_k, cache_mem_v,
           norm_w, w_in, ret_gn_w, conv_w, conv_b, conv_ln_w, conv_ln_b, mem_norm_w,
           w_mem_kv, w_br_ret, w_br_conv, w_br_mem, w_out, final_norm_w):
    depth = w_in.shape[0]
    assert depth == 1, "single-layer step"
    batch, seq, _ = x_prompt.shape
    n_dec, dec_seq, _ = x_sample.shape
    assert dec_seq == 1 and seq % PROMPT_TILE == 0
    n_mem = mem_prompt.shape[1]

    final_w = final_norm_w[None, :]

    intra, qdec, kdec = _decay_tables()
    cos_p, sin_p = _rotary_tables(np.arange(seq))
    cos_s, sin_s = _rotary_tables(np.arange(dec_seq) + PAST_LEN)

    z_s, w_in_bf = _sample_proj(x_sample, norm_w, w_in)

    mk, mv, mk_bf, mv_bf, w_br_ret_bf, w_br_conv_bf, w_br_mem_bf, w_out_bf = _mem_kv(
        mem_prompt, mem_norm_w, w_mem_kv, (w_br_ret, w_br_conv, w_br_mem, w_out))

    y_prompt, s_prompt, hist_prompt, o_s, att_s, s_sample = _prompt_layer(
        x_prompt, cos_p, sin_p, mk_bf, mv_bf, norm_w, w_in_bf, ret_gn_w, conv_w, conv_b,
        conv_ln_w, conv_ln_b, w_br_ret_bf, w_br_conv_bf, w_br_mem_bf, w_out_bf, final_w,
        intra, qdec, kdec, _gamma_powers(RET_CHUNK),
        z_s, cos_s, sin_s, state_ret[0],
        cache_mem_k.reshape(n_dec, n_mem * MEM_HEADS, HEAD_DIM),
        cache_mem_v.reshape(n_dec, n_mem * MEM_HEADS, HEAD_DIM), _gamma_powers(1))

    y_sample, hist_sample = _sample_out(
        x_sample, z_s, o_s, att_s,
        jnp.transpose(state_conv[0], (1, 0, 2)), conv_w, ret_gn_w, conv_b, conv_ln_w, conv_ln_b,
        w_br_ret_bf, w_br_conv_bf, w_br_mem_bf, w_out_bf, final_w)

    heads = lambda a: a.reshape(1, batch, n_mem, MEM_HEADS, HEAD_DIM)
    return (y_prompt, y_sample, s_prompt[None], s_sample[None],
            jnp.transpose(hist_prompt, (1, 0, 2))[None],
            jnp.transpose(hist_sample, (1, 0, 2))[None], heads(mk), heads(mv))
```

```python
import functools

import numpy as np
import jax
import jax.numpy as jnp
from jax import lax
from jax.experimental import pallas as pl
from jax.experimental.pallas import tpu as pltpu

F32 = jnp.float32
BF16 = jnp.bfloat16

D_MODEL = 1024
RET_HEADS = 4
HEAD_DIM = 128
SUBLANES = 8
BRANCH_W = 512
RET_CHUNK = 128
ROPE_BASE = 10000.0
CONV_K = 31
CONV_HIST = CONV_K - 1
MEM_HEADS = 4
N_BRANCH = 3
EPS = 1e-6
PAST_LEN = 16384

OFF_QR, OFF_KR, OFF_VR, OFF_GR = 0, 512, 1024, 1536
OFF_AC, OFF_BC, OFF_GC = 2048, 2560, 3072
OFF_QM, OFF_GM = 3584, 4096
OFF_MERGE = 4608
IN_W = OFF_MERGE + N_BRANCH * D_MODEL

V7X_VMEM_BYTES = 64 * 1024 * 1024
VMEM_LIMIT_BYTES = V7X_VMEM_BYTES - 8 * 1024 * 1024

PROMPT_TILE = 256
HIST_PAD = 32


def _sigmoid(x):
    return 0.5 * jnp.tanh(0.5 * x) + 0.5


def _silu(x):
    hx = 0.5 * x
    return hx * (jnp.tanh(hx) + 1.0)


def _rms_scale(x):
    return lax.rsqrt(jnp.mean(x * x, axis=-1, keepdims=True) + EPS)


def _layernorm_rows(x):
    mu = jnp.mean(x, axis=-1, keepdims=True)
    xc = x - mu
    var = jnp.mean(xc * xc, axis=-1, keepdims=True)
    return xc * lax.rsqrt(var + EPS)


def _rotary(x, cos2, sin2):
    return x * cos2 + pltpu.roll(x, HEAD_DIM // 2, axis=1) * sin2


def _dot(a, b):
    return jnp.dot(a, b, preferred_element_type=F32)


def _dot_nt(a, b):
    return lax.dot_general(a, b, (((1,), (1,)), ((), ())), preferred_element_type=F32)


def _dot_tn(a, b):
    return lax.dot_general(a, b, (((0,), (0,)), ((), ())), preferred_element_type=F32)


def _head(x, h):
    return x[:, h * HEAD_DIM:(h + 1) * HEAD_DIM]


def _group_norm_heads(o, gn_w):
    parts = [_layernorm_rows(_head(o, h)) for h in range(RET_HEADS)]
    return jnp.concatenate(parts, axis=-1) * gn_w


def _merge_and_project(x, gate_pre, yr, yc, ym, w_br_ret_ref, w_br_conv_ref,
                       w_br_mem_ref, w_out_ref, final_w):
    branches = (
        _dot(yr.astype(BF16), w_br_ret_ref[...]),
        _dot(yc.astype(BF16), w_br_conv_ref[...]),
        _dot(ym.astype(BF16), w_br_mem_ref[...]),
    )
    merged = None
    for i, br in enumerate(branches):
        term = _sigmoid(gate_pre(i)) * br
        merged = term if merged is None else merged + term
    out = x + _dot(merged.astype(BF16), w_out_ref[...])
    return out * _rms_scale(out) * final_w


MEM_KV_BLOCK = 4


def _mem_kv_kernel(n_cast, mem_ref, norm_w_ref, w_ref, *refs):
    cast_in, (k_ref, v_ref, kbf_ref, vbf_ref), cast_out = (
        refs[:n_cast], refs[n_cast:n_cast + 4], refs[n_cast + 4:])

    @pl.when(pl.program_id(0) == 0)
    def _():
        for src, dst in zip(cast_in, cast_out):
            dst[...] = src[0].astype(BF16)

    nb, n_mem, _ = mem_ref.shape
    m = mem_ref[...].reshape(nb * n_mem, D_MODEL)
    hm = (m * _rms_scale(m) * norm_w_ref[...]).astype(BF16)
    kv = _dot(hm, w_ref[0].astype(BF16))
    for b in range(nb):
        k = kv[b * n_mem:(b + 1) * n_mem, :BRANCH_W]
        v = kv[b * n_mem:(b + 1) * n_mem, BRANCH_W:]
        for h in range(MEM_HEADS):
            k_ref[b, pl.ds(h, n_mem, stride=MEM_HEADS), :] = _head(k, h)
            v_ref[b, pl.ds(h, n_mem, stride=MEM_HEADS), :] = _head(v, h)
        kbf_ref[b] = k.astype(BF16)
        vbf_ref[b] = v.astype(BF16)


def _mem_kv(mem, mem_norm_w, w_mem_kv, to_cast):
    batch, n_mem, _ = mem.shape
    nb = MEM_KV_BLOCK
    blk = lambda dt: jax.ShapeDtypeStruct((batch, n_mem, BRANCH_W), dt)
    out_spec = pl.BlockSpec((nb, n_mem, BRANCH_W), lambda b: (b, 0, 0))
    rows_shape = (batch, n_mem * MEM_HEADS, HEAD_DIM)
    rows_spec = pl.BlockSpec((nb, n_mem * MEM_HEADS, HEAD_DIM), lambda b: (b, 0, 0))
    return pl.pallas_call(
        functools.partial(_mem_kv_kernel, len(to_cast)),
        grid=(batch // nb,),
        in_specs=[
            pl.BlockSpec((nb, n_mem, D_MODEL), lambda b: (b, 0, 0)),
            pl.BlockSpec((1, D_MODEL), lambda b: (0, 0)),
            pl.BlockSpec((1, D_MODEL, 2 * BRANCH_W), lambda b: (0, 0, 0)),
        ] + [pl.BlockSpec(w.shape, lambda b: (0, 0, 0), pipeline_mode=pl.Buffered(1))
             for w in to_cast],
        out_specs=[rows_spec, rows_spec, out_spec, out_spec]
                  + [pl.BlockSpec(w.shape[1:], lambda b: (0, 0)) for w in to_cast],
        out_shape=[jax.ShapeDtypeStruct(rows_shape, F32), jax.ShapeDtypeStruct(rows_shape, F32),
                   blk(BF16), blk(BF16)]
                  + [jax.ShapeDtypeStruct(w.shape[1:], BF16) for w in to_cast],
        compiler_params=pltpu.CompilerParams(
            dimension_semantics=("arbitrary",), vmem_limit_bytes=VMEM_LIMIT_BYTES),
        name="mem_kv",
    )(mem, mem_norm_w, w_mem_kv, *to_cast)


W_IN_SECTIONS = tuple((off, BRANCH_W) for off in (
    OFF_AC, OFF_BC, OFF_QR, OFF_KR, OFF_VR, OFF_GC, OFF_GR, OFF_QM, OFF_GM)
) + ((OFF_MERGE, N_BRANCH * D_MODEL),)


def _prompt_kernel(c_dec, gamma, tiles_per_seq, n_tiles, *refs):
    *refs, w_in_ref, w_sem = refs
    w_in_pos = 7
    w_in_hbm = refs[w_in_pos]
    refs = refs[:w_in_pos] + [w_in_ref] + refs[w_in_pos + 1:]
    (x_ref, xb_ref, cos_ref, sin_ref, mk_ref, mv_ref, norm_w_ref, _,
     gn_w_ref, conv_w_ref, conv_b_ref, ln_w_ref, ln_b_ref,
     w_br_ret_ref, w_br_conv_ref, w_br_mem_ref, w_out_ref, final_w_ref,
     intra_ref, qdec_ref, kdec_ref,
     zs_ref, cos_s_ref, sin_s_ref, s0_ref, ck_ref, cv_ref,
     y_ref, s_ref, hist_ref, os_ref, atts_ref, s_new_ref,
     ubuf_ref, h_scr, yr_scr, yc_scr, ym_scr, os_scr, atts_scr) = refs
    i = pl.program_id(0)

    def w_copy(k):
        cols = slice(W_IN_SECTIONS[k][0], W_IN_SECTIONS[k][0] + W_IN_SECTIONS[k][1])
        return pltpu.make_async_copy(w_in_hbm.at[:, cols], w_in_ref.at[:, cols], w_sem.at[k])

    @pl.when(i == 0)
    def _():
        for k in range(len(W_IN_SECTIONS)):
            w_copy(k).start()
        os_scr[...] = jnp.zeros_like(os_scr)
        atts_scr[...] = jnp.zeros_like(atts_scr)
        hist_ref[...] = jnp.zeros_like(hist_ref)
        first_use = {off: k for k, (off, _) in enumerate(W_IN_SECTIONS)}
        _prompt_step(c_dec, gamma, tiles_per_seq, *refs,
                     wait_section=lambda off: w_copy(first_use[off]).wait())
        w_copy(first_use[OFF_MERGE]).wait()

    @pl.when(jnp.logical_and(i > 0, i < n_tiles))
    def _():
        _prompt_step(c_dec, gamma, tiles_per_seq, *refs)

    @pl.when(i == n_tiles)
    def _():
        gate = lambda b: _dot(h_scr[...], w_in_ref[:, OFF_MERGE + b * D_MODEL:
                                                   OFF_MERGE + (b + 1) * D_MODEL])
        y_ref[0] = _merge_and_project(
            xb_ref[0], gate, yr_scr[...], yc_scr[...], ym_scr[...],
            w_br_ret_ref, w_br_conv_ref, w_br_mem_ref, w_out_ref, final_w_ref[...])


def _prompt_step(c_dec, gamma, tiles_per_seq,
                 x_ref, xb_ref, cos_ref, sin_ref, mk_ref, mv_ref, norm_w_ref, w_in_ref,
                 gn_w_ref, conv_w_ref, conv_b_ref, ln_w_ref, ln_b_ref,
                 w_br_ret_ref, w_br_conv_ref, w_br_mem_ref, w_out_ref, final_w_ref,
                 intra_ref, qdec_ref, kdec_ref,
                 zs_ref, cos_s_ref, sin_s_ref, s0_ref, ck_ref, cv_ref,
                 y_ref, s_ref, hist_ref, os_ref, atts_ref, s_new_ref,
                 ubuf_ref, h_scr, yr_scr, yc_scr, ym_scr, os_scr, atts_scr,
                 wait_section=None):
    first_step = wait_section is not None
    i = pl.program_id(0)
    t = lax.rem(i, tiles_per_seq)
    tile = x_ref.shape[1]

    @pl.when(t == 0)
    def _():
        s_ref[...] = jnp.zeros_like(s_ref)
        ubuf_ref[0:HIST_PAD, :] = jnp.zeros((HIST_PAD, BRANCH_W), F32)

    h_prev = h_scr[...]
    half = D_MODEL // 2

    def gate_job(k):
        lo = OFF_MERGE + k * half
        return lambda: _dot(h_prev, w_in_ref[:, lo:lo + half])

    back_jobs = [gate_job(k) for k in range(2 * N_BRANCH)] + [
        lambda: _dot(yr_scr[...], w_br_ret_ref[...]),
        lambda: _dot(yc_scr[...], w_br_conv_ref[...]),
        lambda: _dot(ym_scr[...], w_br_mem_ref[...]),
    ]
    back_out = [None] * len(back_jobs)

    def run_back(k):
        if not first_step:
            back_out[k] = back_jobs[k]()

    x = x_ref[0]
    h_bf = (x * _rms_scale(x) * norm_w_ref[...]).astype(BF16)
    run_back(8)

    def proj(off, width=BRANCH_W):
        if first_step:
            wait_section(off)
        return _dot(h_bf, w_in_ref[:, off:off + width])

    za = proj(OFF_AC)
    zb = proj(OFF_BC)
    zq = proj(OFF_QR)
    zk = proj(OFF_KR)
    v_bf = proj(OFF_VR).astype(BF16)
    u = za * _sigmoid(zb)
    ubuf_ref[HIST_PAD:HIST_PAD + tile, :] = u
    first = HIST_PAD - CONV_HIST
    conv = None
    for r in range(8):
        part = None
        for j in range(CONV_K):
            if (first + j) % 8 != r:
                continue
            term = ubuf_ref[first + j:first + j + tile, :] * conv_w_ref[0, j:j + 1, :]
            part = term if part is None else part + term
        conv = part if conv is None else conv + part
        run_back(r)
    conv = conv + conv_b_ref[...]
    ubuf_ref[0:HIST_PAD, :] = ubuf_ref[tile:tile + HIST_PAD, :]

    z_gc = proj(OFF_GC)
    z_gr = proj(OFF_GR)
    z_qm = proj(OFF_QM)
    z_gm = proj(OFF_GM)

    merged = []
    for k in range(0 if first_step else 2):
        cols = slice(k * half, (k + 1) * half)
        acc = None
        for b in range(N_BRANCH):
            term = _sigmoid(back_out[2 * b + k]) * back_out[2 * N_BRANCH + b][:, cols]
            acc = term if acc is None else acc + term
        merged.append(acc.astype(BF16))
    if not first_step:
        merged = jnp.concatenate(merged, axis=-1)

    cos2 = cos_ref[...]
    sin2 = sin_ref[...]
    q_scale = HEAD_DIM ** -0.5
    def block_diag(a, b):
        zero = jnp.zeros_like(a)
        return jnp.concatenate([jnp.concatenate([a, zero], axis=-1),
                                jnp.concatenate([zero, b], axis=-1)], axis=0)

    o_heads = []
    for h0 in range(0, RET_HEADS, 2):
        pair = (h0, h0 + 1)
        q_p = [_rotary(_head(zq, h), cos2, sin2) * q_scale for h in pair]
        k_p = [_rotary(_head(zk, h), cos2, sin2) for h in pair]
        v_p = [_head(v_bf, h) for h in pair]
        s_old = [s_ref[0, h] for h in pair]
        s = list(s_old)
        intra2 = jnp.concatenate([intra_ref[h] for h in pair], axis=-1)
        qdec2 = jnp.concatenate([qdec_ref[h] for h in pair], axis=-1)
        o_chunks = []
        for c in range(tile // RET_CHUNK):
            rows = slice(c * RET_CHUNK, (c + 1) * RET_CHUNK)
            kc = [k[rows] for k in k_p]
            vc = [v[rows] for v in v_p]
            qc2 = jnp.concatenate([q[rows] for q in q_p], axis=-1).astype(BF16)
            sc2 = _dot_nt(qc2, block_diag(*[k.astype(BF16) for k in kc])) * intra2
            o_chunks.append(_dot(sc2.astype(BF16), block_diag(*vc))
                            + _dot(qc2, block_diag(*[x.astype(BF16) for x in s])) * qdec2)
            for n, h in enumerate(pair):
                kd = (kc[n] * kdec_ref[h]).astype(BF16)
                s[n] = s[n] * c_dec[h] + _dot_tn(kd, vc[n])
        o_pair = jnp.concatenate(o_chunks, axis=0)
        for n, h in enumerate(pair):
            s_ref[0, h] = s[n]
            o_heads.append(_layernorm_rows(_head(o_pair, n)))

    if not first_step:
        out = xb_ref[0] + _dot(merged, w_out_ref[...])

    ret = jnp.concatenate(o_heads, axis=-1) * gn_w_ref[...]
    yr = _silu(z_gr) * ret
    cn = _layernorm_rows(conv) * ln_w_ref[...] + ln_b_ref[...]
    yc = _silu(z_gc) * _silu(cn)

    qm = z_qm.astype(BF16)
    mk = mk_ref[0]
    mv = mv_ref[0]
    att = []
    for h in range(MEM_HEADS):
        sc = _dot_nt(_head(qm, h), _head(mk, h)) * (HEAD_DIM ** -0.5)
        e = jnp.exp(sc - jnp.max(sc, axis=-1, keepdims=True))
        p = e / jnp.sum(e, axis=-1, keepdims=True)
        att.append(_dot(p.astype(BF16), _head(mv, h)))
    ym = _silu(z_gm) * jnp.concatenate(att, axis=-1)

    if not first_step:
        y_ref[0] = out * _rms_scale(out) * final_w_ref[...]

    h_scr[...] = h_bf
    yr_scr[...] = yr.astype(BF16)
    yc_scr[...] = yc.astype(BF16)
    ym_scr[...] = ym.astype(BF16)

    @pl.when(t == tiles_per_seq - 1)
    def _():
        seq_id = i // tiles_per_seq
        mine = lax.broadcasted_iota(jnp.int32, (hist_ref.shape[1], 1), 0) == seq_id
        for j in range(CONV_HIST):
            hist_ref[j] = jnp.where(mine, ubuf_ref[first + j:first + j + 1, :], hist_ref[j])

    per_step = s0_ref.shape[0]
    r0 = lax.rem(i, SUBLANES // per_step) * per_step
    row_id = lax.broadcasted_iota(jnp.int32, (SUBLANES, 1), 0)

    def own_rows(off):
        block = zs_ref[:, off:off + BRANCH_W]
        return jnp.concatenate(
            [jnp.sum(jnp.where(row_id == r0 + b, block, 0.0), axis=0, keepdims=True)
             for b in range(per_step)], axis=0)

    o_rows, att_rows = _sample_sequences(
        gamma, own_rows(OFF_QR), own_rows(OFF_KR), own_rows(OFF_VR), own_rows(OFF_QM),
        cos_s_ref[...], sin_s_ref[...], s0_ref, ck_ref, cv_ref, s_new_ref)
    for scr, out_ref, rows in ((os_scr, os_ref, o_rows), (atts_scr, atts_ref, att_rows)):
        block = scr[...]
        for b in range(per_step):
            block = jnp.where(row_id == r0 + b, rows[b], block)
        scr[...] = block
        out_ref[...] = block


def _const_spec(shape, single_buffer=False):
    zeros = (0,) * len(shape)
    if single_buffer:
        return pl.BlockSpec(shape, lambda i: zeros, pipeline_mode=pl.Buffered(1))
    return pl.BlockSpec(shape, lambda i: zeros)


def _prompt_layer(x, cos2, sin2, mk_bf, mv_bf, norm_w, w_in_bf, gn_w, conv_w, conv_b, ln_w,
                  ln_b, w_br_ret_bf, w_br_conv_bf, w_br_mem_bf, w_out_bf, final_w,
                  intra, qdec, kdec, c_dec,
                  z_s, cos_s, sin_s, s0, cache_k, cache_v, gamma):
    batch, seq, _ = x.shape
    n_mem = mk_bf.shape[1]
    tile = PROMPT_TILE
    tiles_per_seq = seq // tile
    n_tiles = batch * tiles_per_seq

    def front(i):
        j = jnp.minimum(i, n_tiles - 1)
        return j // tiles_per_seq, lax.rem(j, tiles_per_seq)

    def back(i):
        j = jnp.maximum(i - 1, 0)
        return j // tiles_per_seq, lax.rem(j, tiles_per_seq)

    row = lambda w: _const_spec((1, w))
    weight = lambda a: _const_spec(a.shape, single_buffer=True)
    table = _const_spec((RET_HEADS, RET_CHUNK, RET_CHUNK))
    n_dec = z_s.shape[0]
    per_step = n_dec // n_tiles
    assert per_step * n_tiles == n_dec
    share = lambda shape: pl.BlockSpec(
        shape, lambda i: (jnp.minimum(i, n_tiles - 1),) + (0,) * (len(shape) - 1))
    assert SUBLANES % per_step == 0
    rows8 = lambda w: pl.BlockSpec(
        (SUBLANES, w), lambda i: (jnp.minimum(i, n_tiles - 1) // (SUBLANES // per_step), 0))
    x_tile = lambda which: pl.BlockSpec((1, tile, D_MODEL), lambda i: (*which(i), 0))
    per_seq = lambda shape: pl.BlockSpec(
        (1,) + shape, lambda i: (front(i)[0],) + (0,) * len(shape))
    return pl.pallas_call(
        functools.partial(_prompt_kernel, c_dec, gamma, tiles_per_seq, n_tiles),
        grid=(n_tiles + 1,),
        in_specs=[
            x_tile(front), x_tile(back),
            pl.BlockSpec((tile, HEAD_DIM), lambda i: (front(i)[1], 0)),
            pl.BlockSpec((tile, HEAD_DIM), lambda i: (front(i)[1], 0)),
            per_seq((n_mem, BRANCH_W)), per_seq((n_mem, BRANCH_W)),
            row(D_MODEL), pl.BlockSpec(memory_space=pl.ANY), row(BRANCH_W),
            _const_spec((1, CONV_K, BRANCH_W)), row(BRANCH_W), row(BRANCH_W), row(BRANCH_W),
            weight(w_br_ret_bf), weight(w_br_conv_bf), weight(w_br_mem_bf), weight(w_out_bf),
            row(D_MODEL), table, table, table,
            rows8(IN_W), row(HEAD_DIM), row(HEAD_DIM),
            share((per_step,) + s0.shape[1:]),
            share((per_step,) + cache_k.shape[1:]), share((per_step,) + cache_v.shape[1:]),
        ],
        out_specs=[
            x_tile(back),
            per_seq((RET_HEADS, HEAD_DIM, HEAD_DIM)),
            _const_spec((CONV_HIST, batch, BRANCH_W)),
            rows8(BRANCH_W), rows8(BRANCH_W),
            share((per_step,) + s0.shape[1:]),
        ],
        out_shape=[
            jax.ShapeDtypeStruct((batch, seq, D_MODEL), F32),
            jax.ShapeDtypeStruct((batch, RET_HEADS, HEAD_DIM, HEAD_DIM), F32),
            jax.ShapeDtypeStruct((CONV_HIST, batch, BRANCH_W), F32),
            jax.ShapeDtypeStruct((n_dec, BRANCH_W), F32),
            jax.ShapeDtypeStruct((n_dec, BRANCH_W), F32),
            jax.ShapeDtypeStruct(s0.shape, F32),
        ],
        scratch_shapes=[
            pltpu.VMEM((HIST_PAD + tile, BRANCH_W), F32),
            pltpu.VMEM((tile, D_MODEL), BF16),
            pltpu.VMEM((tile, BRANCH_W), BF16),
            pltpu.VMEM((tile, BRANCH_W), BF16),
            pltpu.VMEM((tile, BRANCH_W), BF16),
            pltpu.VMEM((SUBLANES, BRANCH_W), F32),
            pltpu.VMEM((SUBLANES, BRANCH_W), F32),
            pltpu.VMEM(w_in_bf.shape, BF16),
            pltpu.SemaphoreType.DMA((len(W_IN_SECTIONS),)),
        ],
        compiler_params=pltpu.CompilerParams(
            dimension_semantics=("arbitrary",), vmem_limit_bytes=VMEM_LIMIT_BYTES),
        name="prompt_layer",
    )(x, x, cos2, sin2, mk_bf, mv_bf, norm_w, w_in_bf, gn_w, conv_w, conv_b, ln_w, ln_b,
      w_br_ret_bf, w_br_conv_bf, w_br_mem_bf, w_out_bf, final_w, intra, qdec, kdec,
      z_s, cos_s, sin_s, s0, cache_k, cache_v)


SAMPLE_PROJ_BLOCK = 1536


def _sample_proj_kernel(x_ref, norm_w_ref, w_ref, z_ref, w_bf_ref):
    x = x_ref[:, 0, :]
    h_bf = (x * _rms_scale(x) * norm_w_ref[...]).astype(BF16)
    w_bf = w_ref[0].astype(BF16)
    w_bf_ref[...] = w_bf
    z_ref[...] = _dot(h_bf, w_bf)


def _sample_proj(xs, norm_w, w_in):
    n = xs.shape[0]
    return pl.pallas_call(
        _sample_proj_kernel,
        grid=(IN_W // SAMPLE_PROJ_BLOCK,),
        in_specs=[
            pl.BlockSpec((n, 1, D_MODEL), lambda j: (0, 0, 0)),
            pl.BlockSpec((1, D_MODEL), lambda j: (0, 0)),
            pl.BlockSpec((1, D_MODEL, SAMPLE_PROJ_BLOCK), lambda j: (0, 0, j)),
        ],
        out_specs=[pl.BlockSpec((n, SAMPLE_PROJ_BLOCK), lambda j: (0, j)),
                   pl.BlockSpec((D_MODEL, SAMPLE_PROJ_BLOCK), lambda j: (0, j))],
        out_shape=[jax.ShapeDtypeStruct((n, IN_W), F32),
                   jax.ShapeDtypeStruct((D_MODEL, IN_W), BF16)],
        compiler_params=pltpu.CompilerParams(
            dimension_semantics=("arbitrary",), vmem_limit_bytes=VMEM_LIMIT_BYTES),
        name="sample_proj",
    )(xs, norm_w, w_in)


def _sample_sequences(gamma, zq, zk, v, qm, cos2, sin2, s0_ref, ck_ref, cv_ref, s_new_ref):
    n = zq.shape[0]
    n_mem = ck_ref.shape[1] // MEM_HEADS
    q_scale = HEAD_DIM ** -0.5
    mem_scale = HEAD_DIM ** -0.5
    q = [_rotary(_head(zq, h), cos2, sin2) * q_scale for h in range(RET_HEADS)]
    k = [_rotary(_head(zk, h), cos2, sin2) for h in range(RET_HEADS)]

    eye = (lax.broadcasted_iota(jnp.int32, (HEAD_DIM, HEAD_DIM), 0)
           == lax.broadcasted_iota(jnp.int32, (HEAD_DIM, HEAD_DIM), 1))
    ones_bf = jnp.ones((HEAD_DIM, HEAD_DIM), BF16)

    def as_columns(row):
        diag = jnp.where(eye, jnp.broadcast_to(row, (HEAD_DIM, HEAD_DIM)), 0.0)
        return _dot(diag.astype(BF16), ones_bf)

    o_rows, att_rows = [], []
    for b in range(n):
        o_heads = []
        for h in range(RET_HEADS):
            q_row = q[h][b:b + 1]
            k_row = k[h][b:b + 1]
            v_row = _head(v, h)[b:b + 1]
            s0 = s0_ref[b, h]
            qk = jnp.sum(q_row * k_row, axis=-1, keepdims=True)
            q_lhs = jnp.broadcast_to(q_row, (SUBLANES, HEAD_DIM)).astype(BF16)
            qs = _dot(q_lhs, s0.astype(BF16))[0:1]
            o_heads.append(qk * v_row + gamma[h] * qs)
            s_new_ref[b, h] = gamma[h] * s0 + as_columns(k_row) * v_row
        o_rows.append(jnp.concatenate(o_heads, axis=-1))

        att_heads = []
        for h in range(MEM_HEADS):
            head_rows = pl.ds(h, n_mem, stride=MEM_HEADS)
            prod = ck_ref[b, head_rows, :] * (_head(qm, h)[b:b + 1] * mem_scale)
            sc = jnp.sum(prod, axis=-1, keepdims=True)
            e = jnp.exp(sc - jnp.max(sc, axis=0, keepdims=True))
            weighted = jnp.sum(e * cv_ref[b, head_rows, :], axis=0, keepdims=True)
            att_heads.append(weighted / jnp.sum(e, axis=0, keepdims=True))
        att_rows.append(jnp.concatenate(att_heads, axis=-1))
    return o_rows, att_rows


def _sample_out_kernel(x_ref, z_ref, o_ref, att_ref, hist_ref, conv_w_ref, gn_w_ref, conv_b_ref,
                       ln_w_ref, ln_b_ref, w_br_ret_ref, w_br_conv_ref, w_br_mem_ref, w_out_ref,
                       final_w_ref, y_ref, hist_new_ref):
    gate = lambda off: _silu(z_ref[:, off:off + BRANCH_W])
    yr = gate(OFF_GR) * _group_norm_heads(o_ref[...], gn_w_ref[...])

    u = z_ref[:, OFF_AC:OFF_AC + BRANCH_W] * _sigmoid(z_ref[:, OFF_BC:OFF_BC + BRANCH_W])
    conv = u * conv_w_ref[0, CONV_HIST:CONV_K, :] + conv_b_ref[...]
    for j in range(CONV_HIST):
        conv = conv + hist_ref[j] * conv_w_ref[0, j:j + 1, :]
        hist_new_ref[j] = hist_ref[j + 1] if j + 1 < CONV_HIST else u
    cn = _layernorm_rows(conv) * ln_w_ref[...] + ln_b_ref[...]
    yc = gate(OFF_GC) * _silu(cn)

    ym = gate(OFF_GM) * att_ref[...]
    gate_pre = lambda i: z_ref[:, OFF_MERGE + i * D_MODEL:OFF_MERGE + (i + 1) * D_MODEL]
    y_ref[:, 0, :] = _merge_and_project(x_ref[:, 0, :], gate_pre, yr, yc, ym, w_br_ret_ref,
                                        w_br_conv_ref, w_br_mem_ref, w_out_ref, final_w_ref[...])


SAMPLE_OUT_BLOCK = 32


def _sample_out(xs, z, o, att, hist, conv_w, gn_w, conv_b, ln_w, ln_b,
                w_br_ret_bf, w_br_conv_bf, w_br_mem_bf, w_out_bf, final_w):
    n = xs.shape[0]
    nb = SAMPLE_OUT_BLOCK
    rows = lambda w: pl.BlockSpec((nb, w), lambda i: (i, 0))
    hist_spec = pl.BlockSpec((CONV_HIST, nb, BRANCH_W), lambda i: (0, i, 0))
    tokens = pl.BlockSpec((nb, 1, D_MODEL), lambda i: (i, 0, 0))
    const = lambda a: pl.BlockSpec(a.shape, lambda i: (0,) * a.ndim)
    weight = lambda a: pl.BlockSpec(a.shape, lambda i: (0,) * a.ndim,
                                    pipeline_mode=pl.Buffered(1))
    return pl.pallas_call(
        _sample_out_kernel,
        grid=(n // nb,),
        in_specs=[tokens, rows(IN_W), rows(BRANCH_W), rows(BRANCH_W), hist_spec,
                  const(conv_w), const(gn_w), const(conv_b), const(ln_w), const(ln_b),
                  weight(w_br_ret_bf), weight(w_br_conv_bf), weight(w_br_mem_bf),
                  weight(w_out_bf), const(final_w)],
        out_specs=[tokens, hist_spec],
        out_shape=[jax.ShapeDtypeStruct(xs.shape, F32), jax.ShapeDtypeStruct(hist.shape, F32)],
        compiler_params=pltpu.CompilerParams(
            dimension_semantics=("arbitrary",), vmem_limit_bytes=VMEM_LIMIT_BYTES),
        name="sample_out",
    )(xs, z, o, att, hist, conv_w, gn_w, conv_b, ln_w, ln_b,
      w_br_ret_bf, w_br_conv_bf, w_br_mem_bf, w_out_bf, final_w)


def _rotary_tables(pos):
    half = HEAD_DIM // 2
    inv = ROPE_BASE ** (-np.arange(half, dtype=np.float64) / half)
    ang = np.asarray(pos, np.float64)[:, None] * inv[None, :]
    cos, sin = np.cos(ang), np.sin(ang)
    table = lambda a, b: np.concatenate([a, b], axis=-1).astype(np.float32)
    return table(cos, cos), table(-sin, sin)


def _decay_tables():
    c = RET_CHUNK
    log_g = np.log1p(-np.exp2(-5.0 - np.arange(RET_HEADS, dtype=np.float64)))
    idx = np.arange(c, dtype=np.float64)
    diff = idx[:, None] - idx[None, :]
    intra = np.where(diff >= 0, np.exp(np.maximum(diff, 0.0)[None] * log_g[:, None, None]), 0.0)
    q_dec = np.exp((idx + 1.0)[None, :] * log_g[:, None])
    k_dec = np.exp((c - 1.0 - idx)[None, :] * log_g[:, None])
    bcast = lambda a: np.ascontiguousarray(
        np.broadcast_to(a[:, :, None], (RET_HEADS, c, HEAD_DIM))).astype(np.float32)
    return intra.astype(np.float32), bcast(q_dec), bcast(k_dec)


def _gamma_powers(n):
    return tuple(float(np.exp(np.log1p(-np.exp2(-5.0 - h)) * n)) for h in range(RET_HEADS))


def kernel(x_prompt, x_sample, mem_prompt, state_ret, state_conv, cache_mem_k, cache_mem_v,
           norm_w, w_in, ret_gn_w, conv_w, conv_b, conv_ln_w, conv_ln_b, mem_norm_w,
           w_mem_kv, w_br_ret, w_br_conv, w_br_mem, w_out, final_norm_w):
    depth = w_in.shape[0]
    assert depth == 1, "single-layer step"
    batch, seq, _ = x_prompt.shape
    n_dec, dec_seq, _ = x_sample.shape
    assert dec_seq == 1 and seq % PROMPT_TILE == 0
    n_mem = mem_prompt.shape[1]

    final_w = final_norm_w[None, :]

    intra, qdec, kdec = _decay_tables()
    cos_p, sin_p = _rotary_tables(np.arange(seq))
    cos_s, sin_s = _rotary_tables(np.arange(dec_seq) + PAST_LEN)

    z_s, w_in_bf = _sample_proj(x_sample, norm_w, w_in)

    mk, mv, mk_bf, mv_bf, w_br_ret_bf, w_br_conv_bf, w_br_mem_bf, w_out_bf = _mem_kv(
        mem_prompt, mem_norm_w, w_mem_kv, (w_br_ret, w_br_conv, w_br_mem, w_out))

    y_prompt, s_prompt, hist_prompt, o_s, att_s, s_sample = _prompt_layer(
        x_prompt, cos_p, sin_p, mk_bf, mv_bf, norm_w, w_in_bf, ret_gn_w, conv_w, conv_b,
        conv_ln_w, conv_ln_b, w_br_ret_bf, w_br_conv_bf, w_br_mem_bf, w_out_bf, final_w,
        intra, qdec, kdec, _gamma_powers(RET_CHUNK),
        z_s, cos_s, sin_s, state_ret[0],
        cache_mem_k.reshape(n_dec, n_mem * MEM_HEADS, HEAD_DIM),
        cache_mem_v.reshape(n_dec, n_mem * MEM_HEADS, HEAD_DIM), _gamma_powers(1))

    y_sample, hist_sample = _sample_out(
        x_sample, z_s, o_s, att_s,
        jnp.transpose(state_conv[0], (1, 0, 2)), conv_w, ret_gn_w, conv_b, conv_ln_w, conv_ln_b,
        w_br_ret_bf, w_br_conv_bf, w_br_mem_bf, w_out_bf, final_w)

    heads = lambda a: a.reshape(1, batch, n_mem, MEM_HEADS, HEAD_DIM)
    return (y_prompt, y_sample, s_prompt[None], s_sample[None],
            jnp.transpose(hist_prompt, (1, 0, 2))[None],
            jnp.transpose(hist_sample, (1, 0, 2))[None], heads(mk), heads(mv))
```

```python
import functools

import numpy as np
import jax
import jax.numpy as jnp
from jax import lax
from jax.experimental import pallas as pl
from jax.experimental.pallas import tpu as pltpu

F32 = jnp.float32
BF16 = jnp.bfloat16

D_MODEL = 1024
RET_HEADS = 4
HEAD_DIM = 128
SUBLANES = 8
BRANCH_W = 512
RET_CHUNK = 128
ROPE_BASE = 10000.0
CONV_K = 31
CONV_HIST = CONV_K - 1
MEM_HEADS = 4
N_BRANCH = 3
EPS = 1e-6
PAST_LEN = 16384

OFF_QR, OFF_KR, OFF_VR, OFF_GR = 0, 512, 1024, 1536
OFF_AC, OFF_BC, OFF_GC = 2048, 2560, 3072
OFF_QM, OFF_GM = 3584, 4096
OFF_MERGE = 4608
IN_W = OFF_MERGE + N_BRANCH * D_MODEL

V7X_VMEM_BYTES = 64 * 1024 * 1024
VMEM_LIMIT_BYTES = V7X_VMEM_BYTES - 8 * 1024 * 1024

PROMPT_TILE = 256
HIST_PAD = 32


def _sigmoid(x):
    return 0.5 * jnp.tanh(0.5 * x) + 0.5


def _silu(x):
    hx = 0.5 * x
    return hx * (jnp.tanh(hx) + 1.0)


def _rms_scale(x):
    return lax.rsqrt(jnp.mean(x * x, axis=-1, keepdims=True) + EPS)


def _layernorm_rows(x):
    mu = jnp.mean(x, axis=-1, keepdims=True)
    xc = x - mu
    var = jnp.mean(xc * xc, axis=-1, keepdims=True)
    return xc * lax.rsqrt(var + EPS)


def _rotary(x, cos2, sin2):
    return x * cos2 + pltpu.roll(x, HEAD_DIM // 2, axis=1) * sin2


def _dot(a, b):
    return jnp.dot(a, b, preferred_element_type=F32)


def _dot_nt(a, b):
    return lax.dot_general(a, b, (((1,), (1,)), ((), ())), preferred_element_type=F32)


def _dot_tn(a, b):
    return lax.dot_general(a, b, (((0,), (0,)), ((), ())), preferred_element_type=F32)


def _head(x, h):
    return x[:, h * HEAD_DIM:(h + 1) * HEAD_DIM]


def _group_norm_heads(o, gn_w):
    parts = [_layernorm_rows(_head(o, h)) for h in range(RET_HEADS)]
    return jnp.concatenate(parts, axis=-1) * gn_w


def _merge_and_project(x, gate_pre, yr, yc, ym, w_br_ret_ref, w_br_conv_ref,
                       w_br_mem_ref, w_out_ref, final_w):
    branches = (
        _dot(yr.astype(BF16), w_br_ret_ref[...]),
        _dot(yc.astype(BF16), w_br_conv_ref[...]),
        _dot(ym.astype(BF16), w_br_mem_ref[...]),
    )
    merged = None
    for i, br in enumerate(branches):
        term = _sigmoid(gate_pre(i)) * br
        merged = term if merged is None else merged + term
    out = x + _dot(merged.astype(BF16), w_out_ref[...])
    return out * _rms_scale(out) * final_w


MEM_KV_BLOCK = 4


def _mem_kv_kernel(n_cast, mem_ref, norm_w_ref, w_ref, *refs):
    cast_in, (k_ref, v_ref, kbf_ref, vbf_ref), cast_out = (
        refs[:n_cast], refs[n_cast:n_cast + 4], refs[n_cast + 4:])

    @pl.when(pl.program_id(0) == 0)
    def _():
        for src, dst in zip(cast_in, cast_out):
            dst[...] = src[0].astype(BF16)

    nb, n_mem, _ = mem_ref.shape
    m = mem_ref[...].reshape(nb * n_mem, D_MODEL)
    hm = (m * _rms_scale(m) * norm_w_ref[...]).astype(BF16)
    kv = _dot(hm, w_ref[0].astype(BF16))
    for b in range(nb):
        k = kv[b * n_mem:(b + 1) * n_mem, :BRANCH_W]
        v = kv[b * n_mem:(b + 1) * n_mem, BRANCH_W:]
        for h in range(MEM_HEADS):
            k_ref[b, pl.ds(h, n_mem, stride=MEM_HEADS), :] = _head(k, h)
            v_ref[b, pl.ds(h, n_mem, stride=MEM_HEADS), :] = _head(v, h)
        kbf_ref[b] = k.astype(BF16)
        vbf_ref[b] = v.astype(BF16)


def _mem_kv(mem, mem_norm_w, w_mem_kv, to_cast):
    batch, n_mem, _ = mem.shape
    nb = MEM_KV_BLOCK
    blk = lambda dt: jax.ShapeDtypeStruct((batch, n_mem, BRANCH_W), dt)
    out_spec = pl.BlockSpec((nb, n_mem, BRANCH_W), lambda b: (b, 0, 0))
    rows_shape = (batch, n_mem * MEM_HEADS, HEAD_DIM)
    rows_spec = pl.BlockSpec((nb, n_mem * MEM_HEADS, HEAD_DIM), lambda b: (b, 0, 0))
    return pl.pallas_call(
        functools.partial(_mem_kv_kernel, len(to_cast)),
        grid=(batch // nb,),
        in_specs=[
            pl.BlockSpec((nb, n_mem, D_MODEL), lambda b: (b, 0, 0)),
            pl.BlockSpec((1, D_MODEL), lambda b: (0, 0)),
            pl.BlockSpec((1, D_MODEL, 2 * BRANCH_W), lambda b: (0, 0, 0)),
        ] + [pl.BlockSpec(w.shape, lambda b: (0, 0, 0), pipeline_mode=pl.Buffered(1))
             for w in to_cast],
        out_specs=[rows_spec, rows_spec, out_spec, out_spec]
                  + [pl.BlockSpec(w.shape[1:], lambda b: (0, 0)) for w in to_cast],
        out_shape=[jax.ShapeDtypeStruct(rows_shape, F32), jax.ShapeDtypeStruct(rows_shape, F32),
                   blk(BF16), blk(BF16)]
                  + [jax.ShapeDtypeStruct(w.shape[1:], BF16) for w in to_cast],
        compiler_params=pltpu.CompilerParams(
            dimension_semantics=("arbitrary",), vmem_limit_bytes=VMEM_LIMIT_BYTES),
        name="mem_kv",
    )(mem, mem_norm_w, w_mem_kv, *to_cast)


def _prompt_kernel(c_dec, gamma, tiles_per_seq, n_tiles, *refs):
    (x_ref, xb_ref, cos_ref, sin_ref, mk_ref, mv_ref, norm_w_ref, w_in_ref,
     gn_w_ref, conv_w_ref, conv_b_ref, ln_w_ref, ln_b_ref,
     w_br_ret_ref, w_br_conv_ref, w_br_mem_ref, w_out_ref, final_w_ref,
     intra_ref, qdec_ref, kdec_ref,
     zs_ref, cos_s_ref, sin_s_ref, s0_ref, ck_ref, cv_ref,
     y_ref, s_ref, hist_ref, os_ref, atts_ref, s_new_ref,
     ubuf_ref, h_scr, yr_scr, yc_scr, ym_scr, os_scr, atts_scr) = refs
    i = pl.program_id(0)

    @pl.when(i == 0)
    def _():
        h_scr[...] = jnp.zeros_like(h_scr)
        yr_scr[...] = jnp.zeros_like(yr_scr)
        yc_scr[...] = jnp.zeros_like(yc_scr)
        ym_scr[...] = jnp.zeros_like(ym_scr)
        os_scr[...] = jnp.zeros_like(os_scr)
        atts_scr[...] = jnp.zeros_like(atts_scr)
        hist_ref[...] = jnp.zeros_like(hist_ref)

    @pl.when(i < n_tiles)
    def _():
        _prompt_step(c_dec, gamma, tiles_per_seq, *refs)

    @pl.when(i == n_tiles)
    def _():
        gate = lambda b: _dot(h_scr[...], w_in_ref[:, OFF_MERGE + b * D_MODEL:
                                                   OFF_MERGE + (b + 1) * D_MODEL])
        y_ref[0] = _merge_and_project(
            xb_ref[0], gate, yr_scr[...], yc_scr[...], ym_scr[...],
            w_br_ret_ref, w_br_conv_ref, w_br_mem_ref, w_out_ref, final_w_ref[...])


def _prompt_step(c_dec, gamma, tiles_per_seq,
                 x_ref, xb_ref, cos_ref, sin_ref, mk_ref, mv_ref, norm_w_ref, w_in_ref,
                 gn_w_ref, conv_w_ref, conv_b_ref, ln_w_ref, ln_b_ref,
                 w_br_ret_ref, w_br_conv_ref, w_br_mem_ref, w_out_ref, final_w_ref,
                 intra_ref, qdec_ref, kdec_ref,
                 zs_ref, cos_s_ref, sin_s_ref, s0_ref, ck_ref, cv_ref,
                 y_ref, s_ref, hist_ref, os_ref, atts_ref, s_new_ref,
                 ubuf_ref, h_scr, yr_scr, yc_scr, ym_scr, os_scr, atts_scr):
    i = pl.program_id(0)
    t = lax.rem(i, tiles_per_seq)
    tile = x_ref.shape[1]

    @pl.when(t == 0)
    def _():
        s_ref[...] = jnp.zeros_like(s_ref)
        ubuf_ref[0:HIST_PAD, :] = jnp.zeros((HIST_PAD, BRANCH_W), F32)

    h_prev = h_scr[...]
    half = D_MODEL // 2

    def gate_job(k):
        lo = OFF_MERGE + k * half
        return lambda: _dot(h_prev, w_in_ref[:, lo:lo + half])

    back_jobs = [gate_job(k) for k in range(2 * N_BRANCH)] + [
        lambda: _dot(yr_scr[...], w_br_ret_ref[...]),
        lambda: _dot(yc_scr[...], w_br_conv_ref[...]),
        lambda: _dot(ym_scr[...], w_br_mem_ref[...]),
    ]
    back_out = [None] * len(back_jobs)

    def run_back(k):
        back_out[k] = back_jobs[k]()

    x = x_ref[0]
    h_bf = (x * _rms_scale(x) * norm_w_ref[...]).astype(BF16)
    run_back(8)

    def proj(off, width=BRANCH_W):
        return _dot(h_bf, w_in_ref[:, off:off + width])

    za = proj(OFF_AC)
    zb = proj(OFF_BC)
    zq = proj(OFF_QR)
    zk = proj(OFF_KR)
    v_bf = proj(OFF_VR).astype(BF16)
    u = za * _sigmoid(zb)
    ubuf_ref[HIST_PAD:HIST_PAD + tile, :] = u
    first = HIST_PAD - CONV_HIST
    conv = None
    for r in range(8):
        part = None
        for j in range(CONV_K):
            if (first + j) % 8 != r:
                continue
            term = ubuf_ref[first + j:first + j + tile, :] * conv_w_ref[0, j:j + 1, :]
            part = term if part is None else part + term
        conv = part if conv is None else conv + part
        run_back(r)
    conv = conv + conv_b_ref[...]
    ubuf_ref[0:HIST_PAD, :] = ubuf_ref[tile:tile + HIST_PAD, :]

    z_gc = proj(OFF_GC)
    z_gr = proj(OFF_GR)
    z_qm = proj(OFF_QM)
    z_gm = proj(OFF_GM)

    merged = []
    for k in range(2):
        cols = slice(k * half, (k + 1) * half)
        acc = None
        for b in range(N_BRANCH):
            term = _sigmoid(back_out[2 * b + k]) * back_out[2 * N_BRANCH + b][:, cols]
            acc = term if acc is None else acc + term
        merged.append(acc.astype(BF16))
    merged = jnp.concatenate(merged, axis=-1)

    cos2 = cos_ref[...]
    sin2 = sin_ref[...]
    q_scale = HEAD_DIM ** -0.5
    def block_diag(a, b):
        zero = jnp.zeros_like(a)
        return jnp.concatenate([jnp.concatenate([a, zero], axis=-1),
                                jnp.concatenate([zero, b], axis=-1)], axis=0)

    o_heads = []
    for h0 in range(0, RET_HEADS, 2):
        pair = (h0, h0 + 1)
        q_p = [_rotary(_head(zq, h), cos2, sin2) * q_scale for h in pair]
        k_p = [_rotary(_head(zk, h), cos2, sin2) for h in pair]
        v_p = [_head(v_bf, h) for h in pair]
        s_old = [s_ref[0, h] for h in pair]
        s = list(s_old)
        intra2 = jnp.concatenate([intra_ref[h] for h in pair], axis=-1)
        qdec2 = jnp.concatenate([qdec_ref[h] for h in pair], axis=-1)
        o_chunks = []
        for c in range(tile // RET_CHUNK):
            rows = slice(c * RET_CHUNK, (c + 1) * RET_CHUNK)
            kc = [k[rows] for k in k_p]
            vc = [v[rows] for v in v_p]
            qc2 = jnp.concatenate([q[rows] for q in q_p], axis=-1).astype(BF16)
            sc2 = _dot_nt(qc2, block_diag(*[k.astype(BF16) for k in kc])) * intra2
            o_chunks.append(_dot(sc2.astype(BF16), block_diag(*vc))
                            + _dot(qc2, block_diag(*[x.astype(BF16) for x in s])) * qdec2)
            for n, h in enumerate(pair):
                kd = (kc[n] * kdec_ref[h]).astype(BF16)
                s[n] = s[n] * c_dec[h] + _dot_tn(kd, vc[n])
        o_pair = jnp.concatenate(o_chunks, axis=0)
        for n, h in enumerate(pair):
            s_ref[0, h] = s[n]
            o_heads.append(_layernorm_rows(_head(o_pair, n)))

    out = xb_ref[0] + _dot(merged, w_out_ref[...])

    ret = jnp.concatenate(o_heads, axis=-1) * gn_w_ref[...]
    yr = _silu(z_gr) * ret
    cn = _layernorm_rows(conv) * ln_w_ref[...] + ln_b_ref[...]
    yc = _silu(z_gc) * _silu(cn)

    qm = z_qm.astype(BF16)
    mk = mk_ref[0]
    mv = mv_ref[0]
    att = []
    for h in range(MEM_HEADS):
        sc = _dot_nt(_head(qm, h), _head(mk, h)) * (HEAD_DIM ** -0.5)
        e = jnp.exp(sc - jnp.max(sc, axis=-1, keepdims=True))
        p = e / jnp.sum(e, axis=-1, keepdims=True)
        att.append(_dot(p.astype(BF16), _head(mv, h)))
    ym = _silu(z_gm) * jnp.concatenate(att, axis=-1)

    y_ref[0] = out * _rms_scale(out) * final_w_ref[...]

    h_scr[...] = h_bf
    yr_scr[...] = yr.astype(BF16)
    yc_scr[...] = yc.astype(BF16)
    ym_scr[...] = ym.astype(BF16)

    @pl.when(t == tiles_per_seq - 1)
    def _():
        seq_id = i // tiles_per_seq
        mine = lax.broadcasted_iota(jnp.int32, (hist_ref.shape[1], 1), 0) == seq_id
        for j in range(CONV_HIST):
            hist_ref[j] = jnp.where(mine, ubuf_ref[first + j:first + j + 1, :], hist_ref[j])

    per_step = s0_ref.shape[0]
    r0 = lax.rem(i, SUBLANES // per_step) * per_step
    row_id = lax.broadcasted_iota(jnp.int32, (SUBLANES, 1), 0)

    def own_rows(off):
        block = zs_ref[:, off:off + BRANCH_W]
        return jnp.concatenate(
            [jnp.sum(jnp.where(row_id == r0 + b, block, 0.0), axis=0, keepdims=True)
             for b in range(per_step)], axis=0)

    o_rows, att_rows = _sample_sequences(
        gamma, own_rows(OFF_QR), own_rows(OFF_KR), own_rows(OFF_VR), own_rows(OFF_QM),
        cos_s_ref[...], sin_s_ref[...], s0_ref, ck_ref, cv_ref, s_new_ref)
    for scr, out_ref, rows in ((os_scr, os_ref, o_rows), (atts_scr, atts_ref, att_rows)):
        block = scr[...]
        for b in range(per_step):
            block = jnp.where(row_id == r0 + b, rows[b], block)
        scr[...] = block
        out_ref[...] = block


def _const_spec(shape, single_buffer=False):
    zeros = (0,) * len(shape)
    if single_buffer:
        return pl.BlockSpec(shape, lambda i: zeros, pipeline_mode=pl.Buffered(1))
    return pl.BlockSpec(shape, lambda i: zeros)


def _prompt_layer(x, cos2, sin2, mk_bf, mv_bf, norm_w, w_in_bf, gn_w, conv_w, conv_b, ln_w,
                  ln_b, w_br_ret_bf, w_br_conv_bf, w_br_mem_bf, w_out_bf, final_w,
                  intra, qdec, kdec, c_dec,
                  z_s, cos_s, sin_s, s0, cache_k, cache_v, gamma):
    batch, seq, _ = x.shape
    n_mem = mk_bf.shape[1]
    tile = PROMPT_TILE
    tiles_per_seq = seq // tile
    n_tiles = batch * tiles_per_seq

    def front(i):
        j = jnp.minimum(i, n_tiles - 1)
        return j // tiles_per_seq, lax.rem(j, tiles_per_seq)

    def back(i):
        j = jnp.maximum(i - 1, 0)
        return j // tiles_per_seq, lax.rem(j, tiles_per_seq)

    row = lambda w: _const_spec((1, w))
    weight = lambda a: _const_spec(a.shape, single_buffer=True)
    table = _const_spec((RET_HEADS, RET_CHUNK, RET_CHUNK))
    n_dec = z_s.shape[0]
    per_step = n_dec // n_tiles
    assert per_step * n_tiles == n_dec
    share = lambda shape: pl.BlockSpec(
        shape, lambda i: (jnp.minimum(i, n_tiles - 1),) + (0,) * (len(shape) - 1))
    assert SUBLANES % per_step == 0
    rows8 = lambda w: pl.BlockSpec(
        (SUBLANES, w), lambda i: (jnp.minimum(i, n_tiles - 1) // (SUBLANES // per_step), 0))
    x_tile = lambda which: pl.BlockSpec((1, tile, D_MODEL), lambda i: (*which(i), 0))
    per_seq = lambda shape: pl.BlockSpec(
        (1,) + shape, lambda i: (front(i)[0],) + (0,) * len(shape))
    return pl.pallas_call(
        functools.partial(_prompt_kernel, c_dec, gamma, tiles_per_seq, n_tiles),
        grid=(n_tiles + 1,),
        in_specs=[
            x_tile(front), x_tile(back),
            pl.BlockSpec((tile, HEAD_DIM), lambda i: (front(i)[1], 0)),
            pl.BlockSpec((tile, HEAD_DIM), lambda i: (front(i)[1], 0)),
            per_seq((n_mem, BRANCH_W)), per_seq((n_mem, BRANCH_W)),
            row(D_MODEL), weight(w_in_bf), row(BRANCH_W),
            _const_spec((1, CONV_K, BRANCH_W)), row(BRANCH_W), row(BRANCH_W), row(BRANCH_W),
            weight(w_br_ret_bf), weight(w_br_conv_bf), weight(w_br_mem_bf), weight(w_out_bf),
            row(D_MODEL), table, table, table,
            rows8(IN_W), row(HEAD_DIM), row(HEAD_DIM),
            share((per_step,) + s0.shape[1:]),
            share((per_step,) + cache_k.shape[1:]), share((per_step,) + cache_v.shape[1:]),
        ],
        out_specs=[
            x_tile(back),
            per_seq((RET_HEADS, HEAD_DIM, HEAD_DIM)),
            _const_spec((CONV_HIST, batch, BRANCH_W)),
            rows8(BRANCH_W), rows8(BRANCH_W),
            share((per_step,) + s0.shape[1:]),
        ],
        out_shape=[
            jax.ShapeDtypeStruct((batch, seq, D_MODEL), F32),
            jax.ShapeDtypeStruct((batch, RET_HEADS, HEAD_DIM, HEAD_DIM), F32),
            jax.ShapeDtypeStruct((CONV_HIST, batch, BRANCH_W), F32),
            jax.ShapeDtypeStruct((n_dec, BRANCH_W), F32),
            jax.ShapeDtypeStruct((n_dec, BRANCH_W), F32),
            jax.ShapeDtypeStruct(s0.shape, F32),
        ],
        scratch_shapes=[
            pltpu.VMEM((HIST_PAD + tile, BRANCH_W), F32),
            pltpu.VMEM((tile, D_MODEL), BF16),
            pltpu.VMEM((tile, BRANCH_W), BF16),
            pltpu.VMEM((tile, BRANCH_W), BF16),
            pltpu.VMEM((tile, BRANCH_W), BF16),
            pltpu.VMEM((SUBLANES, BRANCH_W), F32),
            pltpu.VMEM((SUBLANES, BRANCH_W), F32),
        ],
        compiler_params=pltpu.CompilerParams(
            dimension_semantics=("arbitrary",), vmem_limit_bytes=VMEM_LIMIT_BYTES),
        name="prompt_layer",
    )(x, x, cos2, sin2, mk_bf, mv_bf, norm_w, w_in_bf, gn_w, conv_w, conv_b, ln_w, ln_b,
      w_br_ret_bf, w_br_conv_bf, w_br_mem_bf, w_out_bf, final_w, intra, qdec, kdec,
      z_s, cos_s, sin_s, s0, cache_k, cache_v)


SAMPLE_PROJ_BLOCK = 768


def _sample_proj_kernel(x_ref, norm_w_ref, w_ref, z_ref, w_bf_ref):
    x = x_ref[:, 0, :]
    h_bf = (x * _rms_scale(x) * norm_w_ref[...]).astype(BF16)
    w_bf = w_ref[0].astype(BF16)
    w_bf_ref[...] = w_bf
    z_ref[...] = _dot(h_bf, w_bf)


def _sample_proj(xs, norm_w, w_in):
    n = xs.shape[0]
    return pl.pallas_call(
        _sample_proj_kernel,
        grid=(IN_W // SAMPLE_PROJ_BLOCK,),
        in_specs=[
            pl.BlockSpec((n, 1, D_MODEL), lambda j: (0, 0, 0)),
            pl.BlockSpec((1, D_MODEL), lambda j: (0, 0)),
            pl.BlockSpec((1, D_MODEL, SAMPLE_PROJ_BLOCK), lambda j: (0, 0, j)),
        ],
        out_specs=[pl.BlockSpec((n, SAMPLE_PROJ_BLOCK), lambda j: (0, j)),
                   pl.BlockSpec((D_MODEL, SAMPLE_PROJ_BLOCK), lambda j: (0, j))],
        out_shape=[jax.ShapeDtypeStruct((n, IN_W), F32),
                   jax.ShapeDtypeStruct((D_MODEL, IN_W), BF16)],
        compiler_params=pltpu.CompilerParams(
            dimension_semantics=("arbitrary",), vmem_limit_bytes=VMEM_LIMIT_BYTES),
        name="sample_proj",
    )(xs, norm_w, w_in)


def _sample_sequences(gamma, zq, zk, v, qm, cos2, sin2, s0_ref, ck_ref, cv_ref, s_new_ref):
    n = zq.shape[0]
    n_mem = ck_ref.shape[1] // MEM_HEADS
    q_scale = HEAD_DIM ** -0.5
    mem_scale = HEAD_DIM ** -0.5
    q = [_rotary(_head(zq, h), cos2, sin2) * q_scale for h in range(RET_HEADS)]
    k = [_rotary(_head(zk, h), cos2, sin2) for h in range(RET_HEADS)]

    eye = (lax.broadcasted_iota(jnp.int32, (HEAD_DIM, HEAD_DIM), 0)
           == lax.broadcasted_iota(jnp.int32, (HEAD_DIM, HEAD_DIM), 1))
    ones_bf = jnp.ones((HEAD_DIM, HEAD_DIM), BF16)

    def as_columns(row):
        diag = jnp.where(eye, jnp.broadcast_to(row, (HEAD_DIM, HEAD_DIM)), 0.0)
        return _dot(diag.astype(BF16), ones_bf)

    o_rows, att_rows = [], []
    for b in range(n):
        o_heads = []
        for h in range(RET_HEADS):
            q_row = q[h][b:b + 1]
            k_row = k[h][b:b + 1]
            v_row = _head(v, h)[b:b + 1]
            s0 = s0_ref[b, h]
            qk = jnp.sum(q_row * k_row, axis=-1, keepdims=True)
            q_lhs = jnp.broadcast_to(q_row, (SUBLANES, HEAD_DIM)).astype(BF16)
            qs = _dot(q_lhs, s0.astype(BF16))[0:1]
            o_heads.append(qk * v_row + gamma[h] * qs)
            s_new_ref[b, h] = gamma[h] * s0 + as_columns(k_row) * v_row
        o_rows.append(jnp.concatenate(o_heads, axis=-1))

        att_heads = []
        for h in range(MEM_HEADS):
            head_rows = pl.ds(h, n_mem, stride=MEM_HEADS)
            prod = ck_ref[b, head_rows, :] * (_head(qm, h)[b:b + 1] * mem_scale)
            sc = jnp.sum(prod, axis=-1, keepdims=True)
            e = jnp.exp(sc - jnp.max(sc, axis=0, keepdims=True))
            weighted = jnp.sum(e * cv_ref[b, head_rows, :], axis=0, keepdims=True)
            att_heads.append(weighted / jnp.sum(e, axis=0, keepdims=True))
        att_rows.append(jnp.concatenate(att_heads, axis=-1))
    return o_rows, att_rows


def _sample_out_kernel(x_ref, z_ref, o_ref, att_ref, hist_ref, conv_w_ref, gn_w_ref, conv_b_ref,
                       ln_w_ref, ln_b_ref, w_br_ret_ref, w_br_conv_ref, w_br_mem_ref, w_out_ref,
                       final_w_ref, y_ref, hist_new_ref):
    gate = lambda off: _silu(z_ref[:, off:off + BRANCH_W])
    yr = gate(OFF_GR) * _group_norm_heads(o_ref[...], gn_w_ref[...])

    u = z_ref[:, OFF_AC:OFF_AC + BRANCH_W] * _sigmoid(z_ref[:, OFF_BC:OFF_BC + BRANCH_W])
    conv = u * conv_w_ref[0, CONV_HIST:CONV_K, :] + conv_b_ref[...]
    for j in range(CONV_HIST):
        conv = conv + hist_ref[j] * conv_w_ref[0, j:j + 1, :]
        hist_new_ref[j] = hist_ref[j + 1] if j + 1 < CONV_HIST else u
    cn = _layernorm_rows(conv) * ln_w_ref[...] + ln_b_ref[...]
    yc = gate(OFF_GC) * _silu(cn)

    ym = gate(OFF_GM) * att_ref[...]
    gate_pre = lambda i: z_ref[:, OFF_MERGE + i * D_MODEL:OFF_MERGE + (i + 1) * D_MODEL]
    y_ref[:, 0, :] = _merge_and_project(x_ref[:, 0, :], gate_pre, yr, yc, ym, w_br_ret_ref,
                                        w_br_conv_ref, w_br_mem_ref, w_out_ref, final_w_ref[...])


SAMPLE_OUT_BLOCK = 32


def _sample_out(xs, z, o, att, hist, conv_w, gn_w, conv_b, ln_w, ln_b,
                w_br_ret_bf, w_br_conv_bf, w_br_mem_bf, w_out_bf, final_w):
    n = xs.shape[0]
    nb = SAMPLE_OUT_BLOCK
    rows = lambda w: pl.BlockSpec((nb, w), lambda i: (i, 0))
    hist_spec = pl.BlockSpec((CONV_HIST, nb, BRANCH_W), lambda i: (0, i, 0))
    tokens = pl.BlockSpec((nb, 1, D_MODEL), lambda i: (i, 0, 0))
    const = lambda a: pl.BlockSpec(a.shape, lambda i: (0,) * a.ndim)
    weight = lambda a: pl.BlockSpec(a.shape, lambda i: (0,) * a.ndim,
                                    pipeline_mode=pl.Buffered(1))
    return pl.pallas_call(
        _sample_out_kernel,
        grid=(n // nb,),
        in_specs=[tokens, rows(IN_W), rows(BRANCH_W), rows(BRANCH_W), hist_spec,
                  const(conv_w), const(gn_w), const(conv_b), const(ln_w), const(ln_b),
                  weight(w_br_ret_bf), weight(w_br_conv_bf), weight(w_br_mem_bf),
                  weight(w_out_bf), const(final_w)],
        out_specs=[tokens, hist_spec],
        out_shape=[jax.ShapeDtypeStruct(xs.shape, F32), jax.ShapeDtypeStruct(hist.shape, F32)],
        compiler_params=pltpu.CompilerParams(
            dimension_semantics=("arbitrary",), vmem_limit_bytes=VMEM_LIMIT_BYTES),
        name="sample_out",
    )(xs, z, o, att, hist, conv_w, gn_w, conv_b, ln_w, ln_b,
      w_br_ret_bf, w_br_conv_bf, w_br_mem_bf, w_out_bf, final_w)


def _rotary_tables(pos):
    half = HEAD_DIM // 2
    inv = ROPE_BASE ** (-np.arange(half, dtype=np.float64) / half)
    ang = np.asarray(pos, np.float64)[:, None] * inv[None, :]
    cos, sin = np.cos(ang), np.sin(ang)
    table = lambda a, b: np.concatenate([a, b], axis=-1).astype(np.float32)
    return table(cos, cos), table(-sin, sin)


def _decay_tables():
    c = RET_CHUNK
    log_g = np.log1p(-np.exp2(-5.0 - np.arange(RET_HEADS, dtype=np.float64)))
    idx = np.arange(c, dtype=np.float64)
    diff = idx[:, None] - idx[None, :]
    intra = np.where(diff >= 0, np.exp(np.maximum(diff, 0.0)[None] * log_g[:, None, None]), 0.0)
    q_dec = np.exp((idx + 1.0)[None, :] * log_g[:, None])
    k_dec = np.exp((c - 1.0 - idx)[None, :] * log_g[:, None])
    bcast = lambda a: np.ascontiguousarray(
        np.broadcast_to(a[:, :, None], (RET_HEADS, c, HEAD_DIM))).astype(np.float32)
    return intra.astype(np.float32), bcast(q_dec), bcast(k_dec)


def _gamma_powers(n):
    return tuple(float(np.exp(np.log1p(-np.exp2(-5.0 - h)) * n)) for h in range(RET_HEADS))


def kernel(x_prompt, x_sample, mem_prompt, state_ret, state_conv, cache_mem_k, cache_mem_v,
           norm_w, w_in, ret_gn_w, conv_w, conv_b, conv_ln_w, conv_ln_b, mem_norm_w,
           w_mem_kv, w_br_ret, w_br_conv, w_br_mem, w_out, final_norm_w):
    depth = w_in.shape[0]
    assert depth == 1, "single-layer step"
    batch, seq, _ = x_prompt.shape
    n_dec, dec_seq, _ = x_sample.shape
    assert dec_seq == 1 and seq % PROMPT_TILE == 0
    n_mem = mem_prompt.shape[1]

    final_w = final_norm_w[None, :]

    intra, qdec, kdec = _decay_tables()
    cos_p, sin_p = _rotary_tables(np.arange(seq))
    cos_s, sin_s = _rotary_tables(np.arange(dec_seq) + PAST_LEN)

    z_s, w_in_bf = _sample_proj(x_sample, norm_w, w_in)

    mk, mv, mk_bf, mv_bf, w_br_ret_bf, w_br_conv_bf, w_br_mem_bf, w_out_bf = _mem_kv(
        mem_prompt, mem_norm_w, w_mem_kv, (w_br_ret, w_br_conv, w_br_mem, w_out))

    y_prompt, s_prompt, hist_prompt, o_s, att_s, s_sample = _prompt_layer(
        x_prompt, cos_p, sin_p, mk_bf, mv_bf, norm_w, w_in_bf, ret_gn_w, conv_w, conv_b,
        conv_ln_w, conv_ln_b, w_br_ret_bf, w_br_conv_bf, w_br_mem_bf, w_out_bf, final_w,
        intra, qdec, kdec, _gamma_powers(RET_CHUNK),
        z_s, cos_s, sin_s, state_ret[0],
        cache_mem_k.reshape(n_dec, n_mem * MEM_HEADS, HEAD_DIM),
        cache_mem_v.reshape(n_dec, n_mem * MEM_HEADS, HEAD_DIM), _gamma_powers(1))

    y_sample, hist_sample = _sample_out(
        x_sample, z_s, o_s, att_s,
        jnp.transpose(state_conv[0], (1, 0, 2)), conv_w, ret_gn_w, conv_b, conv_ln_w, conv_ln_b,
        w_br_ret_bf, w_br_conv_bf, w_br_mem_bf, w_out_bf, final_w)

    heads = lambda a: a.reshape(1, batch, n_mem, MEM_HEADS, HEAD_DIM)
    return (y_prompt, y_sample, s_prompt[None], s_sample[None],
            jnp.transpose(hist_prompt, (1, 0, 2))[None],
            jnp.transpose(hist_sample, (1, 0, 2))[None], heads(mk), heads(mv))
```

```python
import functools

import numpy as np
import jax
import jax.numpy as jnp
from jax import lax
from jax.experimental import pallas as pl
from jax.experimental.pallas import tpu as pltpu

F32 = jnp.float32
BF16 = jnp.bfloat16

D_MODEL = 1024
RET_HEADS = 4
HEAD_DIM = 128
SUBLANES = 8
BRANCH_W = 512
RET_CHUNK = 128
ROPE_BASE = 10000.0
CONV_K = 31
CONV_HIST = CONV_K - 1
MEM_HEADS = 4
N_BRANCH = 3
EPS = 1e-6
PAST_LEN = 16384

OFF_QR, OFF_KR, OFF_VR, OFF_GR = 0, 512, 1024, 1536
OFF_AC, OFF_BC, OFF_GC = 2048, 2560, 3072
OFF_QM, OFF_GM = 3584, 4096
OFF_MERGE = 4608
IN_W = OFF_MERGE + N_BRANCH * D_MODEL

V7X_VMEM_BYTES = 64 * 1024 * 1024
VMEM_LIMIT_BYTES = V7X_VMEM_BYTES - 8 * 1024 * 1024

PROMPT_TILE = 256
HIST_PAD = 32


def _sigmoid(x):
    return 0.5 * jnp.tanh(0.5 * x) + 0.5


def _silu(x):
    hx = 0.5 * x
    return hx * (jnp.tanh(hx) + 1.0)


def _rms_scale(x):
    return lax.rsqrt(jnp.mean(x * x, axis=-1, keepdims=True) + EPS)


def _layernorm_rows(x):
    mu = jnp.mean(x, axis=-1, keepdims=True)
    xc = x - mu
    var = jnp.mean(xc * xc, axis=-1, keepdims=True)
    return xc * lax.rsqrt(var + EPS)


def _rotary(x, cos2, sin2):
    return x * cos2 + pltpu.roll(x, HEAD_DIM // 2, axis=1) * sin2


def _dot(a, b):
    return jnp.dot(a, b, preferred_element_type=F32)


def _dot_nt(a, b):
    return lax.dot_general(a, b, (((1,), (1,)), ((), ())), preferred_element_type=F32)


def _dot_tn(a, b):
    return lax.dot_general(a, b, (((0,), (0,)), ((), ())), preferred_element_type=F32)


def _head(x, h):
    return x[:, h * HEAD_DIM:(h + 1) * HEAD_DIM]


def _group_norm_heads(o, gn_w):
    parts = [_layernorm_rows(_head(o, h)) for h in range(RET_HEADS)]
    return jnp.concatenate(parts, axis=-1) * gn_w


def _merge_and_project(x, gate_pre, yr, yc, ym, w_br_ret_ref, w_br_conv_ref,
                       w_br_mem_ref, w_out_ref, final_w):
    branches = (
        _dot(yr.astype(BF16), w_br_ret_ref[...]),
        _dot(yc.astype(BF16), w_br_conv_ref[...]),
        _dot(ym.astype(BF16), w_br_mem_ref[...]),
    )
    merged = None
    for i, br in enumerate(branches):
        term = _sigmoid(gate_pre(i)) * br
        merged = term if merged is None else merged + term
    out = x + _dot(merged.astype(BF16), w_out_ref[...])
    return out * _rms_scale(out) * final_w


MEM_KV_BLOCK = 4


def _mem_kv_kernel(n_cast, mem_ref, norm_w_ref, w_ref, *refs):
    cast_in, (k_ref, v_ref, kbf_ref, vbf_ref), cast_out = (
        refs[:n_cast], refs[n_cast:n_cast + 4], refs[n_cast + 4:])

    @pl.when(pl.program_id(0) == 0)
    def _():
        for src, dst in zip(cast_in, cast_out):
            dst[...] = src[0].astype(BF16)

    nb, n_mem, _ = mem_ref.shape
    m = mem_ref[...].reshape(nb * n_mem, D_MODEL)
    hm = (m * _rms_scale(m) * norm_w_ref[...]).astype(BF16)
    kv = _dot(hm, w_ref[0].astype(BF16))
    for b in range(nb):
        k = kv[b * n_mem:(b + 1) * n_mem, :BRANCH_W]
        v = kv[b * n_mem:(b + 1) * n_mem, BRANCH_W:]
        for h in range(MEM_HEADS):
            k_ref[b, pl.ds(h, n_mem, stride=MEM_HEADS), :] = _head(k, h)
            v_ref[b, pl.ds(h, n_mem, stride=MEM_HEADS), :] = _head(v, h)
        kbf_ref[b] = k.astype(BF16)
        vbf_ref[b] = v.astype(BF16)


def _mem_kv(mem, mem_norm_w, w_mem_kv, to_cast):
    batch, n_mem, _ = mem.shape
    nb = MEM_KV_BLOCK
    blk = lambda dt: jax.ShapeDtypeStruct((batch, n_mem, BRANCH_W), dt)
    out_spec = pl.BlockSpec((nb, n_mem, BRANCH_W), lambda b: (b, 0, 0))
    rows_shape = (batch, n_mem * MEM_HEADS, HEAD_DIM)
    rows_spec = pl.BlockSpec((nb, n_mem * MEM_HEADS, HEAD_DIM), lambda b: (b, 0, 0))
    return pl.pallas_call(
        functools.partial(_mem_kv_kernel, len(to_cast)),
        grid=(batch // nb,),
        in_specs=[
            pl.BlockSpec((nb, n_mem, D_MODEL), lambda b: (b, 0, 0)),
            pl.BlockSpec((1, D_MODEL), lambda b: (0, 0)),
            pl.BlockSpec((1, D_MODEL, 2 * BRANCH_W), lambda b: (0, 0, 0)),
        ] + [pl.BlockSpec(w.shape, lambda b: (0, 0, 0), pipeline_mode=pl.Buffered(1))
             for w in to_cast],
        out_specs=[rows_spec, rows_spec, out_spec, out_spec]
                  + [pl.BlockSpec(w.shape[1:], lambda b: (0, 0)) for w in to_cast],
        out_shape=[jax.ShapeDtypeStruct(rows_shape, F32), jax.ShapeDtypeStruct(rows_shape, F32),
                   blk(BF16), blk(BF16)]
                  + [jax.ShapeDtypeStruct(w.shape[1:], BF16) for w in to_cast],
        compiler_params=pltpu.CompilerParams(
            dimension_semantics=("arbitrary",), vmem_limit_bytes=VMEM_LIMIT_BYTES),
        name="mem_kv",
    )(mem, mem_norm_w, w_mem_kv, *to_cast)


def _prompt_kernel(c_dec, gamma, tiles_per_seq, n_tiles, *refs):
    (x_ref, xb_ref, cos_ref, sin_ref, mk_ref, mv_ref, norm_w_ref, w_in_ref,
     gn_w_ref, conv_w_ref, conv_b_ref, ln_w_ref, ln_b_ref,
     w_br_ret_ref, w_br_conv_ref, w_br_mem_ref, w_out_ref, final_w_ref,
     intra_ref, qdec_ref, kdec_ref,
     zs_ref, cos_s_ref, sin_s_ref, s0_ref, ck_ref, cv_ref,
     y_ref, s_ref, hist_ref, os_ref, atts_ref, s_new_ref,
     ubuf_ref, h_scr, yr_scr, yc_scr, ym_scr, os_scr, atts_scr) = refs
    i = pl.program_id(0)

    @pl.when(i == 0)
    def _():
        h_scr[...] = jnp.zeros_like(h_scr)
        yr_scr[...] = jnp.zeros_like(yr_scr)
        yc_scr[...] = jnp.zeros_like(yc_scr)
        ym_scr[...] = jnp.zeros_like(ym_scr)
        os_scr[...] = jnp.zeros_like(os_scr)
        atts_scr[...] = jnp.zeros_like(atts_scr)
        hist_ref[...] = jnp.zeros_like(hist_ref)

    @pl.when(i < n_tiles)
    def _():
        _prompt_step(c_dec, gamma, tiles_per_seq, *refs)

    @pl.when(i == n_tiles)
    def _():
        gate = lambda b: _dot(h_scr[...], w_in_ref[:, OFF_MERGE + b * D_MODEL:
                                                   OFF_MERGE + (b + 1) * D_MODEL])
        y_ref[0] = _merge_and_project(
            xb_ref[0], gate, yr_scr[...], yc_scr[...], ym_scr[...],
            w_br_ret_ref, w_br_conv_ref, w_br_mem_ref, w_out_ref, final_w_ref[...])


def _prompt_step(c_dec, gamma, tiles_per_seq,
                 x_ref, xb_ref, cos_ref, sin_ref, mk_ref, mv_ref, norm_w_ref, w_in_ref,
                 gn_w_ref, conv_w_ref, conv_b_ref, ln_w_ref, ln_b_ref,
                 w_br_ret_ref, w_br_conv_ref, w_br_mem_ref, w_out_ref, final_w_ref,
                 intra_ref, qdec_ref, kdec_ref,
                 zs_ref, cos_s_ref, sin_s_ref, s0_ref, ck_ref, cv_ref,
                 y_ref, s_ref, hist_ref, os_ref, atts_ref, s_new_ref,
                 ubuf_ref, h_scr, yr_scr, yc_scr, ym_scr, os_scr, atts_scr):
    i = pl.program_id(0)
    t = lax.rem(i, tiles_per_seq)
    tile = x_ref.shape[1]

    @pl.when(t == 0)
    def _():
        s_ref[...] = jnp.zeros_like(s_ref)
        ubuf_ref[0:HIST_PAD, :] = jnp.zeros((HIST_PAD, BRANCH_W), F32)

    h_prev = h_scr[...]
    half = D_MODEL // 2

    def gate_job(k):
        lo = OFF_MERGE + k * half
        return lambda: _dot(h_prev, w_in_ref[:, lo:lo + half])

    back_jobs = [gate_job(k) for k in range(2 * N_BRANCH)] + [
        lambda: _dot(yr_scr[...], w_br_ret_ref[...]),
        lambda: _dot(yc_scr[...], w_br_conv_ref[...]),
        lambda: _dot(ym_scr[...], w_br_mem_ref[...]),
    ]
    back_out = [None] * len(back_jobs)

    def run_back(k):
        back_out[k] = back_jobs[k]()

    x = x_ref[0]
    h_bf = (x * _rms_scale(x) * norm_w_ref[...]).astype(BF16)
    run_back(8)

    per_step = s0_ref.shape[0]
    r0 = lax.rem(i, SUBLANES // per_step) * per_step
    row_id = lax.broadcasted_iota(jnp.int32, (SUBLANES, 1), 0)

    def own_rows(off):
        block = zs_ref[:, off:off + BRANCH_W]
        return jnp.concatenate(
            [jnp.sum(jnp.where(row_id == r0 + b, block, 0.0), axis=0, keepdims=True)
             for b in range(per_step)], axis=0)

    o_rows, att_rows = _sample_sequences(
        gamma, own_rows(OFF_QR), own_rows(OFF_KR), own_rows(OFF_VR), own_rows(OFF_QM),
        cos_s_ref[...], sin_s_ref[...], s0_ref, ck_ref, cv_ref, s_new_ref)
    for scr, out_ref, rows in ((os_scr, os_ref, o_rows), (atts_scr, atts_ref, att_rows)):
        block = scr[...]
        for b in range(per_step):
            block = jnp.where(row_id == r0 + b, rows[b], block)
        scr[...] = block
        out_ref[...] = block

    def proj(off, width=BRANCH_W):
        return _dot(h_bf, w_in_ref[:, off:off + width])

    za = proj(OFF_AC)
    zb = proj(OFF_BC)
    zq = proj(OFF_QR)
    zk = proj(OFF_KR)
    v_bf = proj(OFF_VR).astype(BF16)
    u = za * _sigmoid(zb)
    ubuf_ref[HIST_PAD:HIST_PAD + tile, :] = u
    first = HIST_PAD - CONV_HIST
    conv = None
    for r in range(8):
        part = None
        for j in range(CONV_K):
            if (first + j) % 8 != r:
                continue
            term = ubuf_ref[first + j:first + j + tile, :] * conv_w_ref[0, j:j + 1, :]
            part = term if part is None else part + term
        conv = part if conv is None else conv + part
        run_back(r)
    conv = conv + conv_b_ref[...]
    ubuf_ref[0:HIST_PAD, :] = ubuf_ref[tile:tile + HIST_PAD, :]

    z_gc = proj(OFF_GC)
    z_gr = proj(OFF_GR)
    z_qm = proj(OFF_QM)
    z_gm = proj(OFF_GM)

    merged = []
    for k in range(2):
        cols = slice(k * half, (k + 1) * half)
        acc = None
        for b in range(N_BRANCH):
            term = _sigmoid(back_out[2 * b + k]) * back_out[2 * N_BRANCH + b][:, cols]
            acc = term if acc is None else acc + term
        merged.append(acc.astype(BF16))
    merged = jnp.concatenate(merged, axis=-1)

    cos2 = cos_ref[...]
    sin2 = sin_ref[...]
    q_scale = HEAD_DIM ** -0.5
    def block_diag(a, b):
        zero = jnp.zeros_like(a)
        return jnp.concatenate([jnp.concatenate([a, zero], axis=-1),
                                jnp.concatenate([zero, b], axis=-1)], axis=0)

    o_heads = []
    for h0 in range(0, RET_HEADS, 2):
        pair = (h0, h0 + 1)
        q_p = [_rotary(_head(zq, h), cos2, sin2) * q_scale for h in pair]
        k_p = [_rotary(_head(zk, h), cos2, sin2) for h in pair]
        v_p = [_head(v_bf, h) for h in pair]
        s_old = [s_ref[0, h] for h in pair]
        s = list(s_old)
        intra2 = jnp.concatenate([intra_ref[h] for h in pair], axis=-1)
        qdec2 = jnp.concatenate([qdec_ref[h] for h in pair], axis=-1)
        o_chunks = []
        for c in range(tile // RET_CHUNK):
            rows = slice(c * RET_CHUNK, (c + 1) * RET_CHUNK)
            kc = [k[rows] for k in k_p]
            vc = [v[rows] for v in v_p]
            qc2 = jnp.concatenate([q[rows] for q in q_p], axis=-1).astype(BF16)
            sc2 = _dot_nt(qc2, block_diag(*[k.astype(BF16) for k in kc])) * intra2
            o_chunks.append(_dot(sc2.astype(BF16), block_diag(*vc))
                            + _dot(qc2, block_diag(*[x.astype(BF16) for x in s])) * qdec2)
            for n, h in enumerate(pair):
                kd = (kc[n] * kdec_ref[h]).astype(BF16)
                s[n] = s[n] * c_dec[h] + _dot_tn(kd, vc[n])
        o_pair = jnp.concatenate(o_chunks, axis=0)
        for n, h in enumerate(pair):
            s_ref[0, h] = s[n]
            o_heads.append(_layernorm_rows(_head(o_pair, n)))

    out = xb_ref[0] + _dot(merged, w_out_ref[...])

    ret = jnp.concatenate(o_heads, axis=-1) * gn_w_ref[...]
    yr = _silu(z_gr) * ret
    cn = _layernorm_rows(conv) * ln_w_ref[...] + ln_b_ref[...]
    yc = _silu(z_gc) * _silu(cn)

    qm = z_qm.astype(BF16)
    mk = mk_ref[0]
    mv = mv_ref[0]
    att = []
    for h in range(MEM_HEADS):
        sc = _dot_nt(_head(qm, h), _head(mk, h)) * (HEAD_DIM ** -0.5)
        e = jnp.exp(sc - jnp.max(sc, axis=-1, keepdims=True))
        p = e / jnp.sum(e, axis=-1, keepdims=True)
        att.append(_dot(p.astype(BF16), _head(mv, h)))
    ym = _silu(z_gm) * jnp.concatenate(att, axis=-1)

    y_ref[0] = out * _rms_scale(out) * final_w_ref[...]

    h_scr[...] = h_bf
    yr_scr[...] = yr.astype(BF16)
    yc_scr[...] = yc.astype(BF16)
    ym_scr[...] = ym.astype(BF16)

    @pl.when(t == tiles_per_seq - 1)
    def _():
        seq_id = i // tiles_per_seq
        mine = lax.broadcasted_iota(jnp.int32, (hist_ref.shape[1], 1), 0) == seq_id
        for j in range(CONV_HIST):
            hist_ref[j] = jnp.where(mine, ubuf_ref[first + j:first + j + 1, :], hist_ref[j])


def _const_spec(shape, single_buffer=False):
    zeros = (0,) * len(shape)
    if single_buffer:
        return pl.BlockSpec(shape, lambda i: zeros, pipeline_mode=pl.Buffered(1))
    return pl.BlockSpec(shape, lambda i: zeros)


def _prompt_layer(x, cos2, sin2, mk_bf, mv_bf, norm_w, w_in_bf, gn_w, conv_w, conv_b, ln_w,
                  ln_b, w_br_ret_bf, w_br_conv_bf, w_br_mem_bf, w_out_bf, final_w,
                  intra, qdec, kdec, c_dec,
                  z_s, cos_s, sin_s, s0, cache_k, cache_v, gamma):
    batch, seq, _ = x.shape
    n_mem = mk_bf.shape[1]
    tile = PROMPT_TILE
    tiles_per_seq = seq // tile
    n_tiles = batch * tiles_per_seq

    def front(i):
        j = jnp.minimum(i, n_tiles - 1)
        return j // tiles_per_seq, lax.rem(j, tiles_per_seq)

    def back(i):
        j = jnp.maximum(i - 1, 0)
        return j // tiles_per_seq, lax.rem(j, tiles_per_seq)

    row = lambda w: _const_spec((1, w))
    weight = lambda a: _const_spec(a.shape, single_buffer=True)
    table = _const_spec((RET_HEADS, RET_CHUNK, RET_CHUNK))
    n_dec = z_s.shape[0]
    per_step = n_dec // n_tiles
    assert per_step * n_tiles == n_dec
    share = lambda shape: pl.BlockSpec(
        shape, lambda i: (jnp.minimum(i, n_tiles - 1),) + (0,) * (len(shape) - 1))
    assert SUBLANES % per_step == 0
    rows8 = lambda w: pl.BlockSpec(
        (SUBLANES, w), lambda i: (jnp.minimum(i, n_tiles - 1) // (SUBLANES // per_step), 0))
    x_tile = lambda which: pl.BlockSpec((1, tile, D_MODEL), lambda i: (*which(i), 0))
    per_seq = lambda shape: pl.BlockSpec(
        (1,) + shape, lambda i: (front(i)[0],) + (0,) * len(shape))
    return pl.pallas_call(
        functools.partial(_prompt_kernel, c_dec, gamma, tiles_per_seq, n_tiles),
        grid=(n_tiles + 1,),
        in_specs=[
            x_tile(front), x_tile(back),
            pl.BlockSpec((tile, HEAD_DIM), lambda i: (front(i)[1], 0)),
            pl.BlockSpec((tile, HEAD_DIM), lambda i: (front(i)[1], 0)),
            per_seq((n_mem, BRANCH_W)), per_seq((n_mem, BRANCH_W)),
            row(D_MODEL), weight(w_in_bf), row(BRANCH_W),
            _const_spec((1, CONV_K, BRANCH_W)), row(BRANCH_W), row(BRANCH_W), row(BRANCH_W),
            weight(w_br_ret_bf), weight(w_br_conv_bf), weight(w_br_mem_bf), weight(w_out_bf),
            row(D_MODEL), table, table, table,
            rows8(IN_W), row(HEAD_DIM), row(HEAD_DIM),
            share((per_step,) + s0.shape[1:]),
            share((per_step,) + cache_k.shape[1:]), share((per_step,) + cache_v.shape[1:]),
        ],
        out_specs=[
            x_tile(back),
            per_seq((RET_HEADS, HEAD_DIM, HEAD_DIM)),
            _const_spec((CONV_HIST, batch, BRANCH_W)),
            rows8(BRANCH_W), rows8(BRANCH_W),
            share((per_step,) + s0.shape[1:]),
        ],
        out_shape=[
            jax.ShapeDtypeStruct((batch, seq, D_MODEL), F32),
            jax.ShapeDtypeStruct((batch, RET_HEADS, HEAD_DIM, HEAD_DIM), F32),
            jax.ShapeDtypeStruct((CONV_HIST, batch, BRANCH_W), F32),
            jax.ShapeDtypeStruct((n_dec, BRANCH_W), F32),
            jax.ShapeDtypeStruct((n_dec, BRANCH_W), F32),
            jax.ShapeDtypeStruct(s0.shape, F32),
        ],
        scratch_shapes=[
            pltpu.VMEM((HIST_PAD + tile, BRANCH_W), F32),
            pltpu.VMEM((tile, D_MODEL), BF16),
            pltpu.VMEM((tile, BRANCH_W), BF16),
            pltpu.VMEM((tile, BRANCH_W), BF16),
            pltpu.VMEM((tile, BRANCH_W), BF16),
            pltpu.VMEM((SUBLANES, BRANCH_W), F32),
            pltpu.VMEM((SUBLANES, BRANCH_W), F32),
        ],
        compiler_params=pltpu.CompilerParams(
            dimension_semantics=("arbitrary",), vmem_limit_bytes=VMEM_LIMIT_BYTES),
        name="prompt_layer",
    )(x, x, cos2, sin2, mk_bf, mv_bf, norm_w, w_in_bf, gn_w, conv_w, conv_b, ln_w, ln_b,
      w_br_ret_bf, w_br_conv_bf, w_br_mem_bf, w_out_bf, final_w, intra, qdec, kdec,
      z_s, cos_s, sin_s, s0, cache_k, cache_v)


SAMPLE_PROJ_BLOCK = 1536


def _sample_proj_kernel(x_ref, norm_w_ref, w_ref, z_ref, w_bf_ref):
    x = x_ref[:, 0, :]
    h_bf = (x * _rms_scale(x) * norm_w_ref[...]).astype(BF16)
    w_bf = w_ref[0].astype(BF16)
    w_bf_ref[...] = w_bf
    z_ref[...] = _dot(h_bf, w_bf)


def _sample_proj(xs, norm_w, w_in):
    n = xs.shape[0]
    return pl.pallas_call(
        _sample_proj_kernel,
        grid=(IN_W // SAMPLE_PROJ_BLOCK,),
        in_specs=[
            pl.BlockSpec((n, 1, D_MODEL), lambda j: (0, 0, 0)),
            pl.BlockSpec((1, D_MODEL), lambda j: (0, 0)),
            pl.BlockSpec((1, D_MODEL, SAMPLE_PROJ_BLOCK), lambda j: (0, 0, j)),
        ],
        out_specs=[pl.BlockSpec((n, SAMPLE_PROJ_BLOCK), lambda j: (0, j)),
                   pl.BlockSpec((D_MODEL, SAMPLE_PROJ_BLOCK), lambda j: (0, j))],
        out_shape=[jax.ShapeDtypeStruct((n, IN_W), F32),
                   jax.ShapeDtypeStruct((D_MODEL, IN_W), BF16)],
        compiler_params=pltpu.CompilerParams(
            dimension_semantics=("arbitrary",), vmem_limit_bytes=VMEM_LIMIT_BYTES),
        name="sample_proj",
    )(xs, norm_w, w_in)


def _sample_sequences(gamma, zq, zk, v, qm, cos2, sin2, s0_ref, ck_ref, cv_ref, s_new_ref):
    n = zq.shape[0]
    n_mem = ck_ref.shape[1] // MEM_HEADS
    q_scale = HEAD_DIM ** -0.5
    mem_scale = HEAD_DIM ** -0.5
    q = [_rotary(_head(zq, h), cos2, sin2) * q_scale for h in range(RET_HEADS)]
    k = [_rotary(_head(zk, h), cos2, sin2) for h in range(RET_HEADS)]

    eye = (lax.broadcasted_iota(jnp.int32, (HEAD_DIM, HEAD_DIM), 0)
           == lax.broadcasted_iota(jnp.int32, (HEAD_DIM, HEAD_DIM), 1))
    ones_bf = jnp.ones((HEAD_DIM, HEAD_DIM), BF16)

    def as_columns(row):
        diag = jnp.where(eye, jnp.broadcast_to(row, (HEAD_DIM, HEAD_DIM)), 0.0)
        return _dot(diag.astype(BF16), ones_bf)

    o_rows, att_rows = [], []
    for b in range(n):
        o_heads = []
        for h in range(RET_HEADS):
            q_row = q[h][b:b + 1]
            k_row = k[h][b:b + 1]
            v_row = _head(v, h)[b:b + 1]
            s0 = s0_ref[b, h]
            qk = jnp.sum(q_row * k_row, axis=-1, keepdims=True)
            q_lhs = jnp.broadcast_to(q_row, (SUBLANES, HEAD_DIM)).astype(BF16)
            qs = _dot(q_lhs, s0.astype(BF16))[0:1]
            o_heads.append(qk * v_row + gamma[h] * qs)
            s_new_ref[b, h] = gamma[h] * s0 + as_columns(k_row) * v_row
        o_rows.append(jnp.concatenate(o_heads, axis=-1))

        att_heads = []
        for h in range(MEM_HEADS):
            head_rows = pl.ds(h, n_mem, stride=MEM_HEADS)
            prod = ck_ref[b, head_rows, :] * (_head(qm, h)[b:b + 1] * mem_scale)
            sc = jnp.sum(prod, axis=-1, keepdims=True)
            e = jnp.exp(sc - jnp.max(sc, axis=0, keepdims=True))
            weighted = jnp.sum(e * cv_ref[b, head_rows, :], axis=0, keepdims=True)
            att_heads.append(weighted / jnp.sum(e, axis=0, keepdims=True))
        att_rows.append(jnp.concatenate(att_heads, axis=-1))
    return o_rows, att_rows


def _sample_out_kernel(x_ref, z_ref, o_ref, att_ref, hist_ref, conv_w_ref, gn_w_ref, conv_b_ref,
                       ln_w_ref, ln_b_ref, w_br_ret_ref, w_br_conv_ref, w_br_mem_ref, w_out_ref,
                       final_w_ref, y_ref, hist_new_ref):
    gate = lambda off: _silu(z_ref[:, off:off + BRANCH_W])
    yr = gate(OFF_GR) * _group_norm_heads(o_ref[...], gn_w_ref[...])

    u = z_ref[:, OFF_AC:OFF_AC + BRANCH_W] * _sigmoid(z_ref[:, OFF_BC:OFF_BC + BRANCH_W])
    conv = u * conv_w_ref[0, CONV_HIST:CONV_K, :] + conv_b_ref[...]
    for j in range(CONV_HIST):
        conv = conv + hist_ref[j] * conv_w_ref[0, j:j + 1, :]
        hist_new_ref[j] = hist_ref[j + 1] if j + 1 < CONV_HIST else u
    cn = _layernorm_rows(conv) * ln_w_ref[...] + ln_b_ref[...]
    yc = gate(OFF_GC) * _silu(cn)

    ym = gate(OFF_GM) * att_ref[...]
    gate_pre = lambda i: z_ref[:, OFF_MERGE + i * D_MODEL:OFF_MERGE + (i + 1) * D_MODEL]
    y_ref[:, 0, :] = _merge_and_project(x_ref[:, 0, :], gate_pre, yr, yc, ym, w_br_ret_ref,
                                        w_br_conv_ref, w_br_mem_ref, w_out_ref, final_w_ref[...])


SAMPLE_OUT_BLOCK = 32


def _sample_out(xs, z, o, att, hist, conv_w, gn_w, conv_b, ln_w, ln_b,
                w_br_ret_bf, w_br_conv_bf, w_br_mem_bf, w_out_bf, final_w):
    n = xs.shape[0]
    nb = SAMPLE_OUT_BLOCK
    rows = lambda w: pl.BlockSpec((nb, w), lambda i: (i, 0))
    hist_spec = pl.BlockSpec((CONV_HIST, nb, BRANCH_W), lambda i: (0, i, 0))
    tokens = pl.BlockSpec((nb, 1, D_MODEL), lambda i: (i, 0, 0))
    const = lambda a: pl.BlockSpec(a.shape, lambda i: (0,) * a.ndim)
    weight = lambda a: pl.BlockSpec(a.shape, lambda i: (0,) * a.ndim,
                                    pipeline_mode=pl.Buffered(1))
    return pl.pallas_call(
        _sample_out_kernel,
        grid=(n // nb,),
        in_specs=[tokens, rows(IN_W), rows(BRANCH_W), rows(BRANCH_W), hist_spec,
                  const(conv_w), const(gn_w), const(conv_b), const(ln_w), const(ln_b),
                  weight(w_br_ret_bf), weight(w_br_conv_bf), weight(w_br_mem_bf),
                  weight(w_out_bf), const(final_w)],
        out_specs=[tokens, hist_spec],
        out_shape=[jax.ShapeDtypeStruct(xs.shape, F32), jax.ShapeDtypeStruct(hist.shape, F32)],
        compiler_params=pltpu.CompilerParams(
            dimension_semantics=("arbitrary",), vmem_limit_bytes=VMEM_LIMIT_BYTES),
        name="sample_out",
    )(xs, z, o, att, hist, conv_w, gn_w, conv_b, ln_w, ln_b,
      w_br_ret_bf, w_br_conv_bf, w_br_mem_bf, w_out_bf, final_w)


def _rotary_tables(pos):
    half = HEAD_DIM // 2
    inv = ROPE_BASE ** (-np.arange(half, dtype=np.float64) / half)
    ang = np.asarray(pos, np.float64)[:, None] * inv[None, :]
    cos, sin = np.cos(ang), np.sin(ang)
    table = lambda a, b: np.concatenate([a, b], axis=-1).astype(np.float32)
    return table(cos, cos), table(-sin, sin)


def _decay_tables():
    c = RET_CHUNK
    log_g = np.log1p(-np.exp2(-5.0 - np.arange(RET_HEADS, dtype=np.float64)))
    idx = np.arange(c, dtype=np.float64)
    diff = idx[:, None] - idx[None, :]
    intra = np.where(diff >= 0, np.exp(np.maximum(diff, 0.0)[None] * log_g[:, None, None]), 0.0)
    q_dec = np.exp((idx + 1.0)[None, :] * log_g[:, None])
    k_dec = np.exp((c - 1.0 - idx)[None, :] * log_g[:, None])
    bcast = lambda a: np.ascontiguousarray(
        np.broadcast_to(a[:, :, None], (RET_HEADS, c, HEAD_DIM))).astype(np.float32)
    return intra.astype(np.float32), bcast(q_dec), bcast(k_dec)


def _gamma_powers(n):
    return tuple(float(np.exp(np.log1p(-np.exp2(-5.0 - h)) * n)) for h in range(RET_HEADS))


def kernel(x_prompt, x_sample, mem_prompt, state_ret, state_conv, cache_mem_k, cache_mem_v,
           norm_w, w_in, ret_gn_w, conv_w, conv_b, conv_ln_w, conv_ln_b, mem_norm_w,
           w_mem_kv, w_br_ret, w_br_conv, w_br_mem, w_out, final_norm_w):
    depth = w_in.shape[0]
    assert depth == 1, "single-layer step"
    batch, seq, _ = x_prompt.shape
    n_dec, dec_seq, _ = x_sample.shape
    assert dec_seq == 1 and seq % PROMPT_TILE == 0
    n_mem = mem_prompt.shape[1]

    final_w = final_norm_w[None, :]

    intra, qdec, kdec = _decay_tables()
    cos_p, sin_p = _rotary_tables(np.arange(seq))
    cos_s, sin_s = _rotary_tables(np.arange(dec_seq) + PAST_LEN)

    z_s, w_in_bf = _sample_proj(x_sample, norm_w, w_in)

    mk, mv, mk_bf, mv_bf, w_br_ret_bf, w_br_conv_bf, w_br_mem_bf, w_out_bf = _mem_kv(
        mem_prompt, mem_norm_w, w_mem_kv, (w_br_ret, w_br_conv, w_br_mem, w_out))

    y_prompt, s_prompt, hist_prompt, o_s, att_s, s_sample = _prompt_layer(
        x_prompt, cos_p, sin_p, mk_bf, mv_bf, norm_w, w_in_bf, ret_gn_w, conv_w, conv_b,
        conv_ln_w, conv_ln_b, w_br_ret_bf, w_br_conv_bf, w_br_mem_bf, w_out_bf, final_w,
        intra, qdec, kdec, _gamma_powers(RET_CHUNK),
        z_s, cos_s, sin_s, state_ret[0],
        cache_mem_k.reshape(n_dec, n_mem * MEM_HEADS, HEAD_DIM),
        cache_mem_v.reshape(n_dec, n_mem * MEM_HEADS, HEAD_DIM), _gamma_powers(1))

    y_sample, hist_sample = _sample_out(
        x_sample, z_s, o_s, att_s,
        jnp.transpose(state_conv[0], (1, 0, 2)), conv_w, ret_gn_w, conv_b, conv_ln_w, conv_ln_b,
        w_br_ret_bf, w_br_conv_bf, w_br_mem_bf, w_out_bf, final_w)

    heads = lambda a: a.reshape(1, batch, n_mem, MEM_HEADS, HEAD_DIM)
    return (y_prompt, y_sample, s_prompt[None], s_sample[None],
            jnp.transpose(hist_prompt, (1, 0, 2))[None],
            jnp.transpose(hist_sample, (1, 0, 2))[None], heads(mk), heads(mv))
```

```python
import functools

import numpy as np
import jax
import jax.numpy as jnp
from jax import lax
from jax.experimental import pallas as pl
from jax.experimental.pallas import tpu as pltpu

F32 = jnp.float32
BF16 = jnp.bfloat16

D_MODEL = 1024
RET_HEADS = 4
HEAD_DIM = 128
SUBLANES = 8
BRANCH_W = 512
RET_CHUNK = 128
ROPE_BASE = 10000.0
CONV_K = 31
CONV_HIST = CONV_K - 1
MEM_HEADS = 4
N_BRANCH = 3
EPS = 1e-6
PAST_LEN = 16384

OFF_QR, OFF_KR, OFF_VR, OFF_GR = 0, 512, 1024, 1536
OFF_AC, OFF_BC, OFF_GC = 2048, 2560, 3072
OFF_QM, OFF_GM = 3584, 4096
OFF_MERGE = 4608
IN_W = OFF_MERGE + N_BRANCH * D_MODEL

V7X_VMEM_BYTES = 64 * 1024 * 1024
VMEM_LIMIT_BYTES = V7X_VMEM_BYTES - 8 * 1024 * 1024

PROMPT_TILE = 256
HIST_PAD = 32


def _sigmoid(x):
    return 0.5 * jnp.tanh(0.5 * x) + 0.5


def _silu(x):
    hx = 0.5 * x
    return hx * (jnp.tanh(hx) + 1.0)


def _rms_scale(x):
    return lax.rsqrt(jnp.mean(x * x, axis=-1, keepdims=True) + EPS)


def _layernorm_rows(x):
    mu = jnp.mean(x, axis=-1, keepdims=True)
    xc = x - mu
    var = jnp.mean(xc * xc, axis=-1, keepdims=True)
    return xc * lax.rsqrt(var + EPS)


def _rotary(x, cos2, sin2):
    return x * cos2 + pltpu.roll(x, HEAD_DIM // 2, axis=1) * sin2


def _dot(a, b):
    return jnp.dot(a, b, preferred_element_type=F32)


def _dot_nt(a, b):
    return lax.dot_general(a, b, (((1,), (1,)), ((), ())), preferred_element_type=F32)


def _dot_tn(a, b):
    return lax.dot_general(a, b, (((0,), (0,)), ((), ())), preferred_element_type=F32)


def _head(x, h):
    return x[:, h * HEAD_DIM:(h + 1) * HEAD_DIM]


def _group_norm_heads(o, gn_w):
    parts = [_layernorm_rows(_head(o, h)) for h in range(RET_HEADS)]
    return jnp.concatenate(parts, axis=-1) * gn_w


def _merge_and_project(x, gate_pre, yr, yc, ym, w_br_ret_ref, w_br_conv_ref,
                       w_br_mem_ref, w_out_ref, final_w):
    branches = (
        _dot(yr.astype(BF16), w_br_ret_ref[...]),
        _dot(yc.astype(BF16), w_br_conv_ref[...]),
        _dot(ym.astype(BF16), w_br_mem_ref[...]),
    )
    merged = None
    for i, br in enumerate(branches):
        term = _sigmoid(gate_pre(i)) * br
        merged = term if merged is None else merged + term
    out = x + _dot(merged.astype(BF16), w_out_ref[...])
    return out * _rms_scale(out) * final_w


MEM_KV_BLOCK = 4


def _mem_kv_kernel(n_cast, mem_ref, norm_w_ref, w_ref, *refs):
    cast_in, (k_ref, v_ref, kbf_ref, vbf_ref), cast_out = (
        refs[:n_cast], refs[n_cast:n_cast + 4], refs[n_cast + 4:])

    @pl.when(pl.program_id(0) == 0)
    def _():
        for src, dst in zip(cast_in, cast_out):
            dst[...] = src[0].astype(BF16)

    nb, n_mem, _ = mem_ref.shape
    m = mem_ref[...].reshape(nb * n_mem, D_MODEL)
    hm = (m * _rms_scale(m) * norm_w_ref[...]).astype(BF16)
    kv = _dot(hm, w_ref[0].astype(BF16))
    for b in range(nb):
        k = kv[b * n_mem:(b + 1) * n_mem, :BRANCH_W]
        v = kv[b * n_mem:(b + 1) * n_mem, BRANCH_W:]
        for h in range(MEM_HEADS):
            k_ref[b, pl.ds(h, n_mem, stride=MEM_HEADS), :] = _head(k, h)
            v_ref[b, pl.ds(h, n_mem, stride=MEM_HEADS), :] = _head(v, h)
        kbf_ref[b] = k.astype(BF16)
        vbf_ref[b] = v.astype(BF16)


def _mem_kv(mem, mem_norm_w, w_mem_kv, to_cast):
    batch, n_mem, _ = mem.shape
    nb = MEM_KV_BLOCK
    blk = lambda dt: jax.ShapeDtypeStruct((batch, n_mem, BRANCH_W), dt)
    out_spec = pl.BlockSpec((nb, n_mem, BRANCH_W), lambda b: (b, 0, 0))
    rows_shape = (batch, n_mem * MEM_HEADS, HEAD_DIM)
    rows_spec = pl.BlockSpec((nb, n_mem * MEM_HEADS, HEAD_DIM), lambda b: (b, 0, 0))
    return pl.pallas_call(
        functools.partial(_mem_kv_kernel, len(to_cast)),
        grid=(batch // nb,),
        in_specs=[
            pl.BlockSpec((nb, n_mem, D_MODEL), lambda b: (b, 0, 0)),
            pl.BlockSpec((1, D_MODEL), lambda b: (0, 0)),
            pl.BlockSpec((1, D_MODEL, 2 * BRANCH_W), lambda b: (0, 0, 0)),
        ] + [pl.BlockSpec(w.shape, lambda b: (0, 0, 0), pipeline_mode=pl.Buffered(1))
             for w in to_cast],
        out_specs=[rows_spec, rows_spec, out_spec, out_spec]
                  + [pl.BlockSpec(w.shape[1:], lambda b: (0, 0)) for w in to_cast],
        out_shape=[jax.ShapeDtypeStruct(rows_shape, F32), jax.ShapeDtypeStruct(rows_shape, F32),
                   blk(BF16), blk(BF16)]
                  + [jax.ShapeDtypeStruct(w.shape[1:], BF16) for w in to_cast],
        compiler_params=pltpu.CompilerParams(
            dimension_semantics=("arbitrary",), vmem_limit_bytes=VMEM_LIMIT_BYTES),
        name="mem_kv",
    )(mem, mem_norm_w, w_mem_kv, *to_cast)


def _prompt_kernel(c_dec, gamma, tiles_per_seq, n_tiles, *refs):
    (x_ref, xb_ref, cos_ref, sin_ref, mk_ref, mv_ref, norm_w_ref, w_in_ref,
     gn_w_ref, conv_w_ref, conv_b_ref, ln_w_ref, ln_b_ref,
     w_br_ret_ref, w_br_conv_ref, w_br_mem_ref, w_out_ref, final_w_ref,
     intra_ref, qdec_ref, kdec_ref,
     zs_ref, cos_s_ref, sin_s_ref, s0_ref, ck_ref, cv_ref,
     y_ref, s_ref, hist_ref, os_ref, atts_ref, s_new_ref,
     ubuf_ref, h_scr, yr_scr, yc_scr, ym_scr, os_scr, atts_scr) = refs
    i = pl.program_id(0)

    @pl.when(i == 0)
    def _():
        h_scr[...] = jnp.zeros_like(h_scr)
        yr_scr[...] = jnp.zeros_like(yr_scr)
        yc_scr[...] = jnp.zeros_like(yc_scr)
        ym_scr[...] = jnp.zeros_like(ym_scr)
        os_scr[...] = jnp.zeros_like(os_scr)
        atts_scr[...] = jnp.zeros_like(atts_scr)
        hist_ref[...] = jnp.zeros_like(hist_ref)

    @pl.when(i < n_tiles)
    def _():
        _prompt_step(c_dec, gamma, tiles_per_seq, *refs)

    @pl.when(i == n_tiles)
    def _():
        gate = lambda b: _dot(h_scr[...], w_in_ref[:, OFF_MERGE + b * D_MODEL:
                                                   OFF_MERGE + (b + 1) * D_MODEL])
        y_ref[0] = _merge_and_project(
            xb_ref[0], gate, yr_scr[...], yc_scr[...], ym_scr[...],
            w_br_ret_ref, w_br_conv_ref, w_br_mem_ref, w_out_ref, final_w_ref[...])


def _prompt_step(c_dec, gamma, tiles_per_seq,
                 x_ref, xb_ref, cos_ref, sin_ref, mk_ref, mv_ref, norm_w_ref, w_in_ref,
                 gn_w_ref, conv_w_ref, conv_b_ref, ln_w_ref, ln_b_ref,
                 w_br_ret_ref, w_br_conv_ref, w_br_mem_ref, w_out_ref, final_w_ref,
                 intra_ref, qdec_ref, kdec_ref,
                 zs_ref, cos_s_ref, sin_s_ref, s0_ref, ck_ref, cv_ref,
                 y_ref, s_ref, hist_ref, os_ref, atts_ref, s_new_ref,
                 ubuf_ref, h_scr, yr_scr, yc_scr, ym_scr, os_scr, atts_scr):
    i = pl.program_id(0)
    t = lax.rem(i, tiles_per_seq)
    tile = x_ref.shape[1]

    @pl.when(t == 0)
    def _():
        s_ref[...] = jnp.zeros_like(s_ref)
        ubuf_ref[0:HIST_PAD, :] = jnp.zeros((HIST_PAD, BRANCH_W), F32)

    h_prev = h_scr[...]
    half = D_MODEL // 2

    def gate_job(k):
        lo = OFF_MERGE + k * half
        return lambda: _dot(h_prev, w_in_ref[:, lo:lo + half])

    back_jobs = [gate_job(k) for k in range(2 * N_BRANCH)] + [
        lambda: _dot(yr_scr[...], w_br_ret_ref[...]),
        lambda: _dot(yc_scr[...], w_br_conv_ref[...]),
        lambda: _dot(ym_scr[...], w_br_mem_ref[...]),
    ]
    back_out = [None] * len(back_jobs)

    def run_back(k):
        back_out[k] = back_jobs[k]()

    x = x_ref[0]
    h_bf = (x * _rms_scale(x) * norm_w_ref[...]).astype(BF16)
    run_back(8)

    def proj(off, width=BRANCH_W):
        return _dot(h_bf, w_in_ref[:, off:off + width])

    za = proj(OFF_AC)
    zb = proj(OFF_BC)
    zq = proj(OFF_QR)
    zk = proj(OFF_KR)
    v_bf = proj(OFF_VR).astype(BF16)
    u = za * _sigmoid(zb)
    ubuf_ref[HIST_PAD:HIST_PAD + tile, :] = u
    first = HIST_PAD - CONV_HIST
    conv = None
    for r in range(8):
        part = None
        for j in range(CONV_K):
            if (first + j) % 8 != r:
                continue
            term = ubuf_ref[first + j:first + j + tile, :] * conv_w_ref[0, j:j + 1, :]
            part = term if part is None else part + term
        conv = part if conv is None else conv + part
        run_back(r)
    conv = conv + conv_b_ref[...]
    ubuf_ref[0:HIST_PAD, :] = ubuf_ref[tile:tile + HIST_PAD, :]

    z_gc = proj(OFF_GC)
    z_gr = proj(OFF_GR)
    z_qm = proj(OFF_QM)
    z_gm = proj(OFF_GM)

    per_step = s0_ref.shape[0]
    r0 = lax.rem(i, SUBLANES // per_step) * per_step
    row_id = lax.broadcasted_iota(jnp.int32, (SUBLANES, 1), 0)

    def own_rows(off):
        block = zs_ref[:, off:off + BRANCH_W]
        return jnp.concatenate(
            [jnp.sum(jnp.where(row_id == r0 + b, block, 0.0), axis=0, keepdims=True)
             for b in range(per_step)], axis=0)

    o_rows, att_rows = _sample_sequences(
        gamma, own_rows(OFF_QR), own_rows(OFF_KR), own_rows(OFF_VR), own_rows(OFF_QM),
        cos_s_ref[...], sin_s_ref[...], s0_ref, ck_ref, cv_ref, s_new_ref)
    for scr, out_ref, rows in ((os_scr, os_ref, o_rows), (atts_scr, atts_ref, att_rows)):
        block = scr[...]
        for b in range(per_step):
            block = jnp.where(row_id == r0 + b, rows[b], block)
        scr[...] = block
        out_ref[...] = block

    merged = []
    for k in range(2):
        cols = slice(k * half, (k + 1) * half)
        acc = None
        for b in range(N_BRANCH):
            term = _sigmoid(back_out[2 * b + k]) * back_out[2 * N_BRANCH + b][:, cols]
            acc = term if acc is None else acc + term
        merged.append(acc.astype(BF16))
    merged = jnp.concatenate(merged, axis=-1)

    cos2 = cos_ref[...]
    sin2 = sin_ref[...]
    q_scale = HEAD_DIM ** -0.5
    def block_diag(a, b):
        zero = jnp.zeros_like(a)
        return jnp.concatenate([jnp.concatenate([a, zero], axis=-1),
                                jnp.concatenate([zero, b], axis=-1)], axis=0)

    o_heads = []
    for h0 in range(0, RET_HEADS, 2):
        pair = (h0, h0 + 1)
        q_p = [_rotary(_head(zq, h), cos2, sin2) * q_scale for h in pair]
        k_p = [_rotary(_head(zk, h), cos2, sin2) for h in pair]
        v_p = [_head(v_bf, h) for h in pair]
        s_old = [s_ref[0, h] for h in pair]
        s = list(s_old)
        intra2 = jnp.concatenate([intra_ref[h] for h in pair], axis=-1)
        qdec2 = jnp.concatenate([qdec_ref[h] for h in pair], axis=-1)
        o_chunks = []
        for c in range(tile // RET_CHUNK):
            rows = slice(c * RET_CHUNK, (c + 1) * RET_CHUNK)
            kc = [k[rows] for k in k_p]
            vc = [v[rows] for v in v_p]
            qc2 = jnp.concatenate([q[rows] for q in q_p], axis=-1).astype(BF16)
            sc2 = _dot_nt(qc2, block_diag(*[k.astype(BF16) for k in kc])) * intra2
            o_chunks.append(_dot(sc2.astype(BF16), block_diag(*vc))
                            + _dot(qc2, block_diag(*[x.astype(BF16) for x in s])) * qdec2)
            for n, h in enumerate(pair):
                kd = (kc[n] * kdec_ref[h]).astype(BF16)
                s[n] = s[n] * c_dec[h] + _dot_tn(kd, vc[n])
        o_pair = jnp.concatenate(o_chunks, axis=0)
        for n, h in enumerate(pair):
            s_ref[0, h] = s[n]
            o_heads.append(_layernorm_rows(_head(o_pair, n)))

    out = xb_ref[0] + _dot(merged, w_out_ref[...])

    ret = jnp.concatenate(o_heads, axis=-1) * gn_w_ref[...]
    yr = _silu(z_gr) * ret
    cn = _layernorm_rows(conv) * ln_w_ref[...] + ln_b_ref[...]
    yc = _silu(z_gc) * _silu(cn)

    qm = z_qm.astype(BF16)
    mk = mk_ref[0]
    mv = mv_ref[0]
    att = []
    for h in range(MEM_HEADS):
        sc = _dot_nt(_head(qm, h), _head(mk, h)) * (HEAD_DIM ** -0.5)
        e = jnp.exp(sc - jnp.max(sc, axis=-1, keepdims=True))
        p = e / jnp.sum(e, axis=-1, keepdims=True)
        att.append(_dot(p.astype(BF16), _head(mv, h)))
    ym = _silu(z_gm) * jnp.concatenate(att, axis=-1)

    y_ref[0] = out * _rms_scale(out) * final_w_ref[...]

    h_scr[...] = h_bf
    yr_scr[...] = yr.astype(BF16)
    yc_scr[...] = yc.astype(BF16)
    ym_scr[...] = ym.astype(BF16)

    @pl.when(t == tiles_per_seq - 1)
    def _():
        seq_id = i // tiles_per_seq
        mine = lax.broadcasted_iota(jnp.int32, (hist_ref.shape[1], 1), 0) == seq_id
        for j in range(CONV_HIST):
            hist_ref[j] = jnp.where(mine, ubuf_ref[first + j:first + j + 1, :], hist_ref[j])


def _const_spec(shape, single_buffer=False):
    zeros = (0,) * len(shape)
    if single_buffer:
        return pl.BlockSpec(shape, lambda i: zeros, pipeline_mode=pl.Buffered(1))
    return pl.BlockSpec(shape, lambda i: zeros)


def _prompt_layer(x, cos2, sin2, mk_bf, mv_bf, norm_w, w_in_bf, gn_w, conv_w, conv_b, ln_w,
                  ln_b, w_br_ret_bf, w_br_conv_bf, w_br_mem_bf, w_out_bf, final_w,
                  intra, qdec, kdec, c_dec,
                  z_s, cos_s, sin_s, s0, cache_k, cache_v, gamma):
    batch, seq, _ = x.shape
    n_mem = mk_bf.shape[1]
    tile = PROMPT_TILE
    tiles_per_seq = seq // tile
    n_tiles = batch * tiles_per_seq

    def front(i):
        j = jnp.minimum(i, n_tiles - 1)
        return j // tiles_per_seq, lax.rem(j, tiles_per_seq)

    def back(i):
        j = jnp.maximum(i - 1, 0)
        return j // tiles_per_seq, lax.rem(j, tiles_per_seq)

    row = lambda w: _const_spec((1, w))
    weight = lambda a: _const_spec(a.shape, single_buffer=True)
    table = _const_spec((RET_HEADS, RET_CHUNK, RET_CHUNK))
    n_dec = z_s.shape[0]
    per_step = n_dec // n_tiles
    assert per_step * n_tiles == n_dec
    share = lambda shape: pl.BlockSpec(
        shape, lambda i: (jnp.minimum(i, n_tiles - 1),) + (0,) * (len(shape) - 1))
    assert SUBLANES % per_step == 0
    rows8 = lambda w: pl.BlockSpec(
        (SUBLANES, w), lambda i: (jnp.minimum(i, n_tiles - 1) // (SUBLANES // per_step), 0))
    x_tile = lambda which: pl.BlockSpec((1, tile, D_MODEL), lambda i: (*which(i), 0))
    per_seq = lambda shape: pl.BlockSpec(
        (1,) + shape, lambda i: (front(i)[0],) + (0,) * len(shape))
    return pl.pallas_call(
        functools.partial(_prompt_kernel, c_dec, gamma, tiles_per_seq, n_tiles),
        grid=(n_tiles + 1,),
        in_specs=[
            x_tile(front), x_tile(back),
            pl.BlockSpec((tile, HEAD_DIM), lambda i: (front(i)[1], 0)),
            pl.BlockSpec((tile, HEAD_DIM), lambda i: (front(i)[1], 0)),
            per_seq((n_mem, BRANCH_W)), per_seq((n_mem, BRANCH_W)),
            row(D_MODEL), weight(w_in_bf), row(BRANCH_W),
            _const_spec((1, CONV_K, BRANCH_W)), row(BRANCH_W), row(BRANCH_W), row(BRANCH_W),
            weight(w_br_ret_bf), weight(w_br_conv_bf), weight(w_br_mem_bf), weight(w_out_bf),
            row(D_MODEL), table, table, table,
            rows8(IN_W), row(HEAD_DIM), row(HEAD_DIM),
            share((per_step,) + s0.shape[1:]),
            share((per_step,) + cache_k.shape[1:]), share((per_step,) + cache_v.shape[1:]),
        ],
        out_specs=[
            x_tile(back),
            per_seq((RET_HEADS, HEAD_DIM, HEAD_DIM)),
            _const_spec((CONV_HIST, batch, BRANCH_W)),
            rows8(BRANCH_W), rows8(BRANCH_W),
            share((per_step,) + s0.shape[1:]),
        ],
        out_shape=[
            jax.ShapeDtypeStruct((batch, seq, D_MODEL), F32),
            jax.ShapeDtypeStruct((batch, RET_HEADS, HEAD_DIM, HEAD_DIM), F32),
            jax.ShapeDtypeStruct((CONV_HIST, batch, BRANCH_W), F32),
            jax.ShapeDtypeStruct((n_dec, BRANCH_W), F32),
            jax.ShapeDtypeStruct((n_dec, BRANCH_W), F32),
            jax.ShapeDtypeStruct(s0.shape, F32),
        ],
        scratch_shapes=[
            pltpu.VMEM((HIST_PAD + tile, BRANCH_W), F32),
            pltpu.VMEM((tile, D_MODEL), BF16),
            pltpu.VMEM((tile, BRANCH_W), BF16),
            pltpu.VMEM((tile, BRANCH_W), BF16),
            pltpu.VMEM((tile, BRANCH_W), BF16),
            pltpu.VMEM((SUBLANES, BRANCH_W), F32),
            pltpu.VMEM((SUBLANES, BRANCH_W), F32),
        ],
        compiler_params=pltpu.CompilerParams(
            dimension_semantics=("arbitrary",), vmem_limit_bytes=VMEM_LIMIT_BYTES),
        name="prompt_layer",
    )(x, x, cos2, sin2, mk_bf, mv_bf, norm_w, w_in_bf, gn_w, conv_w, conv_b, ln_w, ln_b,
      w_br_ret_bf, w_br_conv_bf, w_br_mem_bf, w_out_bf, final_w, intra, qdec, kdec,
      z_s, cos_s, sin_s, s0, cache_k, cache_v)


SAMPLE_PROJ_BLOCK = 1536


def _sample_proj_kernel(x_ref, norm_w_ref, w_ref, z_ref, w_bf_ref):
    x = x_ref[:, 0, :]
    h_bf = (x * _rms_scale(x) * norm_w_ref[...]).astype(BF16)
    w_bf = w_ref[0].astype(BF16)
    w_bf_ref[...] = w_bf
    z_ref[...] = _dot(h_bf, w_bf)


def _sample_proj(xs, norm_w, w_in):
    n = xs.shape[0]
    return pl.pallas_call(
        _sample_proj_kernel,
        grid=(IN_W // SAMPLE_PROJ_BLOCK,),
        in_specs=[
            pl.BlockSpec((n, 1, D_MODEL), lambda j: (0, 0, 0)),
            pl.BlockSpec((1, D_MODEL), lambda j: (0, 0)),
            pl.BlockSpec((1, D_MODEL, SAMPLE_PROJ_BLOCK), lambda j: (0, 0, j)),
        ],
        out_specs=[pl.BlockSpec((n, SAMPLE_PROJ_BLOCK), lambda j: (0, j)),
                   pl.BlockSpec((D_MODEL, SAMPLE_PROJ_BLOCK), lambda j: (0, j))],
        out_shape=[jax.ShapeDtypeStruct((n, IN_W), F32),
                   jax.ShapeDtypeStruct((D_MODEL, IN_W), BF16)],
        compiler_params=pltpu.CompilerParams(
            dimension_semantics=("arbitrary",), vmem_limit_bytes=VMEM_LIMIT_BYTES),
        name="sample_proj",
    )(xs, norm_w, w_in)


def _sample_sequences(gamma, zq, zk, v, qm, cos2, sin2, s0_ref, ck_ref, cv_ref, s_new_ref):
    n = zq.shape[0]
    n_mem = ck_ref.shape[1] // MEM_HEADS
    q_scale = HEAD_DIM ** -0.5
    mem_scale = HEAD_DIM ** -0.5
    q = [_rotary(_head(zq, h), cos2, sin2) * q_scale for h in range(RET_HEADS)]
    k = [_rotary(_head(zk, h), cos2, sin2) for h in range(RET_HEADS)]

    eye = (lax.broadcasted_iota(jnp.int32, (HEAD_DIM, HEAD_DIM), 0)
           == lax.broadcasted_iota(jnp.int32, (HEAD_DIM, HEAD_DIM), 1))
    ones_bf = jnp.ones((HEAD_DIM, HEAD_DIM), BF16)

    def as_columns(row):
        diag = jnp.where(eye, jnp.broadcast_to(row, (HEAD_DIM, HEAD_DIM)), 0.0)
        return _dot(diag.astype(BF16), ones_bf)

    o_rows, att_rows = [], []
    for b in range(n):
        o_heads = []
        for h in range(RET_HEADS):
            q_row = q[h][b:b + 1]
            k_row = k[h][b:b + 1]
            v_row = _head(v, h)[b:b + 1]
            s0 = s0_ref[b, h]
            qk = jnp.sum(q_row * k_row, axis=-1, keepdims=True)
            q_lhs = jnp.broadcast_to(q_row, (SUBLANES, HEAD_DIM)).astype(BF16)
            qs = _dot(q_lhs, s0.astype(BF16))[0:1]
            o_heads.append(qk * v_row + gamma[h] * qs)
            s_new_ref[b, h] = gamma[h] * s0 + as_columns(k_row) * v_row
        o_rows.append(jnp.concatenate(o_heads, axis=-1))

        att_heads = []
        for h in range(MEM_HEADS):
            head_rows = pl.ds(h, n_mem, stride=MEM_HEADS)
            prod = ck_ref[b, head_rows, :] * (_head(qm, h)[b:b + 1] * mem_scale)
            sc = jnp.sum(prod, axis=-1, keepdims=True)
            e = jnp.exp(sc - jnp.max(sc, axis=0, keepdims=True))
            weighted = jnp.sum(e * cv_ref[b, head_rows, :], axis=0, keepdims=True)
            att_heads.append(weighted / jnp.sum(e, axis=0, keepdims=True))
        att_rows.append(jnp.concatenate(att_heads, axis=-1))
    return o_rows, att_rows


def _sample_out_kernel(x_ref, z_ref, o_ref, att_ref, hist_ref, conv_w_ref, gn_w_ref, conv_b_ref,
                       ln_w_ref, ln_b_ref, w_br_ret_ref, w_br_conv_ref, w_br_mem_ref, w_out_ref,
                       final_w_ref, y_ref, hist_new_ref):
    gate = lambda off: _silu(z_ref[:, off:off + BRANCH_W])
    yr = gate(OFF_GR) * _group_norm_heads(o_ref[...], gn_w_ref[...])

    u = z_ref[:, OFF_AC:OFF_AC + BRANCH_W] * _sigmoid(z_ref[:, OFF_BC:OFF_BC + BRANCH_W])
    conv = u * conv_w_ref[0, CONV_HIST:CONV_K, :] + conv_b_ref[...]
    for j in range(CONV_HIST):
        conv = conv + hist_ref[j] * conv_w_ref[0, j:j + 1, :]
        hist_new_ref[j] = hist_ref[j + 1] if j + 1 < CONV_HIST else u
    cn = _layernorm_rows(conv) * ln_w_ref[...] + ln_b_ref[...]
    yc = gate(OFF_GC) * _silu(cn)

    ym = gate(OFF_GM) * att_ref[...]
    gate_pre = lambda i: z_ref[:, OFF_MERGE + i * D_MODEL:OFF_MERGE + (i + 1) * D_MODEL]
    y_ref[:, 0, :] = _merge_and_project(x_ref[:, 0, :], gate_pre, yr, yc, ym, w_br_ret_ref,
                                        w_br_conv_ref, w_br_mem_ref, w_out_ref, final_w_ref[...])


SAMPLE_OUT_BLOCK = 32


def _sample_out(xs, z, o, att, hist, conv_w, gn_w, conv_b, ln_w, ln_b,
                w_br_ret_bf, w_br_conv_bf, w_br_mem_bf, w_out_bf, final_w):
    n = xs.shape[0]
    nb = SAMPLE_OUT_BLOCK
    rows = lambda w: pl.BlockSpec((nb, w), lambda i: (i, 0))
    hist_spec = pl.BlockSpec((CONV_HIST, nb, BRANCH_W), lambda i: (0, i, 0))
    tokens = pl.BlockSpec((nb, 1, D_MODEL), lambda i: (i, 0, 0))
    const = lambda a: pl.BlockSpec(a.shape, lambda i: (0,) * a.ndim)
    weight = lambda a: pl.BlockSpec(a.shape, lambda i: (0,) * a.ndim,
                                    pipeline_mode=pl.Buffered(1))
    return pl.pallas_call(
        _sample_out_kernel,
        grid=(n // nb,),
        in_specs=[tokens, rows(IN_W), rows(BRANCH_W), rows(BRANCH_W), hist_spec,
                  const(conv_w), const(gn_w), const(conv_b), const(ln_w), const(ln_b),
                  weight(w_br_ret_bf), weight(w_br_conv_bf), weight(w_br_mem_bf),
                  weight(w_out_bf), const(final_w)],
        out_specs=[tokens, hist_spec],
        out_shape=[jax.ShapeDtypeStruct(xs.shape, F32), jax.ShapeDtypeStruct(hist.shape, F32)],
        compiler_params=pltpu.CompilerParams(
            dimension_semantics=("arbitrary",), vmem_limit_bytes=VMEM_LIMIT_BYTES),
        name="sample_out",
    )(xs, z, o, att, hist, conv_w, gn_w, conv_b, ln_w, ln_b,
      w_br_ret_bf, w_br_conv_bf, w_br_mem_bf, w_out_bf, final_w)


def _rotary_tables(pos):
    half = HEAD_DIM // 2
    inv = ROPE_BASE ** (-np.arange(half, dtype=np.float64) / half)
    ang = np.asarray(pos, np.float64)[:, None] * inv[None, :]
    cos, sin = np.cos(ang), np.sin(ang)
    table = lambda a, b: np.concatenate([a, b], axis=-1).astype(np.float32)
    return table(cos, cos), table(-sin, sin)


def _decay_tables():
    c = RET_CHUNK
    log_g = np.log1p(-np.exp2(-5.0 - np.arange(RET_HEADS, dtype=np.float64)))
    idx = np.arange(c, dtype=np.float64)
    diff = idx[:, None] - idx[None, :]
    intra = np.where(diff >= 0, np.exp(np.maximum(diff, 0.0)[None] * log_g[:, None, None]), 0.0)
    q_dec = np.exp((idx + 1.0)[None, :] * log_g[:, None])
    k_dec = np.exp((c - 1.0 - idx)[None, :] * log_g[:, None])
    bcast = lambda a: np.ascontiguousarray(
        np.broadcast_to(a[:, :, None], (RET_HEADS, c, HEAD_DIM))).astype(np.float32)
    return intra.astype(np.float32), bcast(q_dec), bcast(k_dec)


def _gamma_powers(n):
    return tuple(float(np.exp(np.log1p(-np.exp2(-5.0 - h)) * n)) for h in range(RET_HEADS))


def kernel(x_prompt, x_sample, mem_prompt, state_ret, state_conv, cache_mem_k, cache_mem_v,
           norm_w, w_in, ret_gn_w, conv_w, conv_b, conv_ln_w, conv_ln_b, mem_norm_w,
           w_mem_kv, w_br_ret, w_br_conv, w_br_mem, w_out, final_norm_w):
    depth = w_in.shape[0]
    assert depth == 1, "single-layer step"
    batch, seq, _ = x_prompt.shape
    n_dec, dec_seq, _ = x_sample.shape
    assert dec_seq == 1 and seq % PROMPT_TILE == 0
    n_mem = mem_prompt.shape[1]

    final_w = final_norm_w[None, :]

    intra, qdec, kdec = _decay_tables()
    cos_p, sin_p = _rotary_tables(np.arange(seq))
    cos_s, sin_s = _rotary_tables(np.arange(dec_seq) + PAST_LEN)

    z_s, w_in_bf = _sample_proj(x_sample, norm_w, w_in)

    mk, mv, mk_bf, mv_bf, w_br_ret_bf, w_br_conv_bf, w_br_mem_bf, w_out_bf = _mem_kv(
        mem_prompt, mem_norm_w, w_mem_kv, (w_br_ret, w_br_conv, w_br_mem, w_out))

    y_prompt, s_prompt, hist_prompt, o_s, att_s, s_sample = _prompt_layer(
        x_prompt, cos_p, sin_p, mk_bf, mv_bf, norm_w, w_in_bf, ret_gn_w, conv_w, conv_b,
        conv_ln_w, conv_ln_b, w_br_ret_bf, w_br_conv_bf, w_br_mem_bf, w_out_bf, final_w,
        intra, qdec, kdec, _gamma_powers(RET_CHUNK),
        z_s, cos_s, sin_s, state_ret[0],
        cache_mem_k.reshape(n_dec, n_mem * MEM_HEADS, HEAD_DIM),
        cache_mem_v.reshape(n_dec, n_mem * MEM_HEADS, HEAD_DIM), _gamma_powers(1))

    y_sample, hist_sample = _sample_out(
        x_sample, z_s, o_s, att_s,
        jnp.transpose(state_conv[0], (1, 0, 2)), conv_w, ret_gn_w, conv_b, conv_ln_w, conv_ln_b,
        w_br_ret_bf, w_br_conv_bf, w_br_mem_bf, w_out_bf, final_w)

    heads = lambda a: a.reshape(1, batch, n_mem, MEM_HEADS, HEAD_DIM)
    return (y_prompt, y_sample, s_prompt[None], s_sample[None],
            jnp.transpose(hist_prompt, (1, 0, 2))[None],
            jnp.transpose(hist_sample, (1, 0, 2))[None], heads(mk), heads(mv))
```

```python
import functools

import numpy as np
import jax
import jax.numpy as jnp
from jax import lax
from jax.experimental import pallas as pl
from jax.experimental.pallas import tpu as pltpu

F32 = jnp.float32
BF16 = jnp.bfloat16

D_MODEL = 1024
RET_HEADS = 4
HEAD_DIM = 128
SUBLANES = 8
BRANCH_W = 512
RET_CHUNK = 128
ROPE_BASE = 10000.0
CONV_K = 31
CONV_HIST = CONV_K - 1
MEM_HEADS = 4
N_BRANCH = 3
EPS = 1e-6
PAST_LEN = 16384

OFF_QR, OFF_KR, OFF_VR, OFF_GR = 0, 512, 1024, 1536
OFF_AC, OFF_BC, OFF_GC = 2048, 2560, 3072
OFF_QM, OFF_GM = 3584, 4096
OFF_MERGE = 4608
IN_W = OFF_MERGE + N_BRANCH * D_MODEL

V7X_VMEM_BYTES = 64 * 1024 * 1024
VMEM_LIMIT_BYTES = V7X_VMEM_BYTES - 8 * 1024 * 1024

PROMPT_TILE = 256
HIST_PAD = 32


def _sigmoid(x):
    return 0.5 * jnp.tanh(0.5 * x) + 0.5


def _silu(x):
    hx = 0.5 * x
    return hx * (jnp.tanh(hx) + 1.0)


def _rms_scale(x):
    return lax.rsqrt(jnp.mean(x * x, axis=-1, keepdims=True) + EPS)


def _layernorm_rows(x):
    mu = jnp.mean(x, axis=-1, keepdims=True)
    xc = x - mu
    var = jnp.mean(xc * xc, axis=-1, keepdims=True)
    return xc * lax.rsqrt(var + EPS)


def _rotary(x, cos2, sin2):
    return x * cos2 + pltpu.roll(x, HEAD_DIM // 2, axis=1) * sin2


def _dot(a, b):
    return jnp.dot(a, b, preferred_element_type=F32)


def _dot_nt(a, b):
    return lax.dot_general(a, b, (((1,), (1,)), ((), ())), preferred_element_type=F32)


def _dot_tn(a, b):
    return lax.dot_general(a, b, (((0,), (0,)), ((), ())), preferred_element_type=F32)


def _head(x, h):
    return x[:, h * HEAD_DIM:(h + 1) * HEAD_DIM]


def _group_norm_heads(o, gn_w):
    parts = [_layernorm_rows(_head(o, h)) for h in range(RET_HEADS)]
    return jnp.concatenate(parts, axis=-1) * gn_w


def _merge_and_project(x, gate_pre, yr, yc, ym, w_br_ret_ref, w_br_conv_ref,
                       w_br_mem_ref, w_out_ref, final_w):
    branches = (
        _dot(yr.astype(BF16), w_br_ret_ref[...]),
        _dot(yc.astype(BF16), w_br_conv_ref[...]),
        _dot(ym.astype(BF16), w_br_mem_ref[...]),
    )
    merged = None
    for i, br in enumerate(branches):
        term = _sigmoid(gate_pre(i)) * br
        merged = term if merged is None else merged + term
    out = x + _dot(merged.astype(BF16), w_out_ref[...])
    return out * _rms_scale(out) * final_w


MEM_KV_BLOCK = 4


def _mem_kv_kernel(n_cast, mem_ref, norm_w_ref, w_ref, *refs):
    cast_in, (k_ref, v_ref, kbf_ref, vbf_ref), cast_out = (
        refs[:n_cast], refs[n_cast:n_cast + 4], refs[n_cast + 4:])

    @pl.when(pl.program_id(0) == 0)
    def _():
        for src, dst in zip(cast_in, cast_out):
            dst[...] = src[0].astype(BF16)

    nb, n_mem, _ = mem_ref.shape
    m = mem_ref[...].reshape(nb * n_mem, D_MODEL)
    hm = (m * _rms_scale(m) * norm_w_ref[...]).astype(BF16)
    kv = _dot(hm, w_ref[0].astype(BF16))
    for b in range(nb):
        k = kv[b * n_mem:(b + 1) * n_mem, :BRANCH_W]
        v = kv[b * n_mem:(b + 1) * n_mem, BRANCH_W:]
        for h in range(MEM_HEADS):
            k_ref[b, pl.ds(h, n_mem, stride=MEM_HEADS), :] = _head(k, h)
            v_ref[b, pl.ds(h, n_mem, stride=MEM_HEADS), :] = _head(v, h)
        kbf_ref[b] = k.astype(BF16)
        vbf_ref[b] = v.astype(BF16)


def _mem_kv(mem, mem_norm_w, w_mem_kv, to_cast):
    batch, n_mem, _ = mem.shape
    nb = MEM_KV_BLOCK
    blk = lambda dt: jax.ShapeDtypeStruct((batch, n_mem, BRANCH_W), dt)
    out_spec = pl.BlockSpec((nb, n_mem, BRANCH_W), lambda b: (b, 0, 0))
    rows_shape = (batch, n_mem * MEM_HEADS, HEAD_DIM)
    rows_spec = pl.BlockSpec((nb, n_mem * MEM_HEADS, HEAD_DIM), lambda b: (b, 0, 0))
    return pl.pallas_call(
        functools.partial(_mem_kv_kernel, len(to_cast)),
        grid=(batch // nb,),
        in_specs=[
            pl.BlockSpec((nb, n_mem, D_MODEL), lambda b: (b, 0, 0)),
            pl.BlockSpec((1, D_MODEL), lambda b: (0, 0)),
            pl.BlockSpec((1, D_MODEL, 2 * BRANCH_W), lambda b: (0, 0, 0)),
        ] + [pl.BlockSpec(w.shape, lambda b: (0, 0, 0), pipeline_mode=pl.Buffered(1))
             for w in to_cast],
        out_specs=[rows_spec, rows_spec, out_spec, out_spec]
                  + [pl.BlockSpec(w.shape[1:], lambda b: (0, 0)) for w in to_cast],
        out_shape=[jax.ShapeDtypeStruct(rows_shape, F32), jax.ShapeDtypeStruct(rows_shape, F32),
                   blk(BF16), blk(BF16)]
                  + [jax.ShapeDtypeStruct(w.shape[1:], BF16) for w in to_cast],
        compiler_params=pltpu.CompilerParams(
            dimension_semantics=("arbitrary",), vmem_limit_bytes=VMEM_LIMIT_BYTES),
        name="mem_kv",
    )(mem, mem_norm_w, w_mem_kv, *to_cast)


def _prompt_kernel(c_dec, gamma, tiles_per_seq, n_tiles, *refs):
    (x_ref, xb_ref, cos_ref, sin_ref, mk_ref, mv_ref, norm_w_ref, w_in_ref,
     gn_w_ref, conv_w_ref, conv_b_ref, ln_w_ref, ln_b_ref,
     w_br_ret_ref, w_br_conv_ref, w_br_mem_ref, w_out_ref, final_w_ref,
     intra_ref, qdec_ref, kdec_ref,
     zs_ref, cos_s_ref, sin_s_ref, s0_ref, ck_ref, cv_ref,
     y_ref, s_ref, hist_ref, os_ref, atts_ref, s_new_ref,
     ubuf_ref, h_scr, yr_scr, yc_scr, ym_scr, os_scr, atts_scr) = refs
    i = pl.program_id(0)

    @pl.when(i == 0)
    def _():
        h_scr[...] = jnp.zeros_like(h_scr)
        yr_scr[...] = jnp.zeros_like(yr_scr)
        yc_scr[...] = jnp.zeros_like(yc_scr)
        ym_scr[...] = jnp.zeros_like(ym_scr)
        os_scr[...] = jnp.zeros_like(os_scr)
        atts_scr[...] = jnp.zeros_like(atts_scr)
        hist_ref[...] = jnp.zeros_like(hist_ref)

    @pl.when(i < n_tiles)
    def _():
        _prompt_step(c_dec, gamma, tiles_per_seq, *refs)

    @pl.when(i == n_tiles)
    def _():
        gate = lambda b: _dot(h_scr[...], w_in_ref[:, OFF_MERGE + b * D_MODEL:
                                                   OFF_MERGE + (b + 1) * D_MODEL])
        y_ref[0] = _merge_and_project(
            xb_ref[0], gate, yr_scr[...], yc_scr[...], ym_scr[...],
            w_br_ret_ref, w_br_conv_ref, w_br_mem_ref, w_out_ref, final_w_ref[...])


def _prompt_step(c_dec, gamma, tiles_per_seq,
                 x_ref, xb_ref, cos_ref, sin_ref, mk_ref, mv_ref, norm_w_ref, w_in_ref,
                 gn_w_ref, conv_w_ref, conv_b_ref, ln_w_ref, ln_b_ref,
                 w_br_ret_ref, w_br_conv_ref, w_br_mem_ref, w_out_ref, final_w_ref,
                 intra_ref, qdec_ref, kdec_ref,
                 zs_ref, cos_s_ref, sin_s_ref, s0_ref, ck_ref, cv_ref,
                 y_ref, s_ref, hist_ref, os_ref, atts_ref, s_new_ref,
                 ubuf_ref, h_scr, yr_scr, yc_scr, ym_scr, os_scr, atts_scr):
    i = pl.program_id(0)
    t = lax.rem(i, tiles_per_seq)
    tile = x_ref.shape[1]

    @pl.when(t == 0)
    def _():
        s_ref[...] = jnp.zeros_like(s_ref)
        ubuf_ref[0:HIST_PAD, :] = jnp.zeros((HIST_PAD, BRANCH_W), F32)

    h_prev = h_scr[...]
    half = D_MODEL // 2

    def gate_job(k):
        lo = OFF_MERGE + k * half
        return lambda: _dot(h_prev, w_in_ref[:, lo:lo + half])

    back_jobs = [gate_job(k) for k in range(2 * N_BRANCH)] + [
        lambda: _dot(yr_scr[...], w_br_ret_ref[...]),
        lambda: _dot(yc_scr[...], w_br_conv_ref[...]),
        lambda: _dot(ym_scr[...], w_br_mem_ref[...]),
    ]
    back_out = [None] * len(back_jobs)

    def run_back(k):
        back_out[k] = back_jobs[k]()

    x = x_ref[0]
    h_bf = (x * _rms_scale(x) * norm_w_ref[...]).astype(BF16)
    run_back(8)

    def proj(off, width=BRANCH_W):
        return _dot(h_bf, w_in_ref[:, off:off + width])

    za = proj(OFF_AC)
    zb = proj(OFF_BC)
    zq = proj(OFF_QR)
    zk = proj(OFF_KR)
    v_bf = proj(OFF_VR).astype(BF16)
    u = za * _sigmoid(zb)
    ubuf_ref[HIST_PAD:HIST_PAD + tile, :] = u
    first = HIST_PAD - CONV_HIST
    row_blocks = 2
    conv_blocks = [[None] * (BRANCH_W // HEAD_DIM) for _ in range(row_blocks)]
    block_rows = tile // row_blocks
    n_block = 0
    for g in range(BRANCH_W // HEAD_DIM):
        cols = slice(g * HEAD_DIM, (g + 1) * HEAD_DIM)
        for hb in range(row_blocks):
            row0 = hb * block_rows
            acc = None
            for r in range(8):
                part = None
                for j in range(CONV_K):
                    if (first + j) % 8 != r:
                        continue
                    lo = first + j + row0
                    term = ubuf_ref[lo:lo + block_rows, cols] * conv_w_ref[0, j:j + 1, cols]
                    part = term if part is None else part + term
                acc = part if acc is None else acc + part
            conv_blocks[hb][g] = acc + conv_b_ref[:, cols]
            run_back(n_block)
            n_block += 1
    conv = jnp.concatenate([jnp.concatenate(rows, axis=-1) for rows in conv_blocks], axis=0)
    ubuf_ref[0:HIST_PAD, :] = ubuf_ref[tile:tile + HIST_PAD, :]

    z_gc = proj(OFF_GC)
    z_gr = proj(OFF_GR)
    z_qm = proj(OFF_QM)
    z_gm = proj(OFF_GM)

    per_step = s0_ref.shape[0]
    r0 = lax.rem(i, SUBLANES // per_step) * per_step
    row_id = lax.broadcasted_iota(jnp.int32, (SUBLANES, 1), 0)

    def own_rows(off):
        block = zs_ref[:, off:off + BRANCH_W]
        return jnp.concatenate(
            [jnp.sum(jnp.where(row_id == r0 + b, block, 0.0), axis=0, keepdims=True)
             for b in range(per_step)], axis=0)

    o_rows, att_rows = _sample_sequences(
        gamma, own_rows(OFF_QR), own_rows(OFF_KR), own_rows(OFF_VR), own_rows(OFF_QM),
        cos_s_ref[...], sin_s_ref[...], s0_ref, ck_ref, cv_ref, s_new_ref)
    for scr, out_ref, rows in ((os_scr, os_ref, o_rows), (atts_scr, atts_ref, att_rows)):
        block = scr[...]
        for b in range(per_step):
            block = jnp.where(row_id == r0 + b, rows[b], block)
        scr[...] = block
        out_ref[...] = block

    merged = []
    for k in range(2):
        cols = slice(k * half, (k + 1) * half)
        acc = None
        for b in range(N_BRANCH):
            term = _sigmoid(back_out[2 * b + k]) * back_out[2 * N_BRANCH + b][:, cols]
            acc = term if acc is None else acc + term
        merged.append(acc.astype(BF16))
    merged = jnp.concatenate(merged, axis=-1)

    cos2 = cos_ref[...]
    sin2 = sin_ref[...]
    q_scale = HEAD_DIM ** -0.5
    def block_diag(a, b):
        zero = jnp.zeros_like(a)
        return jnp.concatenate([jnp.concatenate([a, zero], axis=-1),
                                jnp.concatenate([zero, b], axis=-1)], axis=0)

    o_heads = []
    for h0 in range(0, RET_HEADS, 2):
        pair = (h0, h0 + 1)
        q_p = [_rotary(_head(zq, h), cos2, sin2) * q_scale for h in pair]
        k_p = [_rotary(_head(zk, h), cos2, sin2) for h in pair]
        v_p = [_head(v_bf, h) for h in pair]
        s_old = [s_ref[0, h] for h in pair]
        s = list(s_old)
        intra2 = jnp.concatenate([intra_ref[h] for h in pair], axis=-1)
        qdec2 = jnp.concatenate([qdec_ref[h] for h in pair], axis=-1)
        o_chunks = []
        for c in range(tile // RET_CHUNK):
            rows = slice(c * RET_CHUNK, (c + 1) * RET_CHUNK)
            kc = [k[rows] for k in k_p]
            vc = [v[rows] for v in v_p]
            qc2 = jnp.concatenate([q[rows] for q in q_p], axis=-1).astype(BF16)
            sc2 = _dot_nt(qc2, block_diag(*[k.astype(BF16) for k in kc])) * intra2
            o_chunks.append(_dot(sc2.astype(BF16), block_diag(*vc))
                            + _dot(qc2, block_diag(*[x.astype(BF16) for x in s])) * qdec2)
            for n, h in enumerate(pair):
                kd = (kc[n] * kdec_ref[h]).astype(BF16)
                s[n] = s[n] * c_dec[h] + _dot_tn(kd, vc[n])
        o_pair = jnp.concatenate(o_chunks, axis=0)
        for n, h in enumerate(pair):
            s_ref[0, h] = s[n]
            o_heads.append(_layernorm_rows(_head(o_pair, n)))

    out = xb_ref[0] + _dot(merged, w_out_ref[...])

    ret = jnp.concatenate(o_heads, axis=-1) * gn_w_ref[...]
    yr = _silu(z_gr) * ret
    cn = _layernorm_rows(conv) * ln_w_ref[...] + ln_b_ref[...]
    yc = _silu(z_gc) * _silu(cn)

    qm = z_qm.astype(BF16)
    mk = mk_ref[0]
    mv = mv_ref[0]
    att = []
    for h in range(MEM_HEADS):
        sc = _dot_nt(_head(qm, h), _head(mk, h)) * (HEAD_DIM ** -0.5)
        e = jnp.exp(sc - jnp.max(sc, axis=-1, keepdims=True))
        p = e / jnp.sum(e, axis=-1, keepdims=True)
        att.append(_dot(p.astype(BF16), _head(mv, h)))
    ym = _silu(z_gm) * jnp.concatenate(att, axis=-1)

    y_ref[0] = out * _rms_scale(out) * final_w_ref[...]

    h_scr[...] = h_bf
    yr_scr[...] = yr.astype(BF16)
    yc_scr[...] = yc.astype(BF16)
    ym_scr[...] = ym.astype(BF16)

    @pl.when(t == tiles_per_seq - 1)
    def _():
        seq_id = i // tiles_per_seq
        mine = lax.broadcasted_iota(jnp.int32, (hist_ref.shape[1], 1), 0) == seq_id
        for j in range(CONV_HIST):
            hist_ref[j] = jnp.where(mine, ubuf_ref[first + j:first + j + 1, :], hist_ref[j])


def _const_spec(shape, single_buffer=False):
    zeros = (0,) * len(shape)
    if single_buffer:
        return pl.BlockSpec(shape, lambda i: zeros, pipeline_mode=pl.Buffered(1))
    return pl.BlockSpec(shape, lambda i: zeros)


def _prompt_layer(x, cos2, sin2, mk_bf, mv_bf, norm_w, w_in_bf, gn_w, conv_w, conv_b, ln_w,
                  ln_b, w_br_ret_bf, w_br_conv_bf, w_br_mem_bf, w_out_bf, final_w,
                  intra, qdec, kdec, c_dec,
                  z_s, cos_s, sin_s, s0, cache_k, cache_v, gamma):
    batch, seq, _ = x.shape
    n_mem = mk_bf.shape[1]
    tile = PROMPT_TILE
    tiles_per_seq = seq // tile
    n_tiles = batch * tiles_per_seq

    def front(i):
        j = jnp.minimum(i, n_tiles - 1)
        return j // tiles_per_seq, lax.rem(j, tiles_per_seq)

    def back(i):
        j = jnp.maximum(i - 1, 0)
        return j // tiles_per_seq, lax.rem(j, tiles_per_seq)

    row = lambda w: _const_spec((1, w))
    weight = lambda a: _const_spec(a.shape, single_buffer=True)
    table = _const_spec((RET_HEADS, RET_CHUNK, RET_CHUNK))
    n_dec = z_s.shape[0]
    per_step = n_dec // n_tiles
    assert per_step * n_tiles == n_dec
    share = lambda shape: pl.BlockSpec(
        shape, lambda i: (jnp.minimum(i, n_tiles - 1),) + (0,) * (len(shape) - 1))
    assert SUBLANES % per_step == 0
    rows8 = lambda w: pl.BlockSpec(
        (SUBLANES, w), lambda i: (jnp.minimum(i, n_tiles - 1) // (SUBLANES // per_step), 0))
    x_tile = lambda which: pl.BlockSpec((1, tile, D_MODEL), lambda i: (*which(i), 0))
    per_seq = lambda shape: pl.BlockSpec(
        (1,) + shape, lambda i: (front(i)[0],) + (0,) * len(shape))
    return pl.pallas_call(
        functools.partial(_prompt_kernel, c_dec, gamma, tiles_per_seq, n_tiles),
        grid=(n_tiles + 1,),
        in_specs=[
            x_tile(front), x_tile(back),
            pl.BlockSpec((tile, HEAD_DIM), lambda i: (front(i)[1], 0)),
            pl.BlockSpec((tile, HEAD_DIM), lambda i: (front(i)[1], 0)),
            per_seq((n_mem, BRANCH_W)), per_seq((n_mem, BRANCH_W)),
            row(D_MODEL), weight(w_in_bf), row(BRANCH_W),
            _const_spec((1, CONV_K, BRANCH_W)), row(BRANCH_W), row(BRANCH_W), row(BRANCH_W),
            weight(w_br_ret_bf), weight(w_br_conv_bf), weight(w_br_mem_bf), weight(w_out_bf),
            row(D_MODEL), table, table, table,
            rows8(IN_W), row(HEAD_DIM), row(HEAD_DIM),
            share((per_step,) + s0.shape[1:]),
            share((per_step,) + cache_k.shape[1:]), share((per_step,) + cache_v.shape[1:]),
        ],
        out_specs=[
            x_tile(back),
            per_seq((RET_HEADS, HEAD_DIM, HEAD_DIM)),
            _const_spec((CONV_HIST, batch, BRANCH_W)),
            rows8(BRANCH_W), rows8(BRANCH_W),
            share((per_step,) + s0.shape[1:]),
        ],
        out_shape=[
            jax.ShapeDtypeStruct((batch, seq, D_MODEL), F32),
            jax.ShapeDtypeStruct((batch, RET_HEADS, HEAD_DIM, HEAD_DIM), F32),
            jax.ShapeDtypeStruct((CONV_HIST, batch, BRANCH_W), F32),
            jax.ShapeDtypeStruct((n_dec, BRANCH_W), F32),
            jax.ShapeDtypeStruct((n_dec, BRANCH_W), F32),
            jax.ShapeDtypeStruct(s0.shape, F32),
        ],
        scratch_shapes=[
            pltpu.VMEM((HIST_PAD + tile, BRANCH_W), F32),
            pltpu.VMEM((tile, D_MODEL), BF16),
            pltpu.VMEM((tile, BRANCH_W), BF16),
            pltpu.VMEM((tile, BRANCH_W), BF16),
            pltpu.VMEM((tile, BRANCH_W), BF16),
            pltpu.VMEM((SUBLANES, BRANCH_W), F32),
            pltpu.VMEM((SUBLANES, BRANCH_W), F32),
        ],
        compiler_params=pltpu.CompilerParams(
            dimension_semantics=("arbitrary",), vmem_limit_bytes=VMEM_LIMIT_BYTES),
        name="prompt_layer",
    )(x, x, cos2, sin2, mk_bf, mv_bf, norm_w, w_in_bf, gn_w, conv_w, conv_b, ln_w, ln_b,
      w_br_ret_bf, w_br_conv_bf, w_br_mem_bf, w_out_bf, final_w, intra, qdec, kdec,
      z_s, cos_s, sin_s, s0, cache_k, cache_v)


SAMPLE_PROJ_BLOCK = 1536


def _sample_proj_kernel(x_ref, norm_w_ref, w_ref, z_ref, w_bf_ref):
    x = x_ref[:, 0, :]
    h_bf = (x * _rms_scale(x) * norm_w_ref[...]).astype(BF16)
    w_bf = w_ref[0].astype(BF16)
    w_bf_ref[...] = w_bf
    z_ref[...] = _dot(h_bf, w_bf)


def _sample_proj(xs, norm_w, w_in):
    n = xs.shape[0]
    return pl.pallas_call(
        _sample_proj_kernel,
        grid=(IN_W // SAMPLE_PROJ_BLOCK,),
        in_specs=[
            pl.BlockSpec((n, 1, D_MODEL), lambda j: (0, 0, 0)),
            pl.BlockSpec((1, D_MODEL), lambda j: (0, 0)),
            pl.BlockSpec((1, D_MODEL, SAMPLE_PROJ_BLOCK), lambda j: (0, 0, j)),
        ],
        out_specs=[pl.BlockSpec((n, SAMPLE_PROJ_BLOCK), lambda j: (0, j)),
                   pl.BlockSpec((D_MODEL, SAMPLE_PROJ_BLOCK), lambda j: (0, j))],
        out_shape=[jax.ShapeDtypeStruct((n, IN_W), F32),
                   jax.ShapeDtypeStruct((D_MODEL, IN_W), BF16)],
        compiler_params=pltpu.CompilerParams(
            dimension_semantics=("arbitrary",), vmem_limit_bytes=VMEM_LIMIT_BYTES),
        name="sample_proj",
    )(xs, norm_w, w_in)


def _sample_sequences(gamma, zq, zk, v, qm, cos2, sin2, s0_ref, ck_ref, cv_ref, s_new_ref):
    n = zq.shape[0]
    n_mem = ck_ref.shape[1] // MEM_HEADS
    q_scale = HEAD_DIM ** -0.5
    mem_scale = HEAD_DIM ** -0.5
    q = [_rotary(_head(zq, h), cos2, sin2) * q_scale for h in range(RET_HEADS)]
    k = [_rotary(_head(zk, h), cos2, sin2) for h in range(RET_HEADS)]

    eye = (lax.broadcasted_iota(jnp.int32, (HEAD_DIM, HEAD_DIM), 0)
           == lax.broadcasted_iota(jnp.int32, (HEAD_DIM, HEAD_DIM), 1))
    ones_bf = jnp.ones((HEAD_DIM, HEAD_DIM), BF16)

    def as_columns(row):
        diag = jnp.where(eye, jnp.broadcast_to(row, (HEAD_DIM, HEAD_DIM)), 0.0)
        return _dot(diag.astype(BF16), ones_bf)

    o_rows, att_rows = [], []
    for b in range(n):
        o_heads = []
        for h in range(RET_HEADS):
            q_row = q[h][b:b + 1]
            k_row = k[h][b:b + 1]
            v_row = _head(v, h)[b:b + 1]
            s0 = s0_ref[b, h]
            qk = jnp.sum(q_row * k_row, axis=-1, keepdims=True)
            q_lhs = jnp.broadcast_to(q_row, (SUBLANES, HEAD_DIM)).astype(BF16)
            qs = _dot(q_lhs, s0.astype(BF16))[0:1]
            o_heads.append(qk * v_row + gamma[h] * qs)
            s_new_ref[b, h] = gamma[h] * s0 + as_columns(k_row) * v_row
        o_rows.append(jnp.concatenate(o_heads, axis=-1))

        att_heads = []
        for h in range(MEM_HEADS):
            head_rows = pl.ds(h, n_mem, stride=MEM_HEADS)
            prod = ck_ref[b, head_rows, :] * (_head(qm, h)[b:b + 1] * mem_scale)
            sc = jnp.sum(prod, axis=-1, keepdims=True)
            e = jnp.exp(sc - jnp.max(sc, axis=0, keepdims=True))
            weighted = jnp.sum(e * cv_ref[b, head_rows, :], axis=0, keepdims=True)
            att_heads.append(weighted / jnp.sum(e, axis=0, keepdims=True))
        att_rows.append(jnp.concatenate(att_heads, axis=-1))
    return o_rows, att_rows


def _sample_out_kernel(x_ref, z_ref, o_ref, att_ref, hist_ref, conv_w_ref, gn_w_ref, conv_b_ref,
                       ln_w_ref, ln_b_ref, w_br_ret_ref, w_br_conv_ref, w_br_mem_ref, w_out_ref,
                       final_w_ref, y_ref, hist_new_ref):
    gate = lambda off: _silu(z_ref[:, off:off + BRANCH_W])
    yr = gate(OFF_GR) * _group_norm_heads(o_ref[...], gn_w_ref[...])

    u = z_ref[:, OFF_AC:OFF_AC + BRANCH_W] * _sigmoid(z_ref[:, OFF_BC:OFF_BC + BRANCH_W])
    conv = u * conv_w_ref[0, CONV_HIST:CONV_K, :] + conv_b_ref[...]
    for j in range(CONV_HIST):
        conv = conv + hist_ref[j] * conv_w_ref[0, j:j + 1, :]
        hist_new_ref[j] = hist_ref[j + 1] if j + 1 < CONV_HIST else u
    cn = _layernorm_rows(conv) * ln_w_ref[...] + ln_b_ref[...]
    yc = gate(OFF_GC) * _silu(cn)

    ym = gate(OFF_GM) * att_ref[...]
    gate_pre = lambda i: z_ref[:, OFF_MERGE + i * D_MODEL:OFF_MERGE + (i + 1) * D_MODEL]
    y_ref[:, 0, :] = _merge_and_project(x_ref[:, 0, :], gate_pre, yr, yc, ym, w_br_ret_ref,
                                        w_br_conv_ref, w_br_mem_ref, w_out_ref, final_w_ref[...])


SAMPLE_OUT_BLOCK = 32


def _sample_out(xs, z, o, att, hist, conv_w, gn_w, conv_b, ln_w, ln_b,
                w_br_ret_bf, w_br_conv_bf, w_br_mem_bf, w_out_bf, final_w):
    n = xs.shape[0]
    nb = SAMPLE_OUT_BLOCK
    rows = lambda w: pl.BlockSpec((nb, w), lambda i: (i, 0))
    hist_spec = pl.BlockSpec((CONV_HIST, nb, BRANCH_W), lambda i: (0, i, 0))
    tokens = pl.BlockSpec((nb, 1, D_MODEL), lambda i: (i, 0, 0))
    const = lambda a: pl.BlockSpec(a.shape, lambda i: (0,) * a.ndim)
    weight = lambda a: pl.BlockSpec(a.shape, lambda i: (0,) * a.ndim,
                                    pipeline_mode=pl.Buffered(1))
    return pl.pallas_call(
        _sample_out_kernel,
        grid=(n // nb,),
        in_specs=[tokens, rows(IN_W), rows(BRANCH_W), rows(BRANCH_W), hist_spec,
                  const(conv_w), const(gn_w), const(conv_b), const(ln_w), const(ln_b),
                  weight(w_br_ret_bf), weight(w_br_conv_bf), weight(w_br_mem_bf),
                  weight(w_out_bf), const(final_w)],
        out_specs=[tokens, hist_spec],
        out_shape=[jax.ShapeDtypeStruct(xs.shape, F32), jax.ShapeDtypeStruct(hist.shape, F32)],
        compiler_params=pltpu.CompilerParams(
            dimension_semantics=("arbitrary",), vmem_limit_bytes=VMEM_LIMIT_BYTES),
        name="sample_out",
    )(xs, z, o, att, hist, conv_w, gn_w, conv_b, ln_w, ln_b,
      w_br_ret_bf, w_br_conv_bf, w_br_mem_bf, w_out_bf, final_w)


def _rotary_tables(pos):
    half = HEAD_DIM // 2
    inv = ROPE_BASE ** (-np.arange(half, dtype=np.float64) / half)
    ang = np.asarray(pos, np.float64)[:, None] * inv[None, :]
    cos, sin = np.cos(ang), np.sin(ang)
    table = lambda a, b: np.concatenate([a, b], axis=-1).astype(np.float32)
    return table(cos, cos), table(-sin, sin)


def _decay_tables():
    c = RET_CHUNK
    log_g = np.log1p(-np.exp2(-5.0 - np.arange(RET_HEADS, dtype=np.float64)))
    idx = np.arange(c, dtype=np.float64)
    diff = idx[:, None] - idx[None, :]
    intra = np.where(diff >= 0, np.exp(np.maximum(diff, 0.0)[None] * log_g[:, None, None]), 0.0)
    q_dec = np.exp((idx + 1.0)[None, :] * log_g[:, None])
    k_dec = np.exp((c - 1.0 - idx)[None, :] * log_g[:, None])
    bcast = lambda a: np.ascontiguousarray(
        np.broadcast_to(a[:, :, None], (RET_HEADS, c, HEAD_DIM))).astype(np.float32)
    return intra.astype(np.float32), bcast(q_dec), bcast(k_dec)


def _gamma_powers(n):
    return tuple(float(np.exp(np.log1p(-np.exp2(-5.0 - h)) * n)) for h in range(RET_HEADS))


def kernel(x_prompt, x_sample, mem_prompt, state_ret, state_conv, cache_mem_k, cache_mem_v,
           norm_w, w_in, ret_gn_w, conv_w, conv_b, conv_ln_w, conv_ln_b, mem_norm_w,
           w_mem_kv, w_br_ret, w_br_conv, w_br_mem, w_out, final_norm_w):
    depth = w_in.shape[0]
    assert depth == 1, "single-layer step"
    batch, seq, _ = x_prompt.shape
    n_dec, dec_seq, _ = x_sample.shape
    assert dec_seq == 1 and seq % PROMPT_TILE == 0
    n_mem = mem_prompt.shape[1]

    final_w = final_norm_w[None, :]

    intra, qdec, kdec = _decay_tables()
    cos_p, sin_p = _rotary_tables(np.arange(seq))
    cos_s, sin_s = _rotary_tables(np.arange(dec_seq) + PAST_LEN)

    z_s, w_in_bf = _sample_proj(x_sample, norm_w, w_in)

    mk, mv, mk_bf, mv_bf, w_br_ret_bf, w_br_conv_bf, w_br_mem_bf, w_out_bf = _mem_kv(
        mem_prompt, mem_norm_w, w_mem_kv, (w_br_ret, w_br_conv, w_br_mem, w_out))

    y_prompt, s_prompt, hist_prompt, o_s, att_s, s_sample = _prompt_layer(
        x_prompt, cos_p, sin_p, mk_bf, mv_bf, norm_w, w_in_bf, ret_gn_w, conv_w, conv_b,
        conv_ln_w, conv_ln_b, w_br_ret_bf, w_br_conv_bf, w_br_mem_bf, w_out_bf, final_w,
        intra, qdec, kdec, _gamma_powers(RET_CHUNK),
        z_s, cos_s, sin_s, state_ret[0],
        cache_mem_k.reshape(n_dec, n_mem * MEM_HEADS, HEAD_DIM),
        cache_mem_v.reshape(n_dec, n_mem * MEM_HEADS, HEAD_DIM), _gamma_powers(1))

    y_sample, hist_sample = _sample_out(
        x_sample, z_s, o_s, att_s,
        jnp.transpose(state_conv[0], (1, 0, 2)), conv_w, ret_gn_w, conv_b, conv_ln_w, conv_ln_b,
        w_br_ret_bf, w_br_conv_bf, w_br_mem_bf, w_out_bf, final_w)

    heads = lambda a: a.reshape(1, batch, n_mem, MEM_HEADS, HEAD_DIM)
    return (y_prompt, y_sample, s_prompt[None], s_sample[None],
            jnp.transpose(hist_prompt, (1, 0, 2))[None],
            jnp.transpose(hist_sample, (1, 0, 2))[None], heads(mk), heads(mv))
```
